```python
import jax
import jax.numpy as jnp
from jax import lax
import numpy as np

D_MODEL = 1024
BATCH = 8
SEQ = 8192
DEPTH = 1

GRID_W = 64
CTX_LEN = 256
D_MIX = D_MODEL
NA_HEADS = 8
NA_HEAD_DIM = 64
NA_WIN_H = 8
NA_WIN_W = 16
NA_W = NA_HEADS * NA_HEAD_DIM
ML_HEADS = 4
ML_QK_DIM = 64
ML_V_DIM = 128
ML_QK_W = ML_HEADS * ML_QK_DIM
ML_V_W = ML_HEADS * ML_V_DIM
ML_CHUNK = 128
ML_N_GATES = 4 * ML_HEADS
CONV_K = 5
ROPE_BASE = 10000.0
N_EXPERTS = 16
CAPACITY_FACTOR = 2
D_FF = 2816
LN_EPS = 1e-5
DEEPNORM_ALPHA = (2.0 * DEPTH) ** 0.25
DEEPNORM_BETA = (8.0 * DEPTH) ** -0.25
SPLITS = (NA_W, 2 * NA_W, 3 * NA_W, 3 * NA_W + ML_QK_W, 3 * NA_W + 2 * ML_QK_W, 3 * NA_W + 2 * ML_QK_W + ML_V_W, 3 * NA_W + 2 * ML_QK_W + 2 * ML_V_W)
D_IN = 3 * NA_W + 2 * ML_QK_W + 2 * ML_V_W + ML_N_GATES

kernel_name = "hybrid_natten_mlstm_ec_moe_dit_layer"


def _layer_norm(x, g, b):
    xf = x.astype(jnp.float32)
    mu = xf.mean(-1, keepdims=True)
    var = jnp.mean(jnp.square(xf - mu), -1, keepdims=True)
    return ((xf - mu) * lax.rsqrt(var + LN_EPS)).astype(x.dtype) * g + b


def _ada(cvec, w_ada, b_ada):
    m = (jax.nn.silu(cvec) @ w_ada + b_ada)[..., None, :]
    return jnp.split(m, 6, axis=-1)


def _dwconv(x, w):
    k = w.shape[0]
    return lax.conv_general_dilated(x, w[:, None, :], window_strides=(1,), padding=[(k // 2, k // 2)],
                                    dimension_numbers=('NWC', 'WIO', 'NWC'), feature_group_count=x.shape[-1])


def _rope_2d(x, rows, cols):
    dh = x.shape[-1]
    half = dh // 2
    nf = half // 2
    inv = 1.0 / (ROPE_BASE ** (jnp.arange(nf, dtype=jnp.float32) / nf))

    def rot(xa, pos):
        ang = pos.astype(jnp.float32)[:, None] * inv
        cos = jnp.cos(ang)[None, :, None, :].astype(x.dtype)
        sin = jnp.sin(ang)[None, :, None, :].astype(x.dtype)
        x1, x2 = xa[..., :nf], xa[..., nf:]
        return jnp.concatenate([x1 * cos - x2 * sin, x1 * sin + x2 * cos], axis=-1)

    return jnp.concatenate([rot(x[..., :half], rows), rot(x[..., half:], cols)], axis=-1)


def _neighbourhood_attention(q, k, v, k_ctx, v_ctx, bias_table):
    B, T, H, Dh = q.shape
    rows = T // GRID_W
    kh = min(NA_WIN_H, rows)
    qg = (q * Dh ** -0.5).reshape(B, rows, GRID_W, H, Dh)
    kg = k.reshape(B, rows, GRID_W, H, Dh)
    vg = v.reshape(B, rows, GRID_W, H, Dh)
    col_start = np.clip(np.arange(GRID_W) - NA_WIN_W // 2, 0, GRID_W - NA_WIN_W)
    col_idx = col_start[:, None] + np.arange(NA_WIN_W)
    col_off = col_idx - np.arange(GRID_W)[:, None] + (NA_WIN_W - 1)
    bias_cols = bias_table[:, :, col_off]

    def row_block(r):
        rs = jnp.clip(r - kh // 2, 0, rows - kh)
        k_win = lax.dynamic_slice_in_dim(kg, rs, kh, axis=1)[:, :, col_idx]
        v_win = lax.dynamic_slice_in_dim(vg, rs, kh, axis=1)[:, :, col_idx]
        q_r = lax.dynamic_index_in_dim(qg, r, axis=1, keepdims=False)
        row_off = rs + jnp.arange(kh) - r + (NA_WIN_H - 1)
        bias = jnp.transpose(jnp.take(bias_cols, row_off, axis=1), (0, 2, 1, 3))
        s_lat = jnp.einsum('bchd,bicjhd->bhcij', q_r, k_win).astype(jnp.float32) + bias[None].astype(jnp.float32)
        s_ctx = jnp.einsum('bchd,blhd->bhcl', q_r, k_ctx).astype(jnp.float32)
        s = jnp.concatenate([s_lat.reshape(B, H, GRID_W, kh * NA_WIN_W), s_ctx], axis=-1)
        p = jax.nn.softmax(s, axis=-1).astype(v.dtype)
        p_lat = p[..., :kh * NA_WIN_W].reshape(B, H, GRID_W, kh, NA_WIN_W)
        p_ctx = p[..., kh * NA_WIN_W:]
        return (jnp.einsum('bhcij,bicjhd->bchd', p_lat, v_win)
                + jnp.einsum('bhcl,blhd->bchd', p_ctx, v_ctx))

    out = lax.map(row_block, jnp.arange(rows))
    return jnp.moveaxis(out, 0, 1).reshape(B, T, H * Dh)


def _context_attention(q, k, v):
    s = jnp.einsum('bqhd,bkhd->bhqk', q * q.shape[-1] ** -0.5, k).astype(jnp.float32)
    p = jax.nn.softmax(s, axis=-1).astype(v.dtype)
    o = jnp.einsum('bhqk,bkhd->bqhd', p, v)
    return o.reshape(o.shape[0], o.shape[1], -1)


def _mlstm_chunkwise(q, k, v, i_pre, logf, state):
    B, H, T, dk = q.shape
    dv = v.shape[-1]
    L = ML_CHUNK
    nc = T // L
    qc = q.reshape(B, H, nc, L, dk)
    kc = k.reshape(B, H, nc, L, dk)
    vc = v.reshape(B, H, nc, L, dv)
    ic = i_pre.reshape(B, H, nc, L)
    bc = jnp.cumsum(logf.reshape(B, H, nc, L), axis=-1)
    b_last = bc[..., -1]
    a = b_last[..., None] - bc + ic
    m_loc = a.max(-1)
    w = jnp.exp(a - m_loc[..., None])
    c_loc = jnp.einsum('bhnl,bhnlv,bhnlk->bhnvk', w, vc, kc)
    n_loc = jnp.einsum('bhnl,bhnlk->bhnk', w, kc)

    def step(carry, inp):
        c_prev, n_prev, m_prev = carry
        bl, ml, cl, nl = inp
        m_new = jnp.maximum(bl + m_prev, ml)
        s_prev = jnp.exp(bl + m_prev - m_new)
        s_loc = jnp.exp(ml - m_new)
        c_new = s_prev[..., None, None] * c_prev + s_loc[..., None, None] * cl
        n_new = s_prev[..., None] * n_prev + s_loc[..., None] * nl
        return (c_new, n_new, m_new), (c_prev, n_prev, m_prev)

    xs = (jnp.moveaxis(b_last, 2, 0), jnp.moveaxis(m_loc, 2, 0), jnp.moveaxis(c_loc, 2, 0), jnp.moveaxis(n_loc, 2, 0))
    final, (c_in, n_in, m_in) = lax.scan(step, state, xs)
    c_in = jnp.moveaxis(c_in, 0, 2)
    n_in = jnp.moveaxis(n_in, 0, 2)
    m_in = jnp.moveaxis(m_in, 0, 2)

    lower = np.tril(np.ones((L, L), dtype=bool))
    dmat = jnp.where(lower, bc[..., :, None] - bc[..., None, :] + ic[..., None, :], -jnp.inf)
    m_prev_term = bc + m_in[..., None]
    m_t = jnp.maximum(dmat.max(-1), m_prev_term)
    s = jnp.einsum('bhntk,bhnsk->bhnts', qc, kc) * jnp.exp(dmat - m_t[..., None])
    inter = jnp.exp(m_prev_term - m_t)
    num = (jnp.einsum('bhnts,bhnsv->bhntv', s, vc)
           + inter[..., None] * jnp.einsum('bhnvk,bhntk->bhntv', c_in, qc))
    den = s.sum(-1) + inter * jnp.einsum('bhnk,bhntk->bhnt', n_in, qc)
    h = num / jnp.maximum(jnp.abs(den), jnp.exp(-m_t))[..., None]
    return h.reshape(B, H, T, dv), final


def _mlstm_inputs(qm, km, vm, gates, b_gate, conv_w, pos):
    B, T, _ = qm.shape
    qk = jax.nn.silu(_dwconv(jnp.concatenate([qm, km], axis=-1), conv_w))
    q = qk[..., :ML_QK_W].reshape(B, T, ML_HEADS, ML_QK_DIM)
    k = qk[..., ML_QK_W:].reshape(B, T, ML_HEADS, ML_QK_DIM)
    if pos is not None:
        q = _rope_2d(q, pos[0], pos[1])
        k = _rope_2d(k, pos[0], pos[1])
    v = vm.reshape(B, T, ML_HEADS, ML_V_DIM)
    g = (gates + b_gate).astype(jnp.float32).reshape(B, T, 4, ML_HEADS)
    g = jnp.transpose(g, (2, 0, 3, 1))
    tr = lambda a: jnp.transpose(a, (0, 2, 1, 3))
    return tr(q) * ML_QK_DIM ** -0.5, tr(k), tr(v), g


def _mlstm_bidirectional(lat, ctx_in):
    q, k, v, g = lat
    qc, kc, vc, gc = ctx_in
    B, H = q.shape[:2]
    zero = (jnp.zeros((B, H, ML_V_DIM, ML_QK_DIM), jnp.float32), jnp.zeros((B, H, ML_QK_DIM), jnp.float32),
            jnp.zeros((B, H), jnp.float32))
    flip = lambda a: jnp.flip(a, axis=2)
    hc_f, st_f = _mlstm_chunkwise(qc, kc, vc, gc[0], jax.nn.log_sigmoid(gc[1]), zero)
    h_f, _ = _mlstm_chunkwise(q, k, v, g[0], jax.nn.log_sigmoid(g[1]), st_f)
    hc_b, st_b = _mlstm_chunkwise(flip(qc), flip(kc), flip(vc), flip(gc[2]), flip(jax.nn.log_sigmoid(gc[3])), zero)
    h_b, _ = _mlstm_chunkwise(flip(q), flip(k), flip(v), flip(g[2]), flip(jax.nn.log_sigmoid(g[3])), st_b)
    return h_f + flip(h_b), hc_f + flip(hc_b)


def _head_norm(h, g):
    h = jnp.transpose(h, (0, 2, 1, 3)).astype(jnp.float32)
    mu = h.mean(-1, keepdims=True)
    var = jnp.mean(jnp.square(h - mu), -1, keepdims=True)
    hn = (h - mu) * lax.rsqrt(var + LN_EPS)
    return hn.reshape(hn.shape[0], hn.shape[1], -1) * g


def _expert_choice_ffn(h, w_router, w_gate, w_up, w_down):
    B, T, D = h.shape
    cap = CAPACITY_FACTOR * T // N_EXPERTS
    aff = jax.nn.softmax(jnp.einsum('btd,de->bte', h, w_router).astype(jnp.float32), axis=-1)
    g, idx = lax.top_k(jnp.swapaxes(aff, 1, 2), cap)
    g = jnp.moveaxis(g, 1, 0).astype(h.dtype)
    idx = jnp.moveaxis(idx, 1, 0)

    def expert(args):
        idx_e, g_e, wg, wu, wd = args
        xe = jax.vmap(lambda hb, ib: hb[ib])(h, idx_e)
        return ((jax.nn.silu(xe @ wg) * (xe @ wu)) @ wd) * g_e[..., None]

    y = lax.map(expert, (idx, g, w_gate, w_up, w_down))
    return jnp.zeros_like(h).at[jnp.arange(B)[None, :, None], idx].add(y)


def _layer(x, ctx, c, c_ctx, w_ada, b_ada, w_in, b_gate, conv_qk, na_rel_bias, ml_norm_g, w_out,
           ln1_g, ln1_b, w_router, w_expert_gate, w_expert_up, w_expert_down, ln2_g, ln2_b, update_ctx):
    B, T, _ = x.shape
    Lc = ctx.shape[1]
    t = jnp.arange(T)
    pos = (t // GRID_W, t % GRID_W)
    sh1, sc1, g1, sh2, sc2, g2 = _ada(c, w_ada, b_ada)
    csh1, csc1, cg1, csh2, csc2, cg2 = _ada(c_ctx, w_ada, b_ada)

    ux = (x * (1 + sc1) + sh1) @ w_in
    uc = (ctx * (1 + csc1) + csh1) @ w_in
    qa, ka, va, qm, km, vm, om, gt = jnp.split(ux, SPLITS, axis=-1)
    qa_c, ka_c, va_c, qm_c, km_c, vm_c, om_c, gt_c = jnp.split(uc, SPLITS, axis=-1)
    heads = lambda a: a.reshape(a.shape[0], a.shape[1], NA_HEADS, NA_HEAD_DIM)
    ka_c, va_c = heads(ka_c), heads(va_c)

    att = _neighbourhood_attention(heads(qa), heads(ka), heads(va), ka_c, va_c, na_rel_bias)

    h_lat, h_ctx = _mlstm_bidirectional(_mlstm_inputs(qm, km, vm, gt, b_gate, conv_qk, pos),
                                        _mlstm_inputs(qm_c, km_c, vm_c, gt_c, b_gate, conv_qk, None))
    ml = _head_norm(h_lat, ml_norm_g).astype(x.dtype) * jax.nn.sigmoid(om)

    mix = jnp.concatenate([att, ml], axis=-1) @ w_out
    x_new = _layer_norm(DEEPNORM_ALPHA * x + g1 * mix, ln1_g, ln1_b)
    moe = _expert_choice_ffn(x_new * (1 + sc2) + sh2, w_router, w_expert_gate, w_expert_up, w_expert_down)
    x_new = _layer_norm(DEEPNORM_ALPHA * x_new + g2 * moe, ln2_g, ln2_b)

    if update_ctx:
        att_c = _context_attention(heads(qa_c), ka_c, va_c)
        ml_c = _head_norm(h_ctx, ml_norm_g).astype(ctx.dtype) * jax.nn.sigmoid(om_c)
        mix_c = jnp.concatenate([att_c, ml_c], axis=-1) @ w_out
        ctx = _layer_norm(DEEPNORM_ALPHA * ctx + cg1 * mix_c, ln1_g, ln1_b)
        moe_c = _expert_choice_ffn(ctx * (1 + csc2) + csh2, w_router, w_expert_gate, w_expert_up, w_expert_down)
        ctx = _layer_norm(DEEPNORM_ALPHA * ctx + cg2 * moe_c, ln2_g, ln2_b)
    return x_new, ctx


def setup_inputs(seed: int = 0) -> dict:
    key = jax.random.key(seed)
    ks = jax.random.split(key, 20)
    f32 = jnp.float32
    D = D_MODEL
    nrm = lambda k, shape, scale: jax.random.normal(k, shape, f32) * scale
    gate_base = np.concatenate([np.zeros(ML_HEADS), np.linspace(3.0, 6.0, ML_HEADS),
                                np.zeros(ML_HEADS), np.linspace(3.0, 6.0, ML_HEADS)]).astype(np.float32)
    return {
        "x": nrm(ks[0], (BATCH, SEQ, D), 1.0),
        "c": nrm(ks[1], (BATCH, D), 1.0),
        "ctx": nrm(ks[2], (BATCH, CTX_LEN, D), 1.0),
        "c_ctx": nrm(ks[3], (D,), 1.0),
        "w_ada": nrm(ks[4], (DEPTH, D, 6 * D), 0.5 * D ** -0.5),
        "b_ada": nrm(ks[5], (DEPTH, 6 * D), 0.02),
        "w_in": nrm(ks[6], (DEPTH, D, D_IN), D ** -0.5),
        "b_gate": jnp.asarray(gate_base)[None, :] + nrm(ks[7], (DEPTH, ML_N_GATES), 0.1),
        "conv_qk": nrm(ks[8], (DEPTH, CONV_K, 2 * ML_QK_W), CONV_K ** -0.5),
        "na_rel_bias": nrm(ks[9], (DEPTH, NA_HEADS, 2 * NA_WIN_H - 1, 2 * NA_WIN_W - 1), 0.1),
        "ml_norm_g": 1.0 + nrm(ks[10], (DEPTH, ML_V_W), 0.02),
        "w_out": nrm(ks[11], (DEPTH, D_MIX, D), DEEPNORM_BETA * D_MIX ** -0.5),
        "ln1_g": 1.0 + nrm(ks[12], (DEPTH, D), 0.02),
        "ln1_b": nrm(ks[13], (DEPTH, D), 0.02),
        "w_router": nrm(ks[14], (DEPTH, D, N_EXPERTS), D ** -0.5),
        "w_expert_gate": nrm(ks[15], (DEPTH, N_EXPERTS, D, D_FF), D ** -0.5),
        "w_expert_up": nrm(ks[16], (DEPTH, N_EXPERTS, D, D_FF), D ** -0.5),
        "w_expert_down": nrm(ks[17], (DEPTH, N_EXPERTS, D_FF, D), DEEPNORM_BETA * D_FF ** -0.5),
        "ln2_g": 1.0 + nrm(ks[18], (DEPTH, D), 0.02),
        "ln2_b": nrm(ks[19], (DEPTH, D), 0.02),
    }


def reference(x, c, ctx, c_ctx, w_ada, b_ada, w_in, b_gate, conv_qk, na_rel_bias, ml_norm_g, w_out,
              ln1_g, ln1_b, w_router, w_expert_gate, w_expert_up, w_expert_down, ln2_g, ln2_b):
    for layer in range(DEPTH):
        x, ctx = _layer(x, ctx, c, c_ctx, w_ada[layer], b_ada[layer], w_in[layer], b_gate[layer], conv_qk[layer],
                        na_rel_bias[layer], ml_norm_g[layer], w_out[layer], ln1_g[layer], ln1_b[layer],
                        w_router[layer], w_expert_gate[layer], w_expert_up[layer], w_expert_down[layer],
                        ln2_g[layer], ln2_b[layer], update_ctx=layer < DEPTH - 1)
    return x
```

```python
import functools

import numpy as np
import jax
import jax.numpy as jnp
from jax import lax
from jax.experimental import pallas as pl
from jax.experimental.pallas import tpu as pltpu

F32 = jnp.float32
BF16 = jnp.bfloat16
HIGHEST = lax.Precision.HIGHEST

GRID_W = 64
NA_HEADS = 8
NA_HEAD_DIM = 64
NA_WIN_H = 8
NA_WIN_W = 16
NA_W = NA_HEADS * NA_HEAD_DIM
ML_HEADS = 4
ML_QK_DIM = 64
ML_V_DIM = 128
ML_QK_W = ML_HEADS * ML_QK_DIM
ML_V_W = ML_HEADS * ML_V_DIM
ML_CHUNK = 128
ML_N_GATES = 4 * ML_HEADS
CONV_K = 5
ROPE_BASE = 10000.0
N_EXPERTS = 16
CAPACITY_FACTOR = 2
LN_EPS = 1e-5
DEPTH = 1
DEEPNORM_ALPHA = (2.0 * DEPTH) ** 0.25

LANES = 128
NEG_BIG = -1e30
VMEM_LIMIT = 56 * 1024 * 1024

ATT_ROWS = 4
ATT_KROWS = ATT_ROWS + NA_WIN_H - 1


def _cparams(sem):
    return pltpu.CompilerParams(dimension_semantics=sem, vmem_limit_bytes=VMEM_LIMIT)


def _sigmoid(x):
    return 1.0 / (1.0 + jnp.exp(-x))


def _dot(a, b):
    return jnp.dot(a, b, preferred_element_type=F32)


def _dot_nt(a, b):
    return lax.dot_general(a, b, (((1,), (1,)), ((), ())), preferred_element_type=F32)


def _dot_tn(a, b):
    return lax.dot_general(a, b, (((0,), (0,)), ((), ())), preferred_element_type=F32)


def _ada_kernel(c_ref, w_ref, b_ref, o_ref):
    c = c_ref[...]
    s = c * _sigmoid(c)
    o_ref[...] = jnp.dot(s, w_ref[...], precision=HIGHEST, preferred_element_type=F32) + b_ref[...]


def _ada(cs, w_ada, b_ada):
    rows, d = cs.shape
    n = w_ada.shape[1]
    tn = 512
    return pl.pallas_call(
        _ada_kernel,
        out_shape=jax.ShapeDtypeStruct((rows, n), F32),
        grid=(n // tn,),
        in_specs=[pl.BlockSpec((rows, d), lambda j: (0, 0)),
                  pl.BlockSpec((d, tn), lambda j: (0, j)),
                  pl.BlockSpec((1, tn), lambda j: (0, j))],
        out_specs=pl.BlockSpec((rows, tn), lambda j: (0, j)),
        compiler_params=_cparams(("arbitrary",)),
        name="ada",
    )(cs, w_ada, b_ada.reshape(1, n))


def _inproj_kernel(x_ref, mod_ref, wa_ref, wqk_ref, wv_ref, wo_ref, wg_ref,
                   a_ref, qk_ref, v_ref, o_ref, g_ref):
    xm = (x_ref[0] * (1.0 + mod_ref[0, 1:2, :]) + mod_ref[0, 0:1, :]).astype(BF16)
    a_ref[0] = _dot(xm, wa_ref[...]).astype(BF16)
    qk_ref[0] = _dot(xm, wqk_ref[...]).astype(BF16)
    v_ref[0] = _dot(xm, wv_ref[...]).astype(BF16)
    o_ref[0] = _dot(xm, wo_ref[...]).astype(BF16)
    g_ref[0] = _dot(xm, wg_ref[...])


def _inproj(x, mod, ws, tm):
    b, t, d = x.shape
    wa, wqk, wv, wo, wg = ws
    full = lambda w: pl.BlockSpec(w.shape, lambda i, j: (0, 0))
    row = lambda n: pl.BlockSpec((1, tm, n), lambda i, j: (i, j, 0))
    return pl.pallas_call(
        _inproj_kernel,
        out_shape=(jax.ShapeDtypeStruct((b, t, wa.shape[1]), BF16),
                   jax.ShapeDtypeStruct((b, t, wqk.shape[1]), BF16),
                   jax.ShapeDtypeStruct((b, t, wv.shape[1]), BF16),
                   jax.ShapeDtypeStruct((b, t, wo.shape[1]), BF16),
                   jax.ShapeDtypeStruct((b, t, wg.shape[1]), F32)),
        grid=(b, t // tm),
        in_specs=[row(d), pl.BlockSpec((1, 8, d), lambda i, j: (i, 0, 0)),
                  full(wa), full(wqk), full(wv), full(wo), full(wg)],
        out_specs=(row(wa.shape[1]), row(wqk.shape[1]), row(wv.shape[1]), row(wo.shape[1]),
                   row(wg.shape[1])),
        compiler_params=_cparams(("arbitrary", "arbitrary")),
        name="inproj",
    )(x, mod, wa, wqk, wv, wo, wg)


def _qkprep_kernel(x_ref, w_ref, cos_ref, sin_ref, o_ref, *, sub, halo):
    t = x_ref.shape[1]
    n_sub = t // sub
    lane = lax.broadcasted_iota(jnp.int32, (1, LANES), 1)
    first_half = (lane & 31) < 16
    scale = jnp.where(pl.program_id(1) < (ML_QK_W // LANES), ML_QK_DIM ** -0.5, 1.0).astype(F32)
    zeros = jnp.zeros((halo, LANES), x_ref.dtype)
    for s in range(n_sub):
        lo = s * sub
        top = x_ref[0, lo - halo:lo, :] if s > 0 else zeros
        bot = x_ref[0, lo + sub:lo + sub + halo, :] if s < n_sub - 1 else zeros
        ext = jnp.concatenate([top, x_ref[0, lo:lo + sub, :], bot], axis=0).astype(F32)
        n = sub + 2 * halo
        acc = jnp.zeros((sub, LANES), F32)
        for j in range(CONV_K):
            shift = (CONV_K // 2 - j) % n
            sh = ext if shift == 0 else pltpu.roll(ext, shift, axis=0)
            acc = acc + w_ref[j:j + 1, :] * sh[halo:halo + sub, :]
        y = acc * _sigmoid(acc)
        partner = jnp.where(first_half, pltpu.roll(y, LANES - 16, axis=1), pltpu.roll(y, 16, axis=1))
        y = y * cos_ref[lo:lo + sub, :] + partner * sin_ref[lo:lo + sub, :]
        o_ref[0, lo:lo + sub, :] = (y * scale).astype(o_ref.dtype)


def _qkprep(qk, conv_w, cos_t, sin_t):
    b, t, w = qk.shape
    sub = min(t, 1024)
    kern = functools.partial(_qkprep_kernel, sub=sub, halo=16)
    return pl.pallas_call(
        kern,
        out_shape=jax.ShapeDtypeStruct((b, t, w), BF16),
        grid=(b, w // LANES),
        in_specs=[pl.BlockSpec((1, t, LANES), lambda i, j: (i, 0, j)),
                  pl.BlockSpec((8, LANES), lambda i, j: (0, j)),
                  pl.BlockSpec((t, LANES), lambda i, j: (0, 0)),
                  pl.BlockSpec((t, LANES), lambda i, j: (0, 0))],
        out_specs=pl.BlockSpec((1, t, LANES), lambda i, j: (i, 0, j)),
        compiler_params=_cparams(("arbitrary", "arbitrary")),
        name="qkprep",
    )(qk, conv_w, cos_t, sin_t)


def _rope_tables(t):
    nf = ML_QK_DIM // 4
    inv = 1.0 / (ROPE_BASE ** (jnp.arange(nf, dtype=F32) / nf))
    pos = jnp.arange(t)
    ang_r = (pos // GRID_W).astype(F32)[:, None] * inv
    ang_c = (pos % GRID_W).astype(F32)[:, None] * inv
    cos = jnp.concatenate([jnp.cos(ang_r)] * 2 + [jnp.cos(ang_c)] * 2, axis=-1)
    sin = jnp.concatenate([-jnp.sin(ang_r), jnp.sin(ang_r), -jnp.sin(ang_c), jnp.sin(ang_c)], axis=-1)
    return jnp.tile(cos, (1, 2)), jnp.tile(sin, (1, 2))


def _mlstm_kernel(qf_ref, kf_ref, vf_ref, gf_ref, qb_ref, kb_ref, vb_ref, gb_ref, bg_ref,
                  c0_ref, m0_ref, hf_ref, hb_ref, cout_ref, mout_ref, c_scr, m_scr):
    step = pl.program_id(1)
    n_steps = pl.num_programs(1)
    L = ML_CHUNK

    @pl.when(step == 0)
    def _():
        c_scr[...] = c0_ref[0]
        m_scr[...] = m0_ref[0]

    row_i = lax.broadcasted_iota(jnp.int32, (L, L), 0)
    col_i = lax.broadcasted_iota(jnp.int32, (L, L), 1)
    lane = lax.broadcasted_iota(jnp.int32, (1, LANES), 1)
    tri_lo = (col_i <= row_i)
    tri_up = (col_i >= row_i)
    ones_v = jnp.ones((L, ML_V_DIM), BF16)

    for d, (q_ref, k_ref, v_ref, g_ref, h_ref) in enumerate(
            ((qf_ref, kf_ref, vf_ref, gf_ref, hf_ref), (qb_ref, kb_ref, vb_ref, gb_ref, hb_ref))):
        tri = tri_lo if d == 0 else tri_up
        g = g_ref[0] + bg_ref[...]
        logf = jnp.minimum(g, 0.0) - jnp.log(1.0 + jnp.exp(-jnp.abs(g)))
        cum = jnp.dot(tri.astype(F32), logf, precision=HIGHEST, preferred_element_type=F32)
        f_lo = 4 + 8 * d
        z = jnp.where((lane >= f_lo) & (lane < f_lo + ML_HEADS), cum, g)
        zt = z.T
        end = L - 1 if d == 0 else 0
        for h in range(ML_HEADS):
            li, lb = 8 * d + h, f_lo + h
            pair, half = h // 2, h % 2
            head_mask = (lane >= 64 * half) & (lane < 64 * half + 64)
            bcol = jnp.sum(jnp.where(lane == lb, z, 0.0), axis=-1, keepdims=True)
            icol = jnp.sum(jnp.where(lane == li, z, 0.0), axis=-1, keepdims=True)
            brow = zt[lb:lb + 1, :]
            irow = zt[li:li + 1, :]
            total = bcol[end:end + 1, :]
            m_prev = m_scr[d, h:h + 1, 0:1]

            qm = jnp.where(head_mask, q_ref[0, :, pair * LANES:(pair + 1) * LANES], 0).astype(BF16)
            km = jnp.where(head_mask, k_ref[0, :, pair * LANES:(pair + 1) * LANES], 0).astype(BF16)
            vext = jnp.concatenate([v_ref[0, :, h * ML_V_DIM:(h + 1) * ML_V_DIM], ones_v], axis=1)

            dmat = jnp.where(tri, bcol - brow + irow, NEG_BIG)
            m_prev_term = bcol + m_prev
            m_t = jnp.maximum(jnp.max(dmat, axis=-1, keepdims=True), m_prev_term)
            sp = _dot_nt(qm, km) * jnp.exp(dmat - m_t)
            inter = jnp.exp(m_prev_term - m_t)
            c_prev = c_scr[d, h]
            r = _dot(sp.astype(BF16), vext) + inter * _dot(qm, c_prev.astype(BF16))
            num = r[:, :ML_V_DIM]
            den = r[:, ML_V_DIM:]
            h_ref[0, :, h * ML_V_DIM:(h + 1) * ML_V_DIM] = num / jnp.maximum(jnp.abs(den), jnp.exp(-m_t))

            a = total - bcol + icol
            m_loc = jnp.max(a, axis=0, keepdims=True)
            kw = (km.astype(F32) * jnp.exp(a - m_loc)).astype(BF16)
            c_loc = _dot_tn(kw, vext)
            m_new = jnp.maximum(total + m_prev, m_loc)
            c_scr[d, h] = jnp.exp(total + m_prev - m_new) * c_prev + jnp.exp(m_loc - m_new) * c_loc
            m_scr[d, h:h + 1, :] = jnp.broadcast_to(m_new, (1, LANES))

    @pl.when(step == n_steps - 1)
    def _():
        cout_ref[0] = c_scr[...]
        mout_ref[0] = m_scr[...]


def _mlstm(qk, v, gates, bg, c0, m0):
    b, t, _ = qk.shape
    L = ML_CHUNK
    nc = t // L
    kl = ML_QK_W // ML_QK_W
    fwd = lambda n, blk: pl.BlockSpec((1, L, n), lambda i, c: (i, c, blk))
    bwd = lambda n, blk: pl.BlockSpec((1, L, n), lambda i, c: (i, nc - 1 - c, blk))
    st_c = pl.BlockSpec((1,) + c0.shape[1:], lambda i, c: (i, 0, 0, 0, 0))
    st_m = pl.BlockSpec((1,) + m0.shape[1:], lambda i, c: (i, 0, 0, 0))
    return pl.pallas_call(
        _mlstm_kernel,
        out_shape=(jax.ShapeDtypeStruct((b, t, ML_V_W), F32),
                   jax.ShapeDtypeStruct((b, t, ML_V_W), F32),
                   jax.ShapeDtypeStruct(c0.shape, F32),
                   jax.ShapeDtypeStruct(m0.shape, F32)),
        grid=(b, nc),
        in_specs=[fwd(ML_QK_W, 0), fwd(ML_QK_W, kl), fwd(ML_V_W, 0), fwd(LANES, 0),
                  bwd(ML_QK_W, 0), bwd(ML_QK_W, kl), bwd(ML_V_W, 0), bwd(LANES, 0),
                  pl.BlockSpec((1, LANES), lambda i, c: (0, 0)), st_c, st_m],
        out_specs=(fwd(ML_V_W, 0), bwd(ML_V_W, 0), st_c, st_m),
        scratch_shapes=[pltpu.VMEM(c0.shape[1:], F32), pltpu.VMEM(m0.shape[1:], F32)],
        compiler_params=_cparams(("arbitrary", "arbitrary")),
        name="mlstm",
    )(qk, qk, v, gates, qk, qk, v, gates, bg, c0, m0)


def _attn_kernel(case_ref, ws_ref, q_ref, k_ref, v_ref, kc_ref, vc_ref, bias_ref, o_ref):
    j = pl.program_id(2)
    nk = ATT_KROWS * GRID_W
    start = pl.multiple_of(ws_ref[j] * GRID_W, GRID_W)
    lane = lax.broadcasted_iota(jnp.int32, (1, LANES), 1)
    q = q_ref[0]
    k = k_ref[0, pl.ds(start, nk), :]
    v = v_ref[0, pl.ds(start, nk), :]
    kc = kc_ref[0]
    vc = vc_ref[0]
    acc = jnp.zeros(q.shape, F32)
    for h in range(2):
        head_mask = (lane >= NA_HEAD_DIM * h) & (lane < NA_HEAD_DIM * (h + 1))
        qh = jnp.where(head_mask, q, 0).astype(BF16) * jnp.asarray(NA_HEAD_DIM ** -0.5, BF16)
        s = _dot_nt(qh, k) + bias_ref[0, h]
        sc = _dot_nt(qh, kc)
        m = jnp.maximum(jnp.max(s, axis=-1, keepdims=True), jnp.max(sc, axis=-1, keepdims=True))
        p = jnp.exp(s - m)
        pc = jnp.exp(sc - m)
        l = jnp.sum(p, axis=-1, keepdims=True) + jnp.sum(pc, axis=-1, keepdims=True)
        vh = jnp.where(head_mask, v, 0).astype(BF16)
        vch = jnp.where(head_mask, vc, 0).astype(BF16)
        o = _dot(p.astype(BF16), vh) + _dot(pc.astype(BF16), vch)
        acc = acc + o / l
    o_ref[0] = acc.astype(o_ref.dtype)


def _attn_plan(rows):
    kh = min(NA_WIN_H, rows)
    nj = rows // ATT_ROWS
    rs = lambda r: int(np.clip(r - kh // 2, 0, rows - kh))
    ws = [int(np.clip(ATT_ROWS * j - kh // 2, 0, rows - ATT_KROWS)) for j in range(nj)]
    sigs, case = [], []
    for j in range(nj):
        r0 = ATT_ROWS * j
        sig = (ws[j] - r0,) + tuple(rs(r0 + a) - r0 for a in range(ATT_ROWS))
        if sig not in sigs:
            sigs.append(sig)
        case.append(sigs.index(sig))
    return np.asarray(ws, np.int32), np.asarray(case, np.int32), sigs, kh


def _attn_bias(bias_table, sigs, kh):
    col_start = np.clip(np.arange(GRID_W) - NA_WIN_W // 2, 0, GRID_W - NA_WIN_W)
    c = np.arange(GRID_W)
    out = []
    for sig in sigs:
        wsr, rsr = sig[0], np.asarray(sig[1:])
        a = np.arange(ATT_ROWS)[:, None, None, None]
        cq = c[None, :, None, None]
        i = np.arange(ATT_KROWS)[None, None, :, None]
        ck = c[None, None, None, :]
        rk = wsr + i
        rsq = rsr[:, None, None, None]
        valid = (rk >= rsq) & (rk < rsq + kh) & (ck >= col_start[cq]) & (ck < col_start[cq] + NA_WIN_W)
        ridx = np.clip(rk - a + (NA_WIN_H - 1), 0, 2 * NA_WIN_H - 2)
        cidx = np.clip(ck - cq + (NA_WIN_W - 1), 0, 2 * NA_WIN_W - 2)
        shape = (ATT_ROWS, GRID_W, ATT_KROWS, GRID_W)
        valid = np.broadcast_to(valid, shape).reshape(ATT_ROWS * GRID_W, ATT_KROWS * GRID_W)
        ridx = np.broadcast_to(ridx, shape).reshape(valid.shape)
        cidx = np.broadcast_to(cidx, shape).reshape(valid.shape)
        out.append(jnp.where(valid[None], bias_table[:, ridx, cidx], NEG_BIG))
    return jnp.stack(out).astype(F32)


def _attn(a_lat, a_ctx, bias_table):
    b, t, _ = a_lat.shape
    lc = a_ctx.shape[1]
    rows = t // GRID_W
    ws, case, sigs, kh = _attn_plan(rows)
    bias = _attn_bias(bias_table, sigs, kh)
    tq = ATT_ROWS * GRID_W
    nk = ATT_KROWS * GRID_W
    n_pairs = NA_W // LANES
    grid_spec = pltpu.PrefetchScalarGridSpec(
        num_scalar_prefetch=2,
        grid=(n_pairs, b, rows // ATT_ROWS),
        in_specs=[pl.BlockSpec((1, tq, LANES), lambda p, i, j, cs, w: (i, j, p)),
                  pl.BlockSpec((1, t, LANES), lambda p, i, j, cs, w: (i, 0, n_pairs + p)),
                  pl.BlockSpec((1, t, LANES), lambda p, i, j, cs, w: (i, 0, 2 * n_pairs + p)),
                  pl.BlockSpec((1, lc, LANES), lambda p, i, j, cs, w: (i, 0, n_pairs + p)),
                  pl.BlockSpec((1, lc, LANES), lambda p, i, j, cs, w: (i, 0, 2 * n_pairs + p)),
                  pl.BlockSpec((1, 2, tq, nk), lambda p, i, j, cs, w: (cs[j], p, 0, 0))],
        out_specs=pl.BlockSpec((1, tq, LANES), lambda p, i, j, cs, w: (i, j, p)),
    )
    return pl.pallas_call(
        _attn_kernel,
        out_shape=jax.ShapeDtypeStruct((b, t, NA_W), BF16),
        grid_spec=grid_spec,
        compiler_params=_cparams(("arbitrary", "arbitrary", "arbitrary")),
        name="nattn",
    )(jnp.asarray(case), jnp.asarray(ws), a_lat, a_lat, a_lat, a_ctx, a_ctx, bias)


def _outproj_kernel(att_ref, hf_ref, hb_ref, om_ref, x_ref, mod_ref, ng_ref, wa_ref, wm_ref,
                    lng_ref, lnb_ref, wr_ref, x1_ref, hm_ref, aff_ref):
    h = hf_ref[0] + hb_ref[0]
    parts = []
    for hd in range(ML_HEADS):
        hh = h[:, hd * ML_V_DIM:(hd + 1) * ML_V_DIM]
        mu = jnp.mean(hh, axis=-1, keepdims=True)
        var = jnp.mean(jnp.square(hh - mu), axis=-1, keepdims=True)
        parts.append((hh - mu) * lax.rsqrt(var + LN_EPS))
    hn = jnp.concatenate(parts, axis=1) * ng_ref[...]
    ml = (hn * _sigmoid(om_ref[0].astype(F32))).astype(BF16)
    mix = _dot(att_ref[0], wa_ref[...]) + _dot(ml, wm_ref[...])
    y = DEEPNORM_ALPHA * x_ref[0] + mod_ref[0, 2:3, :] * mix
    mu = jnp.mean(y, axis=-1, keepdims=True)
    var = jnp.mean(jnp.square(y - mu), axis=-1, keepdims=True)
    x1 = (y - mu) * lax.rsqrt(var + LN_EPS) * lng_ref[...] + lnb_ref[...]
    x1_ref[0] = x1
    hm = x1 * (1.0 + mod_ref[0, 4:5, :]) + mod_ref[0, 3:4, :]
    hm_ref[0] = hm.astype(BF16)
    logits = jnp.dot(hm, wr_ref[...], precision=HIGHEST, preferred_element_type=F32)
    lane = lax.broadcasted_iota(jnp.int32, (1, LANES), 1)
    logits = jnp.where(lane < N_EXPERTS, logits, NEG_BIG)
    e = jnp.exp(logits - jnp.max(logits, axis=-1, keepdims=True))
    aff = e / jnp.sum(e, axis=-1, keepdims=True)
    aff_ref[0] = aff.T[:N_EXPERTS, :]


def _outproj(att, hf, hb, om, x, mod, ng, w_att, w_ml, lng, lnb, wr, tm):
    b, t, d = x.shape
    row = lambda n: pl.BlockSpec((1, tm, n), lambda i, j: (i, j, 0))
    full = lambda w: pl.BlockSpec(w.shape, lambda i, j: (0,) * w.ndim)
    return pl.pallas_call(
        _outproj_kernel,
        out_shape=(jax.ShapeDtypeStruct((b, t, d), F32),
                   jax.ShapeDtypeStruct((b, t, d), BF16),
                   jax.ShapeDtypeStruct((b, N_EXPERTS, t), F32)),
        grid=(b, t // tm),
        in_specs=[row(NA_W), row(ML_V_W), row(ML_V_W), row(ML_V_W), row(d),
                  pl.BlockSpec((1, 8, d), lambda i, j: (i, 0, 0)),
                  full(ng), full(w_att), full(w_ml), full(lng), full(lnb), full(wr)],
        out_specs=(row(d), row(d), pl.BlockSpec((1, N_EXPERTS, tm), lambda i, j: (i, 0, j))),
        compiler_params=_cparams(("arbitrary", "arbitrary")),
        name="outproj",
    )(att, hf, hb, om, x, mod, ng, w_att, w_ml, lng, lnb, wr)


def _moe_kernel(x_ref, g_ref, wg_ref, wu_ref, wd_ref, o_ref):
    f = pl.program_id(2)
    x = x_ref[0]
    hg = _dot(x, wg_ref[0].astype(BF16))
    hu = _dot(x, wu_ref[0].astype(BF16))
    act = (hg * _sigmoid(hg) * hu).astype(BF16)
    y = _dot(act, wd_ref[0].astype(BF16))

    @pl.when(f == 0)
    def _():
        o_ref[0] = y

    @pl.when(f > 0)
    def _():
        o_ref[0] += y

    @pl.when(f == pl.num_programs(2) - 1)
    def _():
        o_ref[0] = o_ref[0] * g_ref[0].T[:, 0:1]


def _moe(xe, g8, w_gate, w_up, w_down, tm, tf):
    e, m, d = xe.shape
    dff = w_gate.shape[2]
    return pl.pallas_call(
        _moe_kernel,
        out_shape=jax.ShapeDtypeStruct((e, m, d), F32),
        grid=(e, m // tm, dff // tf),
        in_specs=[pl.BlockSpec((1, tm, d), lambda i, j, f: (i, j, 0)),
                  pl.BlockSpec((1, 8, tm), lambda i, j, f: (i, 0, j)),
                  pl.BlockSpec((1, d, tf), lambda i, j, f: (i, 0, f)),
                  pl.BlockSpec((1, d, tf), lambda i, j, f: (i, 0, f)),
                  pl.BlockSpec((1, tf, d), lambda i, j, f: (i, f, 0))],
        out_specs=pl.BlockSpec((1, tm, d), lambda i, j, f: (i, j, 0)),
        compiler_params=_cparams(("arbitrary", "arbitrary", "arbitrary")),
        name="moe",
    )(xe, g8, w_gate, w_up, w_down)


def _final_kernel(x1_ref, moe_ref, mod_ref, lng_ref, lnb_ref, o_ref):
    y = DEEPNORM_ALPHA * x1_ref[0] + mod_ref[0, 5:6, :] * moe_ref[0]
    mu = jnp.mean(y, axis=-1, keepdims=True)
    var = jnp.mean(jnp.square(y - mu), axis=-1, keepdims=True)
    o_ref[0] = (y - mu) * lax.rsqrt(var + LN_EPS) * lng_ref[...] + lnb_ref[...]


def _final(x1, moe, mod, lng, lnb, tm):
    b, t, d = x1.shape
    row = pl.BlockSpec((1, tm, d), lambda i, j: (i, j, 0))
    vec = pl.BlockSpec((1, d), lambda i, j: (0, 0))
    return pl.pallas_call(
        _final_kernel,
        out_shape=jax.ShapeDtypeStruct((b, t, d), F32),
        grid=(b, t // tm),
        in_specs=[row, row, pl.BlockSpec((1, 8, d), lambda i, j: (i, 0, 0)), vec, vec],
        out_specs=row,
        compiler_params=_cparams(("arbitrary", "arbitrary")),
        name="final_ln",
    )(x1, moe, mod, lng, lnb)


def _split_w_in(w_in):
    a_end = 3 * NA_W
    qk_end = a_end + 2 * ML_QK_W
    v_end = qk_end + ML_V_W
    o_end = v_end + ML_V_W
    wg = jnp.pad(w_in[:, o_end:], ((0, 0), (0, LANES - ML_N_GATES)))
    return tuple(w.astype(BF16) for w in
                 (w_in[:, :a_end], w_in[:, a_end:qk_end], w_in[:, qk_end:v_end], w_in[:, v_end:o_end], wg))


def _layer(x, ctx, c, c_ctx, w_ada, b_ada, w_in, b_gate, conv_qk, na_rel_bias, ml_norm_g, w_out,
           ln1_g, ln1_b, w_router, w_expert_gate, w_expert_up, w_expert_down, ln2_g, ln2_b):
    b, t, d = x.shape
    lc = ctx.shape[1]

    cs = jnp.zeros((16, d), F32).at[:b].set(c).at[b].set(c_ctx)
    m = _ada(cs, w_ada, b_ada)
    mod = jnp.pad(m[:b].reshape(b, 6, d), ((0, 0), (0, 2), (0, 0)))
    mod_ctx = jnp.broadcast_to(jnp.pad(m[b].reshape(6, d), ((0, 2), (0, 0)))[None], (b, 8, d))

    ws = _split_w_in(w_in)
    a_lat, qk_lat, v_lat, o_lat, g_lat = _inproj(x, mod, ws, tm=512)
    a_ctx, qk_ctx, v_ctx, _, g_ctx = _inproj(ctx, mod_ctx, ws, tm=lc)

    conv_w = jnp.pad(conv_qk, ((0, 8 - CONV_K), (0, 0)))
    cos_t, sin_t = _rope_tables(t)
    qk_lat = _qkprep(qk_lat, conv_w, cos_t, sin_t)
    qk_ctx = _qkprep(qk_ctx, conv_w, jnp.ones((lc, LANES), F32), jnp.zeros((lc, LANES), F32))

    bg = jnp.pad(b_gate, (0, LANES - ML_N_GATES)).reshape(1, LANES)
    c0 = jnp.zeros((b, 2, ML_HEADS, LANES, 2 * ML_V_DIM), F32)
    m0 = jnp.zeros((b, 2, 8, LANES), F32)
    _, _, c1, m1 = _mlstm(qk_ctx, v_ctx, g_ctx, bg, c0, m0)
    hf, hb, _, _ = _mlstm(qk_lat, v_lat, g_lat, bg, c1, m1)

    att = _attn(a_lat, a_ctx, na_rel_bias)

    wr = jnp.pad(w_router, ((0, 0), (0, LANES - N_EXPERTS)))
    x1, hm, aff_t = _outproj(att, hf, hb, o_lat, x, mod, ml_norm_g.reshape(1, -1),
                             w_out[:NA_W].astype(BF16), w_out[NA_W:].astype(BF16),
                             ln1_g.reshape(1, d), ln1_b.reshape(1, d), wr, tm=256)

    cap = CAPACITY_FACTOR * t // N_EXPERTS
    g, idx = lax.top_k(aff_t, cap)
    xe = jax.vmap(lambda hb_, ib: hb_[ib])(hm, idx)
    xe = jnp.swapaxes(xe, 0, 1).reshape(N_EXPERTS, b * cap, d)
    g_e = jnp.swapaxes(g, 0, 1).reshape(N_EXPERTS, 1, b * cap)
    y = _moe(xe, jnp.broadcast_to(g_e, (N_EXPERTS, 8, b * cap)),
             w_expert_gate, w_expert_up, w_expert_down, tm=min(2048, b * cap), tf=256)
    y = jnp.swapaxes(y.reshape(N_EXPERTS, b, cap, d), 0, 1)
    moe = jnp.zeros((b, t, d), F32).at[jnp.arange(b)[:, None, None], idx].add(y)

    return _final(x1, moe, mod, ln2_g.reshape(1, d), ln2_b.reshape(1, d), tm=512)


def kernel(x, c, ctx, c_ctx, w_ada, b_ada, w_in, b_gate, conv_qk, na_rel_bias, ml_norm_g, w_out,
           ln1_g, ln1_b, w_router, w_expert_gate, w_expert_up, w_expert_down, ln2_g, ln2_b):
    return _layer(x, ctx, c, c_ctx, w_ada[0], b_ada[0], w_in[0], b_gate[0], conv_qk[0], na_rel_bias[0],
                  ml_norm_g[0], w_out[0], ln1_g[0], ln1_b[0], w_router[0], w_expert_gate[0],
                  w_expert_up[0], w_expert_down[0], ln2_g[0], ln2_b[0])
```

```python
import functools

import numpy as np
import jax
import jax.numpy as jnp
from jax import lax
from jax.experimental import pallas as pl
from jax.experimental.pallas import tpu as pltpu

F32 = jnp.float32
BF16 = jnp.bfloat16
HIGHEST = lax.Precision.HIGHEST

GRID_W = 64
NA_HEADS = 8
NA_HEAD_DIM = 64
NA_WIN_H = 8
NA_WIN_W = 16
NA_W = NA_HEADS * NA_HEAD_DIM
ML_HEADS = 4
ML_QK_DIM = 64
ML_V_DIM = 128
ML_QK_W = ML_HEADS * ML_QK_DIM
ML_V_W = ML_HEADS * ML_V_DIM
ML_CHUNK = 128
ML_N_GATES = 4 * ML_HEADS
CONV_K = 5
ROPE_BASE = 10000.0
N_EXPERTS = 16
CAPACITY_FACTOR = 2
LN_EPS = 1e-5
DEPTH = 1
DEEPNORM_ALPHA = (2.0 * DEPTH) ** 0.25

LANES = 128
NEG_BIG = -1e30
VMEM_LIMIT = 56 * 1024 * 1024

ATT_ROWS = 4
ATT_KROWS = ATT_ROWS + NA_WIN_H - 1


def _cparams(sem):
    return pltpu.CompilerParams(dimension_semantics=sem, vmem_limit_bytes=VMEM_LIMIT)


def _sigmoid(x):
    return 1.0 / (1.0 + jnp.exp(-x))


def _dot(a, b):
    return jnp.dot(a, b, preferred_element_type=F32)


def _dot_nt(a, b):
    return lax.dot_general(a, b, (((1,), (1,)), ((), ())), preferred_element_type=F32)


def _dot_tn(a, b):
    return lax.dot_general(a, b, (((0,), (0,)), ((), ())), preferred_element_type=F32)


def _ada_kernel(c_ref, w_ref, b_ref, o_ref):
    c = c_ref[...]
    s = c * _sigmoid(c)
    o_ref[...] = jnp.dot(s, w_ref[...], precision=HIGHEST, preferred_element_type=F32) + b_ref[...]


def _ada(cs, w_ada, b_ada):
    rows, d = cs.shape
    n = w_ada.shape[1]
    tn = 512
    return pl.pallas_call(
        _ada_kernel,
        out_shape=jax.ShapeDtypeStruct((rows, n), F32),
        grid=(n // tn,),
        in_specs=[pl.BlockSpec((rows, d), lambda j: (0, 0)),
                  pl.BlockSpec((d, tn), lambda j: (0, j)),
                  pl.BlockSpec((1, tn), lambda j: (0, j))],
        out_specs=pl.BlockSpec((rows, tn), lambda j: (0, j)),
        compiler_params=_cparams(("arbitrary",)),
        name="ada",
    )(cs, w_ada, b_ada.reshape(1, n))


def _inproj_kernel(x_ref, mod_ref, wa_ref, wqk_ref, wv_ref, wo_ref, wg_ref,
                   a_ref, qk_ref, v_ref, o_ref, g_ref):
    xm = (x_ref[0] * (1.0 + mod_ref[0, 1:2, :]) + mod_ref[0, 0:1, :]).astype(BF16)
    a_ref[0] = _dot(xm, wa_ref[...]).astype(BF16)
    qk_ref[0] = _dot(xm, wqk_ref[...]).astype(BF16)
    v_ref[0] = _dot(xm, wv_ref[...]).astype(BF16)
    o_ref[0] = _dot(xm, wo_ref[...]).astype(BF16)
    g_ref[0] = _dot(xm, wg_ref[...])


def _inproj(x, mod, ws, tm):
    b, t, d = x.shape
    wa, wqk, wv, wo, wg = ws
    full = lambda w: pl.BlockSpec(w.shape, lambda i, j: (0, 0))
    row = lambda n: pl.BlockSpec((1, tm, n), lambda i, j: (i, j, 0))
    return pl.pallas_call(
        _inproj_kernel,
        out_shape=(jax.ShapeDtypeStruct((b, t, wa.shape[1]), BF16),
                   jax.ShapeDtypeStruct((b, t, wqk.shape[1]), BF16),
                   jax.ShapeDtypeStruct((b, t, wv.shape[1]), BF16),
                   jax.ShapeDtypeStruct((b, t, wo.shape[1]), BF16),
                   jax.ShapeDtypeStruct((b, t, wg.shape[1]), F32)),
        grid=(b, t // tm),
        in_specs=[row(d), pl.BlockSpec((1, 8, d), lambda i, j: (i, 0, 0)),
                  full(wa), full(wqk), full(wv), full(wo), full(wg)],
        out_specs=(row(wa.shape[1]), row(wqk.shape[1]), row(wv.shape[1]), row(wo.shape[1]),
                   row(wg.shape[1])),
        compiler_params=_cparams(("arbitrary", "arbitrary")),
        name="inproj",
    )(x, mod, wa, wqk, wv, wo, wg)


def _qkprep_kernel(x_ref, w_ref, cos_ref, sin_ref, o_ref, *, sub, halo):
    t = x_ref.shape[1]
    n_sub = t // sub
    lane = lax.broadcasted_iota(jnp.int32, (1, LANES), 1)
    first_half = (lane & 31) < 16
    scale = jnp.where(pl.program_id(1) < (ML_QK_W // LANES), ML_QK_DIM ** -0.5, 1.0).astype(F32)
    zeros = jnp.zeros((halo, LANES), x_ref.dtype)
    for s in range(n_sub):
        lo = s * sub
        top = x_ref[0, lo - halo:lo, :] if s > 0 else zeros
        bot = x_ref[0, lo + sub:lo + sub + halo, :] if s < n_sub - 1 else zeros
        ext = jnp.concatenate([top, x_ref[0, lo:lo + sub, :], bot], axis=0).astype(F32)
        n = sub + 2 * halo
        acc = jnp.zeros((sub, LANES), F32)
        for j in range(CONV_K):
            shift = (CONV_K // 2 - j) % n
            sh = ext if shift == 0 else pltpu.roll(ext, shift, axis=0)
            acc = acc + w_ref[j:j + 1, :] * sh[halo:halo + sub, :]
        y = acc * _sigmoid(acc)
        partner = jnp.where(first_half, pltpu.roll(y, LANES - 16, axis=1), pltpu.roll(y, 16, axis=1))
        y = y * cos_ref[lo:lo + sub, :] + partner * sin_ref[lo:lo + sub, :]
        o_ref[0, lo:lo + sub, :] = (y * scale).astype(o_ref.dtype)


def _qkprep(qk, conv_w, cos_t, sin_t):
    b, t, w = qk.shape
    sub = min(t, 1024)
    kern = functools.partial(_qkprep_kernel, sub=sub, halo=16)
    return pl.pallas_call(
        kern,
        out_shape=jax.ShapeDtypeStruct((b, t, w), BF16),
        grid=(b, w // LANES),
        in_specs=[pl.BlockSpec((1, t, LANES), lambda i, j: (i, 0, j)),
                  pl.BlockSpec((8, LANES), lambda i, j: (0, j)),
                  pl.BlockSpec((t, LANES), lambda i, j: (0, 0)),
                  pl.BlockSpec((t, LANES), lambda i, j: (0, 0))],
        out_specs=pl.BlockSpec((1, t, LANES), lambda i, j: (i, 0, j)),
        compiler_params=_cparams(("arbitrary", "arbitrary")),
        name="qkprep",
    )(qk, conv_w, cos_t, sin_t)


def _rope_tables(t):
    nf = ML_QK_DIM // 4
    inv = 1.0 / (ROPE_BASE ** (jnp.arange(nf, dtype=F32) / nf))
    pos = jnp.arange(t)
    ang_r = (pos // GRID_W).astype(F32)[:, None] * inv
    ang_c = (pos % GRID_W).astype(F32)[:, None] * inv
    cos = jnp.concatenate([jnp.cos(ang_r)] * 2 + [jnp.cos(ang_c)] * 2, axis=-1)
    sin = jnp.concatenate([-jnp.sin(ang_r), jnp.sin(ang_r), -jnp.sin(ang_c), jnp.sin(ang_c)], axis=-1)
    return jnp.tile(cos, (1, 2)), jnp.tile(sin, (1, 2))


def _mlstm_kernel(qf_ref, kf_ref, vf_ref, gf_ref, qb_ref, kb_ref, vb_ref, gb_ref, bg_ref,
                  c0_ref, m0_ref, hf_ref, hb_ref, cout_ref, mout_ref, c_scr, m_scr):
    step = pl.program_id(1)
    n_steps = pl.num_programs(1)
    L = ML_CHUNK

    @pl.when(step == 0)
    def _():
        c_scr[...] = c0_ref[0]
        m_scr[...] = m0_ref[0]

    row_i = lax.broadcasted_iota(jnp.int32, (L, L), 0)
    col_i = lax.broadcasted_iota(jnp.int32, (L, L), 1)
    lane = lax.broadcasted_iota(jnp.int32, (1, LANES), 1)
    tri_lo = (col_i <= row_i)
    tri_up = (col_i >= row_i)
    ones_v = jnp.ones((L, ML_V_DIM), BF16)

    for d, (q_ref, k_ref, v_ref, g_ref, h_ref) in enumerate(
            ((qf_ref, kf_ref, vf_ref, gf_ref, hf_ref), (qb_ref, kb_ref, vb_ref, gb_ref, hb_ref))):
        tri = tri_lo if d == 0 else tri_up
        g = g_ref[0] + bg_ref[...]
        logf = jnp.minimum(g, 0.0) - jnp.log(1.0 + jnp.exp(-jnp.abs(g)))
        cum = jnp.dot(tri.astype(F32), logf, precision=HIGHEST, preferred_element_type=F32)
        f_lo = 4 + 8 * d
        z = jnp.where((lane >= f_lo) & (lane < f_lo + ML_HEADS), cum, g)
        zt = z.T
        end = L - 1 if d == 0 else 0
        for h in range(ML_HEADS):
            li, lb = 8 * d + h, f_lo + h
            pair, half = h // 2, h % 2
            head_mask = (lane >= 64 * half) & (lane < 64 * half + 64)
            bcol = jnp.sum(jnp.where(lane == lb, z, 0.0), axis=-1, keepdims=True)
            icol = jnp.sum(jnp.where(lane == li, z, 0.0), axis=-1, keepdims=True)
            brow = zt[lb:lb + 1, :]
            irow = zt[li:li + 1, :]
            total = bcol[end:end + 1, :]
            m_prev = m_scr[d, h:h + 1, 0:1]

            qm = jnp.where(head_mask, q_ref[0, :, pair * LANES:(pair + 1) * LANES], 0).astype(BF16)
            km = jnp.where(head_mask, k_ref[0, :, pair * LANES:(pair + 1) * LANES], 0).astype(BF16)
            vext = jnp.concatenate([v_ref[0, :, h * ML_V_DIM:(h + 1) * ML_V_DIM], ones_v], axis=1)

            dmat = jnp.where(tri, bcol - brow + irow, NEG_BIG)
            m_prev_term = bcol + m_prev
            m_t = jnp.maximum(jnp.max(dmat, axis=-1, keepdims=True), m_prev_term)
            sp = _dot_nt(qm, km) * jnp.exp(dmat - m_t)
            inter = jnp.exp(m_prev_term - m_t)
            c_prev = c_scr[d, h]
            r = _dot(sp.astype(BF16), vext) + inter * _dot(qm, c_prev.astype(BF16))
            num = r[:, :ML_V_DIM]
            den = r[:, ML_V_DIM:]
            h_ref[0, :, h * ML_V_DIM:(h + 1) * ML_V_DIM] = num / jnp.maximum(jnp.abs(den), jnp.exp(-m_t))

            a = total - bcol + icol
            m_loc = jnp.max(a, axis=0, keepdims=True)
            kw = (km.astype(F32) * jnp.exp(a - m_loc)).astype(BF16)
            c_loc = _dot_tn(kw, vext)
            m_new = jnp.maximum(total + m_prev, m_loc)
            c_scr[d, h] = jnp.exp(total + m_prev - m_new) * c_prev + jnp.exp(m_loc - m_new) * c_loc
            m_scr[d, h:h + 1, :] = jnp.broadcast_to(m_new, (1, LANES))

    @pl.when(step == n_steps - 1)
    def _():
        cout_ref[0] = c_scr[...]
        mout_ref[0] = m_scr[...]


def _mlstm(qk, v, gates, bg, c0, m0):
    b, t, _ = qk.shape
    L = ML_CHUNK
    nc = t // L
    fwd = lambda n, blk: pl.BlockSpec((1, L, n), lambda i, c: (i, c, blk))
    bwd = lambda n, blk: pl.BlockSpec((1, L, n), lambda i, c: (i, nc - 1 - c, blk))
    st_c = pl.BlockSpec((1,) + c0.shape[1:], lambda i, c: (i, 0, 0, 0, 0))
    st_m = pl.BlockSpec((1,) + m0.shape[1:], lambda i, c: (i, 0, 0, 0))
    return pl.pallas_call(
        _mlstm_kernel,
        out_shape=(jax.ShapeDtypeStruct((b, t, ML_V_W), F32),
                   jax.ShapeDtypeStruct((b, t, ML_V_W), F32),
                   jax.ShapeDtypeStruct(c0.shape, F32),
                   jax.ShapeDtypeStruct(m0.shape, F32)),
        grid=(b, nc),
        in_specs=[fwd(ML_QK_W, 0), fwd(ML_QK_W, 1), fwd(ML_V_W, 0), fwd(LANES, 0),
                  bwd(ML_QK_W, 0), bwd(ML_QK_W, 1), bwd(ML_V_W, 0), bwd(LANES, 0),
                  pl.BlockSpec((1, LANES), lambda i, c: (0, 0)), st_c, st_m],
        out_specs=(fwd(ML_V_W, 0), bwd(ML_V_W, 0), st_c, st_m),
        scratch_shapes=[pltpu.VMEM(c0.shape[1:], F32), pltpu.VMEM(m0.shape[1:], F32)],
        compiler_params=_cparams(("arbitrary", "arbitrary")),
        name="mlstm",
    )(qk, qk, v, gates, qk, qk, v, gates, bg, c0, m0)


def _attn_kernel(case_ref, ws_ref, q_ref, k_ref, v_ref, kc_ref, vc_ref, bias_ref, o_ref):
    j = pl.program_id(2)
    nk = ATT_KROWS * GRID_W
    start = pl.multiple_of(ws_ref[j] * GRID_W, GRID_W)
    lane = lax.broadcasted_iota(jnp.int32, (1, LANES), 1)
    q = q_ref[0]
    k = k_ref[0, pl.ds(start, nk), :]
    v = v_ref[0, pl.ds(start, nk), :]
    kc = kc_ref[0]
    vc = vc_ref[0]
    acc = jnp.zeros(q.shape, F32)
    for h in range(2):
        head_mask = (lane >= NA_HEAD_DIM * h) & (lane < NA_HEAD_DIM * (h + 1))
        qh = jnp.where(head_mask, q, 0).astype(BF16) * jnp.asarray(NA_HEAD_DIM ** -0.5, BF16)
        s = _dot_nt(qh, k) + bias_ref[0, h]
        sc = _dot_nt(qh, kc)
        m = jnp.maximum(jnp.max(s, axis=-1, keepdims=True), jnp.max(sc, axis=-1, keepdims=True))
        p = jnp.exp(s - m)
        pc = jnp.exp(sc - m)
        l = jnp.sum(p, axis=-1, keepdims=True) + jnp.sum(pc, axis=-1, keepdims=True)
        vh = jnp.where(head_mask, v, 0).astype(BF16)
        vch = jnp.where(head_mask, vc, 0).astype(BF16)
        o = _dot(p.astype(BF16), vh) + _dot(pc.astype(BF16), vch)
        acc = acc + o / l
    o_ref[0] = acc.astype(o_ref.dtype)


def _attn_plan(rows):
    kh = min(NA_WIN_H, rows)
    nj = rows // ATT_ROWS
    rs = lambda r: int(np.clip(r - kh // 2, 0, rows - kh))
    ws = [int(np.clip(ATT_ROWS * j - kh // 2, 0, rows - ATT_KROWS)) for j in range(nj)]
    sigs, case = [], []
    for j in range(nj):
        r0 = ATT_ROWS * j
        sig = (ws[j] - r0,) + tuple(rs(r0 + a) - r0 for a in range(ATT_ROWS))
        if sig not in sigs:
            sigs.append(sig)
        case.append(sigs.index(sig))
    return np.asarray(ws, np.int32), np.asarray(case, np.int32), sigs, kh


def _attn_bias(bias_table, sigs, kh):
    col_start = np.clip(np.arange(GRID_W) - NA_WIN_W // 2, 0, GRID_W - NA_WIN_W)
    c = np.arange(GRID_W)
    cidx = c[None, :] - c[:, None] + (NA_WIN_W - 1)
    col_ok = (c[None, :] >= col_start[:, None]) & (c[None, :] < col_start[:, None] + NA_WIN_W)
    expand = (np.arange(2 * NA_WIN_W - 1)[:, None, None] == cidx[None]).astype(np.float32)
    out = []
    for sig in sigs:
        wsr, rsr = sig[0], np.asarray(sig[1:])
        a = np.arange(ATT_ROWS)[:, None]
        rk = wsr + np.arange(ATT_KROWS)[None, :]
        row_ok = (rk >= rsr[:, None]) & (rk < rsr[:, None] + kh)
        ridx = np.clip(rk - a + (NA_WIN_H - 1), 0, 2 * NA_WIN_H - 2)
        rows = bias_table[:, ridx, :]
        full = jnp.einsum('haiv,vqk->haqik', rows, jnp.asarray(expand), precision=HIGHEST)
        ok = row_ok[:, None, :, None] & col_ok[None, :, None, :]
        full = jnp.where(ok[None], full, NEG_BIG)
        out.append(full.reshape(full.shape[0], ATT_ROWS * GRID_W, ATT_KROWS * GRID_W))
    return jnp.stack(out).astype(F32)


def _attn(a_lat, a_ctx, bias_table):
    b, t, _ = a_lat.shape
    lc = a_ctx.shape[1]
    rows = t // GRID_W
    ws, case, sigs, kh = _attn_plan(rows)
    bias = _attn_bias(bias_table, sigs, kh)
    tq = ATT_ROWS * GRID_W
    nk = ATT_KROWS * GRID_W
    n_pairs = NA_W // LANES
    grid_spec = pltpu.PrefetchScalarGridSpec(
        num_scalar_prefetch=2,
        grid=(n_pairs, b, rows // ATT_ROWS),
        in_specs=[pl.BlockSpec((1, tq, LANES), lambda p, i, j, cs, w: (i, j, p)),
                  pl.BlockSpec((1, t, LANES), lambda p, i, j, cs, w: (i, 0, n_pairs + p)),
                  pl.BlockSpec((1, t, LANES), lambda p, i, j, cs, w: (i, 0, 2 * n_pairs + p)),
                  pl.BlockSpec((1, lc, LANES), lambda p, i, j, cs, w: (i, 0, n_pairs + p)),
                  pl.BlockSpec((1, lc, LANES), lambda p, i, j, cs, w: (i, 0, 2 * n_pairs + p)),
                  pl.BlockSpec((1, 2, tq, nk), lambda p, i, j, cs, w: (cs[j], p, 0, 0))],
        out_specs=pl.BlockSpec((1, tq, LANES), lambda p, i, j, cs, w: (i, j, p)),
    )
    return pl.pallas_call(
        _attn_kernel,
        out_shape=jax.ShapeDtypeStruct((b, t, NA_W), BF16),
        grid_spec=grid_spec,
        compiler_params=_cparams(("arbitrary", "arbitrary", "arbitrary")),
        name="nattn",
    )(jnp.asarray(case), jnp.asarray(ws), a_lat, a_lat, a_lat, a_ctx, a_ctx, bias)


def _outproj_kernel(att_ref, hf_ref, hb_ref, om_ref, x_ref, mod_ref, ng_ref, wa_ref, wm_ref,
                    lng_ref, lnb_ref, wrh_ref, wrl_ref, x1_ref, hm_ref, aff_ref, affr_ref):
    h = hf_ref[0] + hb_ref[0]
    parts = []
    for hd in range(ML_HEADS):
        hh = h[:, hd * ML_V_DIM:(hd + 1) * ML_V_DIM]
        mu = jnp.mean(hh, axis=-1, keepdims=True)
        var = jnp.mean(jnp.square(hh - mu), axis=-1, keepdims=True)
        parts.append((hh - mu) * lax.rsqrt(var + LN_EPS))
    hn = jnp.concatenate(parts, axis=1) * ng_ref[...]
    ml = (hn * _sigmoid(om_ref[0].astype(F32))).astype(BF16)
    mix = _dot(att_ref[0], wa_ref[...]) + _dot(ml, wm_ref[...])
    y = DEEPNORM_ALPHA * x_ref[0] + mod_ref[0, 2:3, :] * mix
    mu = jnp.mean(y, axis=-1, keepdims=True)
    var = jnp.mean(jnp.square(y - mu), axis=-1, keepdims=True)
    x1 = (y - mu) * lax.rsqrt(var + LN_EPS) * lng_ref[...] + lnb_ref[...]
    x1_ref[0] = x1
    hm = x1 * (1.0 + mod_ref[0, 4:5, :]) + mod_ref[0, 3:4, :]
    h_hi = hm.astype(BF16)
    h_lo = (hm - h_hi.astype(F32)).astype(BF16)
    hm_ref[0] = h_hi
    logits = _dot(h_hi, wrh_ref[...]) + (_dot(h_lo, wrh_ref[...]) + _dot(h_hi, wrl_ref[...]))
    lane = lax.broadcasted_iota(jnp.int32, (1, LANES), 1)
    logits = jnp.where(lane < N_EXPERTS, logits, NEG_BIG)
    e = jnp.exp(logits - jnp.max(logits, axis=-1, keepdims=True))
    aff = e / jnp.sum(e, axis=-1, keepdims=True)
    affr_ref[0] = aff
    aff_ref[0] = aff.T[:N_EXPERTS, :]


def _outproj(att, hf, hb, om, x, mod, ng, w_att, w_ml, lng, lnb, wr_hi, wr_lo, tm):
    b, t, d = x.shape
    row = lambda n: pl.BlockSpec((1, tm, n), lambda i, j: (i, j, 0))
    full = lambda w: pl.BlockSpec(w.shape, lambda i, j: (0,) * w.ndim)
    return pl.pallas_call(
        _outproj_kernel,
        out_shape=(jax.ShapeDtypeStruct((b, t, d), F32),
                   jax.ShapeDtypeStruct((b, t, d), BF16),
                   jax.ShapeDtypeStruct((b, N_EXPERTS, t), F32),
                   jax.ShapeDtypeStruct((b, t, LANES), F32)),
        grid=(b, t // tm),
        in_specs=[row(NA_W), row(ML_V_W), row(ML_V_W), row(ML_V_W), row(d),
                  pl.BlockSpec((1, 8, d), lambda i, j: (i, 0, 0)),
                  full(ng), full(w_att), full(w_ml), full(lng), full(lnb), full(wr_hi), full(wr_lo)],
        out_specs=(row(d), row(d), pl.BlockSpec((1, N_EXPERTS, tm), lambda i, j: (i, 0, j)), row(LANES)),
        compiler_params=_cparams(("arbitrary", "arbitrary")),
        name="outproj",
    )(att, hf, hb, om, x, mod, ng, w_att, w_ml, lng, lnb, wr_hi, wr_lo)


UNSELECTED = -1e6


def _select_kernel(aff_ref, idx_ref, pos_ref, off_ref, cum_scr, sel_scr, *, cap):
    t = aff_ref.shape[2]
    nb = t // LANES
    lane = lax.broadcasted_iota(jnp.int32, (1, LANES), 1)
    keys = lambda: pltpu.bitcast(aff_ref[0], jnp.int32)

    def count(mask):
        return jnp.sum(jnp.where(mask, 1.0, 0.0), axis=-1, keepdims=True)

    def search(_, c):
        lo, hi = c
        mid = lo + jnp.right_shift(hi - lo, 1)
        ge = count(keys() >= mid) >= cap
        return jnp.where(ge, mid, lo), jnp.where(ge, hi, mid)

    lo0 = jnp.zeros((N_EXPERTS, 1), jnp.int32)
    hi0 = jnp.full((N_EXPERTS, 1), 0x7F800000, jnp.int32)
    thr, _ = lax.fori_loop(0, 31, search, (lo0, hi0))

    r_i = lax.broadcasted_iota(jnp.int32, (LANES, LANES), 0)
    c_i = lax.broadcasted_iota(jnp.int32, (LANES, LANES), 1)
    strict = jnp.where(r_i < c_i, 1.0, 0.0).astype(BF16)
    tr_i = lax.broadcasted_iota(jnp.int32, (t, LANES), 0)
    tc_i = lax.broadcasted_iota(jnp.int32, (t, LANES), 1)
    block_ind = jnp.where(jnp.right_shift(tr_i, 7) == tc_i, 1.0, 0.0).astype(BF16)

    def prefix(x01):
        xb = x01.astype(BF16)
        offs = _dot(_dot(xb, block_ind).astype(BF16), strict)
        for j in range(nb):
            off_j = jnp.sum(jnp.where(lane == j, offs, 0.0), axis=-1, keepdims=True)
            cum_scr[:, j * LANES:(j + 1) * LANES] = _dot(xb[:, j * LANES:(j + 1) * LANES], strict) + off_j
        return offs

    k = keys()
    gt = k > thr
    eq = k == thr
    need = cap - count(gt)
    prefix(jnp.where(eq, 1.0, 0.0))
    sel = jnp.where(gt | (eq & (cum_scr[...] < need)), 1.0, 0.0)
    sel_scr[...] = sel
    offs = prefix(sel)

    pad = jnp.zeros((LANES - N_EXPERTS, LANES), F32)
    for j in range(nb):
        blk = jnp.where(sel_scr[:, j * LANES:(j + 1) * LANES] > 0.0, cum_scr[:, j * LANES:(j + 1) * LANES], UNSELECTED)
        pos_ref[0, j * LANES:(j + 1) * LANES, :] = jnp.concatenate([blk, pad], axis=0).T
    off_ref[0] = jnp.concatenate([offs, pad], axis=0).T

    cum_scr[...] = cum_scr[...] + sel_scr[...]
    idx_ref[...] = jnp.zeros(idx_ref.shape, idx_ref.dtype)
    sub = lax.broadcasted_iota(jnp.int32, (LANES, LANES), 0).astype(F32)
    for e in range(N_EXPERTS):
        def group(pg, carry, e=e):
            slots = jnp.asarray(pg * LANES).astype(F32) + sub

            def block(jb, acc):
                c = cum_scr[pl.ds(e, 1), pl.ds(pl.multiple_of(jb * LANES, LANES), LANES)]
                return acc + jnp.where(jnp.broadcast_to(c, (LANES, LANES)) <= slots, 1.0, 0.0)

            acc = lax.fori_loop(0, nb, block, jnp.zeros((LANES, LANES), F32))
            col = jnp.sum(acc, axis=-1, keepdims=True)
            idx_ref[0, pl.ds(pl.multiple_of(pg * LANES, LANES), LANES), e:e + 1] = col.astype(jnp.int32)
            return carry

        lax.fori_loop(0, cap // LANES, group, 0)


def _select(aff_t, cap):
    b, e, t = aff_t.shape
    assert cap % LANES == 0 and t % LANES == 0 and t // LANES <= LANES
    return pl.pallas_call(
        functools.partial(_select_kernel, cap=cap),
        out_shape=(jax.ShapeDtypeStruct((b, cap, LANES), jnp.int32),
                   jax.ShapeDtypeStruct((b, t, LANES), F32),
                   jax.ShapeDtypeStruct((b, LANES, LANES), F32)),
        grid=(b,),
        in_specs=[pl.BlockSpec((1, e, t), lambda i: (i, 0, 0))],
        out_specs=(pl.BlockSpec((1, cap, LANES), lambda i: (i, 0, 0)),
                   pl.BlockSpec((1, t, LANES), lambda i: (i, 0, 0)),
                   pl.BlockSpec((1, LANES, LANES), lambda i: (i, 0, 0))),
        scratch_shapes=[pltpu.VMEM((e, t), F32), pltpu.VMEM((e, t), F32)],
        compiler_params=_cparams(("arbitrary",)),
        name="select",
    )(aff_t)


def _moe_kernel(x_ref, wg_ref, wu_ref, wd_ref, o_ref, acc_ref):
    f = pl.program_id(2)
    x = x_ref[0]
    hg = _dot(x, wg_ref[0].astype(BF16))
    hu = _dot(x, wu_ref[0].astype(BF16))
    act = (hg * _sigmoid(hg) * hu).astype(BF16)
    y = _dot(act, wd_ref[0].astype(BF16))

    @pl.when(f == 0)
    def _():
        acc_ref[...] = y

    @pl.when(f > 0)
    def _():
        acc_ref[...] += y

    @pl.when(f == pl.num_programs(2) - 1)
    def _():
        o_ref[0] = acc_ref[...].astype(o_ref.dtype)


def _moe(xe, w_gate, w_up, w_down, tm, tf):
    e, m, d = xe.shape
    dff = w_gate.shape[2]
    return pl.pallas_call(
        _moe_kernel,
        out_shape=jax.ShapeDtypeStruct((e, m, d), BF16),
        grid=(e, m // tm, dff // tf),
        in_specs=[pl.BlockSpec((1, tm, d), lambda i, j, f: (i, j, 0)),
                  pl.BlockSpec((1, d, tf), lambda i, j, f: (i, 0, f)),
                  pl.BlockSpec((1, d, tf), lambda i, j, f: (i, 0, f)),
                  pl.BlockSpec((1, tf, d), lambda i, j, f: (i, f, 0))],
        out_specs=pl.BlockSpec((1, tm, d), lambda i, j, f: (i, j, 0)),
        scratch_shapes=[pltpu.VMEM((tm, d), F32)],
        compiler_params=_cparams(("arbitrary", "arbitrary", "arbitrary")),
        name="moe",
    )(xe, w_gate, w_up, w_down)


CMB_TM = 256
CMB_WIN = 64
ROW_ALIGN = 16


def _combine_kernel(p0_ref, x1_ref, pos_ref, aff_ref, mod_ref, lng_ref, lnb_ref, y_hbm, o_ref,
                    ybuf, yextra, acc_ref, sem, sem_x, *, cap):
    b = pl.program_id(0)
    j = pl.program_id(1)
    nt = pl.num_programs(1)
    g = b * nt + j
    n_steps = pl.num_programs(0) * nt
    slot = lax.rem(g, 2)
    m_rows = y_hbm.shape[1]
    win = CMB_WIN
    kdim = N_EXPERTS * win

    def window(bb, jj, r):
        base = (bb * (nt + 1) + jj) * N_EXPERTS
        los, rows = [], []
        for e in range(N_EXPERTS):
            lo = (p0_ref[base + e] // ROW_ALIGN) * ROW_ALIGN + r * win
            los.append(lo)
            rows.append(jnp.minimum(bb * cap + lo, m_rows - win))
        return los, rows

    def copies(rows, buf, sems):
        return [pltpu.make_async_copy(y_hbm.at[e, pl.ds(pl.multiple_of(rows[e], ROW_ALIGN), win), :],
                                      buf.at[pl.ds(e * win, win), :], sems.at[e])
                for e in range(N_EXPERTS)]

    los0, rows0 = window(b, j, 0)

    @pl.when(g == 0)
    def _():
        for c in copies(rows0, ybuf.at[0], sem.at[0]):
            c.start()

    @pl.when(g + 1 < n_steps)
    def _():
        wrap = j + 1 == nt
        _, rows_n = window(jnp.where(wrap, b + 1, b), jnp.where(wrap, 0, j + 1), 0)
        for c in copies(rows_n, ybuf.at[1 - slot], sem.at[1 - slot]):
            c.start()

    lane = lax.broadcasted_iota(jnp.int32, (1, LANES), 1)
    e_i = lax.broadcasted_iota(jnp.int32, (LANES, kdim), 0)
    k_i = lax.broadcasted_iota(jnp.int32, (LANES, kdim), 1)
    expand = jnp.where(k_i // win == e_i, 1.0, 0.0).astype(BF16)
    col_in_win = (lax.broadcasted_iota(jnp.int32, (1, kdim), 1) % win).astype(F32)
    aff_cols = _dot(aff_ref[0].astype(BF16), expand)

    def lane_vec(vals):
        v = jnp.zeros((1, LANES), F32)
        for e in range(N_EXPERTS):
            v = jnp.where(lane == e, jnp.asarray(vals[e]).astype(F32), v)
        return v

    def expand_rows(los, rows, buf):
        rel = pos_ref[0] - lane_vec(los)
        shift = lane_vec([b * cap + lo - row for lo, row in zip(los, rows)])
        tgt = jnp.where((rel >= 0.0) & (rel < float(win)), rel + shift, -1.0)
        tgt_cols = _dot(tgt.astype(BF16), expand)
        s = jnp.where(tgt_cols == col_in_win, aff_cols, 0.0).astype(BF16)
        return _dot(s, buf[...])

    for c in copies(rows0, ybuf.at[slot], sem.at[slot]):
        c.wait()
    acc_ref[...] = expand_rows(los0, rows0, ybuf.at[slot])

    base = (b * (nt + 1) + j) * N_EXPERTS
    rounds = jnp.int32(0)
    for e in range(N_EXPERTS):
        span = p0_ref[base + N_EXPERTS + e] - (p0_ref[base + e] // ROW_ALIGN) * ROW_ALIGN
        rounds = jnp.maximum(rounds, (span + win - 1) // win)

    def extra(r, carry):
        los, rows = window(b, j, r)
        cps = copies(rows, yextra, sem_x)
        for c in cps:
            c.start()
        for c in cps:
            c.wait()
        acc_ref[...] += expand_rows(los, rows, yextra)
        return carry

    lax.fori_loop(1, rounds, extra, 0)

    y = DEEPNORM_ALPHA * x1_ref[0] + mod_ref[0, 5:6, :] * acc_ref[...]
    mu = jnp.mean(y, axis=-1, keepdims=True)
    var = jnp.mean(jnp.square(y - mu), axis=-1, keepdims=True)
    o_ref[0] = (y - mu) * lax.rsqrt(var + LN_EPS) * lng_ref[...] + lnb_ref[...]


def _combine(p0tab, x1, pos_t, aff_r, mod, lng, lnb, y, cap):
    b, t, d = x1.shape
    tm = CMB_TM
    kdim = N_EXPERTS * CMB_WIN
    row = lambda n: pl.BlockSpec((1, tm, n), lambda i, j, p: (i, j, 0))
    vec = pl.BlockSpec((1, d), lambda i, j, p: (0, 0))
    grid_spec = pltpu.PrefetchScalarGridSpec(
        num_scalar_prefetch=1,
        grid=(b, t // tm),
        in_specs=[row(d), row(LANES), row(LANES), pl.BlockSpec((1, 8, d), lambda i, j, p: (i, 0, 0)), vec, vec,
                  pl.BlockSpec(memory_space=pl.ANY)],
        out_specs=row(d),
        scratch_shapes=[pltpu.VMEM((2, kdim, d), y.dtype), pltpu.VMEM((kdim, d), y.dtype),
                        pltpu.VMEM((tm, d), F32),
                        pltpu.SemaphoreType.DMA((2, N_EXPERTS)), pltpu.SemaphoreType.DMA((N_EXPERTS,))],
    )
    return pl.pallas_call(
        functools.partial(_combine_kernel, cap=cap),
        out_shape=jax.ShapeDtypeStruct((b, t, d), F32),
        grid_spec=grid_spec,
        compiler_params=_cparams(("arbitrary", "arbitrary")),
        name="combine",
    )(p0tab, x1, pos_t, aff_r, mod, lng, lnb, y)


def _split_w_in(w_in):
    a_end = 3 * NA_W
    qk_end = a_end + 2 * ML_QK_W
    v_end = qk_end + ML_V_W
    o_end = v_end + ML_V_W
    wg = jnp.pad(w_in[:, o_end:], ((0, 0), (0, LANES - ML_N_GATES)))
    return tuple(w.astype(BF16) for w in
                 (w_in[:, :a_end], w_in[:, a_end:qk_end], w_in[:, qk_end:v_end], w_in[:, v_end:o_end], wg))


def _layer(x, ctx, c, c_ctx, w_ada, b_ada, w_in, b_gate, conv_qk, na_rel_bias, ml_norm_g, w_out,
           ln1_g, ln1_b, w_router, w_expert_gate, w_expert_up, w_expert_down, ln2_g, ln2_b):
    b, t, d = x.shape
    lc = ctx.shape[1]

    cs = jnp.zeros((16, d), F32).at[:b].set(c).at[b].set(c_ctx)
    m = _ada(cs, w_ada, b_ada)
    mod = jnp.pad(m[:b].reshape(b, 6, d), ((0, 0), (0, 2), (0, 0)))
    mod_ctx = jnp.broadcast_to(jnp.pad(m[b].reshape(6, d), ((0, 2), (0, 0)))[None], (b, 8, d))

    ws = _split_w_in(w_in)
    a_lat, qk_lat, v_lat, o_lat, g_lat = _inproj(x, mod, ws, tm=512)
    a_ctx, qk_ctx, v_ctx, _, g_ctx = _inproj(ctx, mod_ctx, ws, tm=lc)

    conv_w = jnp.pad(conv_qk, ((0, 8 - CONV_K), (0, 0)))
    cos_t, sin_t = _rope_tables(t)
    qk_lat = _qkprep(qk_lat, conv_w, cos_t, sin_t)
    qk_ctx = _qkprep(qk_ctx, conv_w, jnp.ones((lc, LANES), F32), jnp.zeros((lc, LANES), F32))

    bg = jnp.pad(b_gate, (0, LANES - ML_N_GATES)).reshape(1, LANES)
    c0 = jnp.zeros((b, 2, ML_HEADS, LANES, 2 * ML_V_DIM), F32)
    m0 = jnp.zeros((b, 2, 8, LANES), F32)
    _, _, c1, m1 = _mlstm(qk_ctx, v_ctx, g_ctx, bg, c0, m0)
    hf, hb, _, _ = _mlstm(qk_lat, v_lat, g_lat, bg, c1, m1)

    att = _attn(a_lat, a_ctx, na_rel_bias)

    wr = jnp.pad(w_router, ((0, 0), (0, LANES - N_EXPERTS)))
    wr_hi = wr.astype(BF16)
    wr_lo = (wr - wr_hi.astype(F32)).astype(BF16)
    x1, hm, aff_t, aff_r = _outproj(att, hf, hb, o_lat, x, mod, ml_norm_g.reshape(1, -1),
                                    w_out[:NA_W].astype(BF16), w_out[NA_W:].astype(BF16),
                                    ln1_g.reshape(1, d), ln1_b.reshape(1, d), wr_hi, wr_lo, tm=256)

    cap = CAPACITY_FACTOR * t // N_EXPERTS
    idx_t, pos_t, off_t = _select(aff_t, cap)
    idx = jnp.swapaxes(idx_t[:, :, :N_EXPERTS], 1, 2)
    xe = jax.vmap(lambda hb_, ib: hb_[ib])(hm, idx)
    xe = jnp.swapaxes(xe, 0, 1).reshape(N_EXPERTS, b * cap, d)
    y = _moe(xe, w_expert_gate, w_expert_up, w_expert_down, tm=min(2048, b * cap), tf=256)

    nt = t // CMB_TM
    p0 = off_t[:, 0:nt * (CMB_TM // LANES):CMB_TM // LANES, :N_EXPERTS]
    p0 = jnp.concatenate([p0, jnp.full((b, 1, N_EXPERTS), cap, F32)], axis=1)
    p0tab = p0.astype(jnp.int32).reshape(-1)
    return _combine(p0tab, x1, pos_t, aff_r, mod, ln2_g.reshape(1, d), ln2_b.reshape(1, d), y, cap)


def kernel(x, c, ctx, c_ctx, w_ada, b_ada, w_in, b_gate, conv_qk, na_rel_bias, ml_norm_g, w_out,
           ln1_g, ln1_b, w_router, w_expert_gate, w_expert_up, w_expert_down, ln2_g, ln2_b):
    return _layer(x, ctx, c, c_ctx, w_ada[0], b_ada[0], w_in[0], b_gate[0], conv_qk[0], na_rel_bias[0],
                  ml_norm_g[0], w_out[0], ln1_g[0], ln1_b[0], w_router[0], w_expert_gate[0],
                  w_expert_up[0], w_expert_down[0], ln2_g[0], ln2_b[0])
```

```python
import functools

import numpy as np
import jax
import jax.numpy as jnp
from jax import lax
from jax.experimental import pallas as pl
from jax.experimental.pallas import tpu as pltpu

F32 = jnp.float32
BF16 = jnp.bfloat16
HIGHEST = lax.Precision.HIGHEST

GRID_W = 64
NA_HEADS = 8
NA_HEAD_DIM = 64
NA_WIN_H = 8
NA_WIN_W = 16
NA_W = NA_HEADS * NA_HEAD_DIM
ML_HEADS = 4
ML_QK_DIM = 64
ML_V_DIM = 128
ML_QK_W = ML_HEADS * ML_QK_DIM
ML_V_W = ML_HEADS * ML_V_DIM
ML_CHUNK = 256
ML_N_GATES = 4 * ML_HEADS
CONV_K = 5
ROPE_BASE = 10000.0
N_EXPERTS = 16
CAPACITY_FACTOR = 2
LN_EPS = 1e-5
DEPTH = 1
DEEPNORM_ALPHA = (2.0 * DEPTH) ** 0.25

LANES = 128
NEG_BIG = -1e30
LOG2E = 1.4426950408889634
VMEM_LIMIT = 56 * 1024 * 1024

ATT_ROWS = 4
ATT_PAIRS = 2
ATT_KROWS = ATT_ROWS + NA_WIN_H - 1


def _cparams(sem):
    return pltpu.CompilerParams(dimension_semantics=sem, vmem_limit_bytes=VMEM_LIMIT)


def _sigmoid(x):
    return 1.0 / (1.0 + jnp.exp(-x))


def _dot(a, b):
    return jnp.dot(a, b, preferred_element_type=F32)


def _dot_nt(a, b):
    return lax.dot_general(a, b, (((1,), (1,)), ((), ())), preferred_element_type=F32)


def _dot_tn(a, b):
    return lax.dot_general(a, b, (((0,), (0,)), ((), ())), preferred_element_type=F32)


def _ada_kernel(c_ref, w_ref, b_ref, o_ref):
    c = c_ref[...]
    s = c * _sigmoid(c)
    o_ref[...] = jnp.dot(s, w_ref[...], precision=HIGHEST, preferred_element_type=F32) + b_ref[...]


def _ada(cs, w_ada, b_ada):
    rows, d = cs.shape
    n = w_ada.shape[1]
    tn = 512
    return pl.pallas_call(
        _ada_kernel,
        out_shape=jax.ShapeDtypeStruct((rows, n), F32),
        grid=(n // tn,),
        in_specs=[pl.BlockSpec((rows, d), lambda j: (0, 0)),
                  pl.BlockSpec((d, tn), lambda j: (0, j)),
                  pl.BlockSpec((1, tn), lambda j: (0, j))],
        out_specs=pl.BlockSpec((rows, tn), lambda j: (0, j)),
        compiler_params=_cparams(("arbitrary",)),
        name="ada",
    )(cs, w_ada, b_ada.reshape(1, n))


def _inproj_kernel(x_ref, mod_ref, wa_ref, wqk_ref, wv_ref, wo_ref, wg_ref,
                   a_ref, qk_ref, v_ref, o_ref, g_ref):
    xm = (x_ref[0] * (1.0 + mod_ref[0, 1:2, :]) + mod_ref[0, 0:1, :]).astype(BF16)
    a_ref[0] = _dot(xm, wa_ref[...]).astype(BF16)
    qk_ref[0] = _dot(xm, wqk_ref[...]).astype(BF16)
    v_ref[0] = _dot(xm, wv_ref[...]).astype(BF16)
    o_ref[0] = _dot(xm, wo_ref[...]).astype(BF16)
    g_ref[0] = _dot(xm, wg_ref[...])


def _inproj(x, mod, ws, tm):
    b, t, d = x.shape
    wa, wqk, wv, wo, wg = ws
    full = lambda w: pl.BlockSpec(w.shape, lambda i, j: (0, 0))
    row = lambda n: pl.BlockSpec((1, tm, n), lambda i, j: (i, j, 0))
    return pl.pallas_call(
        _inproj_kernel,
        out_shape=(jax.ShapeDtypeStruct((b, t, wa.shape[1]), BF16),
                   jax.ShapeDtypeStruct((b, t, wqk.shape[1]), BF16),
                   jax.ShapeDtypeStruct((b, t, wv.shape[1]), BF16),
                   jax.ShapeDtypeStruct((b, t, wo.shape[1]), BF16),
                   jax.ShapeDtypeStruct((b, t, wg.shape[1]), F32)),
        grid=(b, t // tm),
        in_specs=[row(d), pl.BlockSpec((1, 8, d), lambda i, j: (i, 0, 0)),
                  full(wa), full(wqk), full(wv), full(wo), full(wg)],
        out_specs=(row(wa.shape[1]), row(wqk.shape[1]), row(wv.shape[1]), row(wo.shape[1]),
                   row(wg.shape[1])),
        compiler_params=_cparams(("arbitrary", "arbitrary")),
        name="inproj",
    )(x, mod, wa, wqk, wv, wo, wg)


def _qkprep_kernel(x_ref, w_ref, cos_ref, sin_ref, o_ref, *, sub, halo):
    t = x_ref.shape[1]
    n_sub = t // sub
    lane = lax.broadcasted_iota(jnp.int32, (1, LANES), 1)
    first_half = (lane & 31) < 16
    scale = jnp.where(pl.program_id(1) < (ML_QK_W // LANES), ML_QK_DIM ** -0.5, 1.0).astype(F32)
    zeros = jnp.zeros((halo, LANES), x_ref.dtype)
    for s in range(n_sub):
        lo = s * sub
        top = x_ref[0, lo - halo:lo, :] if s > 0 else zeros
        bot = x_ref[0, lo + sub:lo + sub + halo, :] if s < n_sub - 1 else zeros
        ext = jnp.concatenate([top, x_ref[0, lo:lo + sub, :], bot], axis=0).astype(F32)
        n = sub + 2 * halo
        acc = jnp.zeros((sub, LANES), F32)
        for j in range(CONV_K):
            shift = (CONV_K // 2 - j) % n
            sh = ext if shift == 0 else pltpu.roll(ext, shift, axis=0)
            acc = acc + w_ref[j:j + 1, :] * sh[halo:halo + sub, :]
        y = acc * _sigmoid(acc)
        partner = jnp.where(first_half, pltpu.roll(y, LANES - 16, axis=1), pltpu.roll(y, 16, axis=1))
        y = y * cos_ref[lo:lo + sub, :] + partner * sin_ref[lo:lo + sub, :]
        o_ref[0, lo:lo + sub, :] = (y * scale).astype(o_ref.dtype)


def _qkprep(qk, conv_w, cos_t, sin_t):
    b, t, w = qk.shape
    sub = min(t, 1024)
    kern = functools.partial(_qkprep_kernel, sub=sub, halo=16)
    return pl.pallas_call(
        kern,
        out_shape=jax.ShapeDtypeStruct((b, t, w), BF16),
        grid=(b, w // LANES),
        in_specs=[pl.BlockSpec((1, t, LANES), lambda i, j: (i, 0, j)),
                  pl.BlockSpec((8, LANES), lambda i, j: (0, j)),
                  pl.BlockSpec((t, LANES), lambda i, j: (0, 0)),
                  pl.BlockSpec((t, LANES), lambda i, j: (0, 0))],
        out_specs=pl.BlockSpec((1, t, LANES), lambda i, j: (i, 0, j)),
        compiler_params=_cparams(("arbitrary", "arbitrary")),
        name="qkprep",
    )(qk, conv_w, cos_t, sin_t)


def _rope_tables(t):
    nf = ML_QK_DIM // 4
    inv = 1.0 / (ROPE_BASE ** (jnp.arange(nf, dtype=F32) / nf))
    pos = jnp.arange(t)
    ang_r = (pos // GRID_W).astype(F32)[:, None] * inv
    ang_c = (pos % GRID_W).astype(F32)[:, None] * inv
    cos = jnp.concatenate([jnp.cos(ang_r)] * 2 + [jnp.cos(ang_c)] * 2, axis=-1)
    sin = jnp.concatenate([-jnp.sin(ang_r), jnp.sin(ang_r), -jnp.sin(ang_c), jnp.sin(ang_c)], axis=-1)
    return jnp.tile(cos, (1, 2)), jnp.tile(sin, (1, 2))


def _mlstm_kernel(qf_ref, kf_ref, vf_ref, gf_ref, qb_ref, kb_ref, vb_ref, gb_ref, bg_ref,
                  c0_ref, m0_ref, hf_ref, hb_ref, cout_ref, mout_ref, c_scr, m_scr):
    step = pl.program_id(1)
    n_steps = pl.num_programs(1)
    L = ML_CHUNK

    @pl.when(step == 0)
    def _():
        c_scr[...] = c0_ref[0]
        m_scr[...] = m0_ref[0]

    row_i = lax.broadcasted_iota(jnp.int32, (L, L), 0)
    col_i = lax.broadcasted_iota(jnp.int32, (L, L), 1)
    lane = lax.broadcasted_iota(jnp.int32, (1, LANES), 1)
    tri_lo = (col_i <= row_i)
    tri_up = (col_i >= row_i)
    ones_v = jnp.ones((L, ML_V_DIM), BF16)

    for d, (q_ref, k_ref, v_ref, g_ref, h_ref) in enumerate(
            ((qf_ref, kf_ref, vf_ref, gf_ref, hf_ref), (qb_ref, kb_ref, vb_ref, gb_ref, hb_ref))):
        tri = tri_lo if d == 0 else tri_up
        g = g_ref[0] + bg_ref[...]
        logf = jnp.minimum(g, 0.0) - jnp.log(1.0 + jnp.exp(-jnp.abs(g)))
        tri_b = jnp.where(tri, 1.0, 0.0).astype(BF16)
        l1 = logf.astype(BF16)
        r1 = logf - l1.astype(F32)
        l2 = r1.astype(BF16)
        l3 = (r1 - l2.astype(F32)).astype(BF16)
        cum = _dot(tri_b, l1) + (_dot(tri_b, l2) + _dot(tri_b, l3))
        f_lo = 4 + 8 * d
        z = jnp.where((lane >= f_lo) & (lane < f_lo + ML_HEADS), cum, g)
        zt = z.T
        end = L - 1 if d == 0 else 0
        for h in range(ML_HEADS):
            li, lb = 8 * d + h, f_lo + h
            pair, half = h // 2, h % 2
            head_mask = (lane >= 64 * half) & (lane < 64 * half + 64)
            bcol = jnp.sum(jnp.where(lane == lb, z, 0.0), axis=-1, keepdims=True)
            icol = jnp.sum(jnp.where(lane == li, z, 0.0), axis=-1, keepdims=True)
            brow = zt[lb:lb + 1, :]
            irow = zt[li:li + 1, :]
            total = bcol[end:end + 1, :]
            m_prev = m_scr[d, h:h + 1, 0:1]

            qm = jnp.where(head_mask, q_ref[0, :, pair * LANES:(pair + 1) * LANES], 0).astype(BF16)
            km = jnp.where(head_mask, k_ref[0, :, pair * LANES:(pair + 1) * LANES], 0).astype(BF16)
            vext = jnp.concatenate([v_ref[0, :, h * ML_V_DIM:(h + 1) * ML_V_DIM], ones_v], axis=1)

            dmat = jnp.where(tri, bcol - brow + irow, NEG_BIG)
            m_prev_term = bcol + m_prev
            m_t = jnp.maximum(jnp.max(dmat, axis=-1, keepdims=True), m_prev_term)
            sp = _dot_nt(qm, km) * jnp.exp(dmat - m_t)
            inter = jnp.exp(m_prev_term - m_t)
            c_prev = c_scr[d, h]
            r = _dot(sp.astype(BF16), vext) + inter * _dot(qm, c_prev.astype(BF16))
            num = r[:, :ML_V_DIM]
            den = r[:, ML_V_DIM:]
            h_ref[0, :, h * ML_V_DIM:(h + 1) * ML_V_DIM] = num / jnp.maximum(jnp.abs(den), jnp.exp(-m_t))

            a = total - bcol + icol
            m_loc = jnp.max(a, axis=0, keepdims=True)
            kw = (km.astype(F32) * jnp.exp(a - m_loc)).astype(BF16)
            c_loc = _dot_tn(kw, vext)
            m_new = jnp.maximum(total + m_prev, m_loc)
            c_scr[d, h] = jnp.exp(total + m_prev - m_new) * c_prev + jnp.exp(m_loc - m_new) * c_loc
            m_scr[d, h:h + 1, :] = jnp.broadcast_to(m_new, (1, LANES))

    @pl.when(step == n_steps - 1)
    def _():
        cout_ref[0] = c_scr[...]
        mout_ref[0] = m_scr[...]


def _mlstm(qk, v, gates, bg, c0, m0):
    b, t, _ = qk.shape
    L = ML_CHUNK
    nc = t // L
    fwd = lambda n, blk: pl.BlockSpec((1, L, n), lambda i, c: (i, c, blk))
    bwd = lambda n, blk: pl.BlockSpec((1, L, n), lambda i, c: (i, nc - 1 - c, blk))
    st_c = pl.BlockSpec((1,) + c0.shape[1:], lambda i, c: (i, 0, 0, 0, 0))
    st_m = pl.BlockSpec((1,) + m0.shape[1:], lambda i, c: (i, 0, 0, 0))
    return pl.pallas_call(
        _mlstm_kernel,
        out_shape=(jax.ShapeDtypeStruct((b, t, ML_V_W), F32),
                   jax.ShapeDtypeStruct((b, t, ML_V_W), F32),
                   jax.ShapeDtypeStruct(c0.shape, F32),
                   jax.ShapeDtypeStruct(m0.shape, F32)),
        grid=(b, nc),
        in_specs=[fwd(ML_QK_W, 0), fwd(ML_QK_W, 1), fwd(ML_V_W, 0), fwd(LANES, 0),
                  bwd(ML_QK_W, 0), bwd(ML_QK_W, 1), bwd(ML_V_W, 0), bwd(LANES, 0),
                  pl.BlockSpec((1, LANES), lambda i, c: (0, 0)), st_c, st_m],
        out_specs=(fwd(ML_V_W, 0), bwd(ML_V_W, 0), st_c, st_m),
        scratch_shapes=[pltpu.VMEM(c0.shape[1:], F32), pltpu.VMEM(m0.shape[1:], F32)],
        compiler_params=_cparams(("arbitrary", "arbitrary")),
        name="mlstm",
    )(qk, qk, v, gates, qk, qk, v, gates, bg, c0, m0)


def _attn_kernel(case_ref, ws_ref, q_ref, k_ref, v_ref, kc_ref, vc_ref, bias_ref, o_ref):
    j = pl.program_id(2)
    nk = ATT_KROWS * GRID_W
    start = pl.multiple_of(ws_ref[j] * GRID_W, GRID_W)
    lane = lax.broadcasted_iota(jnp.int32, (1, LANES), 1)
    for pp in range(ATT_PAIRS):
        lanes = slice(pp * LANES, (pp + 1) * LANES)
        q = q_ref[0, :, lanes]
        k = k_ref[0, pl.ds(start, nk), lanes]
        v = v_ref[0, pl.ds(start, nk), lanes]
        kc = kc_ref[0, :, lanes]
        vc = vc_ref[0, :, lanes]
        acc = jnp.zeros(q.shape, F32)
        for h in range(2):
            head_mask = (lane >= NA_HEAD_DIM * h) & (lane < NA_HEAD_DIM * (h + 1))
            qh = (jnp.where(head_mask, q, 0).astype(F32) * (NA_HEAD_DIM ** -0.5 * LOG2E)).astype(BF16)
            s = _dot_nt(qh, k) + bias_ref[0, 2 * pp + h]
            sc = _dot_nt(qh, kc)
            m = jnp.maximum(jnp.max(s, axis=-1, keepdims=True), jnp.max(sc, axis=-1, keepdims=True))
            p = jnp.exp2(s - m)
            pc = jnp.exp2(sc - m)
            vh = jnp.where(head_mask, v, 1).astype(BF16)
            vch = jnp.where(head_mask, vc, 1).astype(BF16)
            o = _dot(p.astype(BF16), vh) + _dot(pc.astype(BF16), vch)
            acc = acc + jnp.where(head_mask, o / pltpu.roll(o, NA_HEAD_DIM, axis=1), 0.0)
        o_ref[0, :, lanes] = acc.astype(o_ref.dtype)


def _attn_plan(rows):
    kh = min(NA_WIN_H, rows)
    nj = rows // ATT_ROWS
    rs = lambda r: int(np.clip(r - kh // 2, 0, rows - kh))
    ws = [int(np.clip(ATT_ROWS * j - kh // 2, 0, rows - ATT_KROWS)) for j in range(nj)]
    sigs, case = [], []
    for j in range(nj):
        r0 = ATT_ROWS * j
        sig = (ws[j] - r0,) + tuple(rs(r0 + a) - r0 for a in range(ATT_ROWS))
        if sig not in sigs:
            sigs.append(sig)
        case.append(sigs.index(sig))
    return np.asarray(ws, np.int32), np.asarray(case, np.int32), sigs, kh


def _attn_bias(bias_table, sigs, kh):
    col_start = np.clip(np.arange(GRID_W) - NA_WIN_W // 2, 0, GRID_W - NA_WIN_W)
    c = np.arange(GRID_W)
    cidx = c[None, :] - c[:, None] + (NA_WIN_W - 1)
    col_ok = (c[None, :] >= col_start[:, None]) & (c[None, :] < col_start[:, None] + NA_WIN_W)
    expand = (np.arange(2 * NA_WIN_W - 1)[:, None, None] == cidx[None]).astype(np.float32)
    out = []
    for sig in sigs:
        wsr, rsr = sig[0], np.asarray(sig[1:])
        a = np.arange(ATT_ROWS)[:, None]
        rk = wsr + np.arange(ATT_KROWS)[None, :]
        row_ok = (rk >= rsr[:, None]) & (rk < rsr[:, None] + kh)
        ridx = np.clip(rk - a + (NA_WIN_H - 1), 0, 2 * NA_WIN_H - 2)
        rows = bias_table[:, ridx, :]
        full = jnp.einsum('haiv,vqk->haqik', rows, jnp.asarray(expand), precision=HIGHEST)
        ok = row_ok[:, None, :, None] & col_ok[None, :, None, :]
        full = jnp.where(ok[None], full * LOG2E, NEG_BIG)
        out.append(full.reshape(full.shape[0], ATT_ROWS * GRID_W, ATT_KROWS * GRID_W))
    return jnp.stack(out).astype(F32)


def _attn(a_lat, a_ctx, bias_table):
    b, t, _ = a_lat.shape
    lc = a_ctx.shape[1]
    rows = t // GRID_W
    ws, case, sigs, kh = _attn_plan(rows)
    bias = _attn_bias(bias_table, sigs, kh)
    tq = ATT_ROWS * GRID_W
    nk = ATT_KROWS * GRID_W
    bw = ATT_PAIRS * LANES
    n_blk = NA_W // bw
    grid_spec = pltpu.PrefetchScalarGridSpec(
        num_scalar_prefetch=2,
        grid=(n_blk, b, rows // ATT_ROWS),
        in_specs=[pl.BlockSpec((1, tq, bw), lambda p, i, j, cs, w: (i, j, p)),
                  pl.BlockSpec((1, t, bw), lambda p, i, j, cs, w: (i, 0, n_blk + p)),
                  pl.BlockSpec((1, t, bw), lambda p, i, j, cs, w: (i, 0, 2 * n_blk + p)),
                  pl.BlockSpec((1, lc, bw), lambda p, i, j, cs, w: (i, 0, n_blk + p)),
                  pl.BlockSpec((1, lc, bw), lambda p, i, j, cs, w: (i, 0, 2 * n_blk + p)),
                  pl.BlockSpec((1, 2 * ATT_PAIRS, tq, nk), lambda p, i, j, cs, w: (cs[j], p, 0, 0))],
        out_specs=pl.BlockSpec((1, tq, bw), lambda p, i, j, cs, w: (i, j, p)),
    )
    return pl.pallas_call(
        _attn_kernel,
        out_shape=jax.ShapeDtypeStruct((b, t, NA_W), BF16),
        grid_spec=grid_spec,
        compiler_params=_cparams(("arbitrary", "arbitrary", "arbitrary")),
        name="nattn",
    )(jnp.asarray(case), jnp.asarray(ws), a_lat, a_lat, a_lat, a_ctx, a_ctx, bias)


def _outproj_kernel(att_ref, hf_ref, hb_ref, om_ref, x_ref, mod_ref, ng_ref, wa_ref, wm_ref,
                    lng_ref, lnb_ref, wrh_ref, wrl_ref, x1_ref, hm_ref, aff_ref, affr_ref):
    h = hf_ref[0] + hb_ref[0]
    parts = []
    for hd in range(ML_HEADS):
        hh = h[:, hd * ML_V_DIM:(hd + 1) * ML_V_DIM]
        mu = jnp.mean(hh, axis=-1, keepdims=True)
        var = jnp.mean(jnp.square(hh - mu), axis=-1, keepdims=True)
        parts.append((hh - mu) * lax.rsqrt(var + LN_EPS))
    hn = jnp.concatenate(parts, axis=1) * ng_ref[...]
    ml = (hn * _sigmoid(om_ref[0].astype(F32))).astype(BF16)
    mix = _dot(att_ref[0], wa_ref[...]) + _dot(ml, wm_ref[...])
    y = DEEPNORM_ALPHA * x_ref[0] + mod_ref[0, 2:3, :] * mix
    mu = jnp.mean(y, axis=-1, keepdims=True)
    var = jnp.mean(jnp.square(y - mu), axis=-1, keepdims=True)
    x1 = (y - mu) * lax.rsqrt(var + LN_EPS) * lng_ref[...] + lnb_ref[...]
    x1_ref[0] = x1
    hm = x1 * (1.0 + mod_ref[0, 4:5, :]) + mod_ref[0, 3:4, :]
    h_hi = hm.astype(BF16)
    h_lo = (hm - h_hi.astype(F32)).astype(BF16)
    hm_ref[0] = h_hi
    logits = _dot(h_hi, wrh_ref[...]) + (_dot(h_lo, wrh_ref[...]) + _dot(h_hi, wrl_ref[...]))
    lane = lax.broadcasted_iota(jnp.int32, (1, LANES), 1)
    logits = jnp.where(lane < N_EXPERTS, logits, NEG_BIG)
    e = jnp.exp(logits - jnp.max(logits, axis=-1, keepdims=True))
    aff = e / jnp.sum(e, axis=-1, keepdims=True)
    affr_ref[0] = aff
    aff_ref[0] = aff.T[:N_EXPERTS, :]


def _outproj(att, hf, hb, om, x, mod, ng, w_att, w_ml, lng, lnb, wr_hi, wr_lo, tm):
    b, t, d = x.shape
    row = lambda n: pl.BlockSpec((1, tm, n), lambda i, j: (i, j, 0))
    full = lambda w: pl.BlockSpec(w.shape, lambda i, j: (0,) * w.ndim)
    return pl.pallas_call(
        _outproj_kernel,
        out_shape=(jax.ShapeDtypeStruct((b, t, d), F32),
                   jax.ShapeDtypeStruct((b, t, d), BF16),
                   jax.ShapeDtypeStruct((b, N_EXPERTS, t), F32),
                   jax.ShapeDtypeStruct((b, t, LANES), F32)),
        grid=(b, t // tm),
        in_specs=[row(NA_W), row(ML_V_W), row(ML_V_W), row(ML_V_W), row(d),
                  pl.BlockSpec((1, 8, d), lambda i, j: (i, 0, 0)),
                  full(ng), full(w_att), full(w_ml), full(lng), full(lnb), full(wr_hi), full(wr_lo)],
        out_specs=(row(d), row(d), pl.BlockSpec((1, N_EXPERTS, tm), lambda i, j: (i, 0, j)), row(LANES)),
        compiler_params=_cparams(("arbitrary", "arbitrary")),
        name="outproj",
    )(att, hf, hb, om, x, mod, ng, w_att, w_ml, lng, lnb, wr_hi, wr_lo)


UNSELECTED = -1e6


def _select_kernel(aff_ref, idx_ref, pos_ref, off_ref, cum_scr, sel_scr, offs_v, offs_s, dsem, *, cap):
    t = aff_ref.shape[2]
    nb = t // LANES
    lane = lax.broadcasted_iota(jnp.int32, (1, LANES), 1)
    keys = lambda: pltpu.bitcast(aff_ref[0], jnp.int32)

    def count(mask):
        return jnp.sum(jnp.where(mask, 1.0, 0.0), axis=-1, keepdims=True)

    def search(_, c):
        lo, hi = c
        mid = lo + jnp.right_shift(hi - lo, 1)
        ge = count(keys() >= mid) >= cap
        return jnp.where(ge, mid, lo), jnp.where(ge, hi, mid)

    lo0 = jnp.zeros((N_EXPERTS, 1), jnp.int32)
    hi0 = jnp.full((N_EXPERTS, 1), 0x7F800000, jnp.int32)
    thr, _ = lax.fori_loop(0, 31, search, (lo0, hi0))

    r_i = lax.broadcasted_iota(jnp.int32, (LANES, LANES), 0)
    c_i = lax.broadcasted_iota(jnp.int32, (LANES, LANES), 1)
    strict = jnp.where(r_i < c_i, 1.0, 0.0).astype(BF16)
    tr_i = lax.broadcasted_iota(jnp.int32, (t, LANES), 0)
    tc_i = lax.broadcasted_iota(jnp.int32, (t, LANES), 1)
    block_ind = jnp.where(jnp.right_shift(tr_i, 7) == tc_i, 1.0, 0.0).astype(BF16)

    def prefix(x01):
        xb = x01.astype(BF16)
        offs = _dot(_dot(xb, block_ind).astype(BF16), strict)
        for j in range(nb):
            off_j = jnp.sum(jnp.where(lane == j, offs, 0.0), axis=-1, keepdims=True)
            cum_scr[:, j * LANES:(j + 1) * LANES] = _dot(xb[:, j * LANES:(j + 1) * LANES], strict) + off_j
        return offs

    k = keys()
    gt = k > thr
    eq = k == thr
    need = cap - count(gt)
    prefix(jnp.where(eq, 1.0, 0.0))
    sel = jnp.where(gt | (eq & (cum_scr[...] < need)), 1.0, 0.0)
    sel_scr[...] = sel
    offs = prefix(sel)

    pad = jnp.zeros((LANES - N_EXPERTS, LANES), F32)
    for j in range(nb):
        blk = jnp.where(sel_scr[:, j * LANES:(j + 1) * LANES] > 0.0, cum_scr[:, j * LANES:(j + 1) * LANES], UNSELECTED)
        pos_ref[0, j * LANES:(j + 1) * LANES, :] = jnp.concatenate([blk, pad], axis=0).T
    off_ref[0] = jnp.concatenate([offs, pad], axis=0).T

    cum_scr[...] = cum_scr[...] + sel_scr[...]
    idx_ref[...] = jnp.zeros(idx_ref.shape, idx_ref.dtype)
    offs_v[...] = offs.astype(jnp.int32)
    to_smem = pltpu.make_async_copy(offs_v, offs_s, dsem)
    to_smem.start()
    to_smem.wait()
    sub = lax.broadcasted_iota(jnp.int32, (LANES, LANES), 0).astype(F32)
    for e in range(N_EXPERTS):
        def group(pg, carry, e=e):
            first = jnp.asarray(pg * LANES, jnp.int32)
            jlo, jhi = carry

            def advance(j0, bound, ahead):
                look = lambda j: offs_s[e, jnp.minimum(j + ahead, LANES - 1)]
                j, _ = lax.while_loop(lambda c: (c[0] < nb) & (c[1] <= bound),
                                      lambda c: (c[0] + 1, look(c[0] + 1)), (j0, look(j0)))
                return j

            jlo = advance(jlo, first, 1)
            jhi = advance(jhi, first + (LANES - 1), 0)
            slots = first.astype(F32) + sub

            def block(jb, acc):
                c = cum_scr[pl.ds(e, 1), pl.ds(pl.multiple_of(jb * LANES, LANES), LANES)]
                return acc + jnp.where(jnp.broadcast_to(c, (LANES, LANES)) <= slots, 1.0, 0.0)

            acc = lax.fori_loop(jlo, jhi, block, jnp.zeros((LANES, LANES), F32))
            col = jnp.sum(acc, axis=-1, keepdims=True).astype(jnp.int32) + jlo * LANES
            idx_ref[0, pl.ds(pl.multiple_of(pg * LANES, LANES), LANES), e:e + 1] = col
            return jlo, jhi

        lax.fori_loop(0, cap // LANES, group, (jnp.int32(0), jnp.int32(0)))


def _select(aff_t, cap):
    b, e, t = aff_t.shape
    assert cap % LANES == 0 and t % LANES == 0 and t // LANES < LANES
    return pl.pallas_call(
        functools.partial(_select_kernel, cap=cap),
        out_shape=(jax.ShapeDtypeStruct((b, cap, LANES), jnp.int32),
                   jax.ShapeDtypeStruct((b, t, LANES), F32),
                   jax.ShapeDtypeStruct((b, LANES, LANES), F32)),
        grid=(b,),
        in_specs=[pl.BlockSpec((1, e, t), lambda i: (i, 0, 0))],
        out_specs=(pl.BlockSpec((1, cap, LANES), lambda i: (i, 0, 0)),
                   pl.BlockSpec((1, t, LANES), lambda i: (i, 0, 0)),
                   pl.BlockSpec((1, LANES, LANES), lambda i: (i, 0, 0))),
        scratch_shapes=[pltpu.VMEM((e, t), F32), pltpu.VMEM((e, t), F32),
                        pltpu.VMEM((e, LANES), jnp.int32), pltpu.SMEM((e, LANES), jnp.int32),
                        pltpu.SemaphoreType.DMA(())],
        compiler_params=_cparams(("arbitrary",)),
        name="select",
    )(aff_t)


def _moe_kernel(x_ref, wg_ref, wu_ref, wd_ref, o_ref, acc_ref):
    f = pl.program_id(2)
    x = x_ref[0]
    hg = _dot(x, wg_ref[0].astype(BF16))
    hu = _dot(x, wu_ref[0].astype(BF16))
    act = (hg * _sigmoid(hg) * hu).astype(BF16)

    @pl.when(f == 0)
    def _():
        acc_ref[...] = jnp.zeros(acc_ref.shape, F32)

    acc_ref[...] += _dot(act, wd_ref[0].astype(BF16))

    @pl.when(f == pl.num_programs(2) - 1)
    def _():
        o_ref[0] = acc_ref[...].astype(o_ref.dtype)


def _moe(xe, w_gate, w_up, w_down, tm, tf):
    e, m, d = xe.shape
    dff = w_gate.shape[2]
    return pl.pallas_call(
        _moe_kernel,
        out_shape=jax.ShapeDtypeStruct((e, m, d), BF16),
        grid=(e, m // tm, dff // tf),
        in_specs=[pl.BlockSpec((1, tm, d), lambda i, j, f: (i, j, 0)),
                  pl.BlockSpec((1, d, tf), lambda i, j, f: (i, 0, f)),
                  pl.BlockSpec((1, d, tf), lambda i, j, f: (i, 0, f)),
                  pl.BlockSpec((1, tf, d), lambda i, j, f: (i, f, 0))],
        out_specs=pl.BlockSpec((1, tm, d), lambda i, j, f: (i, j, 0)),
        scratch_shapes=[pltpu.VMEM((tm, d), F32)],
        compiler_params=_cparams(("arbitrary", "arbitrary", "arbitrary")),
        name="moe",
    )(xe, w_gate, w_up, w_down)


CMB_TM = 256
CMB_WIN = 64
ROW_ALIGN = 16


def _combine_kernel(p0_ref, x1_ref, pos_ref, aff_ref, mod_ref, lng_ref, lnb_ref, y_hbm, o_ref,
                    ybuf, yextra, acc_ref, sem, sem_x, *, cap):
    b = pl.program_id(0)
    j = pl.program_id(1)
    nt = pl.num_programs(1)
    g = b * nt + j
    n_steps = pl.num_programs(0) * nt
    slot = lax.rem(g, 2)
    m_rows = y_hbm.shape[1]
    win = CMB_WIN
    kdim = N_EXPERTS * win

    def window(bb, jj, r):
        base = (bb * (nt + 1) + jj) * N_EXPERTS
        los, rows = [], []
        for e in range(N_EXPERTS):
            lo = (p0_ref[base + e] // ROW_ALIGN) * ROW_ALIGN + r * win
            los.append(lo)
            rows.append(jnp.minimum(bb * cap + lo, m_rows - win))
        return los, rows

    def copies(rows, buf, sems):
        return [pltpu.make_async_copy(y_hbm.at[e, pl.ds(pl.multiple_of(rows[e], ROW_ALIGN), win), :],
                                      buf.at[pl.ds(e * win, win), :], sems.at[e])
                for e in range(N_EXPERTS)]

    los0, rows0 = window(b, j, 0)

    @pl.when(g == 0)
    def _():
        for c in copies(rows0, ybuf.at[0], sem.at[0]):
            c.start()

    @pl.when(g + 1 < n_steps)
    def _():
        wrap = j + 1 == nt
        _, rows_n = window(jnp.where(wrap, b + 1, b), jnp.where(wrap, 0, j + 1), 0)
        for c in copies(rows_n, ybuf.at[1 - slot], sem.at[1 - slot]):
            c.start()

    lane = lax.broadcasted_iota(jnp.int32, (1, LANES), 1)
    e_i = lax.broadcasted_iota(jnp.int32, (LANES, kdim), 0)
    k_i = lax.broadcasted_iota(jnp.int32, (LANES, kdim), 1)
    expand = jnp.where(k_i // win == e_i, 1.0, 0.0).astype(BF16)
    col_in_win = (lax.broadcasted_iota(jnp.int32, (1, kdim), 1) % win).astype(F32)
    aff_cols = _dot(aff_ref[0].astype(BF16), expand)

    def lane_vec(vals):
        v = jnp.zeros((1, LANES), F32)
        for e in range(N_EXPERTS):
            v = jnp.where(lane == e, jnp.asarray(vals[e]).astype(F32), v)
        return v

    def expand_rows(los, rows, buf):
        rel = pos_ref[0] - lane_vec(los)
        shift = lane_vec([b * cap + lo - row for lo, row in zip(los, rows)])
        tgt = jnp.where((rel >= 0.0) & (rel < float(win)), rel + shift, -1.0)
        tgt_cols = _dot(tgt.astype(BF16), expand)
        s = jnp.where(tgt_cols == col_in_win, aff_cols, 0.0).astype(BF16)
        return _dot(s, buf[...])

    for c in copies(rows0, ybuf.at[slot], sem.at[slot]):
        c.wait()
    acc_ref[...] = expand_rows(los0, rows0, ybuf.at[slot])

    base = (b * (nt + 1) + j) * N_EXPERTS
    rounds = jnp.int32(0)
    for e in range(N_EXPERTS):
        span = p0_ref[base + N_EXPERTS + e] - (p0_ref[base + e] // ROW_ALIGN) * ROW_ALIGN
        rounds = jnp.maximum(rounds, (span + win - 1) // win)

    def extra(r, carry):
        los, rows = window(b, j, r)
        cps = copies(rows, yextra, sem_x)
        for c in cps:
            c.start()
        for c in cps:
            c.wait()
        acc_ref[...] += expand_rows(los, rows, yextra)
        return carry

    lax.fori_loop(1, rounds, extra, 0)

    y = DEEPNORM_ALPHA * x1_ref[0] + mod_ref[0, 5:6, :] * acc_ref[...]
    mu = jnp.mean(y, axis=-1, keepdims=True)
    var = jnp.mean(jnp.square(y - mu), axis=-1, keepdims=True)
    o_ref[0] = (y - mu) * lax.rsqrt(var + LN_EPS) * lng_ref[...] + lnb_ref[...]


def _combine(p0tab, x1, pos_t, aff_r, mod, lng, lnb, y, cap):
    b, t, d = x1.shape
    tm = CMB_TM
    kdim = N_EXPERTS * CMB_WIN
    row = lambda n: pl.BlockSpec((1, tm, n), lambda i, j, p: (i, j, 0))
    vec = pl.BlockSpec((1, d), lambda i, j, p: (0, 0))
    grid_spec = pltpu.PrefetchScalarGridSpec(
        num_scalar_prefetch=1,
        grid=(b, t // tm),
        in_specs=[row(d), row(LANES), row(LANES), pl.BlockSpec((1, 8, d), lambda i, j, p: (i, 0, 0)), vec, vec,
                  pl.BlockSpec(memory_space=pl.ANY)],
        out_specs=row(d),
        scratch_shapes=[pltpu.VMEM((2, kdim, d), y.dtype), pltpu.VMEM((kdim, d), y.dtype),
                        pltpu.VMEM((tm, d), F32),
                        pltpu.SemaphoreType.DMA((2, N_EXPERTS)), pltpu.SemaphoreType.DMA((N_EXPERTS,))],
    )
    return pl.pallas_call(
        functools.partial(_combine_kernel, cap=cap),
        out_shape=jax.ShapeDtypeStruct((b, t, d), F32),
        grid_spec=grid_spec,
        compiler_params=_cparams(("arbitrary", "arbitrary")),
        name="combine",
    )(p0tab, x1, pos_t, aff_r, mod, lng, lnb, y)


def _split_w_in(w_in):
    a_end = 3 * NA_W
    qk_end = a_end + 2 * ML_QK_W
    v_end = qk_end + ML_V_W
    o_end = v_end + ML_V_W
    wg = jnp.pad(w_in[:, o_end:], ((0, 0), (0, LANES - ML_N_GATES)))
    return tuple(w.astype(BF16) for w in
                 (w_in[:, :a_end], w_in[:, a_end:qk_end], w_in[:, qk_end:v_end], w_in[:, v_end:o_end], wg))


def _layer(x, ctx, c, c_ctx, w_ada, b_ada, w_in, b_gate, conv_qk, na_rel_bias, ml_norm_g, w_out,
           ln1_g, ln1_b, w_router, w_expert_gate, w_expert_up, w_expert_down, ln2_g, ln2_b):
    b, t, d = x.shape
    lc = ctx.shape[1]

    cs = jnp.zeros((16, d), F32).at[:b].set(c).at[b].set(c_ctx)
    m = _ada(cs, w_ada, b_ada)
    mod = jnp.pad(m[:b].reshape(b, 6, d), ((0, 0), (0, 2), (0, 0)))
    mod_ctx = jnp.broadcast_to(jnp.pad(m[b].reshape(6, d), ((0, 2), (0, 0)))[None], (b, 8, d))

    ws = _split_w_in(w_in)
    a_lat, qk_lat, v_lat, o_lat, g_lat = _inproj(x, mod, ws, tm=512)
    a_ctx, qk_ctx, v_ctx, _, g_ctx = _inproj(ctx, mod_ctx, ws, tm=lc)

    conv_w = jnp.pad(conv_qk, ((0, 8 - CONV_K), (0, 0)))
    cos_t, sin_t = _rope_tables(t)
    qk_lat = _qkprep(qk_lat, conv_w, cos_t, sin_t)
    qk_ctx = _qkprep(qk_ctx, conv_w, jnp.ones((lc, LANES), F32), jnp.zeros((lc, LANES), F32))

    bg = jnp.pad(b_gate, (0, LANES - ML_N_GATES)).reshape(1, LANES)
    c0 = jnp.zeros((b, 2, ML_HEADS, LANES, 2 * ML_V_DIM), F32)
    m0 = jnp.zeros((b, 2, 8, LANES), F32)
    _, _, c1, m1 = _mlstm(qk_ctx, v_ctx, g_ctx, bg, c0, m0)
    hf, hb, _, _ = _mlstm(qk_lat, v_lat, g_lat, bg, c1, m1)

    att = _attn(a_lat, a_ctx, na_rel_bias)

    wr = jnp.pad(w_router, ((0, 0), (0, LANES - N_EXPERTS)))
    wr_hi = wr.astype(BF16)
    wr_lo = (wr - wr_hi.astype(F32)).astype(BF16)
    x1, hm, aff_t, aff_r = _outproj(att, hf, hb, o_lat, x, mod, ml_norm_g.reshape(1, -1),
                                    w_out[:NA_W].astype(BF16), w_out[NA_W:].astype(BF16),
                                    ln1_g.reshape(1, d), ln1_b.reshape(1, d), wr_hi, wr_lo, tm=256)

    cap = CAPACITY_FACTOR * t // N_EXPERTS
    idx_t, pos_t, off_t = _select(aff_t, cap)
    idx = jnp.swapaxes(idx_t[:, :, :N_EXPERTS], 1, 2)
    xe = jax.vmap(lambda hb_, ib: hb_[ib])(hm, idx)
    xe = jnp.swapaxes(xe, 0, 1).reshape(N_EXPERTS, b * cap, d)
    y = _moe(xe, w_expert_gate, w_expert_up, w_expert_down, tm=min(2048, b * cap), tf=256)

    nt = t // CMB_TM
    p0 = off_t[:, 0:nt * (CMB_TM // LANES):CMB_TM // LANES, :N_EXPERTS]
    p0 = jnp.concatenate([p0, jnp.full((b, 1, N_EXPERTS), cap, F32)], axis=1)
    p0tab = p0.astype(jnp.int32).reshape(-1)
    return _combine(p0tab, x1, pos_t, aff_r, mod, ln2_g.reshape(1, d), ln2_b.reshape(1, d), y, cap)


def kernel(x, c, ctx, c_ctx, w_ada, b_ada, w_in, b_gate, conv_qk, na_rel_bias, ml_norm_g, w_out,
           ln1_g, ln1_b, w_router, w_expert_gate, w_expert_up, w_expert_down, ln2_g, ln2_b):
    return _layer(x, ctx, c, c_ctx, w_ada[0], b_ada[0], w_in[0], b_gate[0], conv_qk[0], na_rel_bias[0],
                  ml_norm_g[0], w_out[0], ln1_g[0], ln1_b[0], w_router[0], w_expert_gate[0],
                  w_expert_up[0], w_expert_down[0], ln2_g[0], ln2_b[0])
```

```python
import functools

import numpy as np
import jax
import jax.numpy as jnp
from jax import lax
from jax.experimental import pallas as pl
from jax.experimental.pallas import tpu as pltpu
from jax.experimental.pallas import tpu_sc as plsc

F32 = jnp.float32
BF16 = jnp.bfloat16
HIGHEST = lax.Precision.HIGHEST

GRID_W = 64
NA_HEADS = 8
NA_HEAD_DIM = 64
NA_WIN_H = 8
NA_WIN_W = 16
NA_W = NA_HEADS * NA_HEAD_DIM
ML_HEADS = 4
ML_QK_DIM = 64
ML_V_DIM = 128
ML_QK_W = ML_HEADS * ML_QK_DIM
ML_V_W = ML_HEADS * ML_V_DIM
ML_CHUNK = 256
ML_N_GATES = 4 * ML_HEADS
CONV_K = 5
ROPE_BASE = 10000.0
N_EXPERTS = 16
CAPACITY_FACTOR = 2
LN_EPS = 1e-5
DEPTH = 1
DEEPNORM_ALPHA = (2.0 * DEPTH) ** 0.25

LANES = 128
NEG_BIG = -1e30
LOG2E = 1.4426950408889634
VMEM_LIMIT = 56 * 1024 * 1024

ATT_ROWS = 4
ATT_PAIRS = 2
ATT_KROWS = ATT_ROWS + NA_WIN_H - 1


def _cparams(sem):
    return pltpu.CompilerParams(dimension_semantics=sem, vmem_limit_bytes=VMEM_LIMIT)


def _sigmoid(x):
    return 1.0 / (1.0 + jnp.exp(-x))


def _dot(a, b):
    return jnp.dot(a, b, preferred_element_type=F32)


def _dot_nt(a, b):
    return lax.dot_general(a, b, (((1,), (1,)), ((), ())), preferred_element_type=F32)


def _dot_tn(a, b):
    return lax.dot_general(a, b, (((0,), (0,)), ((), ())), preferred_element_type=F32)


def _ada_kernel(c_ref, w_ref, b_ref, o_ref):
    c = c_ref[...]
    s = c * _sigmoid(c)
    o_ref[...] = jnp.dot(s, w_ref[...], precision=HIGHEST, preferred_element_type=F32) + b_ref[...]


def _ada(cs, w_ada, b_ada):
    rows, d = cs.shape
    n = w_ada.shape[1]
    tn = 512
    return pl.pallas_call(
        _ada_kernel,
        out_shape=jax.ShapeDtypeStruct((rows, n), F32),
        grid=(n // tn,),
        in_specs=[pl.BlockSpec((rows, d), lambda j: (0, 0)),
                  pl.BlockSpec((d, tn), lambda j: (0, j)),
                  pl.BlockSpec((1, tn), lambda j: (0, j))],
        out_specs=pl.BlockSpec((rows, tn), lambda j: (0, j)),
        compiler_params=_cparams(("arbitrary",)),
        name="ada",
    )(cs, w_ada, b_ada.reshape(1, n))


def _inproj_kernel(x_ref, mod_ref, wa_ref, wqk_ref, wv_ref, wo_ref, wg_ref,
                   a_ref, qk_ref, v_ref, o_ref, g_ref):
    xm = (x_ref[0] * (1.0 + mod_ref[0, 1:2, :]) + mod_ref[0, 0:1, :]).astype(BF16)
    a_ref[0] = _dot(xm, wa_ref[...]).astype(BF16)
    qk_ref[0] = _dot(xm, wqk_ref[...]).astype(BF16)
    v_ref[0] = _dot(xm, wv_ref[...]).astype(BF16)
    o_ref[0] = _dot(xm, wo_ref[...]).astype(BF16)
    g_ref[0] = _dot(xm, wg_ref[...])


def _inproj(x, mod, ws, tm):
    b, t, d = x.shape
    wa, wqk, wv, wo, wg = ws
    full = lambda w: pl.BlockSpec(w.shape, lambda i, j: (0, 0))
    row = lambda n: pl.BlockSpec((1, tm, n), lambda i, j: (i, j, 0))
    return pl.pallas_call(
        _inproj_kernel,
        out_shape=(jax.ShapeDtypeStruct((b, t, wa.shape[1]), BF16),
                   jax.ShapeDtypeStruct((b, t, wqk.shape[1]), BF16),
                   jax.ShapeDtypeStruct((b, t, wv.shape[1]), BF16),
                   jax.ShapeDtypeStruct((b, t, wo.shape[1]), BF16),
                   jax.ShapeDtypeStruct((b, t, wg.shape[1]), F32)),
        grid=(b, t // tm),
        in_specs=[row(d), pl.BlockSpec((1, 8, d), lambda i, j: (i, 0, 0)),
                  full(wa), full(wqk), full(wv), full(wo), full(wg)],
        out_specs=(row(wa.shape[1]), row(wqk.shape[1]), row(wv.shape[1]), row(wo.shape[1]),
                   row(wg.shape[1])),
        compiler_params=_cparams(("arbitrary", "arbitrary")),
        name="inproj",
    )(x, mod, wa, wqk, wv, wo, wg)


def _qkprep_kernel(x_ref, w_ref, cos_ref, sin_ref, o_ref, *, sub, halo):
    t = x_ref.shape[1]
    n_sub = t // sub
    lane = lax.broadcasted_iota(jnp.int32, (1, LANES), 1)
    first_half = (lane & 31) < 16
    scale = jnp.where(pl.program_id(1) < (ML_QK_W // LANES), ML_QK_DIM ** -0.5, 1.0).astype(F32)
    zeros = jnp.zeros((halo, LANES), x_ref.dtype)
    for s in range(n_sub):
        lo = s * sub
        top = x_ref[0, lo - halo:lo, :] if s > 0 else zeros
        bot = x_ref[0, lo + sub:lo + sub + halo, :] if s < n_sub - 1 else zeros
        ext = jnp.concatenate([top, x_ref[0, lo:lo + sub, :], bot], axis=0).astype(F32)
        n = sub + 2 * halo
        acc = jnp.zeros((sub, LANES), F32)
        for j in range(CONV_K):
            shift = (CONV_K // 2 - j) % n
            sh = ext if shift == 0 else pltpu.roll(ext, shift, axis=0)
            acc = acc + w_ref[j:j + 1, :] * sh[halo:halo + sub, :]
        y = acc * _sigmoid(acc)
        partner = jnp.where(first_half, pltpu.roll(y, LANES - 16, axis=1), pltpu.roll(y, 16, axis=1))
        y = y * cos_ref[lo:lo + sub, :] + partner * sin_ref[lo:lo + sub, :]
        o_ref[0, lo:lo + sub, :] = (y * scale).astype(o_ref.dtype)


def _qkprep(qk, conv_w, cos_t, sin_t):
    b, t, w = qk.shape
    sub = min(t, 1024)
    kern = functools.partial(_qkprep_kernel, sub=sub, halo=16)
    return pl.pallas_call(
        kern,
        out_shape=jax.ShapeDtypeStruct((b, t, w), BF16),
        grid=(b, w // LANES),
        in_specs=[pl.BlockSpec((1, t, LANES), lambda i, j: (i, 0, j)),
                  pl.BlockSpec((8, LANES), lambda i, j: (0, j)),
                  pl.BlockSpec((t, LANES), lambda i, j: (0, 0)),
                  pl.BlockSpec((t, LANES), lambda i, j: (0, 0))],
        out_specs=pl.BlockSpec((1, t, LANES), lambda i, j: (i, 0, j)),
        compiler_params=_cparams(("arbitrary", "arbitrary")),
        name="qkprep",
    )(qk, conv_w, cos_t, sin_t)


def _rope_tables(t):
    nf = ML_QK_DIM // 4
    inv = 1.0 / (ROPE_BASE ** (jnp.arange(nf, dtype=F32) / nf))
    pos = jnp.arange(t)
    ang_r = (pos // GRID_W).astype(F32)[:, None] * inv
    ang_c = (pos % GRID_W).astype(F32)[:, None] * inv
    cos = jnp.concatenate([jnp.cos(ang_r)] * 2 + [jnp.cos(ang_c)] * 2, axis=-1)
    sin = jnp.concatenate([-jnp.sin(ang_r), jnp.sin(ang_r), -jnp.sin(ang_c), jnp.sin(ang_c)], axis=-1)
    return jnp.tile(cos, (1, 2)), jnp.tile(sin, (1, 2))


def _mlstm_kernel(qf_ref, kf_ref, vf_ref, gf_ref, qb_ref, kb_ref, vb_ref, gb_ref, bg_ref,
                  c0_ref, m0_ref, hf_ref, hb_ref, cout_ref, mout_ref, c_scr, m_scr):
    step = pl.program_id(1)
    n_steps = pl.num_programs(1)
    L = ML_CHUNK

    @pl.when(step == 0)
    def _():
        c_scr[...] = c0_ref[0]
        m_scr[...] = m0_ref[0]

    row_i = lax.broadcasted_iota(jnp.int32, (L, L), 0)
    col_i = lax.broadcasted_iota(jnp.int32, (L, L), 1)
    lane = lax.broadcasted_iota(jnp.int32, (1, LANES), 1)
    tri_lo = (col_i <= row_i)
    tri_up = (col_i >= row_i)
    ones_v = jnp.ones((L, ML_V_DIM), BF16)

    for d, (q_ref, k_ref, v_ref, g_ref, h_ref) in enumerate(
            ((qf_ref, kf_ref, vf_ref, gf_ref, hf_ref), (qb_ref, kb_ref, vb_ref, gb_ref, hb_ref))):
        tri = tri_lo if d == 0 else tri_up
        g = g_ref[0] + bg_ref[...]
        logf = jnp.minimum(g, 0.0) - jnp.log(1.0 + jnp.exp(-jnp.abs(g)))
        tri_b = jnp.where(tri, 1.0, 0.0).astype(BF16)
        l1 = logf.astype(BF16)
        r1 = logf - l1.astype(F32)
        l2 = r1.astype(BF16)
        l3 = (r1 - l2.astype(F32)).astype(BF16)
        cum = _dot(tri_b, l1) + (_dot(tri_b, l2) + _dot(tri_b, l3))
        f_lo = 4 + 8 * d
        z = jnp.where((lane >= f_lo) & (lane < f_lo + ML_HEADS), cum, g)
        zt = z.T
        end = L - 1 if d == 0 else 0
        for h in range(ML_HEADS):
            li, lb = 8 * d + h, f_lo + h
            pair, half = h // 2, h % 2
            head_mask = (lane >= 64 * half) & (lane < 64 * half + 64)
            bcol = jnp.sum(jnp.where(lane == lb, z, 0.0), axis=-1, keepdims=True)
            icol = jnp.sum(jnp.where(lane == li, z, 0.0), axis=-1, keepdims=True)
            brow = zt[lb:lb + 1, :]
            irow = zt[li:li + 1, :]
            total = bcol[end:end + 1, :]
            m_prev = m_scr[d, h:h + 1, 0:1]

            qm = jnp.where(head_mask, q_ref[0, :, pair * LANES:(pair + 1) * LANES], 0).astype(BF16)
            km = jnp.where(head_mask, k_ref[0, :, pair * LANES:(pair + 1) * LANES], 0).astype(BF16)
            vext = jnp.concatenate([v_ref[0, :, h * ML_V_DIM:(h + 1) * ML_V_DIM], ones_v], axis=1)

            dmat = jnp.where(tri, bcol - brow + irow, NEG_BIG)
            m_prev_term = bcol + m_prev
            m_t = jnp.maximum(jnp.max(dmat, axis=-1, keepdims=True), m_prev_term)
            sp = _dot_nt(qm, km) * jnp.exp(dmat - m_t)
            inter = jnp.exp(m_prev_term - m_t)
            c_prev = c_scr[d, h]
            r = _dot(sp.astype(BF16), vext) + inter * _dot(qm, c_prev.astype(BF16))
            num = r[:, :ML_V_DIM]
            den = r[:, ML_V_DIM:]
            h_ref[0, :, h * ML_V_DIM:(h + 1) * ML_V_DIM] = num / jnp.maximum(jnp.abs(den), jnp.exp(-m_t))

            a = total - bcol + icol
            m_loc = jnp.max(a, axis=0, keepdims=True)
            kw = (km.astype(F32) * jnp.exp(a - m_loc)).astype(BF16)
            c_loc = _dot_tn(kw, vext)
            m_new = jnp.maximum(total + m_prev, m_loc)
            c_scr[d, h] = jnp.exp(total + m_prev - m_new) * c_prev + jnp.exp(m_loc - m_new) * c_loc
            m_scr[d, h:h + 1, :] = jnp.broadcast_to(m_new, (1, LANES))

    @pl.when(step == n_steps - 1)
    def _():
        cout_ref[0] = c_scr[...]
        mout_ref[0] = m_scr[...]


def _mlstm(qk, v, gates, bg, c0, m0):
    b, t, _ = qk.shape
    L = ML_CHUNK
    nc = t // L
    fwd = lambda n, blk: pl.BlockSpec((1, L, n), lambda i, c: (i, c, blk))
    bwd = lambda n, blk: pl.BlockSpec((1, L, n), lambda i, c: (i, nc - 1 - c, blk))
    st_c = pl.BlockSpec((1,) + c0.shape[1:], lambda i, c: (i, 0, 0, 0, 0))
    st_m = pl.BlockSpec((1,) + m0.shape[1:], lambda i, c: (i, 0, 0, 0))
    return pl.pallas_call(
        _mlstm_kernel,
        out_shape=(jax.ShapeDtypeStruct((b, t, ML_V_W), F32),
                   jax.ShapeDtypeStruct((b, t, ML_V_W), F32),
                   jax.ShapeDtypeStruct(c0.shape, F32),
                   jax.ShapeDtypeStruct(m0.shape, F32)),
        grid=(b, nc),
        in_specs=[fwd(ML_QK_W, 0), fwd(ML_QK_W, 1), fwd(ML_V_W, 0), fwd(LANES, 0),
                  bwd(ML_QK_W, 0), bwd(ML_QK_W, 1), bwd(ML_V_W, 0), bwd(LANES, 0),
                  pl.BlockSpec((1, LANES), lambda i, c: (0, 0)), st_c, st_m],
        out_specs=(fwd(ML_V_W, 0), bwd(ML_V_W, 0), st_c, st_m),
        scratch_shapes=[pltpu.VMEM(c0.shape[1:], F32), pltpu.VMEM(m0.shape[1:], F32)],
        compiler_params=_cparams(("arbitrary", "arbitrary")),
        name="mlstm",
    )(qk, qk, v, gates, qk, qk, v, gates, bg, c0, m0)


def _attn_kernel(case_ref, ws_ref, q_ref, k_ref, v_ref, kc_ref, vc_ref, bias_ref, o_ref):
    j = pl.program_id(2)
    nk = ATT_KROWS * GRID_W
    start = pl.multiple_of(ws_ref[j] * GRID_W, GRID_W)
    lane = lax.broadcasted_iota(jnp.int32, (1, LANES), 1)
    for pp in range(ATT_PAIRS):
        lanes = slice(pp * LANES, (pp + 1) * LANES)
        q = q_ref[0, :, lanes]
        k = k_ref[0, pl.ds(start, nk), lanes]
        v = v_ref[0, pl.ds(start, nk), lanes]
        kc = kc_ref[0, :, lanes]
        vc = vc_ref[0, :, lanes]
        acc = jnp.zeros(q.shape, F32)
        for h in range(2):
            head_mask = (lane >= NA_HEAD_DIM * h) & (lane < NA_HEAD_DIM * (h + 1))
            qh = (jnp.where(head_mask, q, 0).astype(F32) * (NA_HEAD_DIM ** -0.5 * LOG2E)).astype(BF16)
            s = _dot_nt(qh, k) + bias_ref[0, 2 * pp + h]
            sc = _dot_nt(qh, kc)
            m = jnp.maximum(jnp.max(s, axis=-1, keepdims=True), jnp.max(sc, axis=-1, keepdims=True))
            p = jnp.exp2(s - m)
            pc = jnp.exp2(sc - m)
            vh = jnp.where(head_mask, v, 1).astype(BF16)
            vch = jnp.where(head_mask, vc, 1).astype(BF16)
            o = _dot(p.astype(BF16), vh) + _dot(pc.astype(BF16), vch)
            acc = acc + jnp.where(head_mask, o / pltpu.roll(o, NA_HEAD_DIM, axis=1), 0.0)
        o_ref[0, :, lanes] = acc.astype(o_ref.dtype)


def _attn_plan(rows):
    kh = min(NA_WIN_H, rows)
    nj = rows // ATT_ROWS
    rs = lambda r: int(np.clip(r - kh // 2, 0, rows - kh))
    ws = [int(np.clip(ATT_ROWS * j - kh // 2, 0, rows - ATT_KROWS)) for j in range(nj)]
    sigs, case = [], []
    for j in range(nj):
        r0 = ATT_ROWS * j
        sig = (ws[j] - r0,) + tuple(rs(r0 + a) - r0 for a in range(ATT_ROWS))
        if sig not in sigs:
            sigs.append(sig)
        case.append(sigs.index(sig))
    return np.asarray(ws, np.int32), np.asarray(case, np.int32), sigs, kh


def _attn_bias(bias_table, sigs, kh):
    col_start = np.clip(np.arange(GRID_W) - NA_WIN_W // 2, 0, GRID_W - NA_WIN_W)
    c = np.arange(GRID_W)
    cidx = c[None, :] - c[:, None] + (NA_WIN_W - 1)
    col_ok = (c[None, :] >= col_start[:, None]) & (c[None, :] < col_start[:, None] + NA_WIN_W)
    expand = (np.arange(2 * NA_WIN_W - 1)[:, None, None] == cidx[None]).astype(np.float32)
    out = []
    for sig in sigs:
        wsr, rsr = sig[0], np.asarray(sig[1:])
        a = np.arange(ATT_ROWS)[:, None]
        rk = wsr + np.arange(ATT_KROWS)[None, :]
        row_ok = (rk >= rsr[:, None]) & (rk < rsr[:, None] + kh)
        ridx = np.clip(rk - a + (NA_WIN_H - 1), 0, 2 * NA_WIN_H - 2)
        rows = bias_table[:, ridx, :]
        full = jnp.einsum('haiv,vqk->haqik', rows, jnp.asarray(expand), precision=HIGHEST)
        ok = row_ok[:, None, :, None] & col_ok[None, :, None, :]
        full = jnp.where(ok[None], full * LOG2E, NEG_BIG)
        out.append(full.reshape(full.shape[0], ATT_ROWS * GRID_W, ATT_KROWS * GRID_W))
    return jnp.stack(out).astype(F32)


def _attn(a_lat, a_ctx, bias_table):
    b, t, _ = a_lat.shape
    lc = a_ctx.shape[1]
    rows = t // GRID_W
    ws, case, sigs, kh = _attn_plan(rows)
    bias = _attn_bias(bias_table, sigs, kh)
    tq = ATT_ROWS * GRID_W
    nk = ATT_KROWS * GRID_W
    bw = ATT_PAIRS * LANES
    n_blk = NA_W // bw
    grid_spec = pltpu.PrefetchScalarGridSpec(
        num_scalar_prefetch=2,
        grid=(n_blk, b, rows // ATT_ROWS),
        in_specs=[pl.BlockSpec((1, tq, bw), lambda p, i, j, cs, w: (i, j, p)),
                  pl.BlockSpec((1, t, bw), lambda p, i, j, cs, w: (i, 0, n_blk + p)),
                  pl.BlockSpec((1, t, bw), lambda p, i, j, cs, w: (i, 0, 2 * n_blk + p)),
                  pl.BlockSpec((1, lc, bw), lambda p, i, j, cs, w: (i, 0, n_blk + p)),
                  pl.BlockSpec((1, lc, bw), lambda p, i, j, cs, w: (i, 0, 2 * n_blk + p)),
                  pl.BlockSpec((1, 2 * ATT_PAIRS, tq, nk), lambda p, i, j, cs, w: (cs[j], p, 0, 0))],
        out_specs=pl.BlockSpec((1, tq, bw), lambda p, i, j, cs, w: (i, j, p)),
    )
    return pl.pallas_call(
        _attn_kernel,
        out_shape=jax.ShapeDtypeStruct((b, t, NA_W), BF16),
        grid_spec=grid_spec,
        compiler_params=_cparams(("arbitrary", "arbitrary", "arbitrary")),
        name="nattn",
    )(jnp.asarray(case), jnp.asarray(ws), a_lat, a_lat, a_lat, a_ctx, a_ctx, bias)


def _outproj_kernel(att_ref, hf_ref, hb_ref, om_ref, x_ref, mod_ref, ng_ref, wa_ref, wm_ref,
                    lng_ref, lnb_ref, wrh_ref, wrl_ref, x1_ref, hm_ref, aff_ref, affr_ref):
    h = hf_ref[0] + hb_ref[0]
    parts = []
    for hd in range(ML_HEADS):
        hh = h[:, hd * ML_V_DIM:(hd + 1) * ML_V_DIM]
        mu = jnp.mean(hh, axis=-1, keepdims=True)
        var = jnp.mean(jnp.square(hh - mu), axis=-1, keepdims=True)
        parts.append((hh - mu) * lax.rsqrt(var + LN_EPS))
    hn = jnp.concatenate(parts, axis=1) * ng_ref[...]
    ml = (hn * _sigmoid(om_ref[0].astype(F32))).astype(BF16)
    mix = _dot(att_ref[0], wa_ref[...]) + _dot(ml, wm_ref[...])
    y = DEEPNORM_ALPHA * x_ref[0] + mod_ref[0, 2:3, :] * mix
    mu = jnp.mean(y, axis=-1, keepdims=True)
    var = jnp.mean(jnp.square(y - mu), axis=-1, keepdims=True)
    x1 = (y - mu) * lax.rsqrt(var + LN_EPS) * lng_ref[...] + lnb_ref[...]
    x1_ref[0] = x1
    hm = x1 * (1.0 + mod_ref[0, 4:5, :]) + mod_ref[0, 3:4, :]
    h_hi = hm.astype(BF16)
    h_lo = (hm - h_hi.astype(F32)).astype(BF16)
    bits = pltpu.bitcast(h_hi.astype(F32), jnp.uint32)
    half = bits.shape[1] // 2
    word = (bits[:, :half] >> 16) | (bits[:, half:] & jnp.uint32(0xFFFF0000))
    hm_ref[0] = pltpu.bitcast(word, jnp.int32)
    logits = _dot(h_hi, wrh_ref[...]) + (_dot(h_lo, wrh_ref[...]) + _dot(h_hi, wrl_ref[...]))
    lane = lax.broadcasted_iota(jnp.int32, (1, LANES), 1)
    logits = jnp.where(lane < N_EXPERTS, logits, NEG_BIG)
    e = jnp.exp(logits - jnp.max(logits, axis=-1, keepdims=True))
    aff = e / jnp.sum(e, axis=-1, keepdims=True)
    affr_ref[0] = aff
    aff_ref[0] = aff.T[:N_EXPERTS, :]


def _outproj(att, hf, hb, om, x, mod, ng, w_att, w_ml, lng, lnb, wr_hi, wr_lo, tm):
    b, t, d = x.shape
    row = lambda n: pl.BlockSpec((1, tm, n), lambda i, j: (i, j, 0))
    full = lambda w: pl.BlockSpec(w.shape, lambda i, j: (0,) * w.ndim)
    return pl.pallas_call(
        _outproj_kernel,
        out_shape=(jax.ShapeDtypeStruct((b, t, d), F32),
                   jax.ShapeDtypeStruct((b, t, d // 2), jnp.int32),
                   jax.ShapeDtypeStruct((b, N_EXPERTS, t), F32),
                   jax.ShapeDtypeStruct((b, t, LANES), F32)),
        grid=(b, t // tm),
        in_specs=[row(NA_W), row(ML_V_W), row(ML_V_W), row(ML_V_W), row(d),
                  pl.BlockSpec((1, 8, d), lambda i, j: (i, 0, 0)),
                  full(ng), full(w_att), full(w_ml), full(lng), full(lnb), full(wr_hi), full(wr_lo)],
        out_specs=(row(d), row(d // 2), pl.BlockSpec((1, N_EXPERTS, tm), lambda i, j: (i, 0, j)), row(LANES)),
        compiler_params=_cparams(("arbitrary", "arbitrary")),
        name="outproj",
    )(att, hf, hb, om, x, mod, ng, w_att, w_ml, lng, lnb, wr_hi, wr_lo)


UNSELECTED = -1e6


def _select_kernel(aff_ref, idx_ref, pos_ref, off_ref, cum_scr, sel_scr, offs_v, offs_s, dsem, *, cap):
    t = aff_ref.shape[2]
    nb = t // LANES
    lane = lax.broadcasted_iota(jnp.int32, (1, LANES), 1)
    keys = lambda: pltpu.bitcast(aff_ref[0], jnp.int32)

    def count(mask):
        return jnp.sum(jnp.where(mask, 1.0, 0.0), axis=-1, keepdims=True)

    def search(_, c):
        lo, hi = c
        mid = lo + jnp.right_shift(hi - lo, 1)
        ge = count(keys() >= mid) >= cap
        return jnp.where(ge, mid, lo), jnp.where(ge, hi, mid)

    lo0 = jnp.zeros((N_EXPERTS, 1), jnp.int32)
    hi0 = jnp.full((N_EXPERTS, 1), 0x7F800000, jnp.int32)
    thr, _ = lax.fori_loop(0, 31, search, (lo0, hi0))

    r_i = lax.broadcasted_iota(jnp.int32, (LANES, LANES), 0)
    c_i = lax.broadcasted_iota(jnp.int32, (LANES, LANES), 1)
    strict = jnp.where(r_i < c_i, 1.0, 0.0).astype(BF16)
    tr_i = lax.broadcasted_iota(jnp.int32, (t, LANES), 0)
    tc_i = lax.broadcasted_iota(jnp.int32, (t, LANES), 1)
    block_ind = jnp.where(jnp.right_shift(tr_i, 7) == tc_i, 1.0, 0.0).astype(BF16)

    def prefix(x01):
        xb = x01.astype(BF16)
        offs = _dot(_dot(xb, block_ind).astype(BF16), strict)
        for j in range(nb):
            off_j = jnp.sum(jnp.where(lane == j, offs, 0.0), axis=-1, keepdims=True)
            cum_scr[:, j * LANES:(j + 1) * LANES] = _dot(xb[:, j * LANES:(j + 1) * LANES], strict) + off_j
        return offs

    k = keys()
    gt = k > thr
    eq = k == thr
    need = cap - count(gt)
    prefix(jnp.where(eq, 1.0, 0.0))
    sel = jnp.where(gt | (eq & (cum_scr[...] < need)), 1.0, 0.0)
    sel_scr[...] = sel
    offs = prefix(sel)

    pad = jnp.zeros((LANES - N_EXPERTS, LANES), F32)
    for j in range(nb):
        blk = jnp.where(sel_scr[:, j * LANES:(j + 1) * LANES] > 0.0, cum_scr[:, j * LANES:(j + 1) * LANES], UNSELECTED)
        pos_ref[0, j * LANES:(j + 1) * LANES, :] = jnp.concatenate([blk, pad], axis=0).T
    off_ref[0] = jnp.concatenate([offs, pad], axis=0).T

    cum_scr[...] = cum_scr[...] + sel_scr[...]
    idx_ref[...] = jnp.zeros(idx_ref.shape, idx_ref.dtype)
    offs_v[...] = offs.astype(jnp.int32)
    to_smem = pltpu.make_async_copy(offs_v, offs_s, dsem)
    to_smem.start()
    to_smem.wait()
    sub = lax.broadcasted_iota(jnp.int32, (LANES, LANES), 0).astype(F32)
    for e in range(N_EXPERTS):
        def group(pg, carry, e=e):
            first = jnp.asarray(pg * LANES, jnp.int32)
            jlo, jhi = carry

            def advance(j0, bound, ahead):
                look = lambda j: offs_s[e, jnp.minimum(j + ahead, LANES - 1)]
                j, _ = lax.while_loop(lambda c: (c[0] < nb) & (c[1] <= bound),
                                      lambda c: (c[0] + 1, look(c[0] + 1)), (j0, look(j0)))
                return j

            jlo = advance(jlo, first, 1)
            jhi = advance(jhi, first + (LANES - 1), 0)
            slots = first.astype(F32) + sub

            def block(jb, acc):
                c = cum_scr[pl.ds(e, 1), pl.ds(pl.multiple_of(jb * LANES, LANES), LANES)]
                return acc + jnp.where(jnp.broadcast_to(c, (LANES, LANES)) <= slots, 1.0, 0.0)

            acc = lax.fori_loop(jlo, jhi, block, jnp.zeros((LANES, LANES), F32))
            col = jnp.sum(acc, axis=-1, keepdims=True).astype(jnp.int32) + jlo * LANES
            idx_ref[0, pl.ds(pl.multiple_of(pg * LANES, LANES), LANES), e:e + 1] = col
            return jlo, jhi

        lax.fori_loop(0, cap // LANES, group, (jnp.int32(0), jnp.int32(0)))


def _select(aff_t, cap):
    b, e, t = aff_t.shape
    assert cap % LANES == 0 and t % LANES == 0 and t // LANES < LANES
    return pl.pallas_call(
        functools.partial(_select_kernel, cap=cap),
        out_shape=(jax.ShapeDtypeStruct((b, cap, LANES), jnp.int32),
                   jax.ShapeDtypeStruct((b, t, LANES), F32),
                   jax.ShapeDtypeStruct((b, LANES, LANES), F32)),
        grid=(b,),
        in_specs=[pl.BlockSpec((1, e, t), lambda i: (i, 0, 0))],
        out_specs=(pl.BlockSpec((1, cap, LANES), lambda i: (i, 0, 0)),
                   pl.BlockSpec((1, t, LANES), lambda i: (i, 0, 0)),
                   pl.BlockSpec((1, LANES, LANES), lambda i: (i, 0, 0))),
        scratch_shapes=[pltpu.VMEM((e, t), F32), pltpu.VMEM((e, t), F32),
                        pltpu.VMEM((e, LANES), jnp.int32), pltpu.SMEM((e, LANES), jnp.int32),
                        pltpu.SemaphoreType.DMA(())],
        compiler_params=_cparams(("arbitrary",)),
        name="select",
    )(aff_t)


SC_CORES = 2
SC_SUBCORES = 16
SC_CHUNK = 64


def _gather_rows(table, idx):
    n = idx.shape[0]
    v, d = table.shape
    n_workers = SC_CORES * SC_SUBCORES
    per_w = n // n_workers
    assert n % (n_workers * SC_CHUNK) == 0 and d % LANES == 0 and table.dtype.itemsize == 4
    mesh = plsc.VectorSubcoreMesh(core_axis_name="c", subcore_axis_name="s",
                                  num_cores=SC_CORES, num_subcores=SC_SUBCORES)

    @functools.partial(
        pl.kernel, mesh=mesh,
        out_type=jax.ShapeDtypeStruct((n, d), table.dtype),
        scratch_types=[pltpu.VMEM((SC_CHUNK,), jnp.int32),
                       pltpu.VMEM((SC_CHUNK, d), table.dtype),
                       pltpu.SemaphoreType.DMA],
        name="row_gather")
    def gather(table_hbm, idx_hbm, out_hbm, idx_v, rows_v, sem):
        base = (lax.axis_index("s") * SC_CORES + lax.axis_index("c")) * per_w

        @pl.loop(0, per_w, step=SC_CHUNK)
        def _(o):
            pltpu.sync_copy(idx_hbm.at[pl.ds(base + o, SC_CHUNK)], idx_v)
            pltpu.async_copy(table_hbm.at[idx_v], rows_v, sem).wait()
            pltpu.sync_copy(rows_v, out_hbm.at[pl.ds(base + o, SC_CHUNK)])

    return gather(table, idx)


def _moe_kernel(x_ref, wg_ref, wu_ref, wd_ref, o_ref, acc_ref, x_scr):
    f = pl.program_id(2)

    @pl.when(f == 0)
    def _():
        acc_ref[...] = jnp.zeros(acc_ref.shape, F32)
        w = pltpu.bitcast(x_ref[0], jnp.uint32)
        half = w.shape[1]
        x_scr[:, :half] = pltpu.bitcast(w << 16, F32).astype(BF16)
        x_scr[:, half:] = pltpu.bitcast(w & jnp.uint32(0xFFFF0000), F32).astype(BF16)

    x = x_scr[...]
    hg = _dot(x, wg_ref[0].astype(BF16))
    hu = _dot(x, wu_ref[0].astype(BF16))
    act = (hg * _sigmoid(hg) * hu).astype(BF16)
    acc_ref[...] += _dot(act, wd_ref[0].astype(BF16))

    @pl.when(f == pl.num_programs(2) - 1)
    def _():
        o_ref[0] = acc_ref[...].astype(o_ref.dtype)


def _moe(xe, w_gate, w_up, w_down, tm, tf):
    e, m, dw = xe.shape
    d = 2 * dw
    dff = w_gate.shape[2]
    return pl.pallas_call(
        _moe_kernel,
        out_shape=jax.ShapeDtypeStruct((e, m, d), BF16),
        grid=(e, m // tm, dff // tf),
        in_specs=[pl.BlockSpec((1, tm, dw), lambda i, j, f: (i, j, 0)),
                  pl.BlockSpec((1, d, tf), lambda i, j, f: (i, 0, f)),
                  pl.BlockSpec((1, d, tf), lambda i, j, f: (i, 0, f)),
                  pl.BlockSpec((1, tf, d), lambda i, j, f: (i, f, 0))],
        out_specs=pl.BlockSpec((1, tm, d), lambda i, j, f: (i, j, 0)),
        scratch_shapes=[pltpu.VMEM((tm, d), F32), pltpu.VMEM((tm, d), BF16)],
        compiler_params=_cparams(("arbitrary", "arbitrary", "arbitrary")),
        name="moe",
    )(xe, w_gate, w_up, w_down)


CMB_TM = 256
CMB_WIN = 64
ROW_ALIGN = 16


def _combine_kernel(p0_ref, x1_ref, pos_ref, aff_ref, mod_ref, lng_ref, lnb_ref, y_hbm, o_ref,
                    ybuf, yextra, acc_ref, sem, sem_x, *, cap):
    b = pl.program_id(0)
    j = pl.program_id(1)
    nt = pl.num_programs(1)
    g = b * nt + j
    n_steps = pl.num_programs(0) * nt
    slot = lax.rem(g, 2)
    m_rows = y_hbm.shape[1]
    win = CMB_WIN
    kdim = N_EXPERTS * win

    def window(bb, jj, r):
        base = (bb * (nt + 1) + jj) * N_EXPERTS
        los, rows = [], []
        for e in range(N_EXPERTS):
            lo = (p0_ref[base + e] // ROW_ALIGN) * ROW_ALIGN + r * win
            los.append(lo)
            rows.append(jnp.minimum(bb * cap + lo, m_rows - win))
        return los, rows

    def copies(rows, buf, sems):
        return [pltpu.make_async_copy(y_hbm.at[e, pl.ds(pl.multiple_of(rows[e], ROW_ALIGN), win), :],
                                      buf.at[pl.ds(e * win, win), :], sems.at[e])
                for e in range(N_EXPERTS)]

    los0, rows0 = window(b, j, 0)

    @pl.when(g == 0)
    def _():
        for c in copies(rows0, ybuf.at[0], sem.at[0]):
            c.start()

    @pl.when(g + 1 < n_steps)
    def _():
        wrap = j + 1 == nt
        _, rows_n = window(jnp.where(wrap, b + 1, b), jnp.where(wrap, 0, j + 1), 0)
        for c in copies(rows_n, ybuf.at[1 - slot], sem.at[1 - slot]):
            c.start()

    lane = lax.broadcasted_iota(jnp.int32, (1, LANES), 1)
    e_i = lax.broadcasted_iota(jnp.int32, (LANES, kdim), 0)
    k_i = lax.broadcasted_iota(jnp.int32, (LANES, kdim), 1)
    expand = jnp.where(k_i // win == e_i, 1.0, 0.0).astype(BF16)
    col_in_win = (lax.broadcasted_iota(jnp.int32, (1, kdim), 1) % win).astype(F32)
    aff_cols = _dot(aff_ref[0].astype(BF16), expand)

    def lane_vec(vals):
        v = jnp.zeros((1, LANES), F32)
        for e in range(N_EXPERTS):
            v = jnp.where(lane == e, jnp.asarray(vals[e]).astype(F32), v)
        return v

    def expand_rows(los, rows, buf):
        rel = pos_ref[0] - lane_vec(los)
        shift = lane_vec([b * cap + lo - row for lo, row in zip(los, rows)])
        tgt = jnp.where((rel >= 0.0) & (rel < float(win)), rel + shift, -1.0)
        tgt_cols = _dot(tgt.astype(BF16), expand)
        s = jnp.where(tgt_cols == col_in_win, aff_cols, 0.0).astype(BF16)
        return _dot(s, buf[...])

    for c in copies(rows0, ybuf.at[slot], sem.at[slot]):
        c.wait()
    acc_ref[...] = expand_rows(los0, rows0, ybuf.at[slot])

    base = (b * (nt + 1) + j) * N_EXPERTS
    rounds = jnp.int32(0)
    for e in range(N_EXPERTS):
        span = p0_ref[base + N_EXPERTS + e] - (p0_ref[base + e] // ROW_ALIGN) * ROW_ALIGN
        rounds = jnp.maximum(rounds, (span + win - 1) // win)

    def extra(r, carry):
        los, rows = window(b, j, r)
        cps = copies(rows, yextra, sem_x)
        for c in cps:
            c.start()
        for c in cps:
            c.wait()
        acc_ref[...] += expand_rows(los, rows, yextra)
        return carry

    lax.fori_loop(1, rounds, extra, 0)

    y = DEEPNORM_ALPHA * x1_ref[0] + mod_ref[0, 5:6, :] * acc_ref[...]
    mu = jnp.mean(y, axis=-1, keepdims=True)
    var = jnp.mean(jnp.square(y - mu), axis=-1, keepdims=True)
    o_ref[0] = (y - mu) * lax.rsqrt(var + LN_EPS) * lng_ref[...] + lnb_ref[...]


def _combine(p0tab, x1, pos_t, aff_r, mod, lng, lnb, y, cap):
    b, t, d = x1.shape
    tm = CMB_TM
    kdim = N_EXPERTS * CMB_WIN
    row = lambda n: pl.BlockSpec((1, tm, n), lambda i, j, p: (i, j, 0))
    vec = pl.BlockSpec((1, d), lambda i, j, p: (0, 0))
    grid_spec = pltpu.PrefetchScalarGridSpec(
        num_scalar_prefetch=1,
        grid=(b, t // tm),
        in_specs=[row(d), row(LANES), row(LANES), pl.BlockSpec((1, 8, d), lambda i, j, p: (i, 0, 0)), vec, vec,
                  pl.BlockSpec(memory_space=pl.ANY)],
        out_specs=row(d),
        scratch_shapes=[pltpu.VMEM((2, kdim, d), y.dtype), pltpu.VMEM((kdim, d), y.dtype),
                        pltpu.VMEM((tm, d), F32),
                        pltpu.SemaphoreType.DMA((2, N_EXPERTS)), pltpu.SemaphoreType.DMA((N_EXPERTS,))],
    )
    return pl.pallas_call(
        functools.partial(_combine_kernel, cap=cap),
        out_shape=jax.ShapeDtypeStruct((b, t, d), F32),
        grid_spec=grid_spec,
        compiler_params=_cparams(("arbitrary", "arbitrary")),
        name="combine",
    )(p0tab, x1, pos_t, aff_r, mod, lng, lnb, y)


def _split_w_in(w_in):
    a_end = 3 * NA_W
    qk_end = a_end + 2 * ML_QK_W
    v_end = qk_end + ML_V_W
    o_end = v_end + ML_V_W
    wg = jnp.pad(w_in[:, o_end:], ((0, 0), (0, LANES - ML_N_GATES)))
    return tuple(w.astype(BF16) for w in
                 (w_in[:, :a_end], w_in[:, a_end:qk_end], w_in[:, qk_end:v_end], w_in[:, v_end:o_end], wg))


def _layer(x, ctx, c, c_ctx, w_ada, b_ada, w_in, b_gate, conv_qk, na_rel_bias, ml_norm_g, w_out,
           ln1_g, ln1_b, w_router, w_expert_gate, w_expert_up, w_expert_down, ln2_g, ln2_b):
    b, t, d = x.shape
    lc = ctx.shape[1]

    cs = jnp.zeros((16, d), F32).at[:b].set(c).at[b].set(c_ctx)
    m = _ada(cs, w_ada, b_ada)
    mod = jnp.pad(m[:b].reshape(b, 6, d), ((0, 0), (0, 2), (0, 0)))
    mod_ctx = jnp.broadcast_to(jnp.pad(m[b].reshape(6, d), ((0, 2), (0, 0)))[None], (b, 8, d))

    ws = _split_w_in(w_in)
    a_lat, qk_lat, v_lat, o_lat, g_lat = _inproj(x, mod, ws, tm=512)
    a_ctx, qk_ctx, v_ctx, _, g_ctx = _inproj(ctx, mod_ctx, ws, tm=lc)

    conv_w = jnp.pad(conv_qk, ((0, 8 - CONV_K), (0, 0)))
    cos_t, sin_t = _rope_tables(t)
    qk_lat = _qkprep(qk_lat, conv_w, cos_t, sin_t)
    qk_ctx = _qkprep(qk_ctx, conv_w, jnp.ones((lc, LANES), F32), jnp.zeros((lc, LANES), F32))

    bg = jnp.pad(b_gate, (0, LANES - ML_N_GATES)).reshape(1, LANES)
    c0 = jnp.zeros((b, 2, ML_HEADS, LANES, 2 * ML_V_DIM), F32)
    m0 = jnp.zeros((b, 2, 8, LANES), F32)
    _, _, c1, m1 = _mlstm(qk_ctx, v_ctx, g_ctx, bg, c0, m0)
    hf, hb, _, _ = _mlstm(qk_lat, v_lat, g_lat, bg, c1, m1)

    att = _attn(a_lat, a_ctx, na_rel_bias)

    wr = jnp.pad(w_router, ((0, 0), (0, LANES - N_EXPERTS)))
    wr_hi = wr.astype(BF16)
    wr_lo = (wr - wr_hi.astype(F32)).astype(BF16)
    x1, hm, aff_t, aff_r = _outproj(att, hf, hb, o_lat, x, mod, ml_norm_g.reshape(1, -1),
                                    w_out[:NA_W].astype(BF16), w_out[NA_W:].astype(BF16),
                                    ln1_g.reshape(1, d), ln1_b.reshape(1, d), wr_hi, wr_lo, tm=256)

    cap = CAPACITY_FACTOR * t // N_EXPERTS
    idx_t, pos_t, off_t = _select(aff_t, cap)
    rows = jnp.transpose(idx_t[:, :, :N_EXPERTS], (2, 0, 1)) + (jnp.arange(b, dtype=jnp.int32) * t)[None, :, None]
    xe = _gather_rows(hm.reshape(b * t, d // 2), rows.reshape(-1)).reshape(N_EXPERTS, b * cap, d // 2)
    y = _moe(xe, w_expert_gate, w_expert_up, w_expert_down, tm=min(2048, b * cap), tf=256)

    nt = t // CMB_TM
    p0 = off_t[:, 0:nt * (CMB_TM // LANES):CMB_TM // LANES, :N_EXPERTS]
    p0 = jnp.concatenate([p0, jnp.full((b, 1, N_EXPERTS), cap, F32)], axis=1)
    p0tab = p0.astype(jnp.int32).reshape(-1)
    return _combine(p0tab, x1, pos_t, aff_r, mod, ln2_g.reshape(1, d), ln2_b.reshape(1, d), y, cap)


def kernel(x, c, ctx, c_ctx, w_ada, b_ada, w_in, b_gate, conv_qk, na_rel_bias, ml_norm_g, w_out,
           ln1_g, ln1_b, w_router, w_expert_gate, w_expert_up, w_expert_down, ln2_g, ln2_b):
    return _layer(x, ctx, c, c_ctx, w_ada[0], b_ada[0], w_in[0], b_gate[0], conv_qk[0], na_rel_bias[0],
                  ml_norm_g[0], w_out[0], ln1_g[0], ln1_b[0], w_router[0], w_expert_gate[0],
                  w_expert_up[0], w_expert_down[0], ln2_g[0], ln2_b[0])
```

```python
import functools

import numpy as np
import jax
import jax.numpy as jnp
from jax import lax
from jax.experimental import pallas as pl
from jax.experimental.pallas import tpu as pltpu
from jax.experimental.pallas import tpu_sc as plsc

F32 = jnp.float32
BF16 = jnp.bfloat16
HIGHEST = lax.Precision.HIGHEST

GRID_W = 64
NA_HEADS = 8
NA_HEAD_DIM = 64
NA_WIN_H = 8
NA_WIN_W = 16
NA_W = NA_HEADS * NA_HEAD_DIM
ML_HEADS = 4
ML_QK_DIM = 64
ML_V_DIM = 128
ML_QK_W = ML_HEADS * ML_QK_DIM
ML_V_W = ML_HEADS * ML_V_DIM
ML_CHUNK = 256
ML_N_GATES = 4 * ML_HEADS
CONV_K = 5
ROPE_BASE = 10000.0
N_EXPERTS = 16
CAPACITY_FACTOR = 2
LN_EPS = 1e-5
DEPTH = 1
DEEPNORM_ALPHA = (2.0 * DEPTH) ** 0.25

LANES = 128
NEG_BIG = -1e30
LOG2E = 1.4426950408889634
VMEM_LIMIT = 56 * 1024 * 1024

ATT_ROWS = 4
ATT_PAIRS = 2
ATT_KROWS = ATT_ROWS + NA_WIN_H - 1


def _cparams(sem):
    return pltpu.CompilerParams(dimension_semantics=sem, vmem_limit_bytes=VMEM_LIMIT)


def _sigmoid(x):
    return 1.0 / (1.0 + jnp.exp(-x))


def _dot(a, b):
    return jnp.dot(a, b, preferred_element_type=F32)


def _dot_nt(a, b):
    return lax.dot_general(a, b, (((1,), (1,)), ((), ())), preferred_element_type=F32)


def _dot_tn(a, b):
    return lax.dot_general(a, b, (((0,), (0,)), ((), ())), preferred_element_type=F32)


def _ada_kernel(c_ref, w_ref, b_ref, o_ref):
    c = c_ref[...]
    s = c * _sigmoid(c)
    o_ref[...] = jnp.dot(s, w_ref[...], precision=HIGHEST, preferred_element_type=F32) + b_ref[...]


def _ada(cs, w_ada, b_ada):
    rows, d = cs.shape
    n = w_ada.shape[1]
    tn = 512
    return pl.pallas_call(
        _ada_kernel,
        out_shape=jax.ShapeDtypeStruct((rows, n), F32),
        grid=(n // tn,),
        in_specs=[pl.BlockSpec((rows, d), lambda j: (0, 0)),
                  pl.BlockSpec((d, tn), lambda j: (0, j)),
                  pl.BlockSpec((1, tn), lambda j: (0, j))],
        out_specs=pl.BlockSpec((rows, tn), lambda j: (0, j)),
        compiler_params=_cparams(("arbitrary",)),
        name="ada",
    )(cs, w_ada, b_ada.reshape(1, n))


def _inproj_kernel(x_ref, mod_ref, wa_ref, wqk_ref, wv_ref, wo_ref, wg_ref,
                   a_ref, qk_ref, v_ref, o_ref, g_ref):
    xm = (x_ref[0] * (1.0 + mod_ref[0, 1:2, :]) + mod_ref[0, 0:1, :]).astype(BF16)
    a_ref[0] = _dot(xm, wa_ref[...]).astype(BF16)
    qk_ref[0] = _dot(xm, wqk_ref[...]).astype(BF16)
    v_ref[0] = _dot(xm, wv_ref[...]).astype(BF16)
    o_ref[0] = _dot(xm, wo_ref[...]).astype(BF16)
    g_ref[0] = _dot(xm, wg_ref[...])


def _inproj(x, mod, ws, tm):
    b, t, d = x.shape
    wa, wqk, wv, wo, wg = ws
    full = lambda w: pl.BlockSpec(w.shape, lambda i, j: (0, 0))
    row = lambda n: pl.BlockSpec((1, tm, n), lambda i, j: (i, j, 0))
    return pl.pallas_call(
        _inproj_kernel,
        out_shape=(jax.ShapeDtypeStruct((b, t, wa.shape[1]), BF16),
                   jax.ShapeDtypeStruct((b, t, wqk.shape[1]), BF16),
                   jax.ShapeDtypeStruct((b, t, wv.shape[1]), BF16),
                   jax.ShapeDtypeStruct((b, t, wo.shape[1]), BF16),
                   jax.ShapeDtypeStruct((b, t, wg.shape[1]), F32)),
        grid=(b, t // tm),
        in_specs=[row(d), pl.BlockSpec((1, 8, d), lambda i, j: (i, 0, 0)),
                  full(wa), full(wqk), full(wv), full(wo), full(wg)],
        out_specs=(row(wa.shape[1]), row(wqk.shape[1]), row(wv.shape[1]), row(wo.shape[1]),
                   row(wg.shape[1])),
        compiler_params=_cparams(("arbitrary", "arbitrary")),
        name="inproj",
    )(x, mod, wa, wqk, wv, wo, wg)


def _qkprep_kernel(x_ref, w_ref, cos_ref, sin_ref, o_ref, *, sub, halo):
    t = x_ref.shape[1]
    n_sub = t // sub
    lane = lax.broadcasted_iota(jnp.int32, (1, LANES), 1)
    first_half = (lane & 31) < 16
    scale = jnp.where(pl.program_id(1) < (ML_QK_W // LANES), ML_QK_DIM ** -0.5, 1.0).astype(F32)
    zeros = jnp.zeros((halo, LANES), x_ref.dtype)
    for s in range(n_sub):
        lo = s * sub
        top = x_ref[0, lo - halo:lo, :] if s > 0 else zeros
        bot = x_ref[0, lo + sub:lo + sub + halo, :] if s < n_sub - 1 else zeros
        ext = jnp.concatenate([top, x_ref[0, lo:lo + sub, :], bot], axis=0).astype(F32)
        n = sub + 2 * halo
        acc = jnp.zeros((sub, LANES), F32)
        for j in range(CONV_K):
            shift = (CONV_K // 2 - j) % n
            sh = ext if shift == 0 else pltpu.roll(ext, shift, axis=0)
            acc = acc + w_ref[j:j + 1, :] * sh[halo:halo + sub, :]
        y = acc * _sigmoid(acc)
        partner = jnp.where(first_half, pltpu.roll(y, LANES - 16, axis=1), pltpu.roll(y, 16, axis=1))
        y = y * cos_ref[lo:lo + sub, :] + partner * sin_ref[lo:lo + sub, :]
        o_ref[0, lo:lo + sub, :] = (y * scale).astype(o_ref.dtype)


def _qkprep(qk, conv_w, cos_t, sin_t):
    b, t, w = qk.shape
    sub = min(t, 1024)
    kern = functools.partial(_qkprep_kernel, sub=sub, halo=16)
    return pl.pallas_call(
        kern,
        out_shape=jax.ShapeDtypeStruct((b, t, w), BF16),
        grid=(b, w // LANES),
        in_specs=[pl.BlockSpec((1, t, LANES), lambda i, j: (i, 0, j)),
                  pl.BlockSpec((8, LANES), lambda i, j: (0, j)),
                  pl.BlockSpec((t, LANES), lambda i, j: (0, 0)),
                  pl.BlockSpec((t, LANES), lambda i, j: (0, 0))],
        out_specs=pl.BlockSpec((1, t, LANES), lambda i, j: (i, 0, j)),
        compiler_params=_cparams(("arbitrary", "arbitrary")),
        name="qkprep",
    )(qk, conv_w, cos_t, sin_t)


def _rope_tables(t):
    nf = ML_QK_DIM // 4
    inv = 1.0 / (ROPE_BASE ** (jnp.arange(nf, dtype=F32) / nf))
    pos = jnp.arange(t)
    ang_r = (pos // GRID_W).astype(F32)[:, None] * inv
    ang_c = (pos % GRID_W).astype(F32)[:, None] * inv
    cos = jnp.concatenate([jnp.cos(ang_r)] * 2 + [jnp.cos(ang_c)] * 2, axis=-1)
    sin = jnp.concatenate([-jnp.sin(ang_r), jnp.sin(ang_r), -jnp.sin(ang_c), jnp.sin(ang_c)], axis=-1)
    return jnp.tile(cos, (1, 2)), jnp.tile(sin, (1, 2))


def _mlstm_kernel(qf_ref, kf_ref, vf_ref, gf_ref, qb_ref, kb_ref, vb_ref, gb_ref, bg_ref,
                  c0_ref, m0_ref, hf_ref, hb_ref, cout_ref, mout_ref, c_scr, m_scr):
    step = pl.program_id(1)
    n_steps = pl.num_programs(1)
    L = ML_CHUNK

    @pl.when(step == 0)
    def _():
        c_scr[...] = c0_ref[0]
        m_scr[...] = m0_ref[0]

    row_i = lax.broadcasted_iota(jnp.int32, (L, L), 0)
    col_i = lax.broadcasted_iota(jnp.int32, (L, L), 1)
    lane = lax.broadcasted_iota(jnp.int32, (1, LANES), 1)
    tri_lo = (col_i <= row_i)
    tri_up = (col_i >= row_i)
    ones_v = jnp.ones((L, ML_V_DIM), BF16)

    for d, (q_ref, k_ref, v_ref, g_ref, h_ref) in enumerate(
            ((qf_ref, kf_ref, vf_ref, gf_ref, hf_ref), (qb_ref, kb_ref, vb_ref, gb_ref, hb_ref))):
        tri = tri_lo if d == 0 else tri_up
        g = g_ref[0] + bg_ref[...]
        logf = jnp.minimum(g, 0.0) - jnp.log(1.0 + jnp.exp(-jnp.abs(g)))
        tri_b = jnp.where(tri, 1.0, 0.0).astype(BF16)
        l1 = logf.astype(BF16)
        r1 = logf - l1.astype(F32)
        l2 = r1.astype(BF16)
        l3 = (r1 - l2.astype(F32)).astype(BF16)
        cum = _dot(tri_b, l1) + (_dot(tri_b, l2) + _dot(tri_b, l3))
        f_lo = 4 + 8 * d
        z = jnp.where((lane >= f_lo) & (lane < f_lo + ML_HEADS), cum, g)
        zt = z.T
        end = L - 1 if d == 0 else 0
        for h in range(ML_HEADS):
            li, lb = 8 * d + h, f_lo + h
            pair, half = h // 2, h % 2
            head_mask = (lane >= 64 * half) & (lane < 64 * half + 64)
            bcol = jnp.sum(jnp.where(lane == lb, z, 0.0), axis=-1, keepdims=True)
            icol = jnp.sum(jnp.where(lane == li, z, 0.0), axis=-1, keepdims=True)
            brow = zt[lb:lb + 1, :]
            irow = zt[li:li + 1, :]
            total = bcol[end:end + 1, :]
            m_prev = m_scr[d, h:h + 1, 0:1]

            qm = jnp.where(head_mask, q_ref[0, :, pair * LANES:(pair + 1) * LANES], 0).astype(BF16)
            km = jnp.where(head_mask, k_ref[0, :, pair * LANES:(pair + 1) * LANES], 0).astype(BF16)
            vext = jnp.concatenate([v_ref[0, :, h * ML_V_DIM:(h + 1) * ML_V_DIM], ones_v], axis=1)

            dmat = jnp.where(tri, bcol - brow + irow, NEG_BIG)
            m_prev_term = bcol + m_prev
            m_t = jnp.maximum(jnp.max(dmat, axis=-1, keepdims=True), m_prev_term)
            sp = _dot_nt(qm, km) * jnp.exp(dmat - m_t)
            inter = jnp.exp(m_prev_term - m_t)
            c_prev = c_scr[d, h]
            r = _dot(sp.astype(BF16), vext) + inter * _dot(qm, c_prev.astype(BF16))
            num = r[:, :ML_V_DIM]
            den = r[:, ML_V_DIM:]
            h_ref[0, :, h * ML_V_DIM:(h + 1) * ML_V_DIM] = (
                num / jnp.maximum(jnp.abs(den), jnp.exp(-m_t))).astype(h_ref.dtype)

            a = total - bcol + icol
            m_loc = jnp.max(a, axis=0, keepdims=True)
            kw = (km.astype(F32) * jnp.exp(a - m_loc)).astype(BF16)
            c_loc = _dot_tn(kw, vext)
            m_new = jnp.maximum(total + m_prev, m_loc)
            c_scr[d, h] = jnp.exp(total + m_prev - m_new) * c_prev + jnp.exp(m_loc - m_new) * c_loc
            m_scr[d, h:h + 1, :] = jnp.broadcast_to(m_new, (1, LANES))

    @pl.when(step == n_steps - 1)
    def _():
        cout_ref[0] = c_scr[...]
        mout_ref[0] = m_scr[...]


def _mlstm(qk, v, gates, bg, c0, m0):
    b, t, _ = qk.shape
    L = ML_CHUNK
    nc = t // L
    fwd = lambda n, blk: pl.BlockSpec((1, L, n), lambda i, c: (i, c, blk))
    bwd = lambda n, blk: pl.BlockSpec((1, L, n), lambda i, c: (i, nc - 1 - c, blk))
    st_c = pl.BlockSpec((1,) + c0.shape[1:], lambda i, c: (i, 0, 0, 0, 0))
    st_m = pl.BlockSpec((1,) + m0.shape[1:], lambda i, c: (i, 0, 0, 0))
    return pl.pallas_call(
        _mlstm_kernel,
        out_shape=(jax.ShapeDtypeStruct((b, t, ML_V_W), BF16),
                   jax.ShapeDtypeStruct((b, t, ML_V_W), BF16),
                   jax.ShapeDtypeStruct(c0.shape, F32),
                   jax.ShapeDtypeStruct(m0.shape, F32)),
        grid=(b, nc),
        in_specs=[fwd(ML_QK_W, 0), fwd(ML_QK_W, 1), fwd(ML_V_W, 0), fwd(LANES, 0),
                  bwd(ML_QK_W, 0), bwd(ML_QK_W, 1), bwd(ML_V_W, 0), bwd(LANES, 0),
                  pl.BlockSpec((1, LANES), lambda i, c: (0, 0)), st_c, st_m],
        out_specs=(fwd(ML_V_W, 0), bwd(ML_V_W, 0), st_c, st_m),
        scratch_shapes=[pltpu.VMEM(c0.shape[1:], F32), pltpu.VMEM(m0.shape[1:], F32)],
        compiler_params=_cparams(("arbitrary", "arbitrary")),
        name="mlstm",
    )(qk, qk, v, gates, qk, qk, v, gates, bg, c0, m0)


def _attn_kernel(case_ref, ws_ref, q_ref, k_ref, v_ref, kc_ref, vc_ref, bias_ref, o_ref):
    j = pl.program_id(2)
    nk = ATT_KROWS * GRID_W
    start = pl.multiple_of(ws_ref[j] * GRID_W, GRID_W)
    lane = lax.broadcasted_iota(jnp.int32, (1, LANES), 1)
    for pp in range(ATT_PAIRS):
        lanes = slice(pp * LANES, (pp + 1) * LANES)
        q = q_ref[0, :, lanes]
        k = k_ref[0, pl.ds(start, nk), lanes]
        v = v_ref[0, pl.ds(start, nk), lanes]
        kc = kc_ref[0, :, lanes]
        vc = vc_ref[0, :, lanes]
        acc = jnp.zeros(q.shape, F32)
        for h in range(2):
            head_mask = (lane >= NA_HEAD_DIM * h) & (lane < NA_HEAD_DIM * (h + 1))
            qh = (jnp.where(head_mask, q, 0).astype(F32) * (NA_HEAD_DIM ** -0.5 * LOG2E)).astype(BF16)
            s = _dot_nt(qh, k) + bias_ref[0, 2 * pp + h]
            sc = _dot_nt(qh, kc)
            m = jnp.maximum(jnp.max(s, axis=-1, keepdims=True), jnp.max(sc, axis=-1, keepdims=True))
            p = jnp.exp2(s - m)
            pc = jnp.exp2(sc - m)
            vh = jnp.where(head_mask, v, 1).astype(BF16)
            vch = jnp.where(head_mask, vc, 1).astype(BF16)
            o = _dot(p.astype(BF16), vh) + _dot(pc.astype(BF16), vch)
            acc = acc + jnp.where(head_mask, o / pltpu.roll(o, NA_HEAD_DIM, axis=1), 0.0)
        o_ref[0, :, lanes] = acc.astype(o_ref.dtype)


def _attn_plan(rows):
    kh = min(NA_WIN_H, rows)
    nj = rows // ATT_ROWS
    rs = lambda r: int(np.clip(r - kh // 2, 0, rows - kh))
    ws = [int(np.clip(ATT_ROWS * j - kh // 2, 0, rows - ATT_KROWS)) for j in range(nj)]
    sigs, case = [], []
    for j in range(nj):
        r0 = ATT_ROWS * j
        sig = (ws[j] - r0,) + tuple(rs(r0 + a) - r0 for a in range(ATT_ROWS))
        if sig not in sigs:
            sigs.append(sig)
        case.append(sigs.index(sig))
    return np.asarray(ws, np.int32), np.asarray(case, np.int32), sigs, kh


def _attn_bias(bias_table, sigs, kh):
    col_start = np.clip(np.arange(GRID_W) - NA_WIN_W // 2, 0, GRID_W - NA_WIN_W)
    c = np.arange(GRID_W)
    cidx = c[None, :] - c[:, None] + (NA_WIN_W - 1)
    col_ok = (c[None, :] >= col_start[:, None]) & (c[None, :] < col_start[:, None] + NA_WIN_W)
    expand = (np.arange(2 * NA_WIN_W - 1)[:, None, None] == cidx[None]).astype(np.float32)
    out = []
    for sig in sigs:
        wsr, rsr = sig[0], np.asarray(sig[1:])
        a = np.arange(ATT_ROWS)[:, None]
        rk = wsr + np.arange(ATT_KROWS)[None, :]
        row_ok = (rk >= rsr[:, None]) & (rk < rsr[:, None] + kh)
        ridx = np.clip(rk - a + (NA_WIN_H - 1), 0, 2 * NA_WIN_H - 2)
        rows = bias_table[:, ridx, :]
        full = jnp.einsum('haiv,vqk->haqik', rows, jnp.asarray(expand), precision=HIGHEST)
        ok = row_ok[:, None, :, None] & col_ok[None, :, None, :]
        full = jnp.where(ok[None], full * LOG2E, NEG_BIG)
        out.append(full.reshape(full.shape[0], ATT_ROWS * GRID_W, ATT_KROWS * GRID_W))
    return jnp.stack(out).astype(F32)


def _attn(a_lat, a_ctx, bias_table):
    b, t, _ = a_lat.shape
    lc = a_ctx.shape[1]
    rows = t // GRID_W
    ws, case, sigs, kh = _attn_plan(rows)
    bias = _attn_bias(bias_table, sigs, kh)
    tq = ATT_ROWS * GRID_W
    nk = ATT_KROWS * GRID_W
    bw = ATT_PAIRS * LANES
    n_blk = NA_W // bw
    grid_spec = pltpu.PrefetchScalarGridSpec(
        num_scalar_prefetch=2,
        grid=(n_blk, b, rows // ATT_ROWS),
        in_specs=[pl.BlockSpec((1, tq, bw), lambda p, i, j, cs, w: (i, j, p)),
                  pl.BlockSpec((1, t, bw), lambda p, i, j, cs, w: (i, 0, n_blk + p)),
                  pl.BlockSpec((1, t, bw), lambda p, i, j, cs, w: (i, 0, 2 * n_blk + p)),
                  pl.BlockSpec((1, lc, bw), lambda p, i, j, cs, w: (i, 0, n_blk + p)),
                  pl.BlockSpec((1, lc, bw), lambda p, i, j, cs, w: (i, 0, 2 * n_blk + p)),
                  pl.BlockSpec((1, 2 * ATT_PAIRS, tq, nk), lambda p, i, j, cs, w: (cs[j], p, 0, 0))],
        out_specs=pl.BlockSpec((1, tq, bw), lambda p, i, j, cs, w: (i, j, p)),
    )
    return pl.pallas_call(
        _attn_kernel,
        out_shape=jax.ShapeDtypeStruct((b, t, NA_W), BF16),
        grid_spec=grid_spec,
        compiler_params=_cparams(("arbitrary", "arbitrary", "arbitrary")),
        name="nattn",
    )(jnp.asarray(case), jnp.asarray(ws), a_lat, a_lat, a_lat, a_ctx, a_ctx, bias)


def _outproj_kernel(att_ref, hf_ref, hb_ref, om_ref, x_ref, mod_ref, ng_ref, wa_ref, wm_ref,
                    lng_ref, lnb_ref, wrh_ref, wrl_ref, x1_ref, hm_ref, aff_ref, affr_ref):
    h = hf_ref[0].astype(F32) + hb_ref[0].astype(F32)
    parts = []
    for hd in range(ML_HEADS):
        hh = h[:, hd * ML_V_DIM:(hd + 1) * ML_V_DIM]
        mu = jnp.mean(hh, axis=-1, keepdims=True)
        var = jnp.mean(jnp.square(hh - mu), axis=-1, keepdims=True)
        parts.append((hh - mu) * lax.rsqrt(var + LN_EPS))
    hn = jnp.concatenate(parts, axis=1) * ng_ref[...]
    ml = (hn * _sigmoid(om_ref[0].astype(F32))).astype(BF16)
    mix = _dot(att_ref[0], wa_ref[...]) + _dot(ml, wm_ref[...])
    y = DEEPNORM_ALPHA * x_ref[0] + mod_ref[0, 2:3, :] * mix
    mu = jnp.mean(y, axis=-1, keepdims=True)
    var = jnp.mean(jnp.square(y - mu), axis=-1, keepdims=True)
    x1 = (y - mu) * lax.rsqrt(var + LN_EPS) * lng_ref[...] + lnb_ref[...]
    x1_ref[0] = x1
    hm = x1 * (1.0 + mod_ref[0, 4:5, :]) + mod_ref[0, 3:4, :]
    h_hi = hm.astype(BF16)
    h_lo = (hm - h_hi.astype(F32)).astype(BF16)
    bits = pltpu.bitcast(h_hi.astype(F32), jnp.uint32)
    half = bits.shape[1] // 2
    word = (bits[:, :half] >> 16) | (bits[:, half:] & jnp.uint32(0xFFFF0000))
    hm_ref[0] = pltpu.bitcast(word, jnp.int32)
    logits = _dot(h_hi, wrh_ref[...]) + (_dot(h_lo, wrh_ref[...]) + _dot(h_hi, wrl_ref[...]))
    lane = lax.broadcasted_iota(jnp.int32, (1, LANES), 1)
    logits = jnp.where(lane < N_EXPERTS, logits, NEG_BIG)
    e = jnp.exp(logits - jnp.max(logits, axis=-1, keepdims=True))
    aff = e / jnp.sum(e, axis=-1, keepdims=True)
    affr_ref[0] = aff
    aff_ref[0] = aff.T[:N_EXPERTS, :]


def _outproj(att, hf, hb, om, x, mod, ng, w_att, w_ml, lng, lnb, wr_hi, wr_lo, tm):
    b, t, d = x.shape
    row = lambda n: pl.BlockSpec((1, tm, n), lambda i, j: (i, j, 0))
    full = lambda w: pl.BlockSpec(w.shape, lambda i, j: (0,) * w.ndim)
    return pl.pallas_call(
        _outproj_kernel,
        out_shape=(jax.ShapeDtypeStruct((b, t, d), F32),
                   jax.ShapeDtypeStruct((b, t, d // 2), jnp.int32),
                   jax.ShapeDtypeStruct((b, N_EXPERTS, t), F32),
                   jax.ShapeDtypeStruct((b, t, LANES), F32)),
        grid=(b, t // tm),
        in_specs=[row(NA_W), row(ML_V_W), row(ML_V_W), row(ML_V_W), row(d),
                  pl.BlockSpec((1, 8, d), lambda i, j: (i, 0, 0)),
                  full(ng), full(w_att), full(w_ml), full(lng), full(lnb), full(wr_hi), full(wr_lo)],
        out_specs=(row(d), row(d // 2), pl.BlockSpec((1, N_EXPERTS, tm), lambda i, j: (i, 0, j)), row(LANES)),
        compiler_params=_cparams(("arbitrary", "arbitrary")),
        name="outproj",
    )(att, hf, hb, om, x, mod, ng, w_att, w_ml, lng, lnb, wr_hi, wr_lo)


UNSELECTED = -1e6


def _select_kernel(aff_ref, idx_ref, pos_ref, off_ref, cum_scr, sel_scr, offs_v, offs_s, dsem, *, cap):
    t = aff_ref.shape[2]
    nb = t // LANES
    lane = lax.broadcasted_iota(jnp.int32, (1, LANES), 1)
    keys = lambda: pltpu.bitcast(aff_ref[0], jnp.int32)

    def count(mask):
        return jnp.sum(jnp.where(mask, 1.0, 0.0), axis=-1, keepdims=True)

    def search(_, c):
        lo, hi = c
        mid = lo + jnp.right_shift(hi - lo, 1)
        ge = count(keys() >= mid) >= cap
        return jnp.where(ge, mid, lo), jnp.where(ge, hi, mid)

    lo0 = jnp.zeros((N_EXPERTS, 1), jnp.int32)
    hi0 = jnp.full((N_EXPERTS, 1), 0x7F800000, jnp.int32)
    thr, _ = lax.fori_loop(0, 31, search, (lo0, hi0))

    r_i = lax.broadcasted_iota(jnp.int32, (LANES, LANES), 0)
    c_i = lax.broadcasted_iota(jnp.int32, (LANES, LANES), 1)
    strict = jnp.where(r_i < c_i, 1.0, 0.0).astype(BF16)
    tr_i = lax.broadcasted_iota(jnp.int32, (t, LANES), 0)
    tc_i = lax.broadcasted_iota(jnp.int32, (t, LANES), 1)
    block_ind = jnp.where(jnp.right_shift(tr_i, 7) == tc_i, 1.0, 0.0).astype(BF16)

    def prefix(x01):
        xb = x01.astype(BF16)
        offs = _dot(_dot(xb, block_ind).astype(BF16), strict)
        for j in range(nb):
            off_j = jnp.sum(jnp.where(lane == j, offs, 0.0), axis=-1, keepdims=True)
            cum_scr[:, j * LANES:(j + 1) * LANES] = _dot(xb[:, j * LANES:(j + 1) * LANES], strict) + off_j
        return offs

    k = keys()
    gt = k > thr
    eq = k == thr
    need = cap - count(gt)
    prefix(jnp.where(eq, 1.0, 0.0))
    sel = jnp.where(gt | (eq & (cum_scr[...] < need)), 1.0, 0.0)
    sel_scr[...] = sel
    offs = prefix(sel)

    pad = jnp.zeros((LANES - N_EXPERTS, LANES), F32)
    for j in range(nb):
        blk = jnp.where(sel_scr[:, j * LANES:(j + 1) * LANES] > 0.0, cum_scr[:, j * LANES:(j + 1) * LANES], UNSELECTED)
        pos_ref[0, j * LANES:(j + 1) * LANES, :] = jnp.concatenate([blk, pad], axis=0).T
    off_ref[0] = jnp.concatenate([offs, pad], axis=0).T

    cum_scr[...] = cum_scr[...] + sel_scr[...]
    idx_ref[...] = jnp.zeros(idx_ref.shape, idx_ref.dtype)
    offs_v[...] = offs.astype(jnp.int32)
    to_smem = pltpu.make_async_copy(offs_v, offs_s, dsem)
    to_smem.start()
    to_smem.wait()
    sub = lax.broadcasted_iota(jnp.int32, (LANES, LANES), 0).astype(F32)
    for e in range(N_EXPERTS):
        def group(pg, carry, e=e):
            first = jnp.asarray(pg * LANES, jnp.int32)
            jlo, jhi = carry

            def advance(j0, bound, ahead):
                look = lambda j: offs_s[e, jnp.minimum(j + ahead, LANES - 1)]
                j, _ = lax.while_loop(lambda c: (c[0] < nb) & (c[1] <= bound),
                                      lambda c: (c[0] + 1, look(c[0] + 1)), (j0, look(j0)))
                return j

            jlo = advance(jlo, first, 1)
            jhi = advance(jhi, first + (LANES - 1), 0)
            slots = first.astype(F32) + sub

            def block(jb, acc):
                c = cum_scr[pl.ds(e, 1), pl.ds(pl.multiple_of(jb * LANES, LANES), LANES)]
                return acc + jnp.where(jnp.broadcast_to(c, (LANES, LANES)) <= slots, 1.0, 0.0)

            acc = lax.fori_loop(jlo, jhi, block, jnp.zeros((LANES, LANES), F32))
            col = jnp.sum(acc, axis=-1, keepdims=True).astype(jnp.int32) + jlo * LANES
            idx_ref[0, pl.ds(pl.multiple_of(pg * LANES, LANES), LANES), e:e + 1] = col
            return jlo, jhi

        lax.fori_loop(0, cap // LANES, group, (jnp.int32(0), jnp.int32(0)))


def _select(aff_t, cap):
    b, e, t = aff_t.shape
    assert cap % LANES == 0 and t % LANES == 0 and t // LANES < LANES
    return pl.pallas_call(
        functools.partial(_select_kernel, cap=cap),
        out_shape=(jax.ShapeDtypeStruct((b, cap, LANES), jnp.int32),
                   jax.ShapeDtypeStruct((b, t, LANES), F32),
                   jax.ShapeDtypeStruct((b, LANES, LANES), F32)),
        grid=(b,),
        in_specs=[pl.BlockSpec((1, e, t), lambda i: (i, 0, 0))],
        out_specs=(pl.BlockSpec((1, cap, LANES), lambda i: (i, 0, 0)),
                   pl.BlockSpec((1, t, LANES), lambda i: (i, 0, 0)),
                   pl.BlockSpec((1, LANES, LANES), lambda i: (i, 0, 0))),
        scratch_shapes=[pltpu.VMEM((e, t), F32), pltpu.VMEM((e, t), F32),
                        pltpu.VMEM((e, LANES), jnp.int32), pltpu.SMEM((e, LANES), jnp.int32),
                        pltpu.SemaphoreType.DMA(())],
        compiler_params=_cparams(("arbitrary",)),
        name="select",
    )(aff_t)


SC_CORES = 2
SC_SUBCORES = 16
SC_CHUNK = 64


def _gather_rows(table, idx):
    n = idx.shape[0]
    v, d = table.shape
    n_workers = SC_CORES * SC_SUBCORES
    per_w = n // n_workers
    assert n % (n_workers * SC_CHUNK) == 0 and d % LANES == 0 and table.dtype.itemsize == 4
    mesh = plsc.VectorSubcoreMesh(core_axis_name="c", subcore_axis_name="s",
                                  num_cores=SC_CORES, num_subcores=SC_SUBCORES)

    @functools.partial(
        pl.kernel, mesh=mesh,
        out_type=jax.ShapeDtypeStruct((n, d), table.dtype),
        scratch_types=[pltpu.VMEM((SC_CHUNK,), jnp.int32),
                       pltpu.VMEM((SC_CHUNK, d), table.dtype),
                       pltpu.SemaphoreType.DMA],
        name="row_gather")
    def gather(table_hbm, idx_hbm, out_hbm, idx_v, rows_v, sem):
        base = (lax.axis_index("s") * SC_CORES + lax.axis_index("c")) * per_w

        @pl.loop(0, per_w, step=SC_CHUNK)
        def _(o):
            pltpu.sync_copy(idx_hbm.at[pl.ds(base + o, SC_CHUNK)], idx_v)
            pltpu.async_copy(table_hbm.at[idx_v], rows_v, sem).wait()
            pltpu.sync_copy(rows_v, out_hbm.at[pl.ds(base + o, SC_CHUNK)])

    return gather(table, idx)


def _moe_kernel(x_ref, wg_ref, wu_ref, wd_ref, o_ref, x_scr, act_scr, *, tf):
    w = pltpu.bitcast(x_ref[0], jnp.uint32)
    half = w.shape[1]
    x_scr[:, :half] = pltpu.bitcast(w << 16, F32).astype(BF16)
    x_scr[:, half:] = pltpu.bitcast(w & jnp.uint32(0xFFFF0000), F32).astype(BF16)
    x = x_scr[...]
    for c in range(act_scr.shape[1] // tf):
        cols = slice(c * tf, (c + 1) * tf)
        hg = _dot(x, wg_ref[0, :, cols])
        hu = _dot(x, wu_ref[0, :, cols])
        act_scr[:, cols] = (hg * _sigmoid(hg) * hu).astype(BF16)
    o_ref[0] = _dot(act_scr[...], wd_ref[0]).astype(o_ref.dtype)


def _moe(xe, w_gate, w_up, w_down, tm, tf):
    e, m, dw = xe.shape
    d = 2 * dw
    dff = w_gate.shape[2]
    return pl.pallas_call(
        functools.partial(_moe_kernel, tf=tf),
        out_shape=jax.ShapeDtypeStruct((e, m, d), BF16),
        grid=(e, m // tm),
        in_specs=[pl.BlockSpec((1, tm, dw), lambda i, j: (i, j, 0)),
                  pl.BlockSpec((1, d, dff), lambda i, j: (i, 0, 0)),
                  pl.BlockSpec((1, d, dff), lambda i, j: (i, 0, 0)),
                  pl.BlockSpec((1, dff, d), lambda i, j: (i, 0, 0))],
        out_specs=pl.BlockSpec((1, tm, d), lambda i, j: (i, j, 0)),
        scratch_shapes=[pltpu.VMEM((tm, d), BF16), pltpu.VMEM((tm, dff), BF16)],
        compiler_params=_cparams(("arbitrary", "arbitrary")),
        name="moe",
    )(xe, w_gate, w_up, w_down)


CMB_TM = 256
CMB_WIN = 64
ROW_ALIGN = 16


def _combine_kernel(p0_ref, x1_ref, pos_ref, aff_ref, mod_ref, lng_ref, lnb_ref, y_hbm, o_ref,
                    ybuf, yextra, acc_ref, sem, sem_x, *, cap):
    b = pl.program_id(0)
    j = pl.program_id(1)
    nt = pl.num_programs(1)
    g = b * nt + j
    n_steps = pl.num_programs(0) * nt
    slot = lax.rem(g, 2)
    m_rows = y_hbm.shape[1]
    win = CMB_WIN
    kdim = N_EXPERTS * win

    def window(bb, jj, r):
        base = (bb * (nt + 1) + jj) * N_EXPERTS
        los, rows = [], []
        for e in range(N_EXPERTS):
            lo = (p0_ref[base + e] // ROW_ALIGN) * ROW_ALIGN + r * win
            los.append(lo)
            rows.append(jnp.minimum(bb * cap + lo, m_rows - win))
        return los, rows

    def copies(rows, buf, sems):
        return [pltpu.make_async_copy(y_hbm.at[e, pl.ds(pl.multiple_of(rows[e], ROW_ALIGN), win), :],
                                      buf.at[pl.ds(e * win, win), :], sems.at[e])
                for e in range(N_EXPERTS)]

    los0, rows0 = window(b, j, 0)

    @pl.when(g == 0)
    def _():
        for c in copies(rows0, ybuf.at[0], sem.at[0]):
            c.start()

    @pl.when(g + 1 < n_steps)
    def _():
        wrap = j + 1 == nt
        _, rows_n = window(jnp.where(wrap, b + 1, b), jnp.where(wrap, 0, j + 1), 0)
        for c in copies(rows_n, ybuf.at[1 - slot], sem.at[1 - slot]):
            c.start()

    lane = lax.broadcasted_iota(jnp.int32, (1, LANES), 1)
    e_i = lax.broadcasted_iota(jnp.int32, (LANES, kdim), 0)
    k_i = lax.broadcasted_iota(jnp.int32, (LANES, kdim), 1)
    expand = jnp.where(k_i // win == e_i, 1.0, 0.0).astype(BF16)
    col_in_win = (lax.broadcasted_iota(jnp.int32, (1, kdim), 1) % win).astype(F32)
    aff_cols = _dot(aff_ref[0].astype(BF16), expand)

    def lane_vec(vals):
        v = jnp.zeros((1, LANES), F32)
        for e in range(N_EXPERTS):
            v = jnp.where(lane == e, jnp.asarray(vals[e]).astype(F32), v)
        return v

    def expand_rows(los, rows, buf):
        rel = pos_ref[0] - lane_vec(los)
        shift = lane_vec([b * cap + lo - row for lo, row in zip(los, rows)])
        tgt = jnp.where((rel >= 0.0) & (rel < float(win)), rel + shift, -1.0)
        tgt_cols = _dot(tgt.astype(BF16), expand)
        s = jnp.where(tgt_cols == col_in_win, aff_cols, 0.0).astype(BF16)
        return _dot(s, buf[...])

    for c in copies(rows0, ybuf.at[slot], sem.at[slot]):
        c.wait()
    acc_ref[...] = expand_rows(los0, rows0, ybuf.at[slot])

    base = (b * (nt + 1) + j) * N_EXPERTS
    rounds = jnp.int32(0)
    for e in range(N_EXPERTS):
        span = p0_ref[base + N_EXPERTS + e] - (p0_ref[base + e] // ROW_ALIGN) * ROW_ALIGN
        rounds = jnp.maximum(rounds, (span + win - 1) // win)

    def extra(r, carry):
        los, rows = window(b, j, r)
        cps = copies(rows, yextra, sem_x)
        for c in cps:
            c.start()
        for c in cps:
            c.wait()
        acc_ref[...] += expand_rows(los, rows, yextra)
        return carry

    lax.fori_loop(1, rounds, extra, 0)

    y = DEEPNORM_ALPHA * x1_ref[0] + mod_ref[0, 5:6, :] * acc_ref[...]
    mu = jnp.mean(y, axis=-1, keepdims=True)
    var = jnp.mean(jnp.square(y - mu), axis=-1, keepdims=True)
    o_ref[0] = (y - mu) * lax.rsqrt(var + LN_EPS) * lng_ref[...] + lnb_ref[...]


def _combine(p0tab, x1, pos_t, aff_r, mod, lng, lnb, y, cap):
    b, t, d = x1.shape
    tm = CMB_TM
    kdim = N_EXPERTS * CMB_WIN
    row = lambda n: pl.BlockSpec((1, tm, n), lambda i, j, p: (i, j, 0))
    vec = pl.BlockSpec((1, d), lambda i, j, p: (0, 0))
    grid_spec = pltpu.PrefetchScalarGridSpec(
        num_scalar_prefetch=1,
        grid=(b, t // tm),
        in_specs=[row(d), row(LANES), row(LANES), pl.BlockSpec((1, 8, d), lambda i, j, p: (i, 0, 0)), vec, vec,
                  pl.BlockSpec(memory_space=pl.ANY)],
        out_specs=row(d),
        scratch_shapes=[pltpu.VMEM((2, kdim, d), y.dtype), pltpu.VMEM((kdim, d), y.dtype),
                        pltpu.VMEM((tm, d), F32),
                        pltpu.SemaphoreType.DMA((2, N_EXPERTS)), pltpu.SemaphoreType.DMA((N_EXPERTS,))],
    )
    return pl.pallas_call(
        functools.partial(_combine_kernel, cap=cap),
        out_shape=jax.ShapeDtypeStruct((b, t, d), F32),
        grid_spec=grid_spec,
        compiler_params=_cparams(("arbitrary", "arbitrary")),
        name="combine",
    )(p0tab, x1, pos_t, aff_r, mod, lng, lnb, y)


def _split_w_in(w_in):
    a_end = 3 * NA_W
    qk_end = a_end + 2 * ML_QK_W
    v_end = qk_end + ML_V_W
    o_end = v_end + ML_V_W
    wg = jnp.pad(w_in[:, o_end:], ((0, 0), (0, LANES - ML_N_GATES)))
    return tuple(w.astype(BF16) for w in
                 (w_in[:, :a_end], w_in[:, a_end:qk_end], w_in[:, qk_end:v_end], w_in[:, v_end:o_end], wg))


def _layer(x, ctx, c, c_ctx, w_ada, b_ada, w_in, b_gate, conv_qk, na_rel_bias, ml_norm_g, w_out,
           ln1_g, ln1_b, w_router, w_expert_gate, w_expert_up, w_expert_down, ln2_g, ln2_b):
    b, t, d = x.shape
    lc = ctx.shape[1]

    cs = jnp.zeros((16, d), F32).at[:b].set(c).at[b].set(c_ctx)
    m = _ada(cs, w_ada, b_ada)
    mod = jnp.pad(m[:b].reshape(b, 6, d), ((0, 0), (0, 2), (0, 0)))
    mod_ctx = jnp.broadcast_to(jnp.pad(m[b].reshape(6, d), ((0, 2), (0, 0)))[None], (b, 8, d))

    ws = _split_w_in(w_in)
    a_lat, qk_lat, v_lat, o_lat, g_lat = _inproj(x, mod, ws, tm=512)
    a_ctx, qk_ctx, v_ctx, _, g_ctx = _inproj(ctx, mod_ctx, ws, tm=lc)

    conv_w = jnp.pad(conv_qk, ((0, 8 - CONV_K), (0, 0)))
    cos_t, sin_t = _rope_tables(t)
    qk_lat = _qkprep(qk_lat, conv_w, cos_t, sin_t)
    qk_ctx = _qkprep(qk_ctx, conv_w, jnp.ones((lc, LANES), F32), jnp.zeros((lc, LANES), F32))

    bg = jnp.pad(b_gate, (0, LANES - ML_N_GATES)).reshape(1, LANES)
    c0 = jnp.zeros((b, 2, ML_HEADS, LANES, 2 * ML_V_DIM), F32)
    m0 = jnp.zeros((b, 2, 8, LANES), F32)
    _, _, c1, m1 = _mlstm(qk_ctx, v_ctx, g_ctx, bg, c0, m0)
    hf, hb, _, _ = _mlstm(qk_lat, v_lat, g_lat, bg, c1, m1)

    att = _attn(a_lat, a_ctx, na_rel_bias)

    wr = jnp.pad(w_router, ((0, 0), (0, LANES - N_EXPERTS)))
    wr_hi = wr.astype(BF16)
    wr_lo = (wr - wr_hi.astype(F32)).astype(BF16)
    x1, hm, aff_t, aff_r = _outproj(att, hf, hb, o_lat, x, mod, ml_norm_g.reshape(1, -1),
                                    w_out[:NA_W].astype(BF16), w_out[NA_W:].astype(BF16),
                                    ln1_g.reshape(1, d), ln1_b.reshape(1, d), wr_hi, wr_lo, tm=256)

    cap = CAPACITY_FACTOR * t // N_EXPERTS
    idx_t, pos_t, off_t = _select(aff_t, cap)
    rows = jnp.transpose(idx_t[:, :, :N_EXPERTS], (2, 0, 1)) + (jnp.arange(b, dtype=jnp.int32) * t)[None, :, None]
    xe = _gather_rows(hm.reshape(b * t, d // 2), rows.reshape(-1)).reshape(N_EXPERTS, b * cap, d // 2)
    y = _moe(xe, w_expert_gate.astype(BF16), w_expert_up.astype(BF16), w_expert_down.astype(BF16),
             tm=min(512, b * cap), tf=256)

    nt = t // CMB_TM
    p0 = off_t[:, 0:nt * (CMB_TM // LANES):CMB_TM // LANES, :N_EXPERTS]
    p0 = jnp.concatenate([p0, jnp.full((b, 1, N_EXPERTS), cap, F32)], axis=1)
    p0tab = p0.astype(jnp.int32).reshape(-1)
    return _combine(p0tab, x1, pos_t, aff_r, mod, ln2_g.reshape(1, d), ln2_b.reshape(1, d), y, cap)


def kernel(x, c, ctx, c_ctx, w_ada, b_ada, w_in, b_gate, conv_qk, na_rel_bias, ml_norm_g, w_out,
           ln1_g, ln1_b, w_router, w_expert_gate, w_expert_up, w_expert_down, ln2_g, ln2_b):
    return _layer(x, ctx, c, c_ctx, w_ada[0], b_ada[0], w_in[0], b_gate[0], conv_qk[0], na_rel_bias[0],
                  ml_norm_g[0], w_out[0], ln1_g[0], ln1_b[0], w_router[0], w_expert_gate[0],
                  w_expert_up[0], w_expert_down[0], ln2_g[0], ln2_b[0])
```

```python
import functools

import numpy as np
import jax
import jax.numpy as jnp
from jax import lax
from jax.experimental import pallas as pl
from jax.experimental.pallas import tpu as pltpu
from jax.experimental.pallas import tpu_sc as plsc

F32 = jnp.float32
BF16 = jnp.bfloat16
HIGHEST = lax.Precision.HIGHEST

GRID_W = 64
NA_HEADS = 8
NA_HEAD_DIM = 64
NA_WIN_H = 8
NA_WIN_W = 16
NA_W = NA_HEADS * NA_HEAD_DIM
ML_HEADS = 4
ML_QK_DIM = 64
ML_V_DIM = 128
ML_QK_W = ML_HEADS * ML_QK_DIM
ML_V_W = ML_HEADS * ML_V_DIM
ML_CHUNK = 256
ML_N_GATES = 4 * ML_HEADS
CONV_K = 5
ROPE_BASE = 10000.0
N_EXPERTS = 16
CAPACITY_FACTOR = 2
LN_EPS = 1e-5
DEPTH = 1
DEEPNORM_ALPHA = (2.0 * DEPTH) ** 0.25

LANES = 128
ROW_ALIGN = 16
NEG_BIG = -1e30
LOG2E = 1.4426950408889634
VMEM_LIMIT = 56 * 1024 * 1024

ATT_ROWS = 4
ATT_PAIRS = 2
ATT_KROWS = ATT_ROWS + NA_WIN_H - 1


def _cparams(sem):
    return pltpu.CompilerParams(dimension_semantics=sem, vmem_limit_bytes=VMEM_LIMIT)


def _sigmoid(x):
    return 1.0 / (1.0 + jnp.exp(-x))


def _dot(a, b):
    return jnp.dot(a, b, preferred_element_type=F32)


def _dot_nt(a, b):
    return lax.dot_general(a, b, (((1,), (1,)), ((), ())), preferred_element_type=F32)


def _dot_tn(a, b):
    return lax.dot_general(a, b, (((0,), (0,)), ((), ())), preferred_element_type=F32)


def _ada_kernel(c_ref, w_ref, b_ref, o_ref):
    c = c_ref[...]
    s = c * _sigmoid(c)
    o_ref[...] = jnp.dot(s, w_ref[...], precision=HIGHEST, preferred_element_type=F32) + b_ref[...]


def _ada(cs, w_ada, b_ada):
    rows, d = cs.shape
    n = w_ada.shape[1]
    tn = 512
    return pl.pallas_call(
        _ada_kernel,
        out_shape=jax.ShapeDtypeStruct((rows, n), F32),
        grid=(n // tn,),
        in_specs=[pl.BlockSpec((rows, d), lambda j: (0, 0)),
                  pl.BlockSpec((d, tn), lambda j: (0, j)),
                  pl.BlockSpec((1, tn), lambda j: (0, j))],
        out_specs=pl.BlockSpec((rows, tn), lambda j: (0, j)),
        compiler_params=_cparams(("arbitrary",)),
        name="ada",
    )(cs, w_ada, b_ada.reshape(1, n))


def _inproj_kernel(x_ref, mod_ref, wa_ref, wqk_ref, wv_ref, wo_ref, wg_ref,
                   a_ref, qk_ref, v_ref, o_ref, g_ref):
    xm = (x_ref[0] * (1.0 + mod_ref[0, 1:2, :]) + mod_ref[0, 0:1, :]).astype(BF16)
    a_ref[0] = _dot(xm, wa_ref[...]).astype(BF16)
    qk_ref[0] = _dot(xm, wqk_ref[...]).astype(BF16)
    v_ref[0] = _dot(xm, wv_ref[...]).astype(BF16)
    o_ref[0] = _dot(xm, wo_ref[...]).astype(BF16)
    g_ref[0] = _dot(xm, wg_ref[...])


def _inproj(x, mod, ws, tm):
    b, t, d = x.shape
    wa, wqk, wv, wo, wg = ws
    full = lambda w: pl.BlockSpec(w.shape, lambda i, j: (0, 0))
    row = lambda n: pl.BlockSpec((1, tm, n), lambda i, j: (i, j, 0))
    return pl.pallas_call(
        _inproj_kernel,
        out_shape=(jax.ShapeDtypeStruct((b, t, wa.shape[1]), BF16),
                   jax.ShapeDtypeStruct((b, t, wqk.shape[1]), BF16),
                   jax.ShapeDtypeStruct((b, t, wv.shape[1]), BF16),
                   jax.ShapeDtypeStruct((b, t, wo.shape[1]), BF16),
                   jax.ShapeDtypeStruct((b, t, wg.shape[1]), F32)),
        grid=(b, t // tm),
        in_specs=[row(d), pl.BlockSpec((1, 8, d), lambda i, j: (i, 0, 0)),
                  full(wa), full(wqk), full(wv), full(wo), full(wg)],
        out_specs=(row(wa.shape[1]), row(wqk.shape[1]), row(wv.shape[1]), row(wo.shape[1]),
                   row(wg.shape[1])),
        compiler_params=_cparams(("arbitrary", "arbitrary")),
        name="inproj",
    )(x, mod, wa, wqk, wv, wo, wg)


def _qkprep_kernel(x_ref, w_ref, cos_ref, sin_ref, o_ref, *, sub, halo):
    t = x_ref.shape[1]
    n_sub = t // sub
    lane = lax.broadcasted_iota(jnp.int32, (1, LANES), 1)
    first_half = (lane & 31) < 16
    scale = jnp.where(pl.program_id(1) < (ML_QK_W // LANES), ML_QK_DIM ** -0.5, 1.0).astype(F32)
    zeros = jnp.zeros((halo, LANES), x_ref.dtype)
    for s in range(n_sub):
        lo = s * sub
        top = x_ref[0, lo - halo:lo, :] if s > 0 else zeros
        bot = x_ref[0, lo + sub:lo + sub + halo, :] if s < n_sub - 1 else zeros
        ext = jnp.concatenate([top, x_ref[0, lo:lo + sub, :], bot], axis=0).astype(F32)
        n = sub + 2 * halo
        acc = jnp.zeros((sub, LANES), F32)
        for j in range(CONV_K):
            shift = (CONV_K // 2 - j) % n
            sh = ext if shift == 0 else pltpu.roll(ext, shift, axis=0)
            acc = acc + w_ref[j:j + 1, :] * sh[halo:halo + sub, :]
        y = acc * _sigmoid(acc)
        partner = jnp.where(first_half, pltpu.roll(y, LANES - 16, axis=1), pltpu.roll(y, 16, axis=1))
        y = y * cos_ref[lo:lo + sub, :] + partner * sin_ref[lo:lo + sub, :]
        o_ref[0, lo:lo + sub, :] = (y * scale).astype(o_ref.dtype)


def _qkprep(qk, conv_w, cos_t, sin_t):
    b, t, w = qk.shape
    sub = min(t, 1024)
    kern = functools.partial(_qkprep_kernel, sub=sub, halo=16)
    return pl.pallas_call(
        kern,
        out_shape=jax.ShapeDtypeStruct((b, t, w), BF16),
        grid=(b, w // LANES),
        in_specs=[pl.BlockSpec((1, t, LANES), lambda i, j: (i, 0, j)),
                  pl.BlockSpec((8, LANES), lambda i, j: (0, j)),
                  pl.BlockSpec((t, LANES), lambda i, j: (0, 0)),
                  pl.BlockSpec((t, LANES), lambda i, j: (0, 0))],
        out_specs=pl.BlockSpec((1, t, LANES), lambda i, j: (i, 0, j)),
        compiler_params=_cparams(("arbitrary", "arbitrary")),
        name="qkprep",
    )(qk, conv_w, cos_t, sin_t)


def _rope_tables(t):
    nf = ML_QK_DIM // 4
    inv = 1.0 / (ROPE_BASE ** (jnp.arange(nf, dtype=F32) / nf))
    pos = jnp.arange(t)
    ang_r = (pos // GRID_W).astype(F32)[:, None] * inv
    ang_c = (pos % GRID_W).astype(F32)[:, None] * inv
    cos = jnp.concatenate([jnp.cos(ang_r)] * 2 + [jnp.cos(ang_c)] * 2, axis=-1)
    sin = jnp.concatenate([-jnp.sin(ang_r), jnp.sin(ang_r), -jnp.sin(ang_c), jnp.sin(ang_c)], axis=-1)
    return jnp.tile(cos, (1, 2)), jnp.tile(sin, (1, 2))


def _ride_along_casts(rest, n_cast, n_out):
    outs = rest[n_cast:n_cast + n_out]
    for src, dst in zip(rest[:n_cast], rest[n_cast + n_out:2 * n_cast + n_out]):
        dst[...] = src[...].astype(dst.dtype)
    return outs, rest[2 * n_cast + n_out:]


def _cast_specs(casts, n_steps, index_map):
    specs = []
    for a in casts:
        rows = a.shape[0] // n_steps
        assert a.shape[0] % n_steps == 0 and rows % ROW_ALIGN == 0
        specs.append(pl.BlockSpec((rows, a.shape[1]), index_map))
    return specs


def _mlstm_kernel(qf_ref, kf_ref, vf_ref, gf_ref, qb_ref, kb_ref, vb_ref, gb_ref, bg_ref,
                  c0_ref, m0_ref, *rest, n_cast):
    (hf_ref, hb_ref, cout_ref, mout_ref), (c_scr, m_scr) = _ride_along_casts(rest, n_cast, 4)
    step = pl.program_id(1)
    n_steps = pl.num_programs(1)
    L = ML_CHUNK

    @pl.when(step == 0)
    def _():
        c_scr[...] = c0_ref[0]
        m_scr[...] = m0_ref[0]

    row_i = lax.broadcasted_iota(jnp.int32, (L, L), 0)
    col_i = lax.broadcasted_iota(jnp.int32, (L, L), 1)
    lane = lax.broadcasted_iota(jnp.int32, (1, LANES), 1)
    tri_lo = (col_i <= row_i)
    tri_up = (col_i >= row_i)
    ones_v = jnp.ones((L, ML_V_DIM), BF16)

    for d, (q_ref, k_ref, v_ref, g_ref, h_ref) in enumerate(
            ((qf_ref, kf_ref, vf_ref, gf_ref, hf_ref), (qb_ref, kb_ref, vb_ref, gb_ref, hb_ref))):
        tri = tri_lo if d == 0 else tri_up
        g = g_ref[0] + bg_ref[...]
        logf = jnp.minimum(g, 0.0) - jnp.log(1.0 + jnp.exp(-jnp.abs(g)))
        tri_b = jnp.where(tri, 1.0, 0.0).astype(BF16)
        l1 = logf.astype(BF16)
        r1 = logf - l1.astype(F32)
        l2 = r1.astype(BF16)
        l3 = (r1 - l2.astype(F32)).astype(BF16)
        cum = _dot(tri_b, l1) + (_dot(tri_b, l2) + _dot(tri_b, l3))
        f_lo = 4 + 8 * d
        z = jnp.where((lane >= f_lo) & (lane < f_lo + ML_HEADS), cum, g)
        zt = z.T
        end = L - 1 if d == 0 else 0
        for h in range(ML_HEADS):
            li, lb = 8 * d + h, f_lo + h
            pair, half = h // 2, h % 2
            head_mask = (lane >= 64 * half) & (lane < 64 * half + 64)
            bcol = jnp.sum(jnp.where(lane == lb, z, 0.0), axis=-1, keepdims=True)
            icol = jnp.sum(jnp.where(lane == li, z, 0.0), axis=-1, keepdims=True)
            brow = zt[lb:lb + 1, :]
            irow = zt[li:li + 1, :]
            total = bcol[end:end + 1, :]
            m_prev = m_scr[d, h:h + 1, 0:1]

            qm = jnp.where(head_mask, q_ref[0, :, pair * LANES:(pair + 1) * LANES], 0).astype(BF16)
            km = jnp.where(head_mask, k_ref[0, :, pair * LANES:(pair + 1) * LANES], 0).astype(BF16)
            vext = jnp.concatenate([v_ref[0, :, h * ML_V_DIM:(h + 1) * ML_V_DIM], ones_v], axis=1)

            dmat = jnp.where(tri, bcol - brow + irow, NEG_BIG)
            m_prev_term = bcol + m_prev
            m_t = jnp.maximum(jnp.max(dmat, axis=-1, keepdims=True), m_prev_term)
            sp = _dot_nt(qm, km) * jnp.exp(dmat - m_t)
            inter = jnp.exp(m_prev_term - m_t)
            c_prev = c_scr[d, h]
            r = _dot(sp.astype(BF16), vext) + inter * _dot(qm, c_prev.astype(BF16))
            num = r[:, :ML_V_DIM]
            den = r[:, ML_V_DIM:]
            h_ref[0, :, h * ML_V_DIM:(h + 1) * ML_V_DIM] = (
                num / jnp.maximum(jnp.abs(den), jnp.exp(-m_t))).astype(h_ref.dtype)

            a = total - bcol + icol
            m_loc = jnp.max(a, axis=0, keepdims=True)
            kw = (km.astype(F32) * jnp.exp(a - m_loc)).astype(BF16)
            c_loc = _dot_tn(kw, vext)
            m_new = jnp.maximum(total + m_prev, m_loc)
            c_scr[d, h] = jnp.exp(total + m_prev - m_new) * c_prev + jnp.exp(m_loc - m_new) * c_loc
            m_scr[d, h:h + 1, :] = jnp.broadcast_to(m_new, (1, LANES))

    @pl.when(step == n_steps - 1)
    def _():
        cout_ref[0] = c_scr[...]
        mout_ref[0] = m_scr[...]


def _mlstm(qk, v, gates, bg, c0, m0, casts=()):
    b, t, _ = qk.shape
    L = ML_CHUNK
    nc = t // L
    fwd = lambda n, blk: pl.BlockSpec((1, L, n), lambda i, c: (i, c, blk))
    bwd = lambda n, blk: pl.BlockSpec((1, L, n), lambda i, c: (i, nc - 1 - c, blk))
    st_c = pl.BlockSpec((1,) + c0.shape[1:], lambda i, c: (i, 0, 0, 0, 0))
    st_m = pl.BlockSpec((1,) + m0.shape[1:], lambda i, c: (i, 0, 0, 0))
    cast_specs = _cast_specs(casts, b * nc, lambda i, c: (i * nc + c, 0))
    return pl.pallas_call(
        functools.partial(_mlstm_kernel, n_cast=len(casts)),
        out_shape=(jax.ShapeDtypeStruct((b, t, ML_V_W), BF16),
                   jax.ShapeDtypeStruct((b, t, ML_V_W), BF16),
                   jax.ShapeDtypeStruct(c0.shape, F32),
                   jax.ShapeDtypeStruct(m0.shape, F32),
                   *[jax.ShapeDtypeStruct(a.shape, BF16) for a in casts]),
        grid=(b, nc),
        in_specs=[fwd(ML_QK_W, 0), fwd(ML_QK_W, 1), fwd(ML_V_W, 0), fwd(LANES, 0),
                  bwd(ML_QK_W, 0), bwd(ML_QK_W, 1), bwd(ML_V_W, 0), bwd(LANES, 0),
                  pl.BlockSpec((1, LANES), lambda i, c: (0, 0)), st_c, st_m, *cast_specs],
        out_specs=(fwd(ML_V_W, 0), bwd(ML_V_W, 0), st_c, st_m, *cast_specs),
        scratch_shapes=[pltpu.VMEM(c0.shape[1:], F32), pltpu.VMEM(m0.shape[1:], F32)],
        compiler_params=_cparams(("arbitrary", "arbitrary")),
        name="mlstm",
    )(qk, qk, v, gates, qk, qk, v, gates, bg, c0, m0, *casts)


def _attn_kernel(case_ref, ws_ref, q_ref, k_ref, v_ref, kc_ref, vc_ref, bias_ref, *rest, n_cast):
    (o_ref,), _ = _ride_along_casts(rest, n_cast, 1)
    j = pl.program_id(2)
    nk = ATT_KROWS * GRID_W
    start = pl.multiple_of(ws_ref[j] * GRID_W, GRID_W)
    lane = lax.broadcasted_iota(jnp.int32, (1, LANES), 1)
    for pp in range(ATT_PAIRS):
        lanes = slice(pp * LANES, (pp + 1) * LANES)
        q = q_ref[0, :, lanes]
        k = k_ref[0, pl.ds(start, nk), lanes]
        v = v_ref[0, pl.ds(start, nk), lanes]
        kc = kc_ref[0, :, lanes]
        vc = vc_ref[0, :, lanes]
        acc = jnp.zeros(q.shape, F32)
        for h in range(2):
            head_mask = (lane >= NA_HEAD_DIM * h) & (lane < NA_HEAD_DIM * (h + 1))
            qh = (jnp.where(head_mask, q, 0).astype(F32) * (NA_HEAD_DIM ** -0.5 * LOG2E)).astype(BF16)
            s = _dot_nt(qh, k) + bias_ref[0, 2 * pp + h]
            sc = _dot_nt(qh, kc)
            m = jnp.maximum(jnp.max(s, axis=-1, keepdims=True), jnp.max(sc, axis=-1, keepdims=True))
            p = jnp.exp2(s - m)
            pc = jnp.exp2(sc - m)
            vh = jnp.where(head_mask, v, 1).astype(BF16)
            vch = jnp.where(head_mask, vc, 1).astype(BF16)
            o = _dot(p.astype(BF16), vh) + _dot(pc.astype(BF16), vch)
            acc = acc + jnp.where(head_mask, o / pltpu.roll(o, NA_HEAD_DIM, axis=1), 0.0)
        o_ref[0, :, lanes] = acc.astype(o_ref.dtype)


def _attn_plan(rows):
    kh = min(NA_WIN_H, rows)
    nj = rows // ATT_ROWS
    rs = lambda r: int(np.clip(r - kh // 2, 0, rows - kh))
    ws = [int(np.clip(ATT_ROWS * j - kh // 2, 0, rows - ATT_KROWS)) for j in range(nj)]
    sigs, case = [], []
    for j in range(nj):
        r0 = ATT_ROWS * j
        sig = (ws[j] - r0,) + tuple(rs(r0 + a) - r0 for a in range(ATT_ROWS))
        if sig not in sigs:
            sigs.append(sig)
        case.append(sigs.index(sig))
    return np.asarray(ws, np.int32), np.asarray(case, np.int32), sigs, kh


def _attn_bias(bias_table, sigs, kh):
    col_start = np.clip(np.arange(GRID_W) - NA_WIN_W // 2, 0, GRID_W - NA_WIN_W)
    c = np.arange(GRID_W)
    cidx = c[None, :] - c[:, None] + (NA_WIN_W - 1)
    col_ok = (c[None, :] >= col_start[:, None]) & (c[None, :] < col_start[:, None] + NA_WIN_W)
    expand = (np.arange(2 * NA_WIN_W - 1)[:, None, None] == cidx[None]).astype(np.float32)
    out = []
    for sig in sigs:
        wsr, rsr = sig[0], np.asarray(sig[1:])
        a = np.arange(ATT_ROWS)[:, None]
        rk = wsr + np.arange(ATT_KROWS)[None, :]
        row_ok = (rk >= rsr[:, None]) & (rk < rsr[:, None] + kh)
        ridx = np.clip(rk - a + (NA_WIN_H - 1), 0, 2 * NA_WIN_H - 2)
        rows = bias_table[:, ridx, :]
        full = jnp.einsum('haiv,vqk->haqik', rows, jnp.asarray(expand), precision=HIGHEST)
        ok = row_ok[:, None, :, None] & col_ok[None, :, None, :]
        full = jnp.where(ok[None], full * LOG2E, NEG_BIG)
        out.append(full.reshape(full.shape[0], ATT_ROWS * GRID_W, ATT_KROWS * GRID_W))
    return jnp.stack(out).astype(F32)


def _attn(a_lat, a_ctx, bias_table, casts=()):
    b, t, _ = a_lat.shape
    lc = a_ctx.shape[1]
    rows = t // GRID_W
    ws, case, sigs, kh = _attn_plan(rows)
    bias = _attn_bias(bias_table, sigs, kh)
    tq = ATT_ROWS * GRID_W
    nk = ATT_KROWS * GRID_W
    bw = ATT_PAIRS * LANES
    n_blk = NA_W // bw
    nj = rows // ATT_ROWS
    cast_specs = _cast_specs(casts, n_blk * b * nj, lambda p, i, j, cs, w: ((p * b + i) * nj + j, 0))
    grid_spec = pltpu.PrefetchScalarGridSpec(
        num_scalar_prefetch=2,
        grid=(n_blk, b, nj),
        in_specs=[pl.BlockSpec((1, tq, bw), lambda p, i, j, cs, w: (i, j, p)),
                  pl.BlockSpec((1, t, bw), lambda p, i, j, cs, w: (i, 0, n_blk + p)),
                  pl.BlockSpec((1, t, bw), lambda p, i, j, cs, w: (i, 0, 2 * n_blk + p)),
                  pl.BlockSpec((1, lc, bw), lambda p, i, j, cs, w: (i, 0, n_blk + p)),
                  pl.BlockSpec((1, lc, bw), lambda p, i, j, cs, w: (i, 0, 2 * n_blk + p)),
                  pl.BlockSpec((1, 2 * ATT_PAIRS, tq, nk), lambda p, i, j, cs, w: (cs[j], p, 0, 0)),
                  *cast_specs],
        out_specs=(pl.BlockSpec((1, tq, bw), lambda p, i, j, cs, w: (i, j, p)), *cast_specs),
    )
    return pl.pallas_call(
        functools.partial(_attn_kernel, n_cast=len(casts)),
        out_shape=(jax.ShapeDtypeStruct((b, t, NA_W), BF16),
                   *[jax.ShapeDtypeStruct(a.shape, BF16) for a in casts]),
        grid_spec=grid_spec,
        compiler_params=_cparams(("arbitrary", "arbitrary", "arbitrary")),
        name="nattn",
    )(jnp.asarray(case), jnp.asarray(ws), a_lat, a_lat, a_lat, a_ctx, a_ctx, bias, *casts)


def _outproj_kernel(att_ref, hf_ref, hb_ref, om_ref, x_ref, mod_ref, ng_ref, wa_ref, wm_ref,
                    lng_ref, lnb_ref, wrh_ref, wrl_ref, x1_ref, hm_ref, aff_ref, affr_ref):
    h = hf_ref[0].astype(F32) + hb_ref[0].astype(F32)
    parts = []
    for hd in range(ML_HEADS):
        hh = h[:, hd * ML_V_DIM:(hd + 1) * ML_V_DIM]
        mu = jnp.mean(hh, axis=-1, keepdims=True)
        var = jnp.mean(jnp.square(hh - mu), axis=-1, keepdims=True)
        parts.append((hh - mu) * lax.rsqrt(var + LN_EPS))
    hn = jnp.concatenate(parts, axis=1) * ng_ref[...]
    ml = (hn * _sigmoid(om_ref[0].astype(F32))).astype(BF16)
    mix = _dot(att_ref[0], wa_ref[...]) + _dot(ml, wm_ref[...])
    y = DEEPNORM_ALPHA * x_ref[0] + mod_ref[0, 2:3, :] * mix
    mu = jnp.mean(y, axis=-1, keepdims=True)
    var = jnp.mean(jnp.square(y - mu), axis=-1, keepdims=True)
    x1 = (y - mu) * lax.rsqrt(var + LN_EPS) * lng_ref[...] + lnb_ref[...]
    x1_ref[0] = x1
    hm = x1 * (1.0 + mod_ref[0, 4:5, :]) + mod_ref[0, 3:4, :]
    h_hi = hm.astype(BF16)
    h_lo = (hm - h_hi.astype(F32)).astype(BF16)
    bits = pltpu.bitcast(h_hi.astype(F32), jnp.uint32)
    half = bits.shape[1] // 2
    word = (bits[:, :half] >> 16) | (bits[:, half:] & jnp.uint32(0xFFFF0000))
    hm_ref[0] = pltpu.bitcast(word, jnp.int32)
    logits = _dot(h_hi, wrh_ref[...]) + (_dot(h_lo, wrh_ref[...]) + _dot(h_hi, wrl_ref[...]))
    lane = lax.broadcasted_iota(jnp.int32, (1, LANES), 1)
    logits = jnp.where(lane < N_EXPERTS, logits, NEG_BIG)
    e = jnp.exp(logits - jnp.max(logits, axis=-1, keepdims=True))
    aff = e / jnp.sum(e, axis=-1, keepdims=True)
    affr_ref[0] = aff
    aff_ref[0] = aff.T[:N_EXPERTS, :]


def _outproj(att, hf, hb, om, x, mod, ng, w_att, w_ml, lng, lnb, wr_hi, wr_lo, tm):
    b, t, d = x.shape
    row = lambda n: pl.BlockSpec((1, tm, n), lambda i, j: (i, j, 0))
    full = lambda w: pl.BlockSpec(w.shape, lambda i, j: (0,) * w.ndim)
    return pl.pallas_call(
        _outproj_kernel,
        out_shape=(jax.ShapeDtypeStruct((b, t, d), F32),
                   jax.ShapeDtypeStruct((b, t, d // 2), jnp.int32),
                   jax.ShapeDtypeStruct((b, N_EXPERTS, t), F32),
                   jax.ShapeDtypeStruct((b, t, LANES), F32)),
        grid=(b, t // tm),
        in_specs=[row(NA_W), row(ML_V_W), row(ML_V_W), row(ML_V_W), row(d),
                  pl.BlockSpec((1, 8, d), lambda i, j: (i, 0, 0)),
                  full(ng), full(w_att), full(w_ml), full(lng), full(lnb), full(wr_hi), full(wr_lo)],
        out_specs=(row(d), row(d // 2), pl.BlockSpec((1, N_EXPERTS, tm), lambda i, j: (i, 0, j)), row(LANES)),
        compiler_params=_cparams(("arbitrary", "arbitrary")),
        name="outproj",
    )(att, hf, hb, om, x, mod, ng, w_att, w_ml, lng, lnb, wr_hi, wr_lo)


UNSELECTED = -1e6


def _select_kernel(aff_ref, idx_ref, pos_ref, off_ref, cum_scr, sel_scr, offs_v, offs_s, dsem, *, cap):
    t = aff_ref.shape[2]
    nb = t // LANES
    lane = lax.broadcasted_iota(jnp.int32, (1, LANES), 1)
    keys = lambda: pltpu.bitcast(aff_ref[0], jnp.int32)

    def count(mask):
        return jnp.sum(jnp.where(mask, 1.0, 0.0), axis=-1, keepdims=True)

    def search(_, c):
        lo, hi = c
        mid = lo + jnp.right_shift(hi - lo, 1)
        ge = count(keys() >= mid) >= cap
        return jnp.where(ge, mid, lo), jnp.where(ge, hi, mid)

    lo0 = jnp.zeros((N_EXPERTS, 1), jnp.int32)
    hi0 = jnp.full((N_EXPERTS, 1), 0x7F800000, jnp.int32)
    thr, _ = lax.fori_loop(0, 31, search, (lo0, hi0))

    r_i = lax.broadcasted_iota(jnp.int32, (LANES, LANES), 0)
    c_i = lax.broadcasted_iota(jnp.int32, (LANES, LANES), 1)
    strict = jnp.where(r_i < c_i, 1.0, 0.0).astype(BF16)
    tr_i = lax.broadcasted_iota(jnp.int32, (t, LANES), 0)
    tc_i = lax.broadcasted_iota(jnp.int32, (t, LANES), 1)
    block_ind = jnp.where(jnp.right_shift(tr_i, 7) == tc_i, 1.0, 0.0).astype(BF16)

    def prefix(x01):
        xb = x01.astype(BF16)
        offs = _dot(_dot(xb, block_ind).astype(BF16), strict)
        for j in range(nb):
            off_j = jnp.sum(jnp.where(lane == j, offs, 0.0), axis=-1, keepdims=True)
            cum_scr[:, j * LANES:(j + 1) * LANES] = _dot(xb[:, j * LANES:(j + 1) * LANES], strict) + off_j
        return offs

    k = keys()
    gt = k > thr
    eq = k == thr
    need = cap - count(gt)
    prefix(jnp.where(eq, 1.0, 0.0))
    sel = jnp.where(gt | (eq & (cum_scr[...] < need)), 1.0, 0.0)
    sel_scr[...] = sel
    offs = prefix(sel)

    pad = jnp.zeros((LANES - N_EXPERTS, LANES), F32)
    for j in range(nb):
        blk = jnp.where(sel_scr[:, j * LANES:(j + 1) * LANES] > 0.0, cum_scr[:, j * LANES:(j + 1) * LANES], UNSELECTED)
        pos_ref[0, j * LANES:(j + 1) * LANES, :] = jnp.concatenate([blk, pad], axis=0).T
    off_ref[0] = jnp.concatenate([offs, pad], axis=0).T

    cum_scr[...] = cum_scr[...] + sel_scr[...]
    idx_ref[...] = jnp.zeros(idx_ref.shape, idx_ref.dtype)
    offs_v[...] = offs.astype(jnp.int32)
    to_smem = pltpu.make_async_copy(offs_v, offs_s, dsem)
    to_smem.start()
    to_smem.wait()
    sub = lax.broadcasted_iota(jnp.int32, (LANES, LANES), 0).astype(F32)
    for e in range(N_EXPERTS):
        def group(pg, carry, e=e):
            first = jnp.asarray(pg * LANES, jnp.int32)
            jlo, jhi = carry

            def advance(j0, bound, ahead):
                look = lambda j: offs_s[e, jnp.minimum(j + ahead, LANES - 1)]
                j, _ = lax.while_loop(lambda c: (c[0] < nb) & (c[1] <= bound),
                                      lambda c: (c[0] + 1, look(c[0] + 1)), (j0, look(j0)))
                return j

            jlo = advance(jlo, first, 1)
            jhi = advance(jhi, first + (LANES - 1), 0)
            slots = first.astype(F32) + sub

            def block(jb, acc):
                c = cum_scr[pl.ds(e, 1), pl.ds(pl.multiple_of(jb * LANES, LANES), LANES)]
                return acc + jnp.where(jnp.broadcast_to(c, (LANES, LANES)) <= slots, 1.0, 0.0)

            acc = lax.fori_loop(jlo, jhi, block, jnp.zeros((LANES, LANES), F32))
            col = jnp.sum(acc, axis=-1, keepdims=True).astype(jnp.int32) + jlo * LANES
            idx_ref[0, pl.ds(pl.multiple_of(pg * LANES, LANES), LANES), e:e + 1] = col
            return jlo, jhi

        lax.fori_loop(0, cap // LANES, group, (jnp.int32(0), jnp.int32(0)))


def _select(aff_t, cap):
    b, e, t = aff_t.shape
    assert cap % LANES == 0 and t % LANES == 0 and t // LANES < LANES
    return pl.pallas_call(
        functools.partial(_select_kernel, cap=cap),
        out_shape=(jax.ShapeDtypeStruct((b, cap, LANES), jnp.int32),
                   jax.ShapeDtypeStruct((b, t, LANES), F32),
                   jax.ShapeDtypeStruct((b, LANES, LANES), F32)),
        grid=(b,),
        in_specs=[pl.BlockSpec((1, e, t), lambda i: (i, 0, 0))],
        out_specs=(pl.BlockSpec((1, cap, LANES), lambda i: (i, 0, 0)),
                   pl.BlockSpec((1, t, LANES), lambda i: (i, 0, 0)),
                   pl.BlockSpec((1, LANES, LANES), lambda i: (i, 0, 0))),
        scratch_shapes=[pltpu.VMEM((e, t), F32), pltpu.VMEM((e, t), F32),
                        pltpu.VMEM((e, LANES), jnp.int32), pltpu.SMEM((e, LANES), jnp.int32),
                        pltpu.SemaphoreType.DMA(())],
        compiler_params=_cparams(("arbitrary",)),
        name="select",
    )(aff_t)


SC_CORES = 2
SC_SUBCORES = 16
SC_CHUNK = 64


def _gather_rows(table, idx):
    n = idx.shape[0]
    v, d = table.shape
    n_workers = SC_CORES * SC_SUBCORES
    per_w = n // n_workers
    assert n % (n_workers * SC_CHUNK) == 0 and d % LANES == 0 and table.dtype.itemsize == 4
    mesh = plsc.VectorSubcoreMesh(core_axis_name="c", subcore_axis_name="s",
                                  num_cores=SC_CORES, num_subcores=SC_SUBCORES)

    @functools.partial(
        pl.kernel, mesh=mesh,
        out_type=jax.ShapeDtypeStruct((n, d), table.dtype),
        scratch_types=[pltpu.VMEM((SC_CHUNK,), jnp.int32),
                       pltpu.VMEM((SC_CHUNK, d), table.dtype),
                       pltpu.SemaphoreType.DMA],
        name="row_gather")
    def gather(table_hbm, idx_hbm, out_hbm, idx_v, rows_v, sem):
        base = (lax.axis_index("s") * SC_CORES + lax.axis_index("c")) * per_w

        @pl.loop(0, per_w, step=SC_CHUNK)
        def _(o):
            pltpu.sync_copy(idx_hbm.at[pl.ds(base + o, SC_CHUNK)], idx_v)
            pltpu.async_copy(table_hbm.at[idx_v], rows_v, sem).wait()
            pltpu.sync_copy(rows_v, out_hbm.at[pl.ds(base + o, SC_CHUNK)])

    return gather(table, idx)


def _moe_kernel(x_ref, wg_ref, wu_ref, wd_ref, o_ref, x_scr, act_scr, *, tf):
    w = pltpu.bitcast(x_ref[0], jnp.uint32)
    half = w.shape[1]
    x_scr[:, :half] = pltpu.bitcast(w << 16, F32).astype(BF16)
    x_scr[:, half:] = pltpu.bitcast(w & jnp.uint32(0xFFFF0000), F32).astype(BF16)
    x = x_scr[...]
    for c in range(act_scr.shape[1] // tf):
        cols = slice(c * tf, (c + 1) * tf)
        hg = _dot(x, wg_ref[0, :, cols])
        hu = _dot(x, wu_ref[0, :, cols])
        act_scr[:, cols] = (hg * _sigmoid(hg) * hu).astype(BF16)
    o_ref[0] = _dot(act_scr[...], wd_ref[0]).astype(o_ref.dtype)


def _moe(xe, w_gate, w_up, w_down, tm, tf):
    e, m, dw = xe.shape
    d = 2 * dw
    dff = w_gate.shape[2]
    return pl.pallas_call(
        functools.partial(_moe_kernel, tf=tf),
        out_shape=jax.ShapeDtypeStruct((e, m, d), BF16),
        grid=(e, m // tm),
        in_specs=[pl.BlockSpec((1, tm, dw), lambda i, j: (i, j, 0)),
                  pl.BlockSpec((1, d, dff), lambda i, j: (i, 0, 0)),
                  pl.BlockSpec((1, d, dff), lambda i, j: (i, 0, 0)),
                  pl.BlockSpec((1, dff, d), lambda i, j: (i, 0, 0))],
        out_specs=pl.BlockSpec((1, tm, d), lambda i, j: (i, j, 0)),
        scratch_shapes=[pltpu.VMEM((tm, d), BF16), pltpu.VMEM((tm, dff), BF16)],
        compiler_params=_cparams(("arbitrary", "arbitrary")),
        name="moe",
    )(xe, w_gate, w_up, w_down)


CMB_TM = 256
CMB_WIN = 64


def _combine_kernel(p0_ref, x1_ref, pos_ref, aff_ref, mod_ref, lng_ref, lnb_ref, y_hbm, o_ref,
                    ybuf, yextra, acc_ref, sem, sem_x, *, cap):
    b = pl.program_id(0)
    j = pl.program_id(1)
    nt = pl.num_programs(1)
    g = b * nt + j
    n_steps = pl.num_programs(0) * nt
    slot = lax.rem(g, 2)
    m_rows = y_hbm.shape[1]
    win = CMB_WIN
    kdim = N_EXPERTS * win

    def window(bb, jj, r):
        base = (bb * (nt + 1) + jj) * N_EXPERTS
        los, rows = [], []
        for e in range(N_EXPERTS):
            lo = (p0_ref[base + e] // ROW_ALIGN) * ROW_ALIGN + r * win
            los.append(lo)
            rows.append(jnp.minimum(bb * cap + lo, m_rows - win))
        return los, rows

    def copies(rows, buf, sems):
        return [pltpu.make_async_copy(y_hbm.at[e, pl.ds(pl.multiple_of(rows[e], ROW_ALIGN), win), :],
                                      buf.at[pl.ds(e * win, win), :], sems.at[e])
                for e in range(N_EXPERTS)]

    los0, rows0 = window(b, j, 0)

    @pl.when(g == 0)
    def _():
        for c in copies(rows0, ybuf.at[0], sem.at[0]):
            c.start()

    @pl.when(g + 1 < n_steps)
    def _():
        wrap = j + 1 == nt
        _, rows_n = window(jnp.where(wrap, b + 1, b), jnp.where(wrap, 0, j + 1), 0)
        for c in copies(rows_n, ybuf.at[1 - slot], sem.at[1 - slot]):
            c.start()

    lane = lax.broadcasted_iota(jnp.int32, (1, LANES), 1)
    e_i = lax.broadcasted_iota(jnp.int32, (LANES, kdim), 0)
    k_i = lax.broadcasted_iota(jnp.int32, (LANES, kdim), 1)
    expand = jnp.where(k_i // win == e_i, 1.0, 0.0).astype(BF16)
    col_in_win = (lax.broadcasted_iota(jnp.int32, (1, kdim), 1) % win).astype(F32)
    aff_cols = _dot(aff_ref[0].astype(BF16), expand)

    def lane_vec(vals):
        v = jnp.zeros((1, LANES), F32)
        for e in range(N_EXPERTS):
            v = jnp.where(lane == e, jnp.asarray(vals[e]).astype(F32), v)
        return v

    def expand_rows(los, rows, buf):
        rel = pos_ref[0] - lane_vec(los)
        shift = lane_vec([b * cap + lo - row for lo, row in zip(los, rows)])
        tgt = jnp.where((rel >= 0.0) & (rel < float(win)), rel + shift, -1.0)
        tgt_cols = _dot(tgt.astype(BF16), expand)
        s = jnp.where(tgt_cols == col_in_win, aff_cols, 0.0).astype(BF16)
        return _dot(s, buf[...])

    for c in copies(rows0, ybuf.at[slot], sem.at[slot]):
        c.wait()
    acc_ref[...] = expand_rows(los0, rows0, ybuf.at[slot])

    base = (b * (nt + 1) + j) * N_EXPERTS
    rounds = jnp.int32(0)
    for e in range(N_EXPERTS):
        span = p0_ref[base + N_EXPERTS + e] - (p0_ref[base + e] // ROW_ALIGN) * ROW_ALIGN
        rounds = jnp.maximum(rounds, (span + win - 1) // win)

    def extra(r, carry):
        los, rows = window(b, j, r)
        cps = copies(rows, yextra, sem_x)
        for c in cps:
            c.start()
        for c in cps:
            c.wait()
        acc_ref[...] += expand_rows(los, rows, yextra)
        return carry

    lax.fori_loop(1, rounds, extra, 0)

    y = DEEPNORM_ALPHA * x1_ref[0] + mod_ref[0, 5:6, :] * acc_ref[...]
    mu = jnp.mean(y, axis=-1, keepdims=True)
    var = jnp.mean(jnp.square(y - mu), axis=-1, keepdims=True)
    o_ref[0] = (y - mu) * lax.rsqrt(var + LN_EPS) * lng_ref[...] + lnb_ref[...]


def _combine(p0tab, x1, pos_t, aff_r, mod, lng, lnb, y, cap):
    b, t, d = x1.shape
    tm = CMB_TM
    kdim = N_EXPERTS * CMB_WIN
    row = lambda n: pl.BlockSpec((1, tm, n), lambda i, j, p: (i, j, 0))
    vec = pl.BlockSpec((1, d), lambda i, j, p: (0, 0))
    grid_spec = pltpu.PrefetchScalarGridSpec(
        num_scalar_prefetch=1,
        grid=(b, t // tm),
        in_specs=[row(d), row(LANES), row(LANES), pl.BlockSpec((1, 8, d), lambda i, j, p: (i, 0, 0)), vec, vec,
                  pl.BlockSpec(memory_space=pl.ANY)],
        out_specs=row(d),
        scratch_shapes=[pltpu.VMEM((2, kdim, d), y.dtype), pltpu.VMEM((kdim, d), y.dtype),
                        pltpu.VMEM((tm, d), F32),
                        pltpu.SemaphoreType.DMA((2, N_EXPERTS)), pltpu.SemaphoreType.DMA((N_EXPERTS,))],
    )
    return pl.pallas_call(
        functools.partial(_combine_kernel, cap=cap),
        out_shape=jax.ShapeDtypeStruct((b, t, d), F32),
        grid_spec=grid_spec,
        compiler_params=_cparams(("arbitrary", "arbitrary")),
        name="combine",
    )(p0tab, x1, pos_t, aff_r, mod, lng, lnb, y)


def _split_w_in(w_in):
    a_end = 3 * NA_W
    qk_end = a_end + 2 * ML_QK_W
    v_end = qk_end + ML_V_W
    o_end = v_end + ML_V_W
    wg = jnp.pad(w_in[:, o_end:], ((0, 0), (0, LANES - ML_N_GATES)))
    return tuple(w.astype(BF16) for w in
                 (w_in[:, :a_end], w_in[:, a_end:qk_end], w_in[:, qk_end:v_end], w_in[:, v_end:o_end], wg))


def _layer(x, ctx, c, c_ctx, w_ada, b_ada, w_in, b_gate, conv_qk, na_rel_bias, ml_norm_g, w_out,
           ln1_g, ln1_b, w_router, w_expert_gate, w_expert_up, w_expert_down, ln2_g, ln2_b):
    b, t, d = x.shape
    lc = ctx.shape[1]

    cs = jnp.zeros((16, d), F32).at[:b].set(c).at[b].set(c_ctx)
    m = _ada(cs, w_ada, b_ada)
    mod = jnp.pad(m[:b].reshape(b, 6, d), ((0, 0), (0, 2), (0, 0)))
    mod_ctx = jnp.broadcast_to(jnp.pad(m[b].reshape(6, d), ((0, 2), (0, 0)))[None], (b, 8, d))

    ws = _split_w_in(w_in)
    a_lat, qk_lat, v_lat, o_lat, g_lat = _inproj(x, mod, ws, tm=512)
    a_ctx, qk_ctx, v_ctx, _, g_ctx = _inproj(ctx, mod_ctx, ws, tm=lc)

    conv_w = jnp.pad(conv_qk, ((0, 8 - CONV_K), (0, 0)))
    cos_t, sin_t = _rope_tables(t)
    qk_lat = _qkprep(qk_lat, conv_w, cos_t, sin_t)
    qk_ctx = _qkprep(qk_ctx, conv_w, jnp.ones((lc, LANES), F32), jnp.zeros((lc, LANES), F32))

    bg = jnp.pad(b_gate, (0, LANES - ML_N_GATES)).reshape(1, LANES)
    c0 = jnp.zeros((b, 2, ML_HEADS, LANES, 2 * ML_V_DIM), F32)
    m0 = jnp.zeros((b, 2, 8, LANES), F32)
    n_e, _, dff = w_expert_gate.shape
    _, _, c1, m1 = _mlstm(qk_ctx, v_ctx, g_ctx, bg, c0, m0)
    hf, hb, _, _, wd_b = _mlstm(qk_lat, v_lat, g_lat, bg, c1, m1,
                                casts=(w_expert_down.reshape(n_e * dff, d),))
    att, wg_b, wu_b = _attn(a_lat, a_ctx, na_rel_bias,
                            casts=(w_expert_gate.reshape(n_e * d, dff), w_expert_up.reshape(n_e * d, dff)))

    wr = jnp.pad(w_router, ((0, 0), (0, LANES - N_EXPERTS)))
    wr_hi = wr.astype(BF16)
    wr_lo = (wr - wr_hi.astype(F32)).astype(BF16)
    x1, hm, aff_t, aff_r = _outproj(att, hf, hb, o_lat, x, mod, ml_norm_g.reshape(1, -1),
                                    w_out[:NA_W].astype(BF16), w_out[NA_W:].astype(BF16),
                                    ln1_g.reshape(1, d), ln1_b.reshape(1, d), wr_hi, wr_lo, tm=256)

    cap = CAPACITY_FACTOR * t // N_EXPERTS
    idx_t, pos_t, off_t = _select(aff_t, cap)
    rows = jnp.transpose(idx_t[:, :, :N_EXPERTS], (2, 0, 1)) + (jnp.arange(b, dtype=jnp.int32) * t)[None, :, None]
    xe = _gather_rows(hm.reshape(b * t, d // 2), rows.reshape(-1)).reshape(N_EXPERTS, b * cap, d // 2)
    y = _moe(xe, wg_b.reshape(n_e, d, dff), wu_b.reshape(n_e, d, dff), wd_b.reshape(n_e, dff, d),
             tm=min(512, b * cap), tf=256)

    nt = t // CMB_TM
    p0 = off_t[:, 0:nt * (CMB_TM // LANES):CMB_TM // LANES, :N_EXPERTS]
    p0 = jnp.concatenate([p0, jnp.full((b, 1, N_EXPERTS), cap, F32)], axis=1)
    p0tab = p0.astype(jnp.int32).reshape(-1)
    return _combine(p0tab, x1, pos_t, aff_r, mod, ln2_g.reshape(1, d), ln2_b.reshape(1, d), y, cap)


def kernel(x, c, ctx, c_ctx, w_ada, b_ada, w_in, b_gate, conv_qk, na_rel_bias, ml_norm_g, w_out,
           ln1_g, ln1_b, w_router, w_expert_gate, w_expert_up, w_expert_down, ln2_g, ln2_b):
    return _layer(x, ctx, c, c_ctx, w_ada[0], b_ada[0], w_in[0], b_gate[0], conv_qk[0], na_rel_bias[0],
                  ml_norm_g[0], w_out[0], ln1_g[0], ln1_b[0], w_router[0], w_expert_gate[0],
                  w_expert_up[0], w_expert_down[0], ln2_g[0], ln2_b[0])
```

```python
import functools

import numpy as np
import jax
import jax.numpy as jnp
from jax import lax
from jax.experimental import pallas as pl
from jax.experimental.pallas import tpu as pltpu
from jax.experimental.pallas import tpu_sc as plsc

F32 = jnp.float32
BF16 = jnp.bfloat16
HIGHEST = lax.Precision.HIGHEST

GRID_W = 64
NA_HEADS = 8
NA_HEAD_DIM = 64
NA_WIN_H = 8
NA_WIN_W = 16
NA_W = NA_HEADS * NA_HEAD_DIM
ML_HEADS = 4
ML_QK_DIM = 64
ML_V_DIM = 128
ML_QK_W = ML_HEADS * ML_QK_DIM
ML_V_W = ML_HEADS * ML_V_DIM
ML_CHUNK = 256
ML_N_GATES = 4 * ML_HEADS
CONV_K = 5
ROPE_BASE = 10000.0
N_EXPERTS = 16
CAPACITY_FACTOR = 2
LN_EPS = 1e-5
DEPTH = 1
DEEPNORM_ALPHA = (2.0 * DEPTH) ** 0.25

LANES = 128
ROW_ALIGN = 16
NEG_BIG = -1e30
LOG2E = 1.4426950408889634
VMEM_LIMIT = 56 * 1024 * 1024

ATT_ROWS = 4
ATT_PAIRS = 2
ATT_KROWS = ATT_ROWS + NA_WIN_H - 1


def _cparams(sem):
    return pltpu.CompilerParams(dimension_semantics=sem, vmem_limit_bytes=VMEM_LIMIT)


def _sigmoid(x):
    return 1.0 / (1.0 + jnp.exp(-x))


def _dot(a, b):
    return jnp.dot(a, b, preferred_element_type=F32)


def _dot_nt(a, b):
    return lax.dot_general(a, b, (((1,), (1,)), ((), ())), preferred_element_type=F32)


def _dot_tn(a, b):
    return lax.dot_general(a, b, (((0,), (0,)), ((), ())), preferred_element_type=F32)


def _ada_kernel(c_ref, w_ref, b_ref, o_ref):
    c = c_ref[...]
    s = c * _sigmoid(c)
    o_ref[...] = jnp.dot(s, w_ref[...], precision=HIGHEST, preferred_element_type=F32) + b_ref[...]


def _ada(cs, w_ada, b_ada):
    rows, d = cs.shape
    n = w_ada.shape[1]
    tn = 512
    return pl.pallas_call(
        _ada_kernel,
        out_shape=jax.ShapeDtypeStruct((rows, n), F32),
        grid=(n // tn,),
        in_specs=[pl.BlockSpec((rows, d), lambda j: (0, 0)),
                  pl.BlockSpec((d, tn), lambda j: (0, j)),
                  pl.BlockSpec((1, tn), lambda j: (0, j))],
        out_specs=pl.BlockSpec((rows, tn), lambda j: (0, j)),
        compiler_params=_cparams(("arbitrary",)),
        name="ada",
    )(cs, w_ada, b_ada.reshape(1, n))


def _inproj_kernel(x_ref, mod_ref, wa_ref, wqk_ref, wv_ref, wo_ref, wg_ref,
                   a_ref, qk_ref, v_ref, o_ref, g_ref):
    xm = (x_ref[0] * (1.0 + mod_ref[0, 1:2, :]) + mod_ref[0, 0:1, :]).astype(BF16)
    a_ref[0] = _dot(xm, wa_ref[...]).astype(BF16)
    qk_ref[0] = _dot(xm, wqk_ref[...]).astype(BF16)
    v_ref[0] = _dot(xm, wv_ref[...]).astype(BF16)
    o_ref[0] = _dot(xm, wo_ref[...]).astype(BF16)
    g_ref[0] = _dot(xm, wg_ref[...])


def _inproj(x, mod, ws, tm):
    b, t, d = x.shape
    wa, wqk, wv, wo, wg = ws
    full = lambda w: pl.BlockSpec(w.shape, lambda i, j: (0, 0))
    row = lambda n: pl.BlockSpec((1, tm, n), lambda i, j: (i, j, 0))
    return pl.pallas_call(
        _inproj_kernel,
        out_shape=(jax.ShapeDtypeStruct((b, t, wa.shape[1]), BF16),
                   jax.ShapeDtypeStruct((b, t, wqk.shape[1]), BF16),
                   jax.ShapeDtypeStruct((b, t, wv.shape[1]), BF16),
                   jax.ShapeDtypeStruct((b, t, wo.shape[1]), BF16),
                   jax.ShapeDtypeStruct((b, t, wg.shape[1]), F32)),
        grid=(b, t // tm),
        in_specs=[row(d), pl.BlockSpec((1, 8, d), lambda i, j: (i, 0, 0)),
                  full(wa), full(wqk), full(wv), full(wo), full(wg)],
        out_specs=(row(wa.shape[1]), row(wqk.shape[1]), row(wv.shape[1]), row(wo.shape[1]),
                   row(wg.shape[1])),
        compiler_params=_cparams(("arbitrary", "arbitrary")),
        name="inproj",
    )(x, mod, wa, wqk, wv, wo, wg)


def _qkprep_kernel(x_ref, w_ref, cos_ref, sin_ref, o_ref, *, sub, halo):
    t = x_ref.shape[1]
    n_sub = t // sub
    lane = lax.broadcasted_iota(jnp.int32, (1, LANES), 1)
    first_half = (lane & 31) < 16
    scale = jnp.where(pl.program_id(1) < (ML_QK_W // LANES), ML_QK_DIM ** -0.5, 1.0).astype(F32)
    zeros = jnp.zeros((halo, LANES), x_ref.dtype)
    for s in range(n_sub):
        lo = s * sub
        top = x_ref[0, lo - halo:lo, :] if s > 0 else zeros
        bot = x_ref[0, lo + sub:lo + sub + halo, :] if s < n_sub - 1 else zeros
        ext = jnp.concatenate([top, x_ref[0, lo:lo + sub, :], bot], axis=0).astype(F32)
        n = sub + 2 * halo
        acc = jnp.zeros((sub, LANES), F32)
        for j in range(CONV_K):
            shift = (CONV_K // 2 - j) % n
            sh = ext if shift == 0 else pltpu.roll(ext, shift, axis=0)
            acc = acc + w_ref[j:j + 1, :] * sh[halo:halo + sub, :]
        y = acc * _sigmoid(acc)
        partner = jnp.where(first_half, pltpu.roll(y, LANES - 16, axis=1), pltpu.roll(y, 16, axis=1))
        y = y * cos_ref[lo:lo + sub, :] + partner * sin_ref[lo:lo + sub, :]
        o_ref[0, lo:lo + sub, :] = (y * scale).astype(o_ref.dtype)


def _qkprep(qk, conv_w, cos_t, sin_t):
    b, t, w = qk.shape
    sub = min(t, 1024)
    kern = functools.partial(_qkprep_kernel, sub=sub, halo=16)
    return pl.pallas_call(
        kern,
        out_shape=jax.ShapeDtypeStruct((b, t, w), BF16),
        grid=(b, w // LANES),
        in_specs=[pl.BlockSpec((1, t, LANES), lambda i, j: (i, 0, j)),
                  pl.BlockSpec((8, LANES), lambda i, j: (0, j)),
                  pl.BlockSpec((t, LANES), lambda i, j: (0, 0)),
                  pl.BlockSpec((t, LANES), lambda i, j: (0, 0))],
        out_specs=pl.BlockSpec((1, t, LANES), lambda i, j: (i, 0, j)),
        compiler_params=_cparams(("arbitrary", "arbitrary")),
        name="qkprep",
    )(qk, conv_w, cos_t, sin_t)


def _rope_tables(t):
    nf = ML_QK_DIM // 4
    inv = 1.0 / (ROPE_BASE ** (jnp.arange(nf, dtype=F32) / nf))
    pos = jnp.arange(t)
    ang_r = (pos // GRID_W).astype(F32)[:, None] * inv
    ang_c = (pos % GRID_W).astype(F32)[:, None] * inv
    cos = jnp.concatenate([jnp.cos(ang_r)] * 2 + [jnp.cos(ang_c)] * 2, axis=-1)
    sin = jnp.concatenate([-jnp.sin(ang_r), jnp.sin(ang_r), -jnp.sin(ang_c), jnp.sin(ang_c)], axis=-1)
    return jnp.tile(cos, (1, 2)), jnp.tile(sin, (1, 2))


def _ride_along_casts(rest, n_cast, n_out):
    outs = rest[n_cast:n_cast + n_out]
    for src, dst in zip(rest[:n_cast], rest[n_cast + n_out:2 * n_cast + n_out]):
        dst[...] = src[...].astype(dst.dtype)
    return outs, rest[2 * n_cast + n_out:]


def _cast_specs(casts, n_steps, index_map):
    specs = []
    for a in casts:
        rows = a.shape[0] // n_steps
        assert a.shape[0] % n_steps == 0 and rows % ROW_ALIGN == 0
        specs.append(pl.BlockSpec((rows, a.shape[1]), index_map))
    return specs


def _mlstm_kernel(qf_ref, kf_ref, vf_ref, gf_ref, qb_ref, kb_ref, vb_ref, gb_ref, bg_ref,
                  c0_ref, m0_ref, *rest, n_cast):
    (hf_ref, hb_ref, cout_ref, mout_ref), (c_scr, m_scr) = _ride_along_casts(rest, n_cast, 4)
    step = pl.program_id(1)
    n_steps = pl.num_programs(1)
    L = ML_CHUNK

    @pl.when(step == 0)
    def _():
        c_scr[...] = c0_ref[0]
        m_scr[...] = m0_ref[0]

    row_i = lax.broadcasted_iota(jnp.int32, (L, L), 0)
    col_i = lax.broadcasted_iota(jnp.int32, (L, L), 1)
    lane = lax.broadcasted_iota(jnp.int32, (1, LANES), 1)
    tri_lo = (col_i <= row_i)
    tri_up = (col_i >= row_i)
    ones_v = jnp.ones((L, ML_V_DIM), BF16)

    for d, (q_ref, k_ref, v_ref, g_ref, h_ref) in enumerate(
            ((qf_ref, kf_ref, vf_ref, gf_ref, hf_ref), (qb_ref, kb_ref, vb_ref, gb_ref, hb_ref))):
        tri = tri_lo if d == 0 else tri_up
        g = g_ref[0] + bg_ref[...]
        logf = jnp.minimum(g, 0.0) - jnp.log(1.0 + jnp.exp(-jnp.abs(g)))
        tri_b = jnp.where(tri, 1.0, 0.0).astype(BF16)
        l1 = logf.astype(BF16)
        r1 = logf - l1.astype(F32)
        l2 = r1.astype(BF16)
        l3 = (r1 - l2.astype(F32)).astype(BF16)
        cum = _dot(tri_b, l1) + (_dot(tri_b, l2) + _dot(tri_b, l3))
        f_lo = 4 + 8 * d
        z = jnp.where((lane >= f_lo) & (lane < f_lo + ML_HEADS), cum, g)
        zt = z.T
        end = L - 1 if d == 0 else 0
        for h in range(ML_HEADS):
            li, lb = 8 * d + h, f_lo + h
            pair, half = h // 2, h % 2
            head_mask = (lane >= 64 * half) & (lane < 64 * half + 64)
            bcol = jnp.sum(jnp.where(lane == lb, z, 0.0), axis=-1, keepdims=True)
            icol = jnp.sum(jnp.where(lane == li, z, 0.0), axis=-1, keepdims=True)
            brow = zt[lb:lb + 1, :]
            irow = zt[li:li + 1, :]
            total = bcol[end:end + 1, :]
            m_prev = m_scr[d, h:h + 1, 0:1]

            qm = jnp.where(head_mask, q_ref[0, :, pair * LANES:(pair + 1) * LANES], 0).astype(BF16)
            km = jnp.where(head_mask, k_ref[0, :, pair * LANES:(pair + 1) * LANES], 0).astype(BF16)
            vext = jnp.concatenate([v_ref[0, :, h * ML_V_DIM:(h + 1) * ML_V_DIM], ones_v], axis=1)

            dmat = jnp.where(tri, bcol - brow + irow, NEG_BIG)
            m_prev_term = bcol + m_prev
            m_t = jnp.maximum(jnp.max(dmat, axis=-1, keepdims=True), m_prev_term)
            sp = _dot_nt(qm, km) * jnp.exp(dmat - m_t)
            inter = jnp.exp(m_prev_term - m_t)
            c_prev = c_scr[d, h]
            r = _dot(sp.astype(BF16), vext) + inter * _dot(qm, c_prev.astype(BF16))
            num = r[:, :ML_V_DIM]
            den = r[:, ML_V_DIM:]
            h_ref[0, :, h * ML_V_DIM:(h + 1) * ML_V_DIM] = (
                num / jnp.maximum(jnp.abs(den), jnp.exp(-m_t))).astype(h_ref.dtype)

            a = total - bcol + icol
            m_loc = jnp.max(a, axis=0, keepdims=True)
            kw = (km.astype(F32) * jnp.exp(a - m_loc)).astype(BF16)
            c_loc = _dot_tn(kw, vext)
            m_new = jnp.maximum(total + m_prev, m_loc)
            c_scr[d, h] = jnp.exp(total + m_prev - m_new) * c_prev + jnp.exp(m_loc - m_new) * c_loc
            m_scr[d, h:h + 1, :] = jnp.broadcast_to(m_new, (1, LANES))

    @pl.when(step == n_steps - 1)
    def _():
        cout_ref[0] = c_scr[...]
        mout_ref[0] = m_scr[...]


def _mlstm(qk, v, gates, bg, c0, m0, casts=()):
    b, t, _ = qk.shape
    L = ML_CHUNK
    nc = t // L
    fwd = lambda n, blk: pl.BlockSpec((1, L, n), lambda i, c: (i, c, blk))
    bwd = lambda n, blk: pl.BlockSpec((1, L, n), lambda i, c: (i, nc - 1 - c, blk))
    st_c = pl.BlockSpec((1,) + c0.shape[1:], lambda i, c: (i, 0, 0, 0, 0))
    st_m = pl.BlockSpec((1,) + m0.shape[1:], lambda i, c: (i, 0, 0, 0))
    cast_specs = _cast_specs(casts, b * nc, lambda i, c: (i * nc + c, 0))
    return pl.pallas_call(
        functools.partial(_mlstm_kernel, n_cast=len(casts)),
        out_shape=(jax.ShapeDtypeStruct((b, t, ML_V_W), BF16),
                   jax.ShapeDtypeStruct((b, t, ML_V_W), BF16),
                   jax.ShapeDtypeStruct(c0.shape, F32),
                   jax.ShapeDtypeStruct(m0.shape, F32),
                   *[jax.ShapeDtypeStruct(a.shape, BF16) for a in casts]),
        grid=(b, nc),
        in_specs=[fwd(ML_QK_W, 0), fwd(ML_QK_W, 1), fwd(ML_V_W, 0), fwd(LANES, 0),
                  bwd(ML_QK_W, 0), bwd(ML_QK_W, 1), bwd(ML_V_W, 0), bwd(LANES, 0),
                  pl.BlockSpec((1, LANES), lambda i, c: (0, 0)), st_c, st_m, *cast_specs],
        out_specs=(fwd(ML_V_W, 0), bwd(ML_V_W, 0), st_c, st_m, *cast_specs),
        scratch_shapes=[pltpu.VMEM(c0.shape[1:], F32), pltpu.VMEM(m0.shape[1:], F32)],
        compiler_params=_cparams(("arbitrary", "arbitrary")),
        name="mlstm",
    )(qk, qk, v, gates, qk, qk, v, gates, bg, c0, m0, *casts)


def _attn_kernel(case_ref, ws_ref, q_ref, k_ref, v_ref, kc_ref, vc_ref, bias_ref, *rest, n_cast):
    (o_ref,), _ = _ride_along_casts(rest, n_cast, 1)
    j = pl.program_id(2)
    nk = ATT_KROWS * GRID_W
    start = pl.multiple_of(ws_ref[j] * GRID_W, GRID_W)
    lane = lax.broadcasted_iota(jnp.int32, (1, LANES), 1)
    for pp in range(ATT_PAIRS):
        lanes = slice(pp * LANES, (pp + 1) * LANES)
        q = q_ref[0, :, lanes]
        k = k_ref[0, pl.ds(start, nk), lanes]
        v = v_ref[0, pl.ds(start, nk), lanes]
        kc = kc_ref[0, :, lanes]
        vc = vc_ref[0, :, lanes]
        acc = jnp.zeros(q.shape, F32)
        for h in range(2):
            head_mask = (lane >= NA_HEAD_DIM * h) & (lane < NA_HEAD_DIM * (h + 1))
            qh = (jnp.where(head_mask, q, 0).astype(F32) * (NA_HEAD_DIM ** -0.5 * LOG2E)).astype(BF16)
            s = _dot_nt(qh, k) + bias_ref[0, 2 * pp + h]
            sc = _dot_nt(qh, kc)
            m = jnp.maximum(jnp.max(s, axis=-1, keepdims=True), jnp.max(sc, axis=-1, keepdims=True))
            p = jnp.exp2(s - m)
            pc = jnp.exp2(sc - m)
            vh = jnp.where(head_mask, v, 1).astype(BF16)
            vch = jnp.where(head_mask, vc, 1).astype(BF16)
            o = _dot(p.astype(BF16), vh) + _dot(pc.astype(BF16), vch)
            acc = acc + jnp.where(head_mask, o / pltpu.roll(o, NA_HEAD_DIM, axis=1), 0.0)
        o_ref[0, :, lanes] = acc.astype(o_ref.dtype)


def _attn_plan(rows):
    kh = min(NA_WIN_H, rows)
    nj = rows // ATT_ROWS
    rs = lambda r: int(np.clip(r - kh // 2, 0, rows - kh))
    ws = [int(np.clip(ATT_ROWS * j - kh // 2, 0, rows - ATT_KROWS)) for j in range(nj)]
    sigs, case = [], []
    for j in range(nj):
        r0 = ATT_ROWS * j
        sig = (ws[j] - r0,) + tuple(rs(r0 + a) - r0 for a in range(ATT_ROWS))
        if sig not in sigs:
            sigs.append(sig)
        case.append(sigs.index(sig))
    return np.asarray(ws, np.int32), np.asarray(case, np.int32), sigs, kh


def _attn_bias(bias_table, sigs, kh):
    col_start = np.clip(np.arange(GRID_W) - NA_WIN_W // 2, 0, GRID_W - NA_WIN_W)
    c = np.arange(GRID_W)
    cidx = c[None, :] - c[:, None] + (NA_WIN_W - 1)
    col_ok = (c[None, :] >= col_start[:, None]) & (c[None, :] < col_start[:, None] + NA_WIN_W)
    expand = (np.arange(2 * NA_WIN_W - 1)[:, None, None] == cidx[None]).astype(np.float32)
    out = []
    for sig in sigs:
        wsr, rsr = sig[0], np.asarray(sig[1:])
        a = np.arange(ATT_ROWS)[:, None]
        rk = wsr + np.arange(ATT_KROWS)[None, :]
        row_ok = (rk >= rsr[:, None]) & (rk < rsr[:, None] + kh)
        ridx = np.clip(rk - a + (NA_WIN_H - 1), 0, 2 * NA_WIN_H - 2)
        rows = bias_table[:, ridx, :]
        full = jnp.einsum('haiv,vqk->haqik', rows, jnp.asarray(expand), precision=HIGHEST)
        ok = row_ok[:, None, :, None] & col_ok[None, :, None, :]
        full = jnp.where(ok[None], full * LOG2E, NEG_BIG)
        out.append(full.reshape(full.shape[0], ATT_ROWS * GRID_W, ATT_KROWS * GRID_W))
    return jnp.stack(out).astype(F32)


def _attn(a_lat, a_ctx, bias_table, casts=()):
    b, t, _ = a_lat.shape
    lc = a_ctx.shape[1]
    rows = t // GRID_W
    ws, case, sigs, kh = _attn_plan(rows)
    bias = _attn_bias(bias_table, sigs, kh)
    tq = ATT_ROWS * GRID_W
    nk = ATT_KROWS * GRID_W
    bw = ATT_PAIRS * LANES
    n_blk = NA_W // bw
    nj = rows // ATT_ROWS
    cast_specs = _cast_specs(casts, n_blk * b * nj, lambda p, i, j, cs, w: ((p * b + i) * nj + j, 0))
    grid_spec = pltpu.PrefetchScalarGridSpec(
        num_scalar_prefetch=2,
        grid=(n_blk, b, nj),
        in_specs=[pl.BlockSpec((1, tq, bw), lambda p, i, j, cs, w: (i, j, p)),
                  pl.BlockSpec((1, t, bw), lambda p, i, j, cs, w: (i, 0, n_blk + p)),
                  pl.BlockSpec((1, t, bw), lambda p, i, j, cs, w: (i, 0, 2 * n_blk + p)),
                  pl.BlockSpec((1, lc, bw), lambda p, i, j, cs, w: (i, 0, n_blk + p)),
                  pl.BlockSpec((1, lc, bw), lambda p, i, j, cs, w: (i, 0, 2 * n_blk + p)),
                  pl.BlockSpec((1, 2 * ATT_PAIRS, tq, nk), lambda p, i, j, cs, w: (cs[j], p, 0, 0)),
                  *cast_specs],
        out_specs=(pl.BlockSpec((1, tq, bw), lambda p, i, j, cs, w: (i, j, p)), *cast_specs),
    )
    return pl.pallas_call(
        functools.partial(_attn_kernel, n_cast=len(casts)),
        out_shape=(jax.ShapeDtypeStruct((b, t, NA_W), BF16),
                   *[jax.ShapeDtypeStruct(a.shape, BF16) for a in casts]),
        grid_spec=grid_spec,
        compiler_params=_cparams(("arbitrary", "arbitrary", "arbitrary")),
        name="nattn",
    )(jnp.asarray(case), jnp.asarray(ws), a_lat, a_lat, a_lat, a_ctx, a_ctx, bias, *casts)


def _outproj_kernel(att_ref, hf_ref, hb_ref, om_ref, x_ref, mod_ref, ng_ref, wa_ref, wm_ref,
                    lng_ref, lnb_ref, wrh_ref, wrl_ref, x1_ref, hm_ref, aff_ref, affr_ref):
    h = hf_ref[0].astype(F32) + hb_ref[0].astype(F32)
    parts = []
    for hd in range(ML_HEADS):
        hh = h[:, hd * ML_V_DIM:(hd + 1) * ML_V_DIM]
        mu = jnp.mean(hh, axis=-1, keepdims=True)
        var = jnp.mean(jnp.square(hh - mu), axis=-1, keepdims=True)
        parts.append((hh - mu) * lax.rsqrt(var + LN_EPS))
    hn = jnp.concatenate(parts, axis=1) * ng_ref[...]
    ml = (hn * _sigmoid(om_ref[0].astype(F32))).astype(BF16)
    mix = _dot(att_ref[0], wa_ref[...]) + _dot(ml, wm_ref[...])
    y = DEEPNORM_ALPHA * x_ref[0] + mod_ref[0, 2:3, :] * mix
    mu = jnp.mean(y, axis=-1, keepdims=True)
    var = jnp.mean(jnp.square(y - mu), axis=-1, keepdims=True)
    x1 = (y - mu) * lax.rsqrt(var + LN_EPS) * lng_ref[...] + lnb_ref[...]
    x1_ref[0] = x1
    hm = x1 * (1.0 + mod_ref[0, 4:5, :]) + mod_ref[0, 3:4, :]
    h_hi = hm.astype(BF16)
    h_lo = (hm - h_hi.astype(F32)).astype(BF16)
    bits = pltpu.bitcast(h_hi.astype(F32), jnp.uint32)
    half = bits.shape[1] // 2
    word = (bits[:, :half] >> 16) | (bits[:, half:] & jnp.uint32(0xFFFF0000))
    hm_ref[0] = pltpu.bitcast(word, jnp.int32)
    logits = _dot(h_hi, wrh_ref[...]) + (_dot(h_lo, wrh_ref[...]) + _dot(h_hi, wrl_ref[...]))
    lane = lax.broadcasted_iota(jnp.int32, (1, LANES), 1)
    logits = jnp.where(lane < N_EXPERTS, logits, NEG_BIG)
    e = jnp.exp(logits - jnp.max(logits, axis=-1, keepdims=True))
    aff = e / jnp.sum(e, axis=-1, keepdims=True)
    affr_ref[0] = aff
    aff_ref[0] = aff.T[:N_EXPERTS, :]


def _outproj(att, hf, hb, om, x, mod, ng, w_att, w_ml, lng, lnb, wr_hi, wr_lo, tm):
    b, t, d = x.shape
    row = lambda n: pl.BlockSpec((1, tm, n), lambda i, j: (i, j, 0))
    full = lambda w: pl.BlockSpec(w.shape, lambda i, j: (0,) * w.ndim)
    return pl.pallas_call(
        _outproj_kernel,
        out_shape=(jax.ShapeDtypeStruct((b, t, d), F32),
                   jax.ShapeDtypeStruct((b, t, d // 2), jnp.int32),
                   jax.ShapeDtypeStruct((b, N_EXPERTS, t), F32),
                   jax.ShapeDtypeStruct((b, t, LANES), F32)),
        grid=(b, t // tm),
        in_specs=[row(NA_W), row(ML_V_W), row(ML_V_W), row(ML_V_W), row(d),
                  pl.BlockSpec((1, 8, d), lambda i, j: (i, 0, 0)),
                  full(ng), full(w_att), full(w_ml), full(lng), full(lnb), full(wr_hi), full(wr_lo)],
        out_specs=(row(d), row(d // 2), pl.BlockSpec((1, N_EXPERTS, tm), lambda i, j: (i, 0, j)), row(LANES)),
        compiler_params=_cparams(("arbitrary", "arbitrary")),
        name="outproj",
    )(att, hf, hb, om, x, mod, ng, w_att, w_ml, lng, lnb, wr_hi, wr_lo)


UNSELECTED = -1e6


def _select_kernel(aff_ref, idx_ref, pos_ref, off_ref, cum_scr, sel_scr, offs_v, offs_s, dsem, *, cap):
    t = aff_ref.shape[2]
    nb = t // LANES
    lane = lax.broadcasted_iota(jnp.int32, (1, LANES), 1)
    keys = lambda: pltpu.bitcast(aff_ref[0], jnp.int32)

    def count(mask):
        return jnp.sum(jnp.where(mask, 1.0, 0.0), axis=-1, keepdims=True)

    def search(_, c):
        lo, hi = c
        mid = lo + jnp.right_shift(hi - lo, 1)
        ge = count(keys() >= mid) >= cap
        return jnp.where(ge, mid, lo), jnp.where(ge, hi, mid)

    lo0 = jnp.zeros((N_EXPERTS, 1), jnp.int32)
    hi0 = jnp.full((N_EXPERTS, 1), 0x7F800000, jnp.int32)
    thr, _ = lax.fori_loop(0, 31, search, (lo0, hi0))

    r_i = lax.broadcasted_iota(jnp.int32, (LANES, LANES), 0)
    c_i = lax.broadcasted_iota(jnp.int32, (LANES, LANES), 1)
    strict = jnp.where(r_i < c_i, 1.0, 0.0).astype(BF16)
    tr_i = lax.broadcasted_iota(jnp.int32, (t, LANES), 0)
    tc_i = lax.broadcasted_iota(jnp.int32, (t, LANES), 1)
    block_ind = jnp.where(jnp.right_shift(tr_i, 7) == tc_i, 1.0, 0.0).astype(BF16)

    def prefix(x01):
        xb = x01.astype(BF16)
        offs = _dot(_dot(xb, block_ind).astype(BF16), strict)
        for j in range(nb):
            off_j = jnp.sum(jnp.where(lane == j, offs, 0.0), axis=-1, keepdims=True)
            cum_scr[:, j * LANES:(j + 1) * LANES] = _dot(xb[:, j * LANES:(j + 1) * LANES], strict) + off_j
        return offs

    k = keys()
    gt = k > thr
    eq = k == thr
    need = cap - count(gt)
    prefix(jnp.where(eq, 1.0, 0.0))
    sel = jnp.where(gt | (eq & (cum_scr[...] < need)), 1.0, 0.0)
    sel_scr[...] = sel
    offs = prefix(sel)

    pad = jnp.zeros((LANES - N_EXPERTS, LANES), F32)
    for j in range(nb):
        blk = jnp.where(sel_scr[:, j * LANES:(j + 1) * LANES] > 0.0, cum_scr[:, j * LANES:(j + 1) * LANES], UNSELECTED)
        pos_ref[0, j * LANES:(j + 1) * LANES, :] = jnp.concatenate([blk, pad], axis=0).T
    off_ref[0] = jnp.concatenate([offs, pad], axis=0).T

    cum_scr[...] = cum_scr[...] + sel_scr[...]
    idx_ref[...] = jnp.zeros(idx_ref.shape, idx_ref.dtype)
    offs_v[...] = offs.astype(jnp.int32)
    to_smem = pltpu.make_async_copy(offs_v, offs_s, dsem)
    to_smem.start()
    to_smem.wait()
    sub = lax.broadcasted_iota(jnp.int32, (LANES, LANES), 0).astype(F32)
    for e in range(N_EXPERTS):
        def group(pg, carry, e=e):
            first = jnp.asarray(pg * LANES, jnp.int32)
            jlo, jhi = carry

            def advance(j0, bound, ahead):
                look = lambda j: offs_s[e, jnp.minimum(j + ahead, LANES - 1)]
                j, _ = lax.while_loop(lambda c: (c[0] < nb) & (c[1] <= bound),
                                      lambda c: (c[0] + 1, look(c[0] + 1)), (j0, look(j0)))
                return j

            jlo = advance(jlo, first, 1)
            jhi = advance(jhi, first + (LANES - 1), 0)
            slots = first.astype(F32) + sub

            def block(jb, acc):
                c = cum_scr[pl.ds(e, 1), pl.ds(pl.multiple_of(jb * LANES, LANES), LANES)]
                return acc + jnp.where(jnp.broadcast_to(c, (LANES, LANES)) <= slots, 1.0, 0.0)

            acc = lax.fori_loop(jlo, jhi, block, jnp.zeros((LANES, LANES), F32))
            col = jnp.sum(acc, axis=-1, keepdims=True).astype(jnp.int32) + jlo * LANES
            idx_ref[0, pl.ds(pl.multiple_of(pg * LANES, LANES), LANES), e:e + 1] = col
            return jlo, jhi

        lax.fori_loop(0, cap // LANES, group, (jnp.int32(0), jnp.int32(0)))


def _select(aff_t, cap):
    b, e, t = aff_t.shape
    assert cap % LANES == 0 and t % LANES == 0 and t // LANES < LANES
    return pl.pallas_call(
        functools.partial(_select_kernel, cap=cap),
        out_shape=(jax.ShapeDtypeStruct((b, cap, LANES), jnp.int32),
                   jax.ShapeDtypeStruct((b, t, LANES), F32),
                   jax.ShapeDtypeStruct((b, LANES, LANES), F32)),
        grid=(b,),
        in_specs=[pl.BlockSpec((1, e, t), lambda i: (i, 0, 0))],
        out_specs=(pl.BlockSpec((1, cap, LANES), lambda i: (i, 0, 0)),
                   pl.BlockSpec((1, t, LANES), lambda i: (i, 0, 0)),
                   pl.BlockSpec((1, LANES, LANES), lambda i: (i, 0, 0))),
        scratch_shapes=[pltpu.VMEM((e, t), F32), pltpu.VMEM((e, t), F32),
                        pltpu.VMEM((e, LANES), jnp.int32), pltpu.SMEM((e, LANES), jnp.int32),
                        pltpu.SemaphoreType.DMA(())],
        compiler_params=_cparams(("arbitrary",)),
        name="select",
    )(aff_t)


SC_CORES = 2
SC_SUBCORES = 16
SC_CHUNK = 64
MOE_GROUPS = 4


def _gather_rows(table, idx):
    n = idx.shape[0]
    v, d = table.shape
    n_workers = SC_CORES * SC_SUBCORES
    per_w = n // n_workers
    assert n % (n_workers * SC_CHUNK) == 0 and d % LANES == 0 and table.dtype.itemsize == 4
    mesh = plsc.VectorSubcoreMesh(core_axis_name="c", subcore_axis_name="s",
                                  num_cores=SC_CORES, num_subcores=SC_SUBCORES)

    @functools.partial(
        pl.kernel, mesh=mesh,
        out_type=jax.ShapeDtypeStruct((n, d), table.dtype),
        scratch_types=[pltpu.VMEM((SC_CHUNK,), jnp.int32),
                       pltpu.VMEM((SC_CHUNK, d), table.dtype),
                       pltpu.SemaphoreType.DMA],
        name="row_gather")
    def gather(table_hbm, idx_hbm, out_hbm, idx_v, rows_v, sem):
        base = (lax.axis_index("s") * SC_CORES + lax.axis_index("c")) * per_w

        @pl.loop(0, per_w, step=SC_CHUNK)
        def _(o):
            pltpu.sync_copy(idx_hbm.at[pl.ds(base + o, SC_CHUNK)], idx_v)
            pltpu.async_copy(table_hbm.at[idx_v], rows_v, sem).wait()
            pltpu.sync_copy(rows_v, out_hbm.at[pl.ds(base + o, SC_CHUNK)])

    return gather(table, idx)


def _moe_kernel(x_ref, wg_ref, wu_ref, wd_ref, o_ref, x_scr, act_scr, *, tf):
    w = pltpu.bitcast(x_ref[0], jnp.uint32)
    half = w.shape[1]
    x_scr[:, :half] = pltpu.bitcast(w << 16, F32).astype(BF16)
    x_scr[:, half:] = pltpu.bitcast(w & jnp.uint32(0xFFFF0000), F32).astype(BF16)
    x = x_scr[...]
    for c in range(act_scr.shape[1] // tf):
        cols = slice(c * tf, (c + 1) * tf)
        hg = _dot(x, wg_ref[0, :, cols])
        hu = _dot(x, wu_ref[0, :, cols])
        act_scr[:, cols] = (hg * _sigmoid(hg) * hu).astype(BF16)
    o_ref[0] = _dot(act_scr[...], wd_ref[0]).astype(o_ref.dtype)


def _moe(xe, w_gate, w_up, w_down, e0, tm, tf):
    e, m, dw = xe.shape
    d = 2 * dw
    dff = w_gate.shape[2]
    return pl.pallas_call(
        functools.partial(_moe_kernel, tf=tf),
        out_shape=jax.ShapeDtypeStruct((e, m, d), BF16),
        grid=(e, m // tm),
        in_specs=[pl.BlockSpec((1, tm, dw), lambda i, j: (i, j, 0)),
                  pl.BlockSpec((1, d, dff), lambda i, j: (i + e0, 0, 0)),
                  pl.BlockSpec((1, d, dff), lambda i, j: (i + e0, 0, 0)),
                  pl.BlockSpec((1, dff, d), lambda i, j: (i + e0, 0, 0))],
        out_specs=pl.BlockSpec((1, tm, d), lambda i, j: (i, j, 0)),
        scratch_shapes=[pltpu.VMEM((tm, d), BF16), pltpu.VMEM((tm, dff), BF16)],
        compiler_params=_cparams(("arbitrary", "arbitrary")),
        name="moe",
    )(xe, w_gate, w_up, w_down)


CMB_TM = 256
CMB_WIN = 64


def _combine_kernel(p0_ref, x1_ref, pos_ref, aff_ref, mod_ref, lng_ref, lnb_ref, *rest, cap, n_parts):
    y_parts = rest[:n_parts]
    o_ref, ybuf, yextra, acc_ref, sem, sem_x = rest[n_parts:]
    per_part = N_EXPERTS // n_parts
    _combine_body(p0_ref, x1_ref, pos_ref, aff_ref, mod_ref, lng_ref, lnb_ref,
                  lambda e: y_parts[e // per_part].at[e % per_part], y_parts[0].shape[1],
                  o_ref, ybuf, yextra, acc_ref, sem, sem_x, cap)


def _combine_body(p0_ref, x1_ref, pos_ref, aff_ref, mod_ref, lng_ref, lnb_ref, y_of, m_rows, o_ref,
                  ybuf, yextra, acc_ref, sem, sem_x, cap):
    b = pl.program_id(0)
    j = pl.program_id(1)
    nt = pl.num_programs(1)
    g = b * nt + j
    n_steps = pl.num_programs(0) * nt
    slot = lax.rem(g, 2)
    win = CMB_WIN
    kdim = N_EXPERTS * win

    def window(bb, jj, r):
        base = (bb * (nt + 1) + jj) * N_EXPERTS
        los, rows = [], []
        for e in range(N_EXPERTS):
            lo = (p0_ref[base + e] // ROW_ALIGN) * ROW_ALIGN + r * win
            los.append(lo)
            rows.append(jnp.minimum(bb * cap + lo, m_rows - win))
        return los, rows

    def copies(rows, buf, sems):
        return [pltpu.make_async_copy(y_of(e).at[pl.ds(pl.multiple_of(rows[e], ROW_ALIGN), win), :],
                                      buf.at[pl.ds(e * win, win), :], sems.at[e])
                for e in range(N_EXPERTS)]

    los0, rows0 = window(b, j, 0)

    @pl.when(g == 0)
    def _():
        for c in copies(rows0, ybuf.at[0], sem.at[0]):
            c.start()

    @pl.when(g + 1 < n_steps)
    def _():
        wrap = j + 1 == nt
        _, rows_n = window(jnp.where(wrap, b + 1, b), jnp.where(wrap, 0, j + 1), 0)
        for c in copies(rows_n, ybuf.at[1 - slot], sem.at[1 - slot]):
            c.start()

    lane = lax.broadcasted_iota(jnp.int32, (1, LANES), 1)
    e_i = lax.broadcasted_iota(jnp.int32, (LANES, kdim), 0)
    k_i = lax.broadcasted_iota(jnp.int32, (LANES, kdim), 1)
    expand = jnp.where(k_i // win == e_i, 1.0, 0.0).astype(BF16)
    col_in_win = (lax.broadcasted_iota(jnp.int32, (1, kdim), 1) % win).astype(F32)
    aff_cols = _dot(aff_ref[0].astype(BF16), expand)

    def lane_vec(vals):
        v = jnp.zeros((1, LANES), F32)
        for e in range(N_EXPERTS):
            v = jnp.where(lane == e, jnp.asarray(vals[e]).astype(F32), v)
        return v

    def expand_rows(los, rows, buf):
        rel = pos_ref[0] - lane_vec(los)
        shift = lane_vec([b * cap + lo - row for lo, row in zip(los, rows)])
        tgt = jnp.where((rel >= 0.0) & (rel < float(win)), rel + shift, -1.0)
        tgt_cols = _dot(tgt.astype(BF16), expand)
        s = jnp.where(tgt_cols == col_in_win, aff_cols, 0.0).astype(BF16)
        return _dot(s, buf[...])

    for c in copies(rows0, ybuf.at[slot], sem.at[slot]):
        c.wait()
    acc_ref[...] = expand_rows(los0, rows0, ybuf.at[slot])

    base = (b * (nt + 1) + j) * N_EXPERTS
    rounds = jnp.int32(0)
    for e in range(N_EXPERTS):
        span = p0_ref[base + N_EXPERTS + e] - (p0_ref[base + e] // ROW_ALIGN) * ROW_ALIGN
        rounds = jnp.maximum(rounds, (span + win - 1) // win)

    def extra(r, carry):
        los, rows = window(b, j, r)
        cps = copies(rows, yextra, sem_x)
        for c in cps:
            c.start()
        for c in cps:
            c.wait()
        acc_ref[...] += expand_rows(los, rows, yextra)
        return carry

    lax.fori_loop(1, rounds, extra, 0)

    y = DEEPNORM_ALPHA * x1_ref[0] + mod_ref[0, 5:6, :] * acc_ref[...]
    mu = jnp.mean(y, axis=-1, keepdims=True)
    var = jnp.mean(jnp.square(y - mu), axis=-1, keepdims=True)
    o_ref[0] = (y - mu) * lax.rsqrt(var + LN_EPS) * lng_ref[...] + lnb_ref[...]


def _combine(p0tab, x1, pos_t, aff_r, mod, lng, lnb, ys, cap):
    b, t, d = x1.shape
    tm = CMB_TM
    kdim = N_EXPERTS * CMB_WIN
    row = lambda n: pl.BlockSpec((1, tm, n), lambda i, j, p: (i, j, 0))
    vec = pl.BlockSpec((1, d), lambda i, j, p: (0, 0))
    grid_spec = pltpu.PrefetchScalarGridSpec(
        num_scalar_prefetch=1,
        grid=(b, t // tm),
        in_specs=[row(d), row(LANES), row(LANES), pl.BlockSpec((1, 8, d), lambda i, j, p: (i, 0, 0)), vec, vec,
                  *[pl.BlockSpec(memory_space=pl.ANY) for _ in ys]],
        out_specs=row(d),
        scratch_shapes=[pltpu.VMEM((2, kdim, d), ys[0].dtype), pltpu.VMEM((kdim, d), ys[0].dtype),
                        pltpu.VMEM((tm, d), F32),
                        pltpu.SemaphoreType.DMA((2, N_EXPERTS)), pltpu.SemaphoreType.DMA((N_EXPERTS,))],
    )
    return pl.pallas_call(
        functools.partial(_combine_kernel, cap=cap, n_parts=len(ys)),
        out_shape=jax.ShapeDtypeStruct((b, t, d), F32),
        grid_spec=grid_spec,
        compiler_params=_cparams(("arbitrary", "arbitrary")),
        name="combine",
    )(p0tab, x1, pos_t, aff_r, mod, lng, lnb, *ys)


def _split_w_in(w_in):
    a_end = 3 * NA_W
    qk_end = a_end + 2 * ML_QK_W
    v_end = qk_end + ML_V_W
    o_end = v_end + ML_V_W
    wg = jnp.pad(w_in[:, o_end:], ((0, 0), (0, LANES - ML_N_GATES)))
    return tuple(w.astype(BF16) for w in
                 (w_in[:, :a_end], w_in[:, a_end:qk_end], w_in[:, qk_end:v_end], w_in[:, v_end:o_end], wg))


def _layer(x, ctx, c, c_ctx, w_ada, b_ada, w_in, b_gate, conv_qk, na_rel_bias, ml_norm_g, w_out,
           ln1_g, ln1_b, w_router, w_expert_gate, w_expert_up, w_expert_down, ln2_g, ln2_b):
    b, t, d = x.shape
    lc = ctx.shape[1]

    cs = jnp.zeros((16, d), F32).at[:b].set(c).at[b].set(c_ctx)
    m = _ada(cs, w_ada, b_ada)
    mod = jnp.pad(m[:b].reshape(b, 6, d), ((0, 0), (0, 2), (0, 0)))
    mod_ctx = jnp.broadcast_to(jnp.pad(m[b].reshape(6, d), ((0, 2), (0, 0)))[None], (b, 8, d))

    ws = _split_w_in(w_in)
    a_lat, qk_lat, v_lat, o_lat, g_lat = _inproj(x, mod, ws, tm=512)
    a_ctx, qk_ctx, v_ctx, _, g_ctx = _inproj(ctx, mod_ctx, ws, tm=lc)

    conv_w = jnp.pad(conv_qk, ((0, 8 - CONV_K), (0, 0)))
    cos_t, sin_t = _rope_tables(t)
    qk_lat = _qkprep(qk_lat, conv_w, cos_t, sin_t)
    qk_ctx = _qkprep(qk_ctx, conv_w, jnp.ones((lc, LANES), F32), jnp.zeros((lc, LANES), F32))

    bg = jnp.pad(b_gate, (0, LANES - ML_N_GATES)).reshape(1, LANES)
    c0 = jnp.zeros((b, 2, ML_HEADS, LANES, 2 * ML_V_DIM), F32)
    m0 = jnp.zeros((b, 2, 8, LANES), F32)
    n_e, _, dff = w_expert_gate.shape
    _, _, c1, m1 = _mlstm(qk_ctx, v_ctx, g_ctx, bg, c0, m0)
    hf, hb, _, _, wd_b = _mlstm(qk_lat, v_lat, g_lat, bg, c1, m1,
                                casts=(w_expert_down.reshape(n_e * dff, d),))
    att, wg_b, wu_b = _attn(a_lat, a_ctx, na_rel_bias,
                            casts=(w_expert_gate.reshape(n_e * d, dff), w_expert_up.reshape(n_e * d, dff)))

    wr = jnp.pad(w_router, ((0, 0), (0, LANES - N_EXPERTS)))
    wr_hi = wr.astype(BF16)
    wr_lo = (wr - wr_hi.astype(F32)).astype(BF16)
    x1, hm, aff_t, aff_r = _outproj(att, hf, hb, o_lat, x, mod, ml_norm_g.reshape(1, -1),
                                    w_out[:NA_W].astype(BF16), w_out[NA_W:].astype(BF16),
                                    ln1_g.reshape(1, d), ln1_b.reshape(1, d), wr_hi, wr_lo, tm=256)

    cap = CAPACITY_FACTOR * t // N_EXPERTS
    idx_t, pos_t, off_t = _select(aff_t, cap)
    rows = jnp.transpose(idx_t[:, :, :N_EXPERTS], (2, 0, 1)) + (jnp.arange(b, dtype=jnp.int32) * t)[None, :, None]
    per = N_EXPERTS // MOE_GROUPS
    wg3, wu3, wd3 = wg_b.reshape(n_e, d, dff), wu_b.reshape(n_e, d, dff), wd_b.reshape(n_e, dff, d)
    ys = []
    for k in range(MOE_GROUPS):
        xe = _gather_rows(hm.reshape(b * t, d // 2), rows[k * per:(k + 1) * per].reshape(-1))
        ys.append(_moe(xe.reshape(per, b * cap, d // 2), wg3, wu3, wd3, e0=k * per,
                       tm=min(1024, b * cap), tf=256))

    nt = t // CMB_TM
    p0 = off_t[:, 0:nt * (CMB_TM // LANES):CMB_TM // LANES, :N_EXPERTS]
    p0 = jnp.concatenate([p0, jnp.full((b, 1, N_EXPERTS), cap, F32)], axis=1)
    p0tab = p0.astype(jnp.int32).reshape(-1)
    return _combine(p0tab, x1, pos_t, aff_r, mod, ln2_g.reshape(1, d), ln2_b.reshape(1, d), ys, cap)


def kernel(x, c, ctx, c_ctx, w_ada, b_ada, w_in, b_gate, conv_qk, na_rel_bias, ml_norm_g, w_out,
           ln1_g, ln1_b, w_router, w_expert_gate, w_expert_up, w_expert_down, ln2_g, ln2_b):
    return _layer(x, ctx, c, c_ctx, w_ada[0], b_ada[0], w_in[0], b_gate[0], conv_qk[0], na_rel_bias[0],
                  ml_norm_g[0], w_out[0], ln1_g[0], ln1_b[0], w_router[0], w_expert_gate[0],
                  w_expert_up[0], w_expert_down[0], ln2_g[0], ln2_b[0])
```

```python
import functools

import numpy as np
import jax
import jax.numpy as jnp
from jax import lax
from jax.experimental import pallas as pl
from jax.experimental.pallas import tpu as pltpu
from jax.experimental.pallas import tpu_sc as plsc

F32 = jnp.float32
BF16 = jnp.bfloat16
HIGHEST = lax.Precision.HIGHEST

GRID_W = 64
NA_HEADS = 8
NA_HEAD_DIM = 64
NA_WIN_H = 8
NA_WIN_W = 16
NA_W = NA_HEADS * NA_HEAD_DIM
ML_HEADS = 4
ML_QK_DIM = 64
ML_V_DIM = 128
ML_QK_W = ML_HEADS * ML_QK_DIM
ML_V_W = ML_HEADS * ML_V_DIM
ML_CHUNK = 256
ML_N_GATES = 4 * ML_HEADS
CONV_K = 5
ROPE_BASE = 10000.0
N_EXPERTS = 16
CAPACITY_FACTOR = 2
LN_EPS = 1e-5
DEPTH = 1
DEEPNORM_ALPHA = (2.0 * DEPTH) ** 0.25

LANES = 128
ROW_ALIGN = 16
NEG_BIG = -1e30
LOG2E = 1.4426950408889634
VMEM_LIMIT = 56 * 1024 * 1024

ATT_ROWS = 4
ATT_PAIRS = 4
ATT_KROWS = ATT_ROWS + NA_WIN_H - 1


def _cparams(sem):
    return pltpu.CompilerParams(dimension_semantics=sem, vmem_limit_bytes=VMEM_LIMIT)


def _sigmoid(x):
    return 1.0 / (1.0 + jnp.exp(-x))


def _dot(a, b):
    return jnp.dot(a, b, preferred_element_type=F32)


def _dot_nt(a, b):
    return lax.dot_general(a, b, (((1,), (1,)), ((), ())), preferred_element_type=F32)


def _dot_tn(a, b):
    return lax.dot_general(a, b, (((0,), (0,)), ((), ())), preferred_element_type=F32)


def _ada_kernel(c_ref, w_ref, b_ref, o_ref):
    c = c_ref[...]
    s = c * _sigmoid(c)
    o_ref[...] = jnp.dot(s, w_ref[...], precision=HIGHEST, preferred_element_type=F32) + b_ref[...]


def _ada(cs, w_ada, b_ada):
    rows, d = cs.shape
    n = w_ada.shape[1]
    tn = 512
    return pl.pallas_call(
        _ada_kernel,
        out_shape=jax.ShapeDtypeStruct((rows, n), F32),
        grid=(n // tn,),
        in_specs=[pl.BlockSpec((rows, d), lambda j: (0, 0)),
                  pl.BlockSpec((d, tn), lambda j: (0, j)),
                  pl.BlockSpec((1, tn), lambda j: (0, j))],
        out_specs=pl.BlockSpec((rows, tn), lambda j: (0, j)),
        compiler_params=_cparams(("arbitrary",)),
        name="ada",
    )(cs, w_ada, b_ada.reshape(1, n))


def _inproj_kernel(x_ref, mod_ref, wa_ref, wqk_ref, wv_ref, wo_ref, wg_ref,
                   a_ref, qk_ref, v_ref, o_ref, g_ref):
    xm = (x_ref[0] * (1.0 + mod_ref[0, 1:2, :]) + mod_ref[0, 0:1, :]).astype(BF16)
    a_ref[0] = _dot(xm, wa_ref[...]).astype(BF16)
    qk_ref[0] = _dot(xm, wqk_ref[...]).astype(BF16)
    v_ref[0] = _dot(xm, wv_ref[...]).astype(BF16)
    o_ref[0] = _dot(xm, wo_ref[...]).astype(BF16)
    g_ref[0] = _dot(xm, wg_ref[...])


def _inproj(x, mod, ws, tm):
    b, t, d = x.shape
    wa, wqk, wv, wo, wg = ws
    full = lambda w: pl.BlockSpec(w.shape, lambda i, j: (0, 0))
    row = lambda n: pl.BlockSpec((1, tm, n), lambda i, j: (i, j, 0))
    return pl.pallas_call(
        _inproj_kernel,
        out_shape=(jax.ShapeDtypeStruct((b, t, wa.shape[1]), BF16),
                   jax.ShapeDtypeStruct((b, t, wqk.shape[1]), BF16),
                   jax.ShapeDtypeStruct((b, t, wv.shape[1]), BF16),
                   jax.ShapeDtypeStruct((b, t, wo.shape[1]), BF16),
                   jax.ShapeDtypeStruct((b, t, wg.shape[1]), F32)),
        grid=(b, t // tm),
        in_specs=[row(d), pl.BlockSpec((1, 8, d), lambda i, j: (i, 0, 0)),
                  full(wa), full(wqk), full(wv), full(wo), full(wg)],
        out_specs=(row(wa.shape[1]), row(wqk.shape[1]), row(wv.shape[1]), row(wo.shape[1]),
                   row(wg.shape[1])),
        compiler_params=_cparams(("arbitrary", "arbitrary")),
        name="inproj",
    )(x, mod, wa, wqk, wv, wo, wg)


def _qkprep_kernel(x_ref, w_ref, cos_ref, sin_ref, o_ref, *, sub, halo):
    t = x_ref.shape[1]
    n_sub = t // sub
    lane = lax.broadcasted_iota(jnp.int32, (1, LANES), 1)
    first_half = (lane & 31) < 16
    scale = jnp.where(pl.program_id(1) < (ML_QK_W // LANES), ML_QK_DIM ** -0.5, 1.0).astype(F32)
    zeros = jnp.zeros((halo, LANES), x_ref.dtype)
    for s in range(n_sub):
        lo = s * sub
        top = x_ref[0, lo - halo:lo, :] if s > 0 else zeros
        bot = x_ref[0, lo + sub:lo + sub + halo, :] if s < n_sub - 1 else zeros
        ext = jnp.concatenate([top, x_ref[0, lo:lo + sub, :], bot], axis=0).astype(F32)
        n = sub + 2 * halo
        acc = jnp.zeros((sub, LANES), F32)
        for j in range(CONV_K):
            shift = (CONV_K // 2 - j) % n
            sh = ext if shift == 0 else pltpu.roll(ext, shift, axis=0)
            acc = acc + w_ref[j:j + 1, :] * sh[halo:halo + sub, :]
        y = acc * _sigmoid(acc)
        partner = jnp.where(first_half, pltpu.roll(y, LANES - 16, axis=1), pltpu.roll(y, 16, axis=1))
        y = y * cos_ref[lo:lo + sub, :] + partner * sin_ref[lo:lo + sub, :]
        o_ref[0, lo:lo + sub, :] = (y * scale).astype(o_ref.dtype)


def _qkprep(qk, conv_w, cos_t, sin_t):
    b, t, w = qk.shape
    sub = min(t, 1024)
    kern = functools.partial(_qkprep_kernel, sub=sub, halo=16)
    return pl.pallas_call(
        kern,
        out_shape=jax.ShapeDtypeStruct((b, t, w), BF16),
        grid=(b, w // LANES),
        in_specs=[pl.BlockSpec((1, t, LANES), lambda i, j: (i, 0, j)),
                  pl.BlockSpec((8, LANES), lambda i, j: (0, j)),
                  pl.BlockSpec((t, LANES), lambda i, j: (0, 0)),
                  pl.BlockSpec((t, LANES), lambda i, j: (0, 0))],
        out_specs=pl.BlockSpec((1, t, LANES), lambda i, j: (i, 0, j)),
        compiler_params=_cparams(("arbitrary", "arbitrary")),
        name="qkprep",
    )(qk, conv_w, cos_t, sin_t)


def _rope_tables(t):
    nf = ML_QK_DIM // 4
    inv = 1.0 / (ROPE_BASE ** (jnp.arange(nf, dtype=F32) / nf))
    pos = jnp.arange(t)
    ang_r = (pos // GRID_W).astype(F32)[:, None] * inv
    ang_c = (pos % GRID_W).astype(F32)[:, None] * inv
    cos = jnp.concatenate([jnp.cos(ang_r)] * 2 + [jnp.cos(ang_c)] * 2, axis=-1)
    sin = jnp.concatenate([-jnp.sin(ang_r), jnp.sin(ang_r), -jnp.sin(ang_c), jnp.sin(ang_c)], axis=-1)
    return jnp.tile(cos, (1, 2)), jnp.tile(sin, (1, 2))


def _ride_along_casts(rest, n_cast, n_out):
    outs = rest[n_cast:n_cast + n_out]
    for src, dst in zip(rest[:n_cast], rest[n_cast + n_out:2 * n_cast + n_out]):
        dst[...] = src[...].astype(dst.dtype)
    return outs, rest[2 * n_cast + n_out:]


def _cast_specs(casts, n_steps, index_map):
    specs = []
    for a in casts:
        rows = a.shape[0] // n_steps
        assert a.shape[0] % n_steps == 0 and rows % ROW_ALIGN == 0
        specs.append(pl.BlockSpec((rows, a.shape[1]), index_map))
    return specs


def _mlstm_kernel(qf_ref, kf_ref, vf_ref, gf_ref, qb_ref, kb_ref, vb_ref, gb_ref, bg_ref,
                  c0_ref, m0_ref, *rest, n_cast):
    (hf_ref, hb_ref, cout_ref, mout_ref), (c_scr, m_scr) = _ride_along_casts(rest, n_cast, 4)
    step = pl.program_id(1)
    n_steps = pl.num_programs(1)
    L = ML_CHUNK

    @pl.when(step == 0)
    def _():
        c_scr[...] = c0_ref[0]
        m_scr[...] = m0_ref[0]

    row_i = lax.broadcasted_iota(jnp.int32, (L, L), 0)
    col_i = lax.broadcasted_iota(jnp.int32, (L, L), 1)
    lane = lax.broadcasted_iota(jnp.int32, (1, LANES), 1)
    tri_lo = (col_i <= row_i)
    tri_up = (col_i >= row_i)
    ones_v = jnp.ones((L, ML_V_DIM), BF16)

    for d, (q_ref, k_ref, v_ref, g_ref, h_ref) in enumerate(
            ((qf_ref, kf_ref, vf_ref, gf_ref, hf_ref), (qb_ref, kb_ref, vb_ref, gb_ref, hb_ref))):
        tri = tri_lo if d == 0 else tri_up
        g = g_ref[0] + bg_ref[...]
        logf = jnp.minimum(g, 0.0) - jnp.log(1.0 + jnp.exp(-jnp.abs(g)))
        tri_b = jnp.where(tri, 1.0, 0.0).astype(BF16)
        l1 = logf.astype(BF16)
        r1 = logf - l1.astype(F32)
        l2 = r1.astype(BF16)
        l3 = (r1 - l2.astype(F32)).astype(BF16)
        cum = _dot(tri_b, l1) + (_dot(tri_b, l2) + _dot(tri_b, l3))
        f_lo = 4 + 8 * d
        z = jnp.where((lane >= f_lo) & (lane < f_lo + ML_HEADS), cum, g)
        zt = z.T
        end = L - 1 if d == 0 else 0
        for h in range(ML_HEADS):
            li, lb = 8 * d + h, f_lo + h
            pair, half = h // 2, h % 2
            head_mask = (lane >= 64 * half) & (lane < 64 * half + 64)
            bcol = jnp.sum(jnp.where(lane == lb, z, 0.0), axis=-1, keepdims=True)
            icol = jnp.sum(jnp.where(lane == li, z, 0.0), axis=-1, keepdims=True)
            brow = zt[lb:lb + 1, :]
            irow = zt[li:li + 1, :]
            total = bcol[end:end + 1, :]
            m_prev = m_scr[d, h:h + 1, 0:1]

            qm = jnp.where(head_mask, q_ref[0, :, pair * LANES:(pair + 1) * LANES], 0).astype(BF16)
            km = jnp.where(head_mask, k_ref[0, :, pair * LANES:(pair + 1) * LANES], 0).astype(BF16)
            vext = jnp.concatenate([v_ref[0, :, h * ML_V_DIM:(h + 1) * ML_V_DIM], ones_v], axis=1)

            dmat = jnp.where(tri, bcol - brow + irow, NEG_BIG)
            m_prev_term = bcol + m_prev
            m_t = jnp.maximum(jnp.max(dmat, axis=-1, keepdims=True), m_prev_term)
            sp = _dot_nt(qm, km) * jnp.exp(dmat - m_t)
            inter = jnp.exp(m_prev_term - m_t)
            c_prev = c_scr[d, h]
            r = _dot(sp.astype(BF16), vext) + inter * _dot(qm, c_prev.astype(BF16))
            num = r[:, :ML_V_DIM]
            den = r[:, ML_V_DIM:]
            h_ref[0, :, h * ML_V_DIM:(h + 1) * ML_V_DIM] = (
                num / jnp.maximum(jnp.abs(den), jnp.exp(-m_t))).astype(h_ref.dtype)

            a = total - bcol + icol
            m_loc = jnp.max(a, axis=0, keepdims=True)
            kw = (km.astype(F32) * jnp.exp(a - m_loc)).astype(BF16)
            c_loc = _dot_tn(kw, vext)
            m_new = jnp.maximum(total + m_prev, m_loc)
            c_scr[d, h] = jnp.exp(total + m_prev - m_new) * c_prev + jnp.exp(m_loc - m_new) * c_loc
            m_scr[d, h:h + 1, :] = jnp.broadcast_to(m_new, (1, LANES))

    @pl.when(step == n_steps - 1)
    def _():
        cout_ref[0] = c_scr[...]
        mout_ref[0] = m_scr[...]


def _mlstm(qk, v, gates, bg, c0, m0, casts=()):
    b, t, _ = qk.shape
    L = ML_CHUNK
    nc = t // L
    fwd = lambda n, blk: pl.BlockSpec((1, L, n), lambda i, c: (i, c, blk))
    bwd = lambda n, blk: pl.BlockSpec((1, L, n), lambda i, c: (i, nc - 1 - c, blk))
    st_c = pl.BlockSpec((1,) + c0.shape[1:], lambda i, c: (i, 0, 0, 0, 0))
    st_m = pl.BlockSpec((1,) + m0.shape[1:], lambda i, c: (i, 0, 0, 0))
    cast_specs = _cast_specs(casts, b * nc, lambda i, c: (i * nc + c, 0))
    return pl.pallas_call(
        functools.partial(_mlstm_kernel, n_cast=len(casts)),
        out_shape=(jax.ShapeDtypeStruct((b, t, ML_V_W), BF16),
                   jax.ShapeDtypeStruct((b, t, ML_V_W), BF16),
                   jax.ShapeDtypeStruct(c0.shape, F32),
                   jax.ShapeDtypeStruct(m0.shape, F32),
                   *[jax.ShapeDtypeStruct(a.shape, BF16) for a in casts]),
        grid=(b, nc),
        in_specs=[fwd(ML_QK_W, 0), fwd(ML_QK_W, 1), fwd(ML_V_W, 0), fwd(LANES, 0),
                  bwd(ML_QK_W, 0), bwd(ML_QK_W, 1), bwd(ML_V_W, 0), bwd(LANES, 0),
                  pl.BlockSpec((1, LANES), lambda i, c: (0, 0)), st_c, st_m, *cast_specs],
        out_specs=(fwd(ML_V_W, 0), bwd(ML_V_W, 0), st_c, st_m, *cast_specs),
        scratch_shapes=[pltpu.VMEM(c0.shape[1:], F32), pltpu.VMEM(m0.shape[1:], F32)],
        compiler_params=_cparams(("arbitrary", "arbitrary")),
        name="mlstm",
    )(qk, qk, v, gates, qk, qk, v, gates, bg, c0, m0, *casts)


def _attn_kernel(case_ref, ws_ref, q_ref, k_ref, v_ref, kc_ref, vc_ref, bias_ref, *rest, n_cast):
    (o_ref,), _ = _ride_along_casts(rest, n_cast, 1)
    j = pl.program_id(2)
    nk = ATT_KROWS * GRID_W
    start = pl.multiple_of(ws_ref[j] * GRID_W, GRID_W)
    lane = lax.broadcasted_iota(jnp.int32, (1, LANES), 1)
    for pp in range(ATT_PAIRS):
        lanes = slice(pp * LANES, (pp + 1) * LANES)
        q = q_ref[0, :, lanes]
        k = k_ref[0, pl.ds(start, nk), lanes]
        v = v_ref[0, pl.ds(start, nk), lanes]
        kc = kc_ref[0, :, lanes]
        vc = vc_ref[0, :, lanes]
        acc = jnp.zeros(q.shape, F32)
        for h in range(2):
            head_mask = (lane >= NA_HEAD_DIM * h) & (lane < NA_HEAD_DIM * (h + 1))
            qh = (jnp.where(head_mask, q, 0).astype(F32) * (NA_HEAD_DIM ** -0.5 * LOG2E)).astype(BF16)
            s = _dot_nt(qh, k) + bias_ref[0, 2 * pp + h]
            sc = _dot_nt(qh, kc)
            m = jnp.maximum(jnp.max(s, axis=-1, keepdims=True), jnp.max(sc, axis=-1, keepdims=True))
            p = jnp.exp2(s - m)
            pc = jnp.exp2(sc - m)
            vh = jnp.where(head_mask, v, 1).astype(BF16)
            vch = jnp.where(head_mask, vc, 1).astype(BF16)
            o = _dot(p.astype(BF16), vh) + _dot(pc.astype(BF16), vch)
            acc = acc + jnp.where(head_mask, o / pltpu.roll(o, NA_HEAD_DIM, axis=1), 0.0)
        o_ref[0, :, lanes] = acc.astype(o_ref.dtype)


def _attn_plan(rows):
    kh = min(NA_WIN_H, rows)
    nj = rows // ATT_ROWS
    rs = lambda r: int(np.clip(r - kh // 2, 0, rows - kh))
    ws = [int(np.clip(ATT_ROWS * j - kh // 2, 0, rows - ATT_KROWS)) for j in range(nj)]
    sigs, case = [], []
    for j in range(nj):
        r0 = ATT_ROWS * j
        sig = (ws[j] - r0,) + tuple(rs(r0 + a) - r0 for a in range(ATT_ROWS))
        if sig not in sigs:
            sigs.append(sig)
        case.append(sigs.index(sig))
    return np.asarray(ws, np.int32), np.asarray(case, np.int32), sigs, kh


def _attn_bias(bias_table, sigs, kh):
    col_start = np.clip(np.arange(GRID_W) - NA_WIN_W // 2, 0, GRID_W - NA_WIN_W)
    c = np.arange(GRID_W)
    cidx = c[None, :] - c[:, None] + (NA_WIN_W - 1)
    col_ok = (c[None, :] >= col_start[:, None]) & (c[None, :] < col_start[:, None] + NA_WIN_W)
    expand = (np.arange(2 * NA_WIN_W - 1)[:, None, None] == cidx[None]).astype(np.float32)
    out = []
    for sig in sigs:
        wsr, rsr = sig[0], np.asarray(sig[1:])
        a = np.arange(ATT_ROWS)[:, None]
        rk = wsr + np.arange(ATT_KROWS)[None, :]
        row_ok = (rk >= rsr[:, None]) & (rk < rsr[:, None] + kh)
        ridx = np.clip(rk - a + (NA_WIN_H - 1), 0, 2 * NA_WIN_H - 2)
        rows = bias_table[:, ridx, :]
        full = jnp.einsum('haiv,vqk->haqik', rows, jnp.asarray(expand), precision=HIGHEST)
        ok = row_ok[:, None, :, None] & col_ok[None, :, None, :]
        full = jnp.where(ok[None], full * LOG2E, NEG_BIG)
        out.append(full.reshape(full.shape[0], ATT_ROWS * GRID_W, ATT_KROWS * GRID_W))
    return jnp.stack(out).astype(F32)


def _attn(a_lat, a_ctx, bias_table, casts=()):
    b, t, _ = a_lat.shape
    lc = a_ctx.shape[1]
    rows = t // GRID_W
    ws, case, sigs, kh = _attn_plan(rows)
    bias = _attn_bias(bias_table, sigs, kh)
    tq = ATT_ROWS * GRID_W
    nk = ATT_KROWS * GRID_W
    bw = ATT_PAIRS * LANES
    n_blk = NA_W // bw
    nj = rows // ATT_ROWS
    cast_specs = _cast_specs(casts, n_blk * b * nj, lambda p, i, j, cs, w: ((p * b + i) * nj + j, 0))
    grid_spec = pltpu.PrefetchScalarGridSpec(
        num_scalar_prefetch=2,
        grid=(n_blk, b, nj),
        in_specs=[pl.BlockSpec((1, tq, bw), lambda p, i, j, cs, w: (i, j, p)),
                  pl.BlockSpec((1, t, bw), lambda p, i, j, cs, w: (i, 0, n_blk + p)),
                  pl.BlockSpec((1, t, bw), lambda p, i, j, cs, w: (i, 0, 2 * n_blk + p)),
                  pl.BlockSpec((1, lc, bw), lambda p, i, j, cs, w: (i, 0, n_blk + p)),
                  pl.BlockSpec((1, lc, bw), lambda p, i, j, cs, w: (i, 0, 2 * n_blk + p)),
                  pl.BlockSpec((1, 2 * ATT_PAIRS, tq, nk), lambda p, i, j, cs, w: (cs[j], p, 0, 0)),
                  *cast_specs],
        out_specs=(pl.BlockSpec((1, tq, bw), lambda p, i, j, cs, w: (i, j, p)), *cast_specs),
    )
    return pl.pallas_call(
        functools.partial(_attn_kernel, n_cast=len(casts)),
        out_shape=(jax.ShapeDtypeStruct((b, t, NA_W), BF16),
                   *[jax.ShapeDtypeStruct(a.shape, BF16) for a in casts]),
        grid_spec=grid_spec,
        compiler_params=_cparams(("arbitrary", "arbitrary", "arbitrary")),
        name="nattn",
    )(jnp.asarray(case), jnp.asarray(ws), a_lat, a_lat, a_lat, a_ctx, a_ctx, bias, *casts)


def _outproj_kernel(att_ref, hf_ref, hb_ref, om_ref, x_ref, mod_ref, ng_ref, wa_ref, wm_ref,
                    lng_ref, lnb_ref, wrh_ref, wrl_ref, x1_ref, hm_ref, aff_ref, affr_ref):
    h = hf_ref[0].astype(F32) + hb_ref[0].astype(F32)
    parts = []
    for hd in range(ML_HEADS):
        hh = h[:, hd * ML_V_DIM:(hd + 1) * ML_V_DIM]
        mu = jnp.mean(hh, axis=-1, keepdims=True)
        var = jnp.mean(jnp.square(hh - mu), axis=-1, keepdims=True)
        parts.append((hh - mu) * lax.rsqrt(var + LN_EPS))
    hn = jnp.concatenate(parts, axis=1) * ng_ref[...]
    ml = (hn * _sigmoid(om_ref[0].astype(F32))).astype(BF16)
    mix = _dot(att_ref[0], wa_ref[...]) + _dot(ml, wm_ref[...])
    y = DEEPNORM_ALPHA * x_ref[0] + mod_ref[0, 2:3, :] * mix
    mu = jnp.mean(y, axis=-1, keepdims=True)
    var = jnp.mean(jnp.square(y - mu), axis=-1, keepdims=True)
    x1 = (y - mu) * lax.rsqrt(var + LN_EPS) * lng_ref[...] + lnb_ref[...]
    x1_ref[0] = x1
    hm = x1 * (1.0 + mod_ref[0, 4:5, :]) + mod_ref[0, 3:4, :]
    h_hi = hm.astype(BF16)
    h_lo = (hm - h_hi.astype(F32)).astype(BF16)
    bits = pltpu.bitcast(h_hi.astype(F32), jnp.uint32)
    half = bits.shape[1] // 2
    word = (bits[:, :half] >> 16) | (bits[:, half:] & jnp.uint32(0xFFFF0000))
    hm_ref[0] = pltpu.bitcast(word, jnp.int32)
    logits = _dot(h_hi, wrh_ref[...]) + (_dot(h_lo, wrh_ref[...]) + _dot(h_hi, wrl_ref[...]))
    lane = lax.broadcasted_iota(jnp.int32, (1, LANES), 1)
    logits = jnp.where(lane < N_EXPERTS, logits, NEG_BIG)
    e = jnp.exp(logits - jnp.max(logits, axis=-1, keepdims=True))
    aff = e / jnp.sum(e, axis=-1, keepdims=True)
    affr_ref[0] = aff
    aff_ref[0] = aff.T[:N_EXPERTS, :]


def _outproj(att, hf, hb, om, x, mod, ng, w_att, w_ml, lng, lnb, wr_hi, wr_lo, tm):
    b, t, d = x.shape
    row = lambda n: pl.BlockSpec((1, tm, n), lambda i, j: (i, j, 0))
    full = lambda w: pl.BlockSpec(w.shape, lambda i, j: (0,) * w.ndim)
    return pl.pallas_call(
        _outproj_kernel,
        out_shape=(jax.ShapeDtypeStruct((b, t, d), F32),
                   jax.ShapeDtypeStruct((b, t, d // 2), jnp.int32),
                   jax.ShapeDtypeStruct((b, N_EXPERTS, t), F32),
                   jax.ShapeDtypeStruct((b, t, LANES), F32)),
        grid=(b, t // tm),
        in_specs=[row(NA_W), row(ML_V_W), row(ML_V_W), row(ML_V_W), row(d),
                  pl.BlockSpec((1, 8, d), lambda i, j: (i, 0, 0)),
                  full(ng), full(w_att), full(w_ml), full(lng), full(lnb), full(wr_hi), full(wr_lo)],
        out_specs=(row(d), row(d // 2), pl.BlockSpec((1, N_EXPERTS, tm), lambda i, j: (i, 0, j)), row(LANES)),
        compiler_params=_cparams(("arbitrary", "arbitrary")),
        name="outproj",
    )(att, hf, hb, om, x, mod, ng, w_att, w_ml, lng, lnb, wr_hi, wr_lo)


UNSELECTED = -1e6


def _select_kernel(aff_ref, idx_ref, pos_ref, off_ref, cum_scr, sel_scr, offs_v, offs_s, dsem, *, cap):
    t = aff_ref.shape[2]
    nb = t // LANES
    lane = lax.broadcasted_iota(jnp.int32, (1, LANES), 1)
    keys = lambda: pltpu.bitcast(aff_ref[0], jnp.int32)

    def count(mask):
        return jnp.sum(jnp.where(mask, 1.0, 0.0), axis=-1, keepdims=True)

    def search(_, c):
        lo, hi = c
        mid = lo + jnp.right_shift(hi - lo, 1)
        ge = count(keys() >= mid) >= cap
        return jnp.where(ge, mid, lo), jnp.where(ge, hi, mid)

    lo0 = jnp.zeros((N_EXPERTS, 1), jnp.int32)
    hi0 = jnp.full((N_EXPERTS, 1), 0x7F800000, jnp.int32)
    thr, _ = lax.fori_loop(0, 31, search, (lo0, hi0))

    r_i = lax.broadcasted_iota(jnp.int32, (LANES, LANES), 0)
    c_i = lax.broadcasted_iota(jnp.int32, (LANES, LANES), 1)
    strict = jnp.where(r_i < c_i, 1.0, 0.0).astype(BF16)
    tr_i = lax.broadcasted_iota(jnp.int32, (t, LANES), 0)
    tc_i = lax.broadcasted_iota(jnp.int32, (t, LANES), 1)
    block_ind = jnp.where(jnp.right_shift(tr_i, 7) == tc_i, 1.0, 0.0).astype(BF16)

    def prefix(x01):
        xb = x01.astype(BF16)
        offs = _dot(_dot(xb, block_ind).astype(BF16), strict)
        for j in range(nb):
            off_j = jnp.sum(jnp.where(lane == j, offs, 0.0), axis=-1, keepdims=True)
            cum_scr[:, j * LANES:(j + 1) * LANES] = _dot(xb[:, j * LANES:(j + 1) * LANES], strict) + off_j
        return offs

    k = keys()
    gt = k > thr
    eq = k == thr
    need = cap - count(gt)
    prefix(jnp.where(eq, 1.0, 0.0))
    sel = jnp.where(gt | (eq & (cum_scr[...] < need)), 1.0, 0.0)
    sel_scr[...] = sel
    offs = prefix(sel)

    pad = jnp.zeros((LANES - N_EXPERTS, LANES), F32)
    for j in range(nb):
        blk = jnp.where(sel_scr[:, j * LANES:(j + 1) * LANES] > 0.0, cum_scr[:, j * LANES:(j + 1) * LANES], UNSELECTED)
        pos_ref[0, j * LANES:(j + 1) * LANES, :] = jnp.concatenate([blk, pad], axis=0).T
    off_ref[0] = jnp.concatenate([offs, pad], axis=0).T

    cum_scr[...] = cum_scr[...] + sel_scr[...]
    idx_ref[...] = jnp.zeros(idx_ref.shape, idx_ref.dtype)
    n_groups = cap // LANES
    real = lane < nb
    ends = pltpu.roll(offs, LANES - 1, axis=1)
    bounds = jnp.zeros((N_EXPERTS, LANES), F32)
    for g in range(n_groups):
        lo = jnp.sum(jnp.where(real & (ends <= g * LANES), 1.0, 0.0), axis=-1, keepdims=True)
        hi = jnp.sum(jnp.where(real & (offs <= g * LANES + LANES - 1), 1.0, 0.0), axis=-1, keepdims=True)
        bounds = jnp.where(lane == g, lo, jnp.where(lane == n_groups + g, hi, bounds))
    offs_v[...] = bounds.astype(jnp.int32)
    to_smem = pltpu.make_async_copy(offs_v, offs_s, dsem)
    to_smem.start()
    to_smem.wait()
    sub = lax.broadcasted_iota(jnp.int32, (LANES, LANES), 0).astype(F32)
    for e in range(N_EXPERTS):
        def group(pg, carry, e=e):
            first = jnp.asarray(pg * LANES, jnp.int32)
            jlo = offs_s[e, pg]
            jhi = offs_s[e, n_groups + pg]
            slots = first.astype(F32) + sub

            def block(jb, acc):
                c = cum_scr[pl.ds(e, 1), pl.ds(pl.multiple_of(jb * LANES, LANES), LANES)]
                return acc + jnp.where(jnp.broadcast_to(c, (LANES, LANES)) <= slots, 1.0, 0.0)

            acc = lax.fori_loop(jlo, jhi, block, jnp.zeros((LANES, LANES), F32))
            col = jnp.sum(acc, axis=-1, keepdims=True).astype(jnp.int32) + jlo * LANES
            idx_ref[0, pl.ds(pl.multiple_of(pg * LANES, LANES), LANES), e:e + 1] = col
            return carry

        lax.fori_loop(0, n_groups, group, 0)


def _select(aff_t, cap):
    b, e, t = aff_t.shape
    assert cap % LANES == 0 and t % LANES == 0 and t // LANES < LANES and 2 * (cap // LANES) <= LANES
    return pl.pallas_call(
        functools.partial(_select_kernel, cap=cap),
        out_shape=(jax.ShapeDtypeStruct((b, cap, LANES), jnp.int32),
                   jax.ShapeDtypeStruct((b, t, LANES), F32),
                   jax.ShapeDtypeStruct((b, LANES, LANES), F32)),
        grid=(b,),
        in_specs=[pl.BlockSpec((1, e, t), lambda i: (i, 0, 0))],
        out_specs=(pl.BlockSpec((1, cap, LANES), lambda i: (i, 0, 0)),
                   pl.BlockSpec((1, t, LANES), lambda i: (i, 0, 0)),
                   pl.BlockSpec((1, LANES, LANES), lambda i: (i, 0, 0))),
        scratch_shapes=[pltpu.VMEM((e, t), F32), pltpu.VMEM((e, t), F32),
                        pltpu.VMEM((e, LANES), jnp.int32), pltpu.SMEM((e, LANES), jnp.int32),
                        pltpu.SemaphoreType.DMA(())],
        compiler_params=_cparams(("arbitrary",)),
        name="select",
    )(aff_t)


SC_CORES = 2
SC_SUBCORES = 16
SC_CHUNK = 64
MOE_GROUPS = 4


def _gather_rows(table, idx):
    n = idx.shape[0]
    v, d = table.shape
    n_workers = SC_CORES * SC_SUBCORES
    per_w = n // n_workers
    assert n % (n_workers * SC_CHUNK) == 0 and d % LANES == 0 and table.dtype.itemsize == 4
    mesh = plsc.VectorSubcoreMesh(core_axis_name="c", subcore_axis_name="s",
                                  num_cores=SC_CORES, num_subcores=SC_SUBCORES)

    @functools.partial(
        pl.kernel, mesh=mesh,
        out_type=jax.ShapeDtypeStruct((n, d), table.dtype),
        scratch_types=[pltpu.VMEM((SC_CHUNK,), jnp.int32),
                       pltpu.VMEM((SC_CHUNK, d), table.dtype),
                       pltpu.SemaphoreType.DMA],
        name="row_gather")
    def gather(table_hbm, idx_hbm, out_hbm, idx_v, rows_v, sem):
        base = (lax.axis_index("s") * SC_CORES + lax.axis_index("c")) * per_w

        @pl.loop(0, per_w, step=SC_CHUNK)
        def _(o):
            pltpu.sync_copy(idx_hbm.at[pl.ds(base + o, SC_CHUNK)], idx_v)
            pltpu.async_copy(table_hbm.at[idx_v], rows_v, sem).wait()
            pltpu.sync_copy(rows_v, out_hbm.at[pl.ds(base + o, SC_CHUNK)])

    return gather(table, idx)


def _moe_kernel(x_ref, wg_ref, wu_ref, wd_ref, o_ref, x_scr, act_scr, *, tf):
    w = pltpu.bitcast(x_ref[0], jnp.uint32)
    half = w.shape[1]
    x_scr[:, :half] = pltpu.bitcast(w << 16, F32).astype(BF16)
    x_scr[:, half:] = pltpu.bitcast(w & jnp.uint32(0xFFFF0000), F32).astype(BF16)
    x = x_scr[...]
    for c in range(act_scr.shape[1] // tf):
        cols = slice(c * tf, (c + 1) * tf)
        hg = _dot(x, wg_ref[0, :, cols])
        hu = _dot(x, wu_ref[0, :, cols])
        act_scr[:, cols] = (hg * _sigmoid(hg) * hu).astype(BF16)
    o_ref[0] = _dot(act_scr[...], wd_ref[0]).astype(o_ref.dtype)


def _moe(xe, w_gate, w_up, w_down, e0, tm, tf):
    e, m, dw = xe.shape
    d = 2 * dw
    dff = w_gate.shape[2]
    return pl.pallas_call(
        functools.partial(_moe_kernel, tf=tf),
        out_shape=jax.ShapeDtypeStruct((e, m, d), BF16),
        grid=(e, m // tm),
        in_specs=[pl.BlockSpec((1, tm, dw), lambda i, j: (i, j, 0)),
                  pl.BlockSpec((1, d, dff), lambda i, j: (i + e0, 0, 0)),
                  pl.BlockSpec((1, d, dff), lambda i, j: (i + e0, 0, 0)),
                  pl.BlockSpec((1, dff, d), lambda i, j: (i + e0, 0, 0))],
        out_specs=pl.BlockSpec((1, tm, d), lambda i, j: (i, j, 0)),
        scratch_shapes=[pltpu.VMEM((tm, d), BF16), pltpu.VMEM((tm, dff), BF16)],
        compiler_params=_cparams(("arbitrary", "arbitrary")),
        name="moe",
    )(xe, w_gate, w_up, w_down)


CMB_TM = 256
CMB_WIN = 64


def _combine_kernel(p0_ref, x1_ref, pos_ref, aff_ref, mod_ref, lng_ref, lnb_ref, *rest, cap, n_parts):
    y_parts = rest[:n_parts]
    o_ref, ybuf, yextra, acc_ref, sem, sem_x = rest[n_parts:]
    per_part = N_EXPERTS // n_parts
    _combine_body(p0_ref, x1_ref, pos_ref, aff_ref, mod_ref, lng_ref, lnb_ref,
                  lambda e: y_parts[e // per_part].at[e % per_part], y_parts[0].shape[1],
                  o_ref, ybuf, yextra, acc_ref, sem, sem_x, cap)


def _combine_body(p0_ref, x1_ref, pos_ref, aff_ref, mod_ref, lng_ref, lnb_ref, y_of, m_rows, o_ref,
                  ybuf, yextra, acc_ref, sem, sem_x, cap):
    b = pl.program_id(0)
    j = pl.program_id(1)
    nt = pl.num_programs(1)
    g = b * nt + j
    n_steps = pl.num_programs(0) * nt
    slot = lax.rem(g, 2)
    win = CMB_WIN
    kdim = N_EXPERTS * win

    def window(bb, jj, r):
        base = (bb * (nt + 1) + jj) * N_EXPERTS
        los, rows = [], []
        for e in range(N_EXPERTS):
            lo = (p0_ref[base + e] // ROW_ALIGN) * ROW_ALIGN + r * win
            los.append(lo)
            rows.append(jnp.minimum(bb * cap + lo, m_rows - win))
        return los, rows

    def copies(rows, buf, sems):
        return [pltpu.make_async_copy(y_of(e).at[pl.ds(pl.multiple_of(rows[e], ROW_ALIGN), win), :],
                                      buf.at[pl.ds(e * win, win), :], sems.at[e])
                for e in range(N_EXPERTS)]

    los0, rows0 = window(b, j, 0)

    @pl.when(g == 0)
    def _():
        for c in copies(rows0, ybuf.at[0], sem.at[0]):
            c.start()

    @pl.when(g + 1 < n_steps)
    def _():
        wrap = j + 1 == nt
        _, rows_n = window(jnp.where(wrap, b + 1, b), jnp.where(wrap, 0, j + 1), 0)
        for c in copies(rows_n, ybuf.at[1 - slot], sem.at[1 - slot]):
            c.start()

    lane = lax.broadcasted_iota(jnp.int32, (1, LANES), 1)
    e_i = lax.broadcasted_iota(jnp.int32, (LANES, kdim), 0)
    k_i = lax.broadcasted_iota(jnp.int32, (LANES, kdim), 1)
    expand = jnp.where(k_i // win == e_i, 1.0, 0.0).astype(BF16)
    col_in_win = (lax.broadcasted_iota(jnp.int32, (1, kdim), 1) % win).astype(F32)
    aff_cols = _dot(aff_ref[0].astype(BF16), expand)

    def lane_vec(vals):
        v = jnp.zeros((1, LANES), F32)
        for e in range(N_EXPERTS):
            v = jnp.where(lane == e, jnp.asarray(vals[e]).astype(F32), v)
        return v

    def expand_rows(los, rows, buf):
        rel = pos_ref[0] - lane_vec(los)
        shift = lane_vec([b * cap + lo - row for lo, row in zip(los, rows)])
        tgt = jnp.where((rel >= 0.0) & (rel < float(win)), rel + shift, -1.0)
        tgt_cols = _dot(tgt.astype(BF16), expand)
        s = jnp.where(tgt_cols == col_in_win, aff_cols, 0.0).astype(BF16)
        return _dot(s, buf[...])

    for c in copies(rows0, ybuf.at[slot], sem.at[slot]):
        c.wait()
    acc_ref[...] = expand_rows(los0, rows0, ybuf.at[slot])

    base = (b * (nt + 1) + j) * N_EXPERTS
    rounds = jnp.int32(0)
    for e in range(N_EXPERTS):
        span = p0_ref[base + N_EXPERTS + e] - (p0_ref[base + e] // ROW_ALIGN) * ROW_ALIGN
        rounds = jnp.maximum(rounds, (span + win - 1) // win)

    def extra(r, carry):
        los, rows = window(b, j, r)
        cps = copies(rows, yextra, sem_x)
        for c in cps:
            c.start()
        for c in cps:
            c.wait()
        acc_ref[...] += expand_rows(los, rows, yextra)
        return carry

    lax.fori_loop(1, rounds, extra, 0)

    y = DEEPNORM_ALPHA * x1_ref[0] + mod_ref[0, 5:6, :] * acc_ref[...]
    mu = jnp.mean(y, axis=-1, keepdims=True)
    var = jnp.mean(jnp.square(y - mu), axis=-1, keepdims=True)
    o_ref[0] = (y - mu) * lax.rsqrt(var + LN_EPS) * lng_ref[...] + lnb_ref[...]


def _combine(p0tab, x1, pos_t, aff_r, mod, lng, lnb, ys, cap):
    b, t, d = x1.shape
    tm = CMB_TM
    kdim = N_EXPERTS * CMB_WIN
    row = lambda n: pl.BlockSpec((1, tm, n), lambda i, j, p: (i, j, 0))
    vec = pl.BlockSpec((1, d), lambda i, j, p: (0, 0))
    grid_spec = pltpu.PrefetchScalarGridSpec(
        num_scalar_prefetch=1,
        grid=(b, t // tm),
        in_specs=[row(d), row(LANES), row(LANES), pl.BlockSpec((1, 8, d), lambda i, j, p: (i, 0, 0)), vec, vec,
                  *[pl.BlockSpec(memory_space=pl.ANY) for _ in ys]],
        out_specs=row(d),
        scratch_shapes=[pltpu.VMEM((2, kdim, d), ys[0].dtype), pltpu.VMEM((kdim, d), ys[0].dtype),
                        pltpu.VMEM((tm, d), F32),
                        pltpu.SemaphoreType.DMA((2, N_EXPERTS)), pltpu.SemaphoreType.DMA((N_EXPERTS,))],
    )
    return pl.pallas_call(
        functools.partial(_combine_kernel, cap=cap, n_parts=len(ys)),
        out_shape=jax.ShapeDtypeStruct((b, t, d), F32),
        grid_spec=grid_spec,
        compiler_params=_cparams(("arbitrary", "arbitrary")),
        name="combine",
    )(p0tab, x1, pos_t, aff_r, mod, lng, lnb, *ys)


def _split_w_in(w_in):
    a_end = 3 * NA_W
    qk_end = a_end + 2 * ML_QK_W
    v_end = qk_end + ML_V_W
    o_end = v_end + ML_V_W
    wg = jnp.pad(w_in[:, o_end:], ((0, 0), (0, LANES - ML_N_GATES)))
    return tuple(w.astype(BF16) for w in
                 (w_in[:, :a_end], w_in[:, a_end:qk_end], w_in[:, qk_end:v_end], w_in[:, v_end:o_end], wg))


def _layer(x, ctx, c, c_ctx, w_ada, b_ada, w_in, b_gate, conv_qk, na_rel_bias, ml_norm_g, w_out,
           ln1_g, ln1_b, w_router, w_expert_gate, w_expert_up, w_expert_down, ln2_g, ln2_b):
    b, t, d = x.shape
    lc = ctx.shape[1]

    cs = jnp.zeros((16, d), F32).at[:b].set(c).at[b].set(c_ctx)
    m = _ada(cs, w_ada, b_ada)
    mod = jnp.pad(m[:b].reshape(b, 6, d), ((0, 0), (0, 2), (0, 0)))
    mod_ctx = jnp.broadcast_to(jnp.pad(m[b].reshape(6, d), ((0, 2), (0, 0)))[None], (b, 8, d))

    ws = _split_w_in(w_in)
    a_lat, qk_lat, v_lat, o_lat, g_lat = _inproj(x, mod, ws, tm=512)
    a_ctx, qk_ctx, v_ctx, _, g_ctx = _inproj(ctx, mod_ctx, ws, tm=lc)

    conv_w = jnp.pad(conv_qk, ((0, 8 - CONV_K), (0, 0)))
    cos_t, sin_t = _rope_tables(t)
    qk_lat = _qkprep(qk_lat, conv_w, cos_t, sin_t)
    qk_ctx = _qkprep(qk_ctx, conv_w, jnp.ones((lc, LANES), F32), jnp.zeros((lc, LANES), F32))

    bg = jnp.pad(b_gate, (0, LANES - ML_N_GATES)).reshape(1, LANES)
    c0 = jnp.zeros((b, 2, ML_HEADS, LANES, 2 * ML_V_DIM), F32)
    m0 = jnp.zeros((b, 2, 8, LANES), F32)
    n_e, _, dff = w_expert_gate.shape
    _, _, c1, m1 = _mlstm(qk_ctx, v_ctx, g_ctx, bg, c0, m0)
    hf, hb, _, _, wd_b = _mlstm(qk_lat, v_lat, g_lat, bg, c1, m1,
                                casts=(w_expert_down.reshape(n_e * dff, d),))
    att, wg_b, wu_b = _attn(a_lat, a_ctx, na_rel_bias,
                            casts=(w_expert_gate.reshape(n_e * d, dff), w_expert_up.reshape(n_e * d, dff)))

    wr = jnp.pad(w_router, ((0, 0), (0, LANES - N_EXPERTS)))
    wr_hi = wr.astype(BF16)
    wr_lo = (wr - wr_hi.astype(F32)).astype(BF16)
    x1, hm, aff_t, aff_r = _outproj(att, hf, hb, o_lat, x, mod, ml_norm_g.reshape(1, -1),
                                    w_out[:NA_W].astype(BF16), w_out[NA_W:].astype(BF16),
                                    ln1_g.reshape(1, d), ln1_b.reshape(1, d), wr_hi, wr_lo, tm=256)

    cap = CAPACITY_FACTOR * t // N_EXPERTS
    idx_t, pos_t, off_t = _select(aff_t, cap)
    rows = jnp.transpose(idx_t[:, :, :N_EXPERTS], (2, 0, 1)) + (jnp.arange(b, dtype=jnp.int32) * t)[None, :, None]
    per = N_EXPERTS // MOE_GROUPS
    wg3, wu3, wd3 = wg_b.reshape(n_e, d, dff), wu_b.reshape(n_e, d, dff), wd_b.reshape(n_e, dff, d)
    ys = []
    for k in range(MOE_GROUPS):
        xe = _gather_rows(hm.reshape(b * t, d // 2), rows[k * per:(k + 1) * per].reshape(-1))
        ys.append(_moe(xe.reshape(per, b * cap, d // 2), wg3, wu3, wd3, e0=k * per,
                       tm=min(1024, b * cap), tf=256))

    nt = t // CMB_TM
    p0 = off_t[:, 0:nt * (CMB_TM // LANES):CMB_TM // LANES, :N_EXPERTS]
    p0 = jnp.concatenate([p0, jnp.full((b, 1, N_EXPERTS), cap, F32)], axis=1)
    p0tab = p0.astype(jnp.int32).reshape(-1)
    return _combine(p0tab, x1, pos_t, aff_r, mod, ln2_g.reshape(1, d), ln2_b.reshape(1, d), ys, cap)


def kernel(x, c, ctx, c_ctx, w_ada, b_ada, w_in, b_gate, conv_qk, na_rel_bias, ml_norm_g, w_out,
           ln1_g, ln1_b, w_router, w_expert_gate, w_expert_up, w_expert_down, ln2_g, ln2_b):
    return _layer(x, ctx, c, c_ctx, w_ada[0], b_ada[0], w_in[0], b_gate[0], conv_qk[0], na_rel_bias[0],
                  ml_norm_g[0], w_out[0], ln1_g[0], ln1_b[0], w_router[0], w_expert_gate[0],
                  w_expert_up[0], w_expert_down[0], ln2_g[0], ln2_b[0])
```

```python
import functools

import numpy as np
import jax
import jax.numpy as jnp
from jax import lax
from jax.experimental import pallas as pl
from jax.experimental.pallas import tpu as pltpu
from jax.experimental.pallas import tpu_sc as plsc

F32 = jnp.float32
BF16 = jnp.bfloat16
HIGHEST = lax.Precision.HIGHEST

GRID_W = 64
NA_HEADS = 8
NA_HEAD_DIM = 64
NA_WIN_H = 8
NA_WIN_W = 16
NA_W = NA_HEADS * NA_HEAD_DIM
ML_HEADS = 4
ML_QK_DIM = 64
ML_V_DIM = 128
ML_QK_W = ML_HEADS * ML_QK_DIM
ML_V_W = ML_HEADS * ML_V_DIM
ML_CHUNK = 256
ML_SAMPLES = 2
ML_N_GATES = 4 * ML_HEADS
CONV_K = 5
ROPE_BASE = 10000.0
N_EXPERTS = 16
CAPACITY_FACTOR = 2
LN_EPS = 1e-5
DEPTH = 1
DEEPNORM_ALPHA = (2.0 * DEPTH) ** 0.25

LANES = 128
ROW_ALIGN = 16
NEG_BIG = -1e30
LOG2E = 1.4426950408889634
VMEM_LIMIT = 56 * 1024 * 1024

ATT_ROWS = 4
ATT_PAIRS = 4
ATT_KROWS = ATT_ROWS + NA_WIN_H - 1


def _cparams(sem):
    return pltpu.CompilerParams(dimension_semantics=sem, vmem_limit_bytes=VMEM_LIMIT)


def _sigmoid(x):
    return 1.0 / (1.0 + jnp.exp(-x))


def _dot(a, b):
    return jnp.dot(a, b, preferred_element_type=F32)


def _dot_nt(a, b):
    return lax.dot_general(a, b, (((1,), (1,)), ((), ())), preferred_element_type=F32)


def _dot_tn(a, b):
    return lax.dot_general(a, b, (((0,), (0,)), ((), ())), preferred_element_type=F32)


def _ada_kernel(c_ref, w_ref, b_ref, o_ref):
    c = c_ref[...]
    s = c * _sigmoid(c)
    o_ref[...] = jnp.dot(s, w_ref[...], precision=HIGHEST, preferred_element_type=F32) + b_ref[...]


def _ada(cs, w_ada, b_ada):
    rows, d = cs.shape
    n = w_ada.shape[1]
    tn = 512
    return pl.pallas_call(
        _ada_kernel,
        out_shape=jax.ShapeDtypeStruct((rows, n), F32),
        grid=(n // tn,),
        in_specs=[pl.BlockSpec((rows, d), lambda j: (0, 0)),
                  pl.BlockSpec((d, tn), lambda j: (0, j)),
                  pl.BlockSpec((1, tn), lambda j: (0, j))],
        out_specs=pl.BlockSpec((rows, tn), lambda j: (0, j)),
        compiler_params=_cparams(("arbitrary",)),
        name="ada",
    )(cs, w_ada, b_ada.reshape(1, n))


def _inproj_kernel(x_ref, mod_ref, wa_ref, wqk_ref, wv_ref, wo_ref, wg_ref,
                   a_ref, qk_ref, v_ref, o_ref, g_ref):
    xm = (x_ref[0] * (1.0 + mod_ref[0, 1:2, :]) + mod_ref[0, 0:1, :]).astype(BF16)
    a_ref[0] = _dot(xm, wa_ref[...]).astype(BF16)
    qk_ref[0] = _dot(xm, wqk_ref[...]).astype(BF16)
    v_ref[0] = _dot(xm, wv_ref[...]).astype(BF16)
    o_ref[0] = _dot(xm, wo_ref[...]).astype(BF16)
    g_ref[0] = _dot(xm, wg_ref[...])


def _inproj(x, mod, ws, tm):
    b, t, d = x.shape
    wa, wqk, wv, wo, wg = ws
    full = lambda w: pl.BlockSpec(w.shape, lambda i, j: (0, 0))
    row = lambda n: pl.BlockSpec((1, tm, n), lambda i, j: (i, j, 0))
    return pl.pallas_call(
        _inproj_kernel,
        out_shape=(jax.ShapeDtypeStruct((b, t, wa.shape[1]), BF16),
                   jax.ShapeDtypeStruct((b, t, wqk.shape[1]), BF16),
                   jax.ShapeDtypeStruct((b, t, wv.shape[1]), BF16),
                   jax.ShapeDtypeStruct((b, t, wo.shape[1]), BF16),
                   jax.ShapeDtypeStruct((b, t, wg.shape[1]), F32)),
        grid=(b, t // tm),
        in_specs=[row(d), pl.BlockSpec((1, 8, d), lambda i, j: (i, 0, 0)),
                  full(wa), full(wqk), full(wv), full(wo), full(wg)],
        out_specs=(row(wa.shape[1]), row(wqk.shape[1]), row(wv.shape[1]), row(wo.shape[1]),
                   row(wg.shape[1])),
        compiler_params=_cparams(("arbitrary", "arbitrary")),
        name="inproj",
    )(x, mod, wa, wqk, wv, wo, wg)


def _qkprep_kernel(x_ref, w_ref, cos_ref, sin_ref, o_ref, *, sub, halo):
    t = x_ref.shape[1]
    n_sub = t // sub
    lane = lax.broadcasted_iota(jnp.int32, (1, LANES), 1)
    first_half = (lane & 31) < 16
    scale = jnp.where(pl.program_id(1) < (ML_QK_W // LANES), ML_QK_DIM ** -0.5, 1.0).astype(F32)
    zeros = jnp.zeros((halo, LANES), x_ref.dtype)
    for s in range(n_sub):
        lo = s * sub
        top = x_ref[0, lo - halo:lo, :] if s > 0 else zeros
        bot = x_ref[0, lo + sub:lo + sub + halo, :] if s < n_sub - 1 else zeros
        ext = jnp.concatenate([top, x_ref[0, lo:lo + sub, :], bot], axis=0).astype(F32)
        n = sub + 2 * halo
        acc = jnp.zeros((sub, LANES), F32)
        for j in range(CONV_K):
            shift = (CONV_K // 2 - j) % n
            sh = ext if shift == 0 else pltpu.roll(ext, shift, axis=0)
            acc = acc + w_ref[j:j + 1, :] * sh[halo:halo + sub, :]
        y = acc * _sigmoid(acc)
        partner = jnp.where(first_half, pltpu.roll(y, LANES - 16, axis=1), pltpu.roll(y, 16, axis=1))
        y = y * cos_ref[lo:lo + sub, :] + partner * sin_ref[lo:lo + sub, :]
        o_ref[0, lo:lo + sub, :] = (y * scale).astype(o_ref.dtype)


def _qkprep(qk, conv_w, cos_t, sin_t):
    b, t, w = qk.shape
    sub = min(t, 1024)
    kern = functools.partial(_qkprep_kernel, sub=sub, halo=16)
    return pl.pallas_call(
        kern,
        out_shape=jax.ShapeDtypeStruct((b, t, w), BF16),
        grid=(b, w // LANES),
        in_specs=[pl.BlockSpec((1, t, LANES), lambda i, j: (i, 0, j)),
                  pl.BlockSpec((8, LANES), lambda i, j: (0, j)),
                  pl.BlockSpec((t, LANES), lambda i, j: (0, 0)),
                  pl.BlockSpec((t, LANES), lambda i, j: (0, 0))],
        out_specs=pl.BlockSpec((1, t, LANES), lambda i, j: (i, 0, j)),
        compiler_params=_cparams(("arbitrary", "arbitrary")),
        name="qkprep",
    )(qk, conv_w, cos_t, sin_t)


def _rope_tables(t):
    nf = ML_QK_DIM // 4
    inv = 1.0 / (ROPE_BASE ** (jnp.arange(nf, dtype=F32) / nf))
    pos = jnp.arange(t)
    ang_r = (pos // GRID_W).astype(F32)[:, None] * inv
    ang_c = (pos % GRID_W).astype(F32)[:, None] * inv
    cos = jnp.concatenate([jnp.cos(ang_r)] * 2 + [jnp.cos(ang_c)] * 2, axis=-1)
    sin = jnp.concatenate([-jnp.sin(ang_r), jnp.sin(ang_r), -jnp.sin(ang_c), jnp.sin(ang_c)], axis=-1)
    return jnp.tile(cos, (1, 2)), jnp.tile(sin, (1, 2))


def _ride_along_casts(rest, n_cast, n_out):
    outs = rest[n_cast:n_cast + n_out]
    for src, dst in zip(rest[:n_cast], rest[n_cast + n_out:2 * n_cast + n_out]):
        dst[...] = src[...].astype(dst.dtype)
    return outs, rest[2 * n_cast + n_out:]


def _cast_specs(casts, n_steps, index_map):
    specs = []
    for a in casts:
        rows = a.shape[0] // n_steps
        assert a.shape[0] % n_steps == 0 and rows % ROW_ALIGN == 0
        specs.append(pl.BlockSpec((rows, a.shape[1]), index_map))
    return specs


def _mlstm_kernel(qf_ref, kf_ref, vf_ref, gf_ref, qb_ref, kb_ref, vb_ref, gb_ref, bg_ref,
                  c0_ref, m0_ref, *rest, n_cast):
    (hf_ref, hb_ref, cout_ref, mout_ref), (c_scr, m_scr) = _ride_along_casts(rest, n_cast, 4)
    step = pl.program_id(1)
    n_steps = pl.num_programs(1)
    L = ML_CHUNK

    @pl.when(step == 0)
    def _():
        c_scr[...] = c0_ref[...]
        m_scr[...] = m0_ref[...]

    row_i = lax.broadcasted_iota(jnp.int32, (L, L), 0)
    col_i = lax.broadcasted_iota(jnp.int32, (L, L), 1)
    lane = lax.broadcasted_iota(jnp.int32, (1, LANES), 1)
    tri_lo = (col_i <= row_i)
    tri_up = (col_i >= row_i)
    ones_v = jnp.ones((L, ML_V_DIM), BF16)

    for s, d in [(s, d) for s in range(c_scr.shape[0]) for d in range(2)]:
        q_ref, k_ref, v_ref, g_ref, h_ref = ((qf_ref, kf_ref, vf_ref, gf_ref, hf_ref),
                                             (qb_ref, kb_ref, vb_ref, gb_ref, hb_ref))[d]
        tri = tri_lo if d == 0 else tri_up
        g = g_ref[s] + bg_ref[...]
        logf = jnp.minimum(g, 0.0) - jnp.log(1.0 + jnp.exp(-jnp.abs(g)))
        tri_b = jnp.where(tri, 1.0, 0.0).astype(BF16)
        l1 = logf.astype(BF16)
        r1 = logf - l1.astype(F32)
        l2 = r1.astype(BF16)
        l3 = (r1 - l2.astype(F32)).astype(BF16)
        cum = _dot(tri_b, l1) + (_dot(tri_b, l2) + _dot(tri_b, l3))
        f_lo = 4 + 8 * d
        z = jnp.where((lane >= f_lo) & (lane < f_lo + ML_HEADS), cum, g) * LOG2E
        zt = z.T
        end = L - 1 if d == 0 else 0
        for h in range(ML_HEADS):
            li, lb = 8 * d + h, f_lo + h
            pair, half = h // 2, h % 2
            head_mask = (lane >= 64 * half) & (lane < 64 * half + 64)
            bcol = jnp.sum(jnp.where(lane == lb, z, 0.0), axis=-1, keepdims=True)
            icol = jnp.sum(jnp.where(lane == li, z, 0.0), axis=-1, keepdims=True)
            brow = zt[lb:lb + 1, :]
            irow = zt[li:li + 1, :]
            total = bcol[end:end + 1, :]
            m_prev = m_scr[s, d, h:h + 1, 0:1]

            qm = jnp.where(head_mask, q_ref[s, :, pair * LANES:(pair + 1) * LANES], 0).astype(BF16)
            km = jnp.where(head_mask, k_ref[s, :, pair * LANES:(pair + 1) * LANES], 0).astype(BF16)
            vext = jnp.concatenate([v_ref[s, :, h * ML_V_DIM:(h + 1) * ML_V_DIM], ones_v], axis=1)

            dmat = jnp.where(tri, bcol + (irow - brow), NEG_BIG)
            m_prev_term = bcol + m_prev
            m_t = jnp.maximum(jnp.max(dmat, axis=-1, keepdims=True), m_prev_term)
            sp = _dot_nt(qm, km) * jnp.exp2(dmat - m_t)
            inter = jnp.exp2(m_prev_term - m_t)
            c_prev = c_scr[s, d, h]
            r = _dot(sp.astype(BF16), vext) + inter * _dot(qm, c_prev.astype(BF16))
            num = r[:, :ML_V_DIM]
            den = r[:, ML_V_DIM:]
            h_ref[s, :, h * ML_V_DIM:(h + 1) * ML_V_DIM] = (
                num / jnp.maximum(jnp.abs(den), jnp.exp2(-m_t))).astype(h_ref.dtype)

            a = total - bcol + icol
            m_loc = jnp.max(a, axis=0, keepdims=True)
            kw = (km.astype(F32) * jnp.exp2(a - m_loc)).astype(BF16)
            c_loc = _dot_tn(kw, vext)
            m_new = jnp.maximum(total + m_prev, m_loc)
            c_scr[s, d, h] = jnp.exp2(total + m_prev - m_new) * c_prev + jnp.exp2(m_loc - m_new) * c_loc
            m_scr[s, d, h:h + 1, :] = jnp.broadcast_to(m_new, (1, LANES))

    @pl.when(step == n_steps - 1)
    def _():
        cout_ref[...] = c_scr[...]
        mout_ref[...] = m_scr[...]


def _mlstm(qk, v, gates, bg, c0, m0, casts=()):
    b, t, _ = qk.shape
    L = ML_CHUNK
    nc = t // L
    ns = ML_SAMPLES if b % ML_SAMPLES == 0 else 1
    fwd = lambda n, blk: pl.BlockSpec((ns, L, n), lambda i, c: (i, c, blk))
    bwd = lambda n, blk: pl.BlockSpec((ns, L, n), lambda i, c: (i, nc - 1 - c, blk))
    st_c = pl.BlockSpec((ns,) + c0.shape[1:], lambda i, c: (i, 0, 0, 0, 0))
    st_m = pl.BlockSpec((ns,) + m0.shape[1:], lambda i, c: (i, 0, 0, 0))
    cast_specs = _cast_specs(casts, (b // ns) * nc, lambda i, c: (i * nc + c, 0))
    return pl.pallas_call(
        functools.partial(_mlstm_kernel, n_cast=len(casts)),
        out_shape=(jax.ShapeDtypeStruct((b, t, ML_V_W), BF16),
                   jax.ShapeDtypeStruct((b, t, ML_V_W), BF16),
                   jax.ShapeDtypeStruct(c0.shape, F32),
                   jax.ShapeDtypeStruct(m0.shape, F32),
                   *[jax.ShapeDtypeStruct(a.shape, BF16) for a in casts]),
        grid=(b // ns, nc),
        in_specs=[fwd(ML_QK_W, 0), fwd(ML_QK_W, 1), fwd(ML_V_W, 0), fwd(LANES, 0),
                  bwd(ML_QK_W, 0), bwd(ML_QK_W, 1), bwd(ML_V_W, 0), bwd(LANES, 0),
                  pl.BlockSpec((1, LANES), lambda i, c: (0, 0)), st_c, st_m, *cast_specs],
        out_specs=(fwd(ML_V_W, 0), bwd(ML_V_W, 0), st_c, st_m, *cast_specs),
        scratch_shapes=[pltpu.VMEM((ns,) + c0.shape[1:], F32), pltpu.VMEM((ns,) + m0.shape[1:], F32)],
        compiler_params=_cparams(("arbitrary", "arbitrary")),
        name="mlstm",
    )(qk, qk, v, gates, qk, qk, v, gates, bg, c0, m0, *casts)


def _attn_kernel(case_ref, ws_ref, q_ref, k_ref, v_ref, kc_ref, vc_ref, bias_ref, *rest, n_cast):
    (o_ref,), _ = _ride_along_casts(rest, n_cast, 1)
    j = pl.program_id(2)
    nk = ATT_KROWS * GRID_W
    start = pl.multiple_of(ws_ref[j] * GRID_W, GRID_W)
    lane = lax.broadcasted_iota(jnp.int32, (1, LANES), 1)
    for pp in range(ATT_PAIRS):
        lanes = slice(pp * LANES, (pp + 1) * LANES)
        q = q_ref[0, :, lanes]
        k = k_ref[0, pl.ds(start, nk), lanes]
        v = v_ref[0, pl.ds(start, nk), lanes]
        kc = kc_ref[0, :, lanes]
        vc = vc_ref[0, :, lanes]
        acc = jnp.zeros(q.shape, F32)
        for h in range(2):
            head_mask = (lane >= NA_HEAD_DIM * h) & (lane < NA_HEAD_DIM * (h + 1))
            qh = (jnp.where(head_mask, q, 0).astype(F32) * (NA_HEAD_DIM ** -0.5 * LOG2E)).astype(BF16)
            s = _dot_nt(qh, k) + bias_ref[0, 2 * pp + h]
            sc = _dot_nt(qh, kc)
            m = jnp.maximum(jnp.max(s, axis=-1, keepdims=True), jnp.max(sc, axis=-1, keepdims=True))
            p = jnp.exp2(s - m)
            pc = jnp.exp2(sc - m)
            vh = jnp.where(head_mask, v, 1).astype(BF16)
            vch = jnp.where(head_mask, vc, 1).astype(BF16)
            o = _dot(p.astype(BF16), vh) + _dot(pc.astype(BF16), vch)
            acc = acc + jnp.where(head_mask, o / pltpu.roll(o, NA_HEAD_DIM, axis=1), 0.0)
        o_ref[0, :, lanes] = acc.astype(o_ref.dtype)


def _attn_plan(rows):
    kh = min(NA_WIN_H, rows)
    nj = rows // ATT_ROWS
    rs = lambda r: int(np.clip(r - kh // 2, 0, rows - kh))
    ws = [int(np.clip(ATT_ROWS * j - kh // 2, 0, rows - ATT_KROWS)) for j in range(nj)]
    sigs, case = [], []
    for j in range(nj):
        r0 = ATT_ROWS * j
        sig = (ws[j] - r0,) + tuple(rs(r0 + a) - r0 for a in range(ATT_ROWS))
        if sig not in sigs:
            sigs.append(sig)
        case.append(sigs.index(sig))
    return np.asarray(ws, np.int32), np.asarray(case, np.int32), sigs, kh


def _attn_bias(bias_table, sigs, kh):
    col_start = np.clip(np.arange(GRID_W) - NA_WIN_W // 2, 0, GRID_W - NA_WIN_W)
    c = np.arange(GRID_W)
    cidx = c[None, :] - c[:, None] + (NA_WIN_W - 1)
    col_ok = (c[None, :] >= col_start[:, None]) & (c[None, :] < col_start[:, None] + NA_WIN_W)
    expand = (np.arange(2 * NA_WIN_W - 1)[:, None, None] == cidx[None]).astype(np.float32)
    out = []
    for sig in sigs:
        wsr, rsr = sig[0], np.asarray(sig[1:])
        a = np.arange(ATT_ROWS)[:, None]
        rk = wsr + np.arange(ATT_KROWS)[None, :]
        row_ok = (rk >= rsr[:, None]) & (rk < rsr[:, None] + kh)
        ridx = np.clip(rk - a + (NA_WIN_H - 1), 0, 2 * NA_WIN_H - 2)
        rows = bias_table[:, ridx, :]
        full = jnp.einsum('haiv,vqk->haqik', rows, jnp.asarray(expand), precision=HIGHEST)
        ok = row_ok[:, None, :, None] & col_ok[None, :, None, :]
        full = jnp.where(ok[None], full * LOG2E, NEG_BIG)
        out.append(full.reshape(full.shape[0], ATT_ROWS * GRID_W, ATT_KROWS * GRID_W))
    return jnp.stack(out).astype(F32)


def _attn(a_lat, a_ctx, bias_table, casts=()):
    b, t, _ = a_lat.shape
    lc = a_ctx.shape[1]
    rows = t // GRID_W
    ws, case, sigs, kh = _attn_plan(rows)
    bias = _attn_bias(bias_table, sigs, kh)
    tq = ATT_ROWS * GRID_W
    nk = ATT_KROWS * GRID_W
    bw = ATT_PAIRS * LANES
    n_blk = NA_W // bw
    nj = rows // ATT_ROWS
    cast_specs = _cast_specs(casts, n_blk * b * nj, lambda p, i, j, cs, w: ((p * b + i) * nj + j, 0))
    grid_spec = pltpu.PrefetchScalarGridSpec(
        num_scalar_prefetch=2,
        grid=(n_blk, b, nj),
        in_specs=[pl.BlockSpec((1, tq, bw), lambda p, i, j, cs, w: (i, j, p)),
                  pl.BlockSpec((1, t, bw), lambda p, i, j, cs, w: (i, 0, n_blk + p)),
                  pl.BlockSpec((1, t, bw), lambda p, i, j, cs, w: (i, 0, 2 * n_blk + p)),
                  pl.BlockSpec((1, lc, bw), lambda p, i, j, cs, w: (i, 0, n_blk + p)),
                  pl.BlockSpec((1, lc, bw), lambda p, i, j, cs, w: (i, 0, 2 * n_blk + p)),
                  pl.BlockSpec((1, 2 * ATT_PAIRS, tq, nk), lambda p, i, j, cs, w: (cs[j], p, 0, 0)),
                  *cast_specs],
        out_specs=(pl.BlockSpec((1, tq, bw), lambda p, i, j, cs, w: (i, j, p)), *cast_specs),
    )
    return pl.pallas_call(
        functools.partial(_attn_kernel, n_cast=len(casts)),
        out_shape=(jax.ShapeDtypeStruct((b, t, NA_W), BF16),
                   *[jax.ShapeDtypeStruct(a.shape, BF16) for a in casts]),
        grid_spec=grid_spec,
        compiler_params=_cparams(("arbitrary", "arbitrary", "arbitrary")),
        name="nattn",
    )(jnp.asarray(case), jnp.asarray(ws), a_lat, a_lat, a_lat, a_ctx, a_ctx, bias, *casts)


def _outproj_kernel(att_ref, hf_ref, hb_ref, om_ref, x_ref, mod_ref, ng_ref, wa_ref, wm_ref,
                    lng_ref, lnb_ref, wrh_ref, wrl_ref, x1_ref, hm_ref, aff_ref, affr_ref):
    h = hf_ref[0].astype(F32) + hb_ref[0].astype(F32)
    parts = []
    for hd in range(ML_HEADS):
        hh = h[:, hd * ML_V_DIM:(hd + 1) * ML_V_DIM]
        mu = jnp.mean(hh, axis=-1, keepdims=True)
        var = jnp.mean(jnp.square(hh - mu), axis=-1, keepdims=True)
        parts.append((hh - mu) * lax.rsqrt(var + LN_EPS))
    hn = jnp.concatenate(parts, axis=1) * ng_ref[...]
    ml = (hn * _sigmoid(om_ref[0].astype(F32))).astype(BF16)
    mix = _dot(att_ref[0], wa_ref[...]) + _dot(ml, wm_ref[...])
    y = DEEPNORM_ALPHA * x_ref[0] + mod_ref[0, 2:3, :] * mix
    mu = jnp.mean(y, axis=-1, keepdims=True)
    var = jnp.mean(jnp.square(y - mu), axis=-1, keepdims=True)
    x1 = (y - mu) * lax.rsqrt(var + LN_EPS) * lng_ref[...] + lnb_ref[...]
    x1_ref[0] = x1
    hm = x1 * (1.0 + mod_ref[0, 4:5, :]) + mod_ref[0, 3:4, :]
    h_hi = hm.astype(BF16)
    h_lo = (hm - h_hi.astype(F32)).astype(BF16)
    bits = pltpu.bitcast(h_hi.astype(F32), jnp.uint32)
    half = bits.shape[1] // 2
    word = (bits[:, :half] >> 16) | (bits[:, half:] & jnp.uint32(0xFFFF0000))
    hm_ref[0] = pltpu.bitcast(word, jnp.int32)
    logits = _dot(h_hi, wrh_ref[...]) + (_dot(h_lo, wrh_ref[...]) + _dot(h_hi, wrl_ref[...]))
    lane = lax.broadcasted_iota(jnp.int32, (1, LANES), 1)
    logits = jnp.where(lane < N_EXPERTS, logits, NEG_BIG)
    e = jnp.exp(logits - jnp.max(logits, axis=-1, keepdims=True))
    aff = e / jnp.sum(e, axis=-1, keepdims=True)
    affr_ref[0] = aff
    aff_ref[0] = aff.T[:N_EXPERTS, :]


def _outproj(att, hf, hb, om, x, mod, ng, w_att, w_ml, lng, lnb, wr_hi, wr_lo, tm):
    b, t, d = x.shape
    row = lambda n: pl.BlockSpec((1, tm, n), lambda i, j: (i, j, 0))
    full = lambda w: pl.BlockSpec(w.shape, lambda i, j: (0,) * w.ndim)
    return pl.pallas_call(
        _outproj_kernel,
        out_shape=(jax.ShapeDtypeStruct((b, t, d), F32),
                   jax.ShapeDtypeStruct((b, t, d // 2), jnp.int32),
                   jax.ShapeDtypeStruct((b, N_EXPERTS, t), F32),
                   jax.ShapeDtypeStruct((b, t, LANES), F32)),
        grid=(b, t // tm),
        in_specs=[row(NA_W), row(ML_V_W), row(ML_V_W), row(ML_V_W), row(d),
                  pl.BlockSpec((1, 8, d), lambda i, j: (i, 0, 0)),
                  full(ng), full(w_att), full(w_ml), full(lng), full(lnb), full(wr_hi), full(wr_lo)],
        out_specs=(row(d), row(d // 2), pl.BlockSpec((1, N_EXPERTS, tm), lambda i, j: (i, 0, j)), row(LANES)),
        compiler_params=_cparams(("arbitrary", "arbitrary")),
        name="outproj",
    )(att, hf, hb, om, x, mod, ng, w_att, w_ml, lng, lnb, wr_hi, wr_lo)


UNSELECTED = -1e6


def _select_kernel(aff_ref, idx_ref, pos_ref, off_ref, cum_scr, sel_scr, offs_v, offs_s, dsem, *, cap):
    t = aff_ref.shape[2]
    nb = t // LANES
    lane = lax.broadcasted_iota(jnp.int32, (1, LANES), 1)
    keys = lambda: pltpu.bitcast(aff_ref[0], jnp.int32)

    def count(mask):
        return jnp.sum(jnp.where(mask, 1.0, 0.0), axis=-1, keepdims=True)

    def search(_, c):
        lo, hi = c
        mid = lo + jnp.right_shift(hi - lo, 1)
        ge = count(keys() >= mid) >= cap
        return jnp.where(ge, mid, lo), jnp.where(ge, hi, mid)

    lo0 = jnp.zeros((N_EXPERTS, 1), jnp.int32)
    hi0 = jnp.full((N_EXPERTS, 1), 0x7F800000, jnp.int32)
    thr, _ = lax.fori_loop(0, 31, search, (lo0, hi0))

    r_i = lax.broadcasted_iota(jnp.int32, (LANES, LANES), 0)
    c_i = lax.broadcasted_iota(jnp.int32, (LANES, LANES), 1)
    strict = jnp.where(r_i < c_i, 1.0, 0.0).astype(BF16)
    tr_i = lax.broadcasted_iota(jnp.int32, (t, LANES), 0)
    tc_i = lax.broadcasted_iota(jnp.int32, (t, LANES), 1)
    block_ind = jnp.where(jnp.right_shift(tr_i, 7) == tc_i, 1.0, 0.0).astype(BF16)

    def prefix(x01):
        xb = x01.astype(BF16)
        offs = _dot(_dot(xb, block_ind).astype(BF16), strict)
        for j in range(nb):
            off_j = jnp.sum(jnp.where(lane == j, offs, 0.0), axis=-1, keepdims=True)
            cum_scr[:, j * LANES:(j + 1) * LANES] = _dot(xb[:, j * LANES:(j + 1) * LANES], strict) + off_j
        return offs

    k = keys()
    gt = k > thr
    eq = k == thr
    need = cap - count(gt)
    prefix(jnp.where(eq, 1.0, 0.0))
    sel = jnp.where(gt | (eq & (cum_scr[...] < need)), 1.0, 0.0)
    sel_scr[...] = sel
    offs = prefix(sel)

    pad = jnp.zeros((LANES - N_EXPERTS, LANES), F32)
    for j in range(nb):
        blk = jnp.where(sel_scr[:, j * LANES:(j + 1) * LANES] > 0.0, cum_scr[:, j * LANES:(j + 1) * LANES], UNSELECTED)
        pos_ref[0, j * LANES:(j + 1) * LANES, :] = jnp.concatenate([blk, pad], axis=0).T
    off_ref[0] = jnp.concatenate([offs, pad], axis=0).T

    cum_scr[...] = cum_scr[...] + sel_scr[...]
    idx_ref[...] = jnp.zeros(idx_ref.shape, idx_ref.dtype)
    n_groups = cap // LANES
    real = lane < nb
    ends = pltpu.roll(offs, LANES - 1, axis=1)
    bounds = jnp.zeros((N_EXPERTS, LANES), F32)
    for g in range(n_groups):
        lo = jnp.sum(jnp.where(real & (ends <= g * LANES), 1.0, 0.0), axis=-1, keepdims=True)
        hi = jnp.sum(jnp.where(real & (offs <= g * LANES + LANES - 1), 1.0, 0.0), axis=-1, keepdims=True)
        bounds = jnp.where(lane == g, lo, jnp.where(lane == n_groups + g, hi, bounds))
    offs_v[...] = bounds.astype(jnp.int32)
    to_smem = pltpu.make_async_copy(offs_v, offs_s, dsem)
    to_smem.start()
    to_smem.wait()
    sub = lax.broadcasted_iota(jnp.int32, (LANES, LANES), 0).astype(F32)
    for e in range(N_EXPERTS):
        def group(pg, carry, e=e):
            first = jnp.asarray(pg * LANES, jnp.int32)
            jlo = offs_s[e, pg]
            jhi = offs_s[e, n_groups + pg]
            slots = first.astype(F32) + sub

            def block(jb, acc):
                c = cum_scr[pl.ds(e, 1), pl.ds(pl.multiple_of(jb * LANES, LANES), LANES)]
                return acc + jnp.where(jnp.broadcast_to(c, (LANES, LANES)) <= slots, 1.0, 0.0)

            acc = lax.fori_loop(jlo, jhi, block, jnp.zeros((LANES, LANES), F32))
            col = jnp.sum(acc, axis=-1, keepdims=True).astype(jnp.int32) + jlo * LANES
            idx_ref[0, pl.ds(pl.multiple_of(pg * LANES, LANES), LANES), e:e + 1] = col
            return carry

        lax.fori_loop(0, n_groups, group, 0)


def _select(aff_t, cap):
    b, e, t = aff_t.shape
    assert cap % LANES == 0 and t % LANES == 0 and t // LANES < LANES and 2 * (cap // LANES) <= LANES
    return pl.pallas_call(
        functools.partial(_select_kernel, cap=cap),
        out_shape=(jax.ShapeDtypeStruct((b, cap, LANES), jnp.int32),
                   jax.ShapeDtypeStruct((b, t, LANES), F32),
                   jax.ShapeDtypeStruct((b, LANES, LANES), F32)),
        grid=(b,),
        in_specs=[pl.BlockSpec((1, e, t), lambda i: (i, 0, 0))],
        out_specs=(pl.BlockSpec((1, cap, LANES), lambda i: (i, 0, 0)),
                   pl.BlockSpec((1, t, LANES), lambda i: (i, 0, 0)),
                   pl.BlockSpec((1, LANES, LANES), lambda i: (i, 0, 0))),
        scratch_shapes=[pltpu.VMEM((e, t), F32), pltpu.VMEM((e, t), F32),
                        pltpu.VMEM((e, LANES), jnp.int32), pltpu.SMEM((e, LANES), jnp.int32),
                        pltpu.SemaphoreType.DMA(())],
        compiler_params=_cparams(("arbitrary",)),
        name="select",
    )(aff_t)


SC_CORES = 2
SC_SUBCORES = 16
SC_CHUNK = 64
MOE_GROUPS = 4


def _gather_rows(table, idx):
    n = idx.shape[0]
    v, d = table.shape
    n_workers = SC_CORES * SC_SUBCORES
    per_w = n // n_workers
    assert n % (n_workers * SC_CHUNK) == 0 and d % LANES == 0 and table.dtype.itemsize == 4
    mesh = plsc.VectorSubcoreMesh(core_axis_name="c", subcore_axis_name="s",
                                  num_cores=SC_CORES, num_subcores=SC_SUBCORES)

    @functools.partial(
        pl.kernel, mesh=mesh,
        out_type=jax.ShapeDtypeStruct((n, d), table.dtype),
        scratch_types=[pltpu.VMEM((SC_CHUNK,), jnp.int32),
                       pltpu.VMEM((SC_CHUNK, d), table.dtype),
                       pltpu.SemaphoreType.DMA],
        name="row_gather")
    def gather(table_hbm, idx_hbm, out_hbm, idx_v, rows_v, sem):
        base = (lax.axis_index("s") * SC_CORES + lax.axis_index("c")) * per_w

        @pl.loop(0, per_w, step=SC_CHUNK)
        def _(o):
            pltpu.sync_copy(idx_hbm.at[pl.ds(base + o, SC_CHUNK)], idx_v)
            pltpu.async_copy(table_hbm.at[idx_v], rows_v, sem).wait()
            pltpu.sync_copy(rows_v, out_hbm.at[pl.ds(base + o, SC_CHUNK)])

    return gather(table, idx)


def _moe_kernel(x_ref, wg_ref, wu_ref, wd_ref, o_ref, x_scr, act_scr, *, tf):
    w = pltpu.bitcast(x_ref[0], jnp.uint32)
    half = w.shape[1]
    x_scr[:, :half] = pltpu.bitcast(w << 16, F32).astype(BF16)
    x_scr[:, half:] = pltpu.bitcast(w & jnp.uint32(0xFFFF0000), F32).astype(BF16)
    x = x_scr[...]
    for c in range(act_scr.shape[1] // tf):
        cols = slice(c * tf, (c + 1) * tf)
        hg = _dot(x, wg_ref[0, :, cols])
        hu = _dot(x, wu_ref[0, :, cols])
        act_scr[:, cols] = (hg * _sigmoid(hg) * hu).astype(BF16)
    o_ref[0] = _dot(act_scr[...], wd_ref[0]).astype(o_ref.dtype)


def _moe(xe, w_gate, w_up, w_down, e0, tm, tf):
    e, m, dw = xe.shape
    d = 2 * dw
    dff = w_gate.shape[2]
    return pl.pallas_call(
        functools.partial(_moe_kernel, tf=tf),
        out_shape=jax.ShapeDtypeStruct((e, m, d), BF16),
        grid=(e, m // tm),
        in_specs=[pl.BlockSpec((1, tm, dw), lambda i, j: (i, j, 0)),
                  pl.BlockSpec((1, d, dff), lambda i, j: (i + e0, 0, 0)),
                  pl.BlockSpec((1, d, dff), lambda i, j: (i + e0, 0, 0)),
                  pl.BlockSpec((1, dff, d), lambda i, j: (i + e0, 0, 0))],
        out_specs=pl.BlockSpec((1, tm, d), lambda i, j: (i, j, 0)),
        scratch_shapes=[pltpu.VMEM((tm, d), BF16), pltpu.VMEM((tm, dff), BF16)],
        compiler_params=_cparams(("arbitrary", "arbitrary")),
        name="moe",
    )(xe, w_gate, w_up, w_down)


CMB_TM = 256
CMB_WIN = 64


def _combine_kernel(p0_ref, x1_ref, pos_ref, aff_ref, mod_ref, lng_ref, lnb_ref, *rest, cap, n_parts):
    y_parts = rest[:n_parts]
    o_ref, ybuf, yextra, acc_ref, sem, sem_x = rest[n_parts:]
    per_part = N_EXPERTS // n_parts
    _combine_body(p0_ref, x1_ref, pos_ref, aff_ref, mod_ref, lng_ref, lnb_ref,
                  lambda e: y_parts[e // per_part].at[e % per_part], y_parts[0].shape[1],
                  o_ref, ybuf, yextra, acc_ref, sem, sem_x, cap)


def _combine_body(p0_ref, x1_ref, pos_ref, aff_ref, mod_ref, lng_ref, lnb_ref, y_of, m_rows, o_ref,
                  ybuf, yextra, acc_ref, sem, sem_x, cap):
    b = pl.program_id(0)
    j = pl.program_id(1)
    nt = pl.num_programs(1)
    g = b * nt + j
    n_steps = pl.num_programs(0) * nt
    slot = lax.rem(g, 2)
    win = CMB_WIN
    kdim = N_EXPERTS * win

    def window(bb, jj, r):
        base = (bb * (nt + 1) + jj) * N_EXPERTS
        los, rows = [], []
        for e in range(N_EXPERTS):
            lo = (p0_ref[base + e] // ROW_ALIGN) * ROW_ALIGN + r * win
            los.append(lo)
            rows.append(jnp.minimum(bb * cap + lo, m_rows - win))
        return los, rows

    def copies(rows, buf, sems):
        return [pltpu.make_async_copy(y_of(e).at[pl.ds(pl.multiple_of(rows[e], ROW_ALIGN), win), :],
                                      buf.at[pl.ds(e * win, win), :], sems.at[e])
                for e in range(N_EXPERTS)]

    los0, rows0 = window(b, j, 0)

    @pl.when(g == 0)
    def _():
        for c in copies(rows0, ybuf.at[0], sem.at[0]):
            c.start()

    @pl.when(g + 1 < n_steps)
    def _():
        wrap = j + 1 == nt
        _, rows_n = window(jnp.where(wrap, b + 1, b), jnp.where(wrap, 0, j + 1), 0)
        for c in copies(rows_n, ybuf.at[1 - slot], sem.at[1 - slot]):
            c.start()

    lane = lax.broadcasted_iota(jnp.int32, (1, LANES), 1)
    e_i = lax.broadcasted_iota(jnp.int32, (LANES, kdim), 0)
    k_i = lax.broadcasted_iota(jnp.int32, (LANES, kdim), 1)
    expand = jnp.where(k_i // win == e_i, 1.0, 0.0).astype(BF16)
    col_in_win = (lax.broadcasted_iota(jnp.int32, (1, kdim), 1) % win).astype(F32)
    aff_cols = _dot(aff_ref[0].astype(BF16), expand)

    def lane_vec(vals):
        v = jnp.zeros((1, LANES), F32)
        for e in range(N_EXPERTS):
            v = jnp.where(lane == e, jnp.asarray(vals[e]).astype(F32), v)
        return v

    def expand_rows(los, rows, buf):
        rel = pos_ref[0] - lane_vec(los)
        shift = lane_vec([b * cap + lo - row for lo, row in zip(los, rows)])
        tgt = jnp.where((rel >= 0.0) & (rel < float(win)), rel + shift, -1.0)
        tgt_cols = _dot(tgt.astype(BF16), expand)
        s = jnp.where(tgt_cols == col_in_win, aff_cols, 0.0).astype(BF16)
        return _dot(s, buf[...])

    for c in copies(rows0, ybuf.at[slot], sem.at[slot]):
        c.wait()
    acc_ref[...] = expand_rows(los0, rows0, ybuf.at[slot])

    base = (b * (nt + 1) + j) * N_EXPERTS
    rounds = jnp.int32(0)
    for e in range(N_EXPERTS):
        span = p0_ref[base + N_EXPERTS + e] - (p0_ref[base + e] // ROW_ALIGN) * ROW_ALIGN
        rounds = jnp.maximum(rounds, (span + win - 1) // win)

    def extra(r, carry):
        los, rows = window(b, j, r)
        cps = copies(rows, yextra, sem_x)
        for c in cps:
            c.start()
        for c in cps:
            c.wait()
        acc_ref[...] += expand_rows(los, rows, yextra)
        return carry

    lax.fori_loop(1, rounds, extra, 0)

    y = DEEPNORM_ALPHA * x1_ref[0] + mod_ref[0, 5:6, :] * acc_ref[...]
    mu = jnp.mean(y, axis=-1, keepdims=True)
    var = jnp.mean(jnp.square(y - mu), axis=-1, keepdims=True)
    o_ref[0] = (y - mu) * lax.rsqrt(var + LN_EPS) * lng_ref[...] + lnb_ref[...]


def _combine(p0tab, x1, pos_t, aff_r, mod, lng, lnb, ys, cap):
    b, t, d = x1.shape
    tm = CMB_TM
    kdim = N_EXPERTS * CMB_WIN
    row = lambda n: pl.BlockSpec((1, tm, n), lambda i, j, p: (i, j, 0))
    vec = pl.BlockSpec((1, d), lambda i, j, p: (0, 0))
    grid_spec = pltpu.PrefetchScalarGridSpec(
        num_scalar_prefetch=1,
        grid=(b, t // tm),
        in_specs=[row(d), row(LANES), row(LANES), pl.BlockSpec((1, 8, d), lambda i, j, p: (i, 0, 0)), vec, vec,
                  *[pl.BlockSpec(memory_space=pl.ANY) for _ in ys]],
        out_specs=row(d),
        scratch_shapes=[pltpu.VMEM((2, kdim, d), ys[0].dtype), pltpu.VMEM((kdim, d), ys[0].dtype),
                        pltpu.VMEM((tm, d), F32),
                        pltpu.SemaphoreType.DMA((2, N_EXPERTS)), pltpu.SemaphoreType.DMA((N_EXPERTS,))],
    )
    return pl.pallas_call(
        functools.partial(_combine_kernel, cap=cap, n_parts=len(ys)),
        out_shape=jax.ShapeDtypeStruct((b, t, d), F32),
        grid_spec=grid_spec,
        compiler_params=_cparams(("arbitrary", "arbitrary")),
        name="combine",
    )(p0tab, x1, pos_t, aff_r, mod, lng, lnb, *ys)


def _split_w_in(w_in):
    a_end = 3 * NA_W
    qk_end = a_end + 2 * ML_QK_W
    v_end = qk_end + ML_V_W
    o_end = v_end + ML_V_W
    wg = jnp.pad(w_in[:, o_end:], ((0, 0), (0, LANES - ML_N_GATES)))
    return tuple(w.astype(BF16) for w in
                 (w_in[:, :a_end], w_in[:, a_end:qk_end], w_in[:, qk_end:v_end], w_in[:, v_end:o_end], wg))


def _layer(x, ctx, c, c_ctx, w_ada, b_ada, w_in, b_gate, conv_qk, na_rel_bias, ml_norm_g, w_out,
           ln1_g, ln1_b, w_router, w_expert_gate, w_expert_up, w_expert_down, ln2_g, ln2_b):
    b, t, d = x.shape
    lc = ctx.shape[1]

    cs = jnp.zeros((16, d), F32).at[:b].set(c).at[b].set(c_ctx)
    m = _ada(cs, w_ada, b_ada)
    mod = jnp.pad(m[:b].reshape(b, 6, d), ((0, 0), (0, 2), (0, 0)))
    mod_ctx = jnp.broadcast_to(jnp.pad(m[b].reshape(6, d), ((0, 2), (0, 0)))[None], (b, 8, d))

    ws = _split_w_in(w_in)
    a_lat, qk_lat, v_lat, o_lat, g_lat = _inproj(x, mod, ws, tm=512)
    a_ctx, qk_ctx, v_ctx, _, g_ctx = _inproj(ctx, mod_ctx, ws, tm=lc)

    conv_w = jnp.pad(conv_qk, ((0, 8 - CONV_K), (0, 0)))
    cos_t, sin_t = _rope_tables(t)
    qk_lat = _qkprep(qk_lat, conv_w, cos_t, sin_t)
    qk_ctx = _qkprep(qk_ctx, conv_w, jnp.ones((lc, LANES), F32), jnp.zeros((lc, LANES), F32))

    bg = jnp.pad(b_gate, (0, LANES - ML_N_GATES)).reshape(1, LANES)
    c0 = jnp.zeros((b, 2, ML_HEADS, LANES, 2 * ML_V_DIM), F32)
    m0 = jnp.zeros((b, 2, 8, LANES), F32)
    n_e, _, dff = w_expert_gate.shape
    _, _, c1, m1 = _mlstm(qk_ctx, v_ctx, g_ctx, bg, c0, m0)
    hf, hb, _, _, wd_b = _mlstm(qk_lat, v_lat, g_lat, bg, c1, m1,
                                casts=(w_expert_down.reshape(n_e * dff, d),))
    att, wg_b, wu_b = _attn(a_lat, a_ctx, na_rel_bias,
                            casts=(w_expert_gate.reshape(n_e * d, dff), w_expert_up.reshape(n_e * d, dff)))

    wr = jnp.pad(w_router, ((0, 0), (0, LANES - N_EXPERTS)))
    wr_hi = wr.astype(BF16)
    wr_lo = (wr - wr_hi.astype(F32)).astype(BF16)
    x1, hm, aff_t, aff_r = _outproj(att, hf, hb, o_lat, x, mod, ml_norm_g.reshape(1, -1),
                                    w_out[:NA_W].astype(BF16), w_out[NA_W:].astype(BF16),
                                    ln1_g.reshape(1, d), ln1_b.reshape(1, d), wr_hi, wr_lo, tm=256)

    cap = CAPACITY_FACTOR * t // N_EXPERTS
    idx_t, pos_t, off_t = _select(aff_t, cap)
    rows = jnp.transpose(idx_t[:, :, :N_EXPERTS], (2, 0, 1)) + (jnp.arange(b, dtype=jnp.int32) * t)[None, :, None]
    per = N_EXPERTS // MOE_GROUPS
    wg3, wu3, wd3 = wg_b.reshape(n_e, d, dff), wu_b.reshape(n_e, d, dff), wd_b.reshape(n_e, dff, d)
    ys = []
    for k in range(MOE_GROUPS):
        xe = _gather_rows(hm.reshape(b * t, d // 2), rows[k * per:(k + 1) * per].reshape(-1))
        ys.append(_moe(xe.reshape(per, b * cap, d // 2), wg3, wu3, wd3, e0=k * per,
                       tm=min(1024, b * cap), tf=256))

    nt = t // CMB_TM
    p0 = off_t[:, 0:nt * (CMB_TM // LANES):CMB_TM // LANES, :N_EXPERTS]
    p0 = jnp.concatenate([p0, jnp.full((b, 1, N_EXPERTS), cap, F32)], axis=1)
    p0tab = p0.astype(jnp.int32).reshape(-1)
    return _combine(p0tab, x1, pos_t, aff_r, mod, ln2_g.reshape(1, d), ln2_b.reshape(1, d), ys, cap)


def kernel(x, c, ctx, c_ctx, w_ada, b_ada, w_in, b_gate, conv_qk, na_rel_bias, ml_norm_g, w_out,
           ln1_g, ln1_b, w_router, w_expert_gate, w_expert_up, w_expert_down, ln2_g, ln2_b):
    return _layer(x, ctx, c, c_ctx, w_ada[0], b_ada[0], w_in[0], b_gate[0], conv_qk[0], na_rel_bias[0],
                  ml_norm_g[0], w_out[0], ln1_g[0], ln1_b[0], w_router[0], w_expert_gate[0],
                  w_expert_up[0], w_expert_down[0], ln2_g[0], ln2_b[0])
```

```python
import functools

import numpy as np
import jax
import jax.numpy as jnp
from jax import lax
from jax.experimental import pallas as pl
from jax.experimental.pallas import tpu as pltpu
from jax.experimental.pallas import tpu_sc as plsc

F32 = jnp.float32
BF16 = jnp.bfloat16
HIGHEST = lax.Precision.HIGHEST

GRID_W = 64
NA_HEADS = 8
NA_HEAD_DIM = 64
NA_WIN_H = 8
NA_WIN_W = 16
NA_W = NA_HEADS * NA_HEAD_DIM
ML_HEADS = 4
ML_QK_DIM = 64
ML_V_DIM = 128
ML_QK_W = ML_HEADS * ML_QK_DIM
ML_V_W = ML_HEADS * ML_V_DIM
ML_CHUNK = 256
ML_SAMPLES = 1
ML_N_GATES = 4 * ML_HEADS
CONV_K = 5
ROPE_BASE = 10000.0
N_EXPERTS = 16
CAPACITY_FACTOR = 2
LN_EPS = 1e-5
DEPTH = 1
DEEPNORM_ALPHA = (2.0 * DEPTH) ** 0.25

LANES = 128
ROW_ALIGN = 16
NEG_BIG = -1e30
LOG2E = 1.4426950408889634
VMEM_LIMIT = 56 * 1024 * 1024

ATT_ROWS = 4
ATT_PAIRS = 4
ATT_KROWS = ATT_ROWS + NA_WIN_H - 1


def _cparams(sem):
    return pltpu.CompilerParams(dimension_semantics=sem, vmem_limit_bytes=VMEM_LIMIT)


def _sigmoid(x):
    return 1.0 / (1.0 + jnp.exp(-x))


def _dot(a, b):
    return jnp.dot(a, b, preferred_element_type=F32)


def _dot_nt(a, b):
    return lax.dot_general(a, b, (((1,), (1,)), ((), ())), preferred_element_type=F32)


def _dot_tn(a, b):
    return lax.dot_general(a, b, (((0,), (0,)), ((), ())), preferred_element_type=F32)


def _ada_kernel(c_ref, w_ref, b_ref, o_ref):
    c = c_ref[...]
    s = c * _sigmoid(c)
    o_ref[...] = jnp.dot(s, w_ref[...], precision=HIGHEST, preferred_element_type=F32) + b_ref[...]


def _ada(cs, w_ada, b_ada):
    rows, d = cs.shape
    n = w_ada.shape[1]
    tn = 512
    return pl.pallas_call(
        _ada_kernel,
        out_shape=jax.ShapeDtypeStruct((rows, n), F32),
        grid=(n // tn,),
        in_specs=[pl.BlockSpec((rows, d), lambda j: (0, 0)),
                  pl.BlockSpec((d, tn), lambda j: (0, j)),
                  pl.BlockSpec((1, tn), lambda j: (0, j))],
        out_specs=pl.BlockSpec((rows, tn), lambda j: (0, j)),
        compiler_params=_cparams(("arbitrary",)),
        name="ada",
    )(cs, w_ada, b_ada.reshape(1, n))


def _inproj_kernel(x_ref, mod_ref, wa_ref, wqk_ref, wv_ref, wo_ref, wg_ref,
                   a_ref, qk_ref, v_ref, o_ref, g_ref):
    xm = (x_ref[0] * (1.0 + mod_ref[0, 1:2, :]) + mod_ref[0, 0:1, :]).astype(BF16)
    a_ref[0] = _dot(xm, wa_ref[...]).astype(BF16)
    qk_ref[0] = _dot(xm, wqk_ref[...]).astype(BF16)
    v_ref[0] = _dot(xm, wv_ref[...]).astype(BF16)
    o_ref[0] = _dot(xm, wo_ref[...]).astype(BF16)
    g_ref[0] = _dot(xm, wg_ref[...])


def _inproj(x, mod, ws, tm):
    b, t, d = x.shape
    wa, wqk, wv, wo, wg = ws
    full = lambda w: pl.BlockSpec(w.shape, lambda i, j: (0, 0))
    row = lambda n: pl.BlockSpec((1, tm, n), lambda i, j: (i, j, 0))
    return pl.pallas_call(
        _inproj_kernel,
        out_shape=(jax.ShapeDtypeStruct((b, t, wa.shape[1]), BF16),
                   jax.ShapeDtypeStruct((b, t, wqk.shape[1]), BF16),
                   jax.ShapeDtypeStruct((b, t, wv.shape[1]), BF16),
                   jax.ShapeDtypeStruct((b, t, wo.shape[1]), BF16),
                   jax.ShapeDtypeStruct((b, t, wg.shape[1]), F32)),
        grid=(b, t // tm),
        in_specs=[row(d), pl.BlockSpec((1, 8, d), lambda i, j: (i, 0, 0)),
                  full(wa), full(wqk), full(wv), full(wo), full(wg)],
        out_specs=(row(wa.shape[1]), row(wqk.shape[1]), row(wv.shape[1]), row(wo.shape[1]),
                   row(wg.shape[1])),
        compiler_params=_cparams(("arbitrary", "arbitrary")),
        name="inproj",
    )(x, mod, wa, wqk, wv, wo, wg)


def _qkprep_kernel(x_ref, w_ref, cos_ref, sin_ref, o_ref, *, sub, halo):
    t = x_ref.shape[1]
    n_sub = t // sub
    lane = lax.broadcasted_iota(jnp.int32, (1, LANES), 1)
    first_half = (lane & 31) < 16
    scale = jnp.where(pl.program_id(1) < (ML_QK_W // LANES), ML_QK_DIM ** -0.5, 1.0).astype(F32)
    zeros = jnp.zeros((halo, LANES), x_ref.dtype)
    for s in range(n_sub):
        lo = s * sub
        top = x_ref[0, lo - halo:lo, :] if s > 0 else zeros
        bot = x_ref[0, lo + sub:lo + sub + halo, :] if s < n_sub - 1 else zeros
        ext = jnp.concatenate([top, x_ref[0, lo:lo + sub, :], bot], axis=0).astype(F32)
        n = sub + 2 * halo
        acc = jnp.zeros((sub, LANES), F32)
        for j in range(CONV_K):
            shift = (CONV_K // 2 - j) % n
            sh = ext if shift == 0 else pltpu.roll(ext, shift, axis=0)
            acc = acc + w_ref[j:j + 1, :] * sh[halo:halo + sub, :]
        y = acc * _sigmoid(acc)
        partner = jnp.where(first_half, pltpu.roll(y, LANES - 16, axis=1), pltpu.roll(y, 16, axis=1))
        y = y * cos_ref[lo:lo + sub, :] + partner * sin_ref[lo:lo + sub, :]
        o_ref[0, lo:lo + sub, :] = (y * scale).astype(o_ref.dtype)


def _qkprep(qk, conv_w, cos_t, sin_t):
    b, t, w = qk.shape
    sub = min(t, 1024)
    kern = functools.partial(_qkprep_kernel, sub=sub, halo=16)
    return pl.pallas_call(
        kern,
        out_shape=jax.ShapeDtypeStruct((b, t, w), BF16),
        grid=(b, w // LANES),
        in_specs=[pl.BlockSpec((1, t, LANES), lambda i, j: (i, 0, j)),
                  pl.BlockSpec((8, LANES), lambda i, j: (0, j)),
                  pl.BlockSpec((t, LANES), lambda i, j: (0, 0)),
                  pl.BlockSpec((t, LANES), lambda i, j: (0, 0))],
        out_specs=pl.BlockSpec((1, t, LANES), lambda i, j: (i, 0, j)),
        compiler_params=_cparams(("arbitrary", "arbitrary")),
        name="qkprep",
    )(qk, conv_w, cos_t, sin_t)


def _rope_tables(t):
    nf = ML_QK_DIM // 4
    inv = 1.0 / (ROPE_BASE ** (jnp.arange(nf, dtype=F32) / nf))
    pos = jnp.arange(t)
    ang_r = (pos // GRID_W).astype(F32)[:, None] * inv
    ang_c = (pos % GRID_W).astype(F32)[:, None] * inv
    cos = jnp.concatenate([jnp.cos(ang_r)] * 2 + [jnp.cos(ang_c)] * 2, axis=-1)
    sin = jnp.concatenate([-jnp.sin(ang_r), jnp.sin(ang_r), -jnp.sin(ang_c), jnp.sin(ang_c)], axis=-1)
    return jnp.tile(cos, (1, 2)), jnp.tile(sin, (1, 2))


def _ride_along_casts(rest, n_cast, n_out):
    outs = rest[n_cast:n_cast + n_out]
    for src, dst in zip(rest[:n_cast], rest[n_cast + n_out:2 * n_cast + n_out]):
        dst[...] = src[...].astype(dst.dtype)
    return outs, rest[2 * n_cast + n_out:]


def _cast_specs(casts, n_steps, index_map):
    specs = []
    for a in casts:
        rows = a.shape[0] // n_steps
        assert a.shape[0] % n_steps == 0 and rows % ROW_ALIGN == 0
        specs.append(pl.BlockSpec((rows, a.shape[1]), index_map))
    return specs


def _mlstm_kernel(qf_ref, kf_ref, vf_ref, gf_ref, qb_ref, kb_ref, vb_ref, gb_ref, bg_ref,
                  c0_ref, m0_ref, *rest, n_cast):
    (hf_ref, hb_ref, cout_ref, mout_ref), (c_scr, m_scr) = _ride_along_casts(rest, n_cast, 4)
    step = pl.program_id(1)
    n_steps = pl.num_programs(1)
    L = ML_CHUNK

    @pl.when(step == 0)
    def _():
        c_scr[...] = c0_ref[...]
        m_scr[...] = m0_ref[...]

    row_i = lax.broadcasted_iota(jnp.int32, (L, L), 0)
    col_i = lax.broadcasted_iota(jnp.int32, (L, L), 1)
    lane = lax.broadcasted_iota(jnp.int32, (1, LANES), 1)
    tri_lo = (col_i <= row_i)
    tri_up = (col_i >= row_i)
    ones_v = jnp.ones((L, ML_V_DIM), BF16)

    for s, d in [(s, d) for s in range(c_scr.shape[0]) for d in range(2)]:
        q_ref, k_ref, v_ref, g_ref, h_ref = ((qf_ref, kf_ref, vf_ref, gf_ref, hf_ref),
                                             (qb_ref, kb_ref, vb_ref, gb_ref, hb_ref))[d]
        tri = tri_lo if d == 0 else tri_up
        g = g_ref[s] + bg_ref[...]
        logf = jnp.minimum(g, 0.0) - jnp.log(1.0 + jnp.exp(-jnp.abs(g)))
        tri_b = jnp.where(tri, 1.0, 0.0).astype(BF16)
        l1 = logf.astype(BF16)
        r1 = logf - l1.astype(F32)
        l2 = r1.astype(BF16)
        l3 = (r1 - l2.astype(F32)).astype(BF16)
        cum = _dot(tri_b, l1) + (_dot(tri_b, l2) + _dot(tri_b, l3))
        f_lo = 4 + 8 * d
        z = jnp.where((lane >= f_lo) & (lane < f_lo + ML_HEADS), cum, g) * LOG2E
        zt = z.T
        end = L - 1 if d == 0 else 0
        for h in range(ML_HEADS):
            li, lb = 8 * d + h, f_lo + h
            pair, half = h // 2, h % 2
            head_mask = (lane >= 64 * half) & (lane < 64 * half + 64)
            bcol = jnp.sum(jnp.where(lane == lb, z, 0.0), axis=-1, keepdims=True)
            icol = jnp.sum(jnp.where(lane == li, z, 0.0), axis=-1, keepdims=True)
            brow = zt[lb:lb + 1, :]
            irow = zt[li:li + 1, :]
            total = bcol[end:end + 1, :]
            m_prev = m_scr[s, d, h:h + 1, 0:1]

            qm = jnp.where(head_mask, q_ref[s, :, pair * LANES:(pair + 1) * LANES], 0).astype(BF16)
            km = jnp.where(head_mask, k_ref[s, :, pair * LANES:(pair + 1) * LANES], 0).astype(BF16)
            vext = jnp.concatenate([v_ref[s, :, h * ML_V_DIM:(h + 1) * ML_V_DIM], ones_v], axis=1)

            dmat = jnp.where(tri, bcol + (irow - brow), NEG_BIG)
            m_prev_term = bcol + m_prev
            m_t = jnp.maximum(jnp.max(dmat, axis=-1, keepdims=True), m_prev_term)
            sp = _dot_nt(qm, km) * jnp.exp2(dmat - m_t)
            inter = jnp.exp2(m_prev_term - m_t)
            c_prev = c_scr[s, d, h]
            r = _dot(sp.astype(BF16), vext) + inter * _dot(qm, c_prev.astype(BF16))
            num = r[:, :ML_V_DIM]
            den = r[:, ML_V_DIM:]
            h_ref[s, :, h * ML_V_DIM:(h + 1) * ML_V_DIM] = (
                num / jnp.maximum(jnp.abs(den), jnp.exp2(-m_t))).astype(h_ref.dtype)

            a = total - bcol + icol
            m_loc = jnp.max(a, axis=0, keepdims=True)
            kw = (km.astype(F32) * jnp.exp2(a - m_loc)).astype(BF16)
            c_loc = _dot_tn(kw, vext)
            m_new = jnp.maximum(total + m_prev, m_loc)
            c_scr[s, d, h] = jnp.exp2(total + m_prev - m_new) * c_prev + jnp.exp2(m_loc - m_new) * c_loc
            m_scr[s, d, h:h + 1, :] = jnp.broadcast_to(m_new, (1, LANES))

    @pl.when(step == n_steps - 1)
    def _():
        cout_ref[...] = c_scr[...]
        mout_ref[...] = m_scr[...]


def _mlstm(qk, v, gates, bg, c0, m0, casts=()):
    b, t, _ = qk.shape
    L = ML_CHUNK
    nc = t // L
    ns = ML_SAMPLES if b % ML_SAMPLES == 0 else 1
    fwd = lambda n, blk: pl.BlockSpec((ns, L, n), lambda i, c: (i, c, blk))
    bwd = lambda n, blk: pl.BlockSpec((ns, L, n), lambda i, c: (i, nc - 1 - c, blk))
    st_c = pl.BlockSpec((ns,) + c0.shape[1:], lambda i, c: (i, 0, 0, 0, 0))
    st_m = pl.BlockSpec((ns,) + m0.shape[1:], lambda i, c: (i, 0, 0, 0))
    cast_specs = _cast_specs(casts, (b // ns) * nc, lambda i, c: (i * nc + c, 0))
    return pl.pallas_call(
        functools.partial(_mlstm_kernel, n_cast=len(casts)),
        out_shape=(jax.ShapeDtypeStruct((b, t, ML_V_W), BF16),
                   jax.ShapeDtypeStruct((b, t, ML_V_W), BF16),
                   jax.ShapeDtypeStruct(c0.shape, F32),
                   jax.ShapeDtypeStruct(m0.shape, F32),
                   *[jax.ShapeDtypeStruct(a.shape, BF16) for a in casts]),
        grid=(b // ns, nc),
        in_specs=[fwd(ML_QK_W, 0), fwd(ML_QK_W, 1), fwd(ML_V_W, 0), fwd(LANES, 0),
                  bwd(ML_QK_W, 0), bwd(ML_QK_W, 1), bwd(ML_V_W, 0), bwd(LANES, 0),
                  pl.BlockSpec((1, LANES), lambda i, c: (0, 0)), st_c, st_m, *cast_specs],
        out_specs=(fwd(ML_V_W, 0), bwd(ML_V_W, 0), st_c, st_m, *cast_specs),
        scratch_shapes=[pltpu.VMEM((ns,) + c0.shape[1:], F32), pltpu.VMEM((ns,) + m0.shape[1:], F32)],
        compiler_params=_cparams(("arbitrary", "arbitrary")),
        name="mlstm",
    )(qk, qk, v, gates, qk, qk, v, gates, bg, c0, m0, *casts)


def _attn_kernel(case_ref, ws_ref, q_ref, k_ref, v_ref, kc_ref, vc_ref, bias_ref, *rest, n_cast):
    (o_ref,), _ = _ride_along_casts(rest, n_cast, 1)
    j = pl.program_id(2)
    nk = ATT_KROWS * GRID_W
    start = pl.multiple_of(ws_ref[j] * GRID_W, GRID_W)
    lane = lax.broadcasted_iota(jnp.int32, (1, LANES), 1)
    for pp in range(ATT_PAIRS):
        lanes = slice(pp * LANES, (pp + 1) * LANES)
        q = q_ref[0, :, lanes]
        k = k_ref[0, pl.ds(start, nk), lanes]
        v = v_ref[0, pl.ds(start, nk), lanes]
        kc = kc_ref[0, :, lanes]
        vc = vc_ref[0, :, lanes]
        acc = jnp.zeros(q.shape, F32)
        tq = q.shape[0]
        masks = [(lane >= NA_HEAD_DIM * h) & (lane < NA_HEAD_DIM * (h + 1)) for h in range(2)]
        qs = jnp.concatenate(
            [(jnp.where(mk, q, 0).astype(F32) * (NA_HEAD_DIM ** -0.5 * LOG2E)).astype(BF16) for mk in masks], axis=0)
        s_both = _dot_nt(qs, k)
        sc_both = _dot_nt(qs, kc)
        for h in range(2):
            head_mask = masks[h]
            s = s_both[h * tq:(h + 1) * tq] + bias_ref[0, 2 * pp + h]
            sc = sc_both[h * tq:(h + 1) * tq]
            m = jnp.maximum(jnp.max(s, axis=-1, keepdims=True), jnp.max(sc, axis=-1, keepdims=True))
            p = jnp.exp2(s - m)
            pc = jnp.exp2(sc - m)
            vh = jnp.where(head_mask, v, 1).astype(BF16)
            vch = jnp.where(head_mask, vc, 1).astype(BF16)
            o = _dot(p.astype(BF16), vh) + _dot(pc.astype(BF16), vch)
            acc = acc + jnp.where(head_mask, o / pltpu.roll(o, NA_HEAD_DIM, axis=1), 0.0)
        o_ref[0, :, lanes] = acc.astype(o_ref.dtype)


def _attn_plan(rows):
    kh = min(NA_WIN_H, rows)
    nj = rows // ATT_ROWS
    rs = lambda r: int(np.clip(r - kh // 2, 0, rows - kh))
    ws = [int(np.clip(ATT_ROWS * j - kh // 2, 0, rows - ATT_KROWS)) for j in range(nj)]
    sigs, case = [], []
    for j in range(nj):
        r0 = ATT_ROWS * j
        sig = (ws[j] - r0,) + tuple(rs(r0 + a) - r0 for a in range(ATT_ROWS))
        if sig not in sigs:
            sigs.append(sig)
        case.append(sigs.index(sig))
    return np.asarray(ws, np.int32), np.asarray(case, np.int32), sigs, kh


def _attn_bias(bias_table, sigs, kh):
    col_start = np.clip(np.arange(GRID_W) - NA_WIN_W // 2, 0, GRID_W - NA_WIN_W)
    c = np.arange(GRID_W)
    cidx = c[None, :] - c[:, None] + (NA_WIN_W - 1)
    col_ok = (c[None, :] >= col_start[:, None]) & (c[None, :] < col_start[:, None] + NA_WIN_W)
    expand = (np.arange(2 * NA_WIN_W - 1)[:, None, None] == cidx[None]).astype(np.float32)
    out = []
    for sig in sigs:
        wsr, rsr = sig[0], np.asarray(sig[1:])
        a = np.arange(ATT_ROWS)[:, None]
        rk = wsr + np.arange(ATT_KROWS)[None, :]
        row_ok = (rk >= rsr[:, None]) & (rk < rsr[:, None] + kh)
        ridx = np.clip(rk - a + (NA_WIN_H - 1), 0, 2 * NA_WIN_H - 2)
        rows = bias_table[:, ridx, :]
        full = jnp.einsum('haiv,vqk->haqik', rows, jnp.asarray(expand), precision=HIGHEST)
        ok = row_ok[:, None, :, None] & col_ok[None, :, None, :]
        full = jnp.where(ok[None], full * LOG2E, NEG_BIG)
        out.append(full.reshape(full.shape[0], ATT_ROWS * GRID_W, ATT_KROWS * GRID_W))
    return jnp.stack(out).astype(F32)


def _attn(a_lat, a_ctx, bias_table, casts=()):
    b, t, _ = a_lat.shape
    lc = a_ctx.shape[1]
    rows = t // GRID_W
    ws, case, sigs, kh = _attn_plan(rows)
    bias = _attn_bias(bias_table, sigs, kh)
    tq = ATT_ROWS * GRID_W
    nk = ATT_KROWS * GRID_W
    bw = ATT_PAIRS * LANES
    n_blk = NA_W // bw
    nj = rows // ATT_ROWS
    cast_specs = _cast_specs(casts, n_blk * b * nj, lambda p, i, j, cs, w: ((p * b + i) * nj + j, 0))
    grid_spec = pltpu.PrefetchScalarGridSpec(
        num_scalar_prefetch=2,
        grid=(n_blk, b, nj),
        in_specs=[pl.BlockSpec((1, tq, bw), lambda p, i, j, cs, w: (i, j, p)),
                  pl.BlockSpec((1, t, bw), lambda p, i, j, cs, w: (i, 0, n_blk + p)),
                  pl.BlockSpec((1, t, bw), lambda p, i, j, cs, w: (i, 0, 2 * n_blk + p)),
                  pl.BlockSpec((1, lc, bw), lambda p, i, j, cs, w: (i, 0, n_blk + p)),
                  pl.BlockSpec((1, lc, bw), lambda p, i, j, cs, w: (i, 0, 2 * n_blk + p)),
                  pl.BlockSpec((1, 2 * ATT_PAIRS, tq, nk), lambda p, i, j, cs, w: (cs[j], p, 0, 0)),
                  *cast_specs],
        out_specs=(pl.BlockSpec((1, tq, bw), lambda p, i, j, cs, w: (i, j, p)), *cast_specs),
    )
    return pl.pallas_call(
        functools.partial(_attn_kernel, n_cast=len(casts)),
        out_shape=(jax.ShapeDtypeStruct((b, t, NA_W), BF16),
                   *[jax.ShapeDtypeStruct(a.shape, BF16) for a in casts]),
        grid_spec=grid_spec,
        compiler_params=_cparams(("arbitrary", "arbitrary", "arbitrary")),
        name="nattn",
    )(jnp.asarray(case), jnp.asarray(ws), a_lat, a_lat, a_lat, a_ctx, a_ctx, bias, *casts)


def _outproj_kernel(att_ref, hf_ref, hb_ref, om_ref, x_ref, mod_ref, ng_ref, wa_ref, wm_ref,
                    lng_ref, lnb_ref, wrh_ref, wrl_ref, x1_ref, hm_ref, aff_ref, affr_ref):
    h = hf_ref[0].astype(F32) + hb_ref[0].astype(F32)
    parts = []
    for hd in range(ML_HEADS):
        hh = h[:, hd * ML_V_DIM:(hd + 1) * ML_V_DIM]
        mu = jnp.mean(hh, axis=-1, keepdims=True)
        var = jnp.mean(jnp.square(hh - mu), axis=-1, keepdims=True)
        parts.append((hh - mu) * lax.rsqrt(var + LN_EPS))
    hn = jnp.concatenate(parts, axis=1) * ng_ref[...]
    ml = (hn * _sigmoid(om_ref[0].astype(F32))).astype(BF16)
    mix = _dot(att_ref[0], wa_ref[...]) + _dot(ml, wm_ref[...])
    y = DEEPNORM_ALPHA * x_ref[0] + mod_ref[0, 2:3, :] * mix
    mu = jnp.mean(y, axis=-1, keepdims=True)
    var = jnp.mean(jnp.square(y - mu), axis=-1, keepdims=True)
    x1 = (y - mu) * lax.rsqrt(var + LN_EPS) * lng_ref[...] + lnb_ref[...]
    x1_ref[0] = x1
    hm = x1 * (1.0 + mod_ref[0, 4:5, :]) + mod_ref[0, 3:4, :]
    h_hi = hm.astype(BF16)
    h_lo = (hm - h_hi.astype(F32)).astype(BF16)
    bits = pltpu.bitcast(h_hi.astype(F32), jnp.uint32)
    half = bits.shape[1] // 2
    word = (bits[:, :half] >> 16) | (bits[:, half:] & jnp.uint32(0xFFFF0000))
    hm_ref[0] = pltpu.bitcast(word, jnp.int32)
    logits = _dot(h_hi, wrh_ref[...]) + (_dot(h_lo, wrh_ref[...]) + _dot(h_hi, wrl_ref[...]))
    lane = lax.broadcasted_iota(jnp.int32, (1, LANES), 1)
    logits = jnp.where(lane < N_EXPERTS, logits, NEG_BIG)
    e = jnp.exp(logits - jnp.max(logits, axis=-1, keepdims=True))
    aff = e / jnp.sum(e, axis=-1, keepdims=True)
    affr_ref[0] = aff
    aff_ref[0] = aff.T[:N_EXPERTS, :]


def _outproj(att, hf, hb, om, x, mod, ng, w_att, w_ml, lng, lnb, wr_hi, wr_lo, tm):
    b, t, d = x.shape
    row = lambda n: pl.BlockSpec((1, tm, n), lambda i, j: (i, j, 0))
    full = lambda w: pl.BlockSpec(w.shape, lambda i, j: (0,) * w.ndim)
    return pl.pallas_call(
        _outproj_kernel,
        out_shape=(jax.ShapeDtypeStruct((b, t, d), F32),
                   jax.ShapeDtypeStruct((b, t, d // 2), jnp.int32),
                   jax.ShapeDtypeStruct((b, N_EXPERTS, t), F32),
                   jax.ShapeDtypeStruct((b, t, LANES), F32)),
        grid=(b, t // tm),
        in_specs=[row(NA_W), row(ML_V_W), row(ML_V_W), row(ML_V_W), row(d),
                  pl.BlockSpec((1, 8, d), lambda i, j: (i, 0, 0)),
                  full(ng), full(w_att), full(w_ml), full(lng), full(lnb), full(wr_hi), full(wr_lo)],
        out_specs=(row(d), row(d // 2), pl.BlockSpec((1, N_EXPERTS, tm), lambda i, j: (i, 0, j)), row(LANES)),
        compiler_params=_cparams(("arbitrary", "arbitrary")),
        name="outproj",
    )(att, hf, hb, om, x, mod, ng, w_att, w_ml, lng, lnb, wr_hi, wr_lo)


UNSELECTED = -1e6


def _select_kernel(aff_ref, idx_ref, pos_ref, off_ref, cum_scr, sel_scr, offs_v, offs_s, dsem, *, cap):
    t = aff_ref.shape[2]
    nb = t // LANES
    lane = lax.broadcasted_iota(jnp.int32, (1, LANES), 1)
    keys = lambda: pltpu.bitcast(aff_ref[0], jnp.int32)

    def count(mask):
        return jnp.sum(jnp.where(mask, 1.0, 0.0), axis=-1, keepdims=True)

    def search(_, c):
        lo, hi = c
        mid = lo + jnp.right_shift(hi - lo, 1)
        ge = count(keys() >= mid) >= cap
        return jnp.where(ge, mid, lo), jnp.where(ge, hi, mid)

    lo0 = jnp.zeros((N_EXPERTS, 1), jnp.int32)
    hi0 = jnp.full((N_EXPERTS, 1), 0x7F800000, jnp.int32)
    thr, _ = lax.fori_loop(0, 31, search, (lo0, hi0))

    r_i = lax.broadcasted_iota(jnp.int32, (LANES, LANES), 0)
    c_i = lax.broadcasted_iota(jnp.int32, (LANES, LANES), 1)
    strict = jnp.where(r_i < c_i, 1.0, 0.0).astype(BF16)
    tr_i = lax.broadcasted_iota(jnp.int32, (t, LANES), 0)
    tc_i = lax.broadcasted_iota(jnp.int32, (t, LANES), 1)
    block_ind = jnp.where(jnp.right_shift(tr_i, 7) == tc_i, 1.0, 0.0).astype(BF16)

    def prefix(x01):
        xb = x01.astype(BF16)
        offs = _dot(_dot(xb, block_ind).astype(BF16), strict)
        for j in range(nb):
            off_j = jnp.sum(jnp.where(lane == j, offs, 0.0), axis=-1, keepdims=True)
            cum_scr[:, j * LANES:(j + 1) * LANES] = _dot(xb[:, j * LANES:(j + 1) * LANES], strict) + off_j
        return offs

    k = keys()
    gt = k > thr
    eq = k == thr
    need = cap - count(gt)
    prefix(jnp.where(eq, 1.0, 0.0))
    sel = jnp.where(gt | (eq & (cum_scr[...] < need)), 1.0, 0.0)
    sel_scr[...] = sel
    offs = prefix(sel)

    pad = jnp.zeros((LANES - N_EXPERTS, LANES), F32)
    for j in range(nb):
        blk = jnp.where(sel_scr[:, j * LANES:(j + 1) * LANES] > 0.0, cum_scr[:, j * LANES:(j + 1) * LANES], UNSELECTED)
        pos_ref[0, j * LANES:(j + 1) * LANES, :] = jnp.concatenate([blk, pad], axis=0).T
    off_ref[0] = jnp.concatenate([offs, pad], axis=0).T

    cum_scr[...] = cum_scr[...] + sel_scr[...]
    idx_ref[...] = jnp.zeros(idx_ref.shape, idx_ref.dtype)
    n_groups = cap // LANES
    real = lane < nb
    ends = pltpu.roll(offs, LANES - 1, axis=1)
    bounds = jnp.zeros((N_EXPERTS, LANES), F32)
    for g in range(n_groups):
        lo = jnp.sum(jnp.where(real & (ends <= g * LANES), 1.0, 0.0), axis=-1, keepdims=True)
        hi = jnp.sum(jnp.where(real & (offs <= g * LANES + LANES - 1), 1.0, 0.0), axis=-1, keepdims=True)
        bounds = jnp.where(lane == g, lo, jnp.where(lane == n_groups + g, hi, bounds))
    offs_v[...] = bounds.astype(jnp.int32)
    to_smem = pltpu.make_async_copy(offs_v, offs_s, dsem)
    to_smem.start()
    to_smem.wait()
    sub = lax.broadcasted_iota(jnp.int32, (LANES, LANES), 0).astype(F32)
    for e in range(N_EXPERTS):
        def group(pg, carry, e=e):
            first = jnp.asarray(pg * LANES, jnp.int32)
            jlo = offs_s[e, pg]
            jhi = offs_s[e, n_groups + pg]
            slots = first.astype(F32) + sub

            def block(jb, acc):
                c = cum_scr[pl.ds(e, 1), pl.ds(pl.multiple_of(jb * LANES, LANES), LANES)]
                return acc + jnp.where(jnp.broadcast_to(c, (LANES, LANES)) <= slots, 1.0, 0.0)

            acc = lax.fori_loop(jlo, jhi, block, jnp.zeros((LANES, LANES), F32))
            col = jnp.sum(acc, axis=-1, keepdims=True).astype(jnp.int32) + jlo * LANES
            idx_ref[0, pl.ds(pl.multiple_of(pg * LANES, LANES), LANES), e:e + 1] = col
            return carry

        lax.fori_loop(0, n_groups, group, 0)


def _select(aff_t, cap):
    b, e, t = aff_t.shape
    assert cap % LANES == 0 and t % LANES == 0 and t // LANES < LANES and 2 * (cap // LANES) <= LANES
    return pl.pallas_call(
        functools.partial(_select_kernel, cap=cap),
        out_shape=(jax.ShapeDtypeStruct((b, cap, LANES), jnp.int32),
                   jax.ShapeDtypeStruct((b, t, LANES), F32),
                   jax.ShapeDtypeStruct((b, LANES, LANES), F32)),
        grid=(b,),
        in_specs=[pl.BlockSpec((1, e, t), lambda i: (i, 0, 0))],
        out_specs=(pl.BlockSpec((1, cap, LANES), lambda i: (i, 0, 0)),
                   pl.BlockSpec((1, t, LANES), lambda i: (i, 0, 0)),
                   pl.BlockSpec((1, LANES, LANES), lambda i: (i, 0, 0))),
        scratch_shapes=[pltpu.VMEM((e, t), F32), pltpu.VMEM((e, t), F32),
                        pltpu.VMEM((e, LANES), jnp.int32), pltpu.SMEM((e, LANES), jnp.int32),
                        pltpu.SemaphoreType.DMA(())],
        compiler_params=_cparams(("arbitrary",)),
        name="select",
    )(aff_t)


SC_CORES = 2
SC_SUBCORES = 16
SC_CHUNK = 64
MOE_GROUPS = 4


def _gather_rows(table, idx):
    n = idx.shape[0]
    v, d = table.shape
    n_workers = SC_CORES * SC_SUBCORES
    per_w = n // n_workers
    assert n % (n_workers * SC_CHUNK) == 0 and d % LANES == 0 and table.dtype.itemsize == 4
    mesh = plsc.VectorSubcoreMesh(core_axis_name="c", subcore_axis_name="s",
                                  num_cores=SC_CORES, num_subcores=SC_SUBCORES)

    @functools.partial(
        pl.kernel, mesh=mesh,
        out_type=jax.ShapeDtypeStruct((n, d), table.dtype),
        scratch_types=[pltpu.VMEM((SC_CHUNK,), jnp.int32),
                       pltpu.VMEM((SC_CHUNK, d), table.dtype),
                       pltpu.SemaphoreType.DMA],
        name="row_gather")
    def gather(table_hbm, idx_hbm, out_hbm, idx_v, rows_v, sem):
        base = (lax.axis_index("s") * SC_CORES + lax.axis_index("c")) * per_w

        @pl.loop(0, per_w, step=SC_CHUNK)
        def _(o):
            pltpu.sync_copy(idx_hbm.at[pl.ds(base + o, SC_CHUNK)], idx_v)
            pltpu.async_copy(table_hbm.at[idx_v], rows_v, sem).wait()
            pltpu.sync_copy(rows_v, out_hbm.at[pl.ds(base + o, SC_CHUNK)])

    return gather(table, idx)


def _moe_kernel(x_ref, wg_ref, wu_ref, wd_ref, o_ref, x_scr, act_scr, *, tf):
    w = pltpu.bitcast(x_ref[0], jnp.uint32)
    half = w.shape[1]
    x_scr[:, :half] = pltpu.bitcast(w << 16, F32).astype(BF16)
    x_scr[:, half:] = pltpu.bitcast(w & jnp.uint32(0xFFFF0000), F32).astype(BF16)
    x = x_scr[...]
    for c in range(act_scr.shape[1] // tf):
        cols = slice(c * tf, (c + 1) * tf)
        hg = _dot(x, wg_ref[0, :, cols])
        hu = _dot(x, wu_ref[0, :, cols])
        act_scr[:, cols] = (hg * _sigmoid(hg) * hu).astype(BF16)
    o_ref[0] = _dot(act_scr[...], wd_ref[0]).astype(o_ref.dtype)


def _moe(xe, w_gate, w_up, w_down, e0, tm, tf):
    e, m, dw = xe.shape
    d = 2 * dw
    dff = w_gate.shape[2]
    return pl.pallas_call(
        functools.partial(_moe_kernel, tf=tf),
        out_shape=jax.ShapeDtypeStruct((e, m, d), BF16),
        grid=(e, m // tm),
        in_specs=[pl.BlockSpec((1, tm, dw), lambda i, j: (i, j, 0)),
                  pl.BlockSpec((1, d, dff), lambda i, j: (i + e0, 0, 0)),
                  pl.BlockSpec((1, d, dff), lambda i, j: (i + e0, 0, 0)),
                  pl.BlockSpec((1, dff, d), lambda i, j: (i + e0, 0, 0))],
        out_specs=pl.BlockSpec((1, tm, d), lambda i, j: (i, j, 0)),
        scratch_shapes=[pltpu.VMEM((tm, d), BF16), pltpu.VMEM((tm, dff), BF16)],
        compiler_params=_cparams(("arbitrary", "arbitrary")),
        name="moe",
    )(xe, w_gate, w_up, w_down)


CMB_TM = 256
CMB_WIN = 64


def _combine_kernel(p0_ref, x1_ref, pos_ref, aff_ref, mod_ref, lng_ref, lnb_ref, *rest, cap, n_parts):
    y_parts = rest[:n_parts]
    o_ref, ybuf, yextra, acc_ref, sem, sem_x = rest[n_parts:]
    per_part = N_EXPERTS // n_parts
    _combine_body(p0_ref, x1_ref, pos_ref, aff_ref, mod_ref, lng_ref, lnb_ref,
                  lambda e: y_parts[e // per_part].at[e % per_part], y_parts[0].shape[1],
                  o_ref, ybuf, yextra, acc_ref, sem, sem_x, cap)


def _combine_body(p0_ref, x1_ref, pos_ref, aff_ref, mod_ref, lng_ref, lnb_ref, y_of, m_rows, o_ref,
                  ybuf, yextra, acc_ref, sem, sem_x, cap):
    b = pl.program_id(0)
    j = pl.program_id(1)
    nt = pl.num_programs(1)
    g = b * nt + j
    n_steps = pl.num_programs(0) * nt
    slot = lax.rem(g, 2)
    win = CMB_WIN
    kdim = N_EXPERTS * win

    def window(bb, jj, r):
        base = (bb * (nt + 1) + jj) * N_EXPERTS
        los, rows = [], []
        for e in range(N_EXPERTS):
            lo = (p0_ref[base + e] // ROW_ALIGN) * ROW_ALIGN + r * win
            los.append(lo)
            rows.append(jnp.minimum(bb * cap + lo, m_rows - win))
        return los, rows

    def copies(rows, buf, sems):
        return [pltpu.make_async_copy(y_of(e).at[pl.ds(pl.multiple_of(rows[e], ROW_ALIGN), win), :],
                                      buf.at[pl.ds(e * win, win), :], sems.at[e])
                for e in range(N_EXPERTS)]

    los0, rows0 = window(b, j, 0)

    @pl.when(g == 0)
    def _():
        for c in copies(rows0, ybuf.at[0], sem.at[0]):
            c.start()

    @pl.when(g + 1 < n_steps)
    def _():
        wrap = j + 1 == nt
        _, rows_n = window(jnp.where(wrap, b + 1, b), jnp.where(wrap, 0, j + 1), 0)
        for c in copies(rows_n, ybuf.at[1 - slot], sem.at[1 - slot]):
            c.start()

    lane = lax.broadcasted_iota(jnp.int32, (1, LANES), 1)
    e_i = lax.broadcasted_iota(jnp.int32, (LANES, kdim), 0)
    k_i = lax.broadcasted_iota(jnp.int32, (LANES, kdim), 1)
    expand = jnp.where(k_i // win == e_i, 1.0, 0.0).astype(BF16)
    col_in_win = (lax.broadcasted_iota(jnp.int32, (1, kdim), 1) % win).astype(F32)
    aff_cols = _dot(aff_ref[0].astype(BF16), expand)

    def lane_vec(vals):
        v = jnp.zeros((1, LANES), F32)
        for e in range(N_EXPERTS):
            v = jnp.where(lane == e, jnp.asarray(vals[e]).astype(F32), v)
        return v

    def expand_rows(los, rows, buf):
        rel = pos_ref[0] - lane_vec(los)
        shift = lane_vec([b * cap + lo - row for lo, row in zip(los, rows)])
        tgt = jnp.where((rel >= 0.0) & (rel < float(win)), rel + shift, -1.0)
        tgt_cols = _dot(tgt.astype(BF16), expand)
        s = jnp.where(tgt_cols == col_in_win, aff_cols, 0.0).astype(BF16)
        return _dot(s, buf[...])

    for c in copies(rows0, ybuf.at[slot], sem.at[slot]):
        c.wait()
    acc_ref[...] = expand_rows(los0, rows0, ybuf.at[slot])

    base = (b * (nt + 1) + j) * N_EXPERTS
    rounds = jnp.int32(0)
    for e in range(N_EXPERTS):
        span = p0_ref[base + N_EXPERTS + e] - (p0_ref[base + e] // ROW_ALIGN) * ROW_ALIGN
        rounds = jnp.maximum(rounds, (span + win - 1) // win)

    def extra(r, carry):
        los, rows = window(b, j, r)
        cps = copies(rows, yextra, sem_x)
        for c in cps:
            c.start()
        for c in cps:
            c.wait()
        acc_ref[...] += expand_rows(los, rows, yextra)
        return carry

    lax.fori_loop(1, rounds, extra, 0)

    y = DEEPNORM_ALPHA * x1_ref[0] + mod_ref[0, 5:6, :] * acc_ref[...]
    mu = jnp.mean(y, axis=-1, keepdims=True)
    var = jnp.mean(jnp.square(y - mu), axis=-1, keepdims=True)
    o_ref[0] = (y - mu) * lax.rsqrt(var + LN_EPS) * lng_ref[...] + lnb_ref[...]


def _combine(p0tab, x1, pos_t, aff_r, mod, lng, lnb, ys, cap):
    b, t, d = x1.shape
    tm = CMB_TM
    kdim = N_EXPERTS * CMB_WIN
    row = lambda n: pl.BlockSpec((1, tm, n), lambda i, j, p: (i, j, 0))
    vec = pl.BlockSpec((1, d), lambda i, j, p: (0, 0))
    grid_spec = pltpu.PrefetchScalarGridSpec(
        num_scalar_prefetch=1,
        grid=(b, t // tm),
        in_specs=[row(d), row(LANES), row(LANES), pl.BlockSpec((1, 8, d), lambda i, j, p: (i, 0, 0)), vec, vec,
                  *[pl.BlockSpec(memory_space=pl.ANY) for _ in ys]],
        out_specs=row(d),
        scratch_shapes=[pltpu.VMEM((2, kdim, d), ys[0].dtype), pltpu.VMEM((kdim, d), ys[0].dtype),
                        pltpu.VMEM((tm, d), F32),
                        pltpu.SemaphoreType.DMA((2, N_EXPERTS)), pltpu.SemaphoreType.DMA((N_EXPERTS,))],
    )
    return pl.pallas_call(
        functools.partial(_combine_kernel, cap=cap, n_parts=len(ys)),
        out_shape=jax.ShapeDtypeStruct((b, t, d), F32),
        grid_spec=grid_spec,
        compiler_params=_cparams(("arbitrary", "arbitrary")),
        name="combine",
    )(p0tab, x1, pos_t, aff_r, mod, lng, lnb, *ys)


def _split_w_in(w_in):
    a_end = 3 * NA_W
    qk_end = a_end + 2 * ML_QK_W
    v_end = qk_end + ML_V_W
    o_end = v_end + ML_V_W
    wg = jnp.pad(w_in[:, o_end:], ((0, 0), (0, LANES - ML_N_GATES)))
    return tuple(w.astype(BF16) for w in
                 (w_in[:, :a_end], w_in[:, a_end:qk_end], w_in[:, qk_end:v_end], w_in[:, v_end:o_end], wg))


def _layer(x, ctx, c, c_ctx, w_ada, b_ada, w_in, b_gate, conv_qk, na_rel_bias, ml_norm_g, w_out,
           ln1_g, ln1_b, w_router, w_expert_gate, w_expert_up, w_expert_down, ln2_g, ln2_b):
    b, t, d = x.shape
    lc = ctx.shape[1]

    cs = jnp.zeros((16, d), F32).at[:b].set(c).at[b].set(c_ctx)
    m = _ada(cs, w_ada, b_ada)
    mod = jnp.pad(m[:b].reshape(b, 6, d), ((0, 0), (0, 2), (0, 0)))
    mod_ctx = jnp.broadcast_to(jnp.pad(m[b].reshape(6, d), ((0, 2), (0, 0)))[None], (b, 8, d))

    ws = _split_w_in(w_in)
    a_lat, qk_lat, v_lat, o_lat, g_lat = _inproj(x, mod, ws, tm=512)
    a_ctx, qk_ctx, v_ctx, _, g_ctx = _inproj(ctx, mod_ctx, ws, tm=lc)

    conv_w = jnp.pad(conv_qk, ((0, 8 - CONV_K), (0, 0)))
    cos_t, sin_t = _rope_tables(t)
    qk_lat = _qkprep(qk_lat, conv_w, cos_t, sin_t)
    qk_ctx = _qkprep(qk_ctx, conv_w, jnp.ones((lc, LANES), F32), jnp.zeros((lc, LANES), F32))

    bg = jnp.pad(b_gate, (0, LANES - ML_N_GATES)).reshape(1, LANES)
    c0 = jnp.zeros((b, 2, ML_HEADS, LANES, 2 * ML_V_DIM), F32)
    m0 = jnp.zeros((b, 2, 8, LANES), F32)
    n_e, _, dff = w_expert_gate.shape
    _, _, c1, m1 = _mlstm(qk_ctx, v_ctx, g_ctx, bg, c0, m0)
    hf, hb, _, _, wd_b = _mlstm(qk_lat, v_lat, g_lat, bg, c1, m1,
                                casts=(w_expert_down.reshape(n_e * dff, d),))
    att, wg_b, wu_b = _attn(a_lat, a_ctx, na_rel_bias,
                            casts=(w_expert_gate.reshape(n_e * d, dff), w_expert_up.reshape(n_e * d, dff)))

    wr = jnp.pad(w_router, ((0, 0), (0, LANES - N_EXPERTS)))
    wr_hi = wr.astype(BF16)
    wr_lo = (wr - wr_hi.astype(F32)).astype(BF16)
    x1, hm, aff_t, aff_r = _outproj(att, hf, hb, o_lat, x, mod, ml_norm_g.reshape(1, -1),
                                    w_out[:NA_W].astype(BF16), w_out[NA_W:].astype(BF16),
                                    ln1_g.reshape(1, d), ln1_b.reshape(1, d), wr_hi, wr_lo, tm=256)

    cap = CAPACITY_FACTOR * t // N_EXPERTS
    idx_t, pos_t, off_t = _select(aff_t, cap)
    rows = jnp.transpose(idx_t[:, :, :N_EXPERTS], (2, 0, 1)) + (jnp.arange(b, dtype=jnp.int32) * t)[None, :, None]
    per = N_EXPERTS // MOE_GROUPS
    wg3, wu3, wd3 = wg_b.reshape(n_e, d, dff), wu_b.reshape(n_e, d, dff), wd_b.reshape(n_e, dff, d)
    ys = []
    for k in range(MOE_GROUPS):
        xe = _gather_rows(hm.reshape(b * t, d // 2), rows[k * per:(k + 1) * per].reshape(-1))
        ys.append(_moe(xe.reshape(per, b * cap, d // 2), wg3, wu3, wd3, e0=k * per,
                       tm=min(1024, b * cap), tf=256))

    nt = t // CMB_TM
    p0 = off_t[:, 0:nt * (CMB_TM // LANES):CMB_TM // LANES, :N_EXPERTS]
    p0 = jnp.concatenate([p0, jnp.full((b, 1, N_EXPERTS), cap, F32)], axis=1)
    p0tab = p0.astype(jnp.int32).reshape(-1)
    return _combine(p0tab, x1, pos_t, aff_r, mod, ln2_g.reshape(1, d), ln2_b.reshape(1, d), ys, cap)


def kernel(x, c, ctx, c_ctx, w_ada, b_ada, w_in, b_gate, conv_qk, na_rel_bias, ml_norm_g, w_out,
           ln1_g, ln1_b, w_router, w_expert_gate, w_expert_up, w_expert_down, ln2_g, ln2_b):
    return _layer(x, ctx, c, c_ctx, w_ada[0], b_ada[0], w_in[0], b_gate[0], conv_qk[0], na_rel_bias[0],
                  ml_norm_g[0], w_out[0], ln1_g[0], ln1_b[0], w_router[0], w_expert_gate[0],
                  w_expert_up[0], w_expert_down[0], ln2_g[0], ln2_b[0])
```

```python
import functools

import numpy as np
import jax
import jax.numpy as jnp
from jax import lax
from jax.experimental import pallas as pl
from jax.experimental.pallas import tpu as pltpu
from jax.experimental.pallas import tpu_sc as plsc

F32 = jnp.float32
BF16 = jnp.bfloat16
HIGHEST = lax.Precision.HIGHEST

GRID_W = 64
NA_HEADS = 8
NA_HEAD_DIM = 64
NA_WIN_H = 8
NA_WIN_W = 16
NA_W = NA_HEADS * NA_HEAD_DIM
ML_HEADS = 4
ML_QK_DIM = 64
ML_V_DIM = 128
ML_QK_W = ML_HEADS * ML_QK_DIM
ML_V_W = ML_HEADS * ML_V_DIM
ML_CHUNK = 256
ML_SAMPLES = 1
ML_N_GATES = 4 * ML_HEADS
CONV_K = 5
ROPE_BASE = 10000.0
N_EXPERTS = 16
CAPACITY_FACTOR = 2
LN_EPS = 1e-5
DEPTH = 1
DEEPNORM_ALPHA = (2.0 * DEPTH) ** 0.25

LANES = 128
ROW_ALIGN = 16
NEG_BIG = -1e30
LOG2E = 1.4426950408889634
VMEM_LIMIT = 56 * 1024 * 1024

ATT_ROWS = 4
ATT_PAIRS = 4
ATT_KROWS = ATT_ROWS + NA_WIN_H - 1


def _cparams(sem):
    return pltpu.CompilerParams(dimension_semantics=sem, vmem_limit_bytes=VMEM_LIMIT)


def _sigmoid(x):
    return 1.0 / (1.0 + jnp.exp(-x))


def _dot(a, b):
    return jnp.dot(a, b, preferred_element_type=F32)


def _dot_nt(a, b):
    return lax.dot_general(a, b, (((1,), (1,)), ((), ())), preferred_element_type=F32)


def _dot_tn(a, b):
    return lax.dot_general(a, b, (((0,), (0,)), ((), ())), preferred_element_type=F32)


def _ada_kernel(c_ref, w_ref, b_ref, o_ref):
    c = c_ref[...]
    s = c * _sigmoid(c)
    o_ref[...] = jnp.dot(s, w_ref[...], precision=HIGHEST, preferred_element_type=F32) + b_ref[...]


def _ada(cs, w_ada, b_ada):
    rows, d = cs.shape
    n = w_ada.shape[1]
    tn = 512
    return pl.pallas_call(
        _ada_kernel,
        out_shape=jax.ShapeDtypeStruct((rows, n), F32),
        grid=(n // tn,),
        in_specs=[pl.BlockSpec((rows, d), lambda j: (0, 0)),
                  pl.BlockSpec((d, tn), lambda j: (0, j)),
                  pl.BlockSpec((1, tn), lambda j: (0, j))],
        out_specs=pl.BlockSpec((rows, tn), lambda j: (0, j)),
        compiler_params=_cparams(("arbitrary",)),
        name="ada",
    )(cs, w_ada, b_ada.reshape(1, n))


def _inproj_kernel(x_ref, mod_ref, wa_ref, wqk_ref, wv_ref, wo_ref, wg_ref,
                   a_ref, qk_ref, v_ref, o_ref, g_ref):
    xm = (x_ref[0] * (1.0 + mod_ref[0, 1:2, :]) + mod_ref[0, 0:1, :]).astype(BF16)
    a_ref[0] = _dot(xm, wa_ref[...]).astype(BF16)
    qk_ref[0] = _dot(xm, wqk_ref[...]).astype(BF16)
    v_ref[0] = _dot(xm, wv_ref[...]).astype(BF16)
    o_ref[0] = _dot(xm, wo_ref[...]).astype(BF16)
    g_ref[0] = _dot(xm, wg_ref[...])


def _inproj(x, mod, ws, tm):
    b, t, d = x.shape
    wa, wqk, wv, wo, wg = ws
    full = lambda w: pl.BlockSpec(w.shape, lambda i, j: (0, 0))
    row = lambda n: pl.BlockSpec((1, tm, n), lambda i, j: (i, j, 0))
    return pl.pallas_call(
        _inproj_kernel,
        out_shape=(jax.ShapeDtypeStruct((b, t, wa.shape[1]), BF16),
                   jax.ShapeDtypeStruct((b, t, wqk.shape[1]), BF16),
                   jax.ShapeDtypeStruct((b, t, wv.shape[1]), BF16),
                   jax.ShapeDtypeStruct((b, t, wo.shape[1]), BF16),
                   jax.ShapeDtypeStruct((b, t, wg.shape[1]), F32)),
        grid=(b, t // tm),
        in_specs=[row(d), pl.BlockSpec((1, 8, d), lambda i, j: (i, 0, 0)),
                  full(wa), full(wqk), full(wv), full(wo), full(wg)],
        out_specs=(row(wa.shape[1]), row(wqk.shape[1]), row(wv.shape[1]), row(wo.shape[1]),
                   row(wg.shape[1])),
        compiler_params=_cparams(("arbitrary", "arbitrary")),
        name="inproj",
    )(x, mod, wa, wqk, wv, wo, wg)


def _qkprep_kernel(x_ref, w_ref, cos_ref, sin_ref, o_ref, *, sub, halo):
    t = x_ref.shape[1]
    n_sub = t // sub
    lane = lax.broadcasted_iota(jnp.int32, (1, LANES), 1)
    first_half = (lane & 31) < 16
    scale = jnp.where(pl.program_id(1) < (ML_QK_W // LANES), ML_QK_DIM ** -0.5, 1.0).astype(F32)
    zeros = jnp.zeros((halo, LANES), x_ref.dtype)
    for s in range(n_sub):
        lo = s * sub
        top = x_ref[0, lo - halo:lo, :] if s > 0 else zeros
        bot = x_ref[0, lo + sub:lo + sub + halo, :] if s < n_sub - 1 else zeros
        ext = jnp.concatenate([top, x_ref[0, lo:lo + sub, :], bot], axis=0).astype(F32)
        n = sub + 2 * halo
        acc = jnp.zeros((sub, LANES), F32)
        for j in range(CONV_K):
            shift = (CONV_K // 2 - j) % n
            sh = ext if shift == 0 else pltpu.roll(ext, shift, axis=0)
            acc = acc + w_ref[j:j + 1, :] * sh[halo:halo + sub, :]
        y = acc * _sigmoid(acc)
        partner = jnp.where(first_half, pltpu.roll(y, LANES - 16, axis=1), pltpu.roll(y, 16, axis=1))
        y = y * cos_ref[lo:lo + sub, :] + partner * sin_ref[lo:lo + sub, :]
        o_ref[0, lo:lo + sub, :] = (y * scale).astype(o_ref.dtype)


def _qkprep(qk, conv_w, cos_t, sin_t):
    b, t, w = qk.shape
    sub = min(t, 1024)
    kern = functools.partial(_qkprep_kernel, sub=sub, halo=16)
    return pl.pallas_call(
        kern,
        out_shape=jax.ShapeDtypeStruct((b, t, w), BF16),
        grid=(b, w // LANES),
        in_specs=[pl.BlockSpec((1, t, LANES), lambda i, j: (i, 0, j)),
                  pl.BlockSpec((8, LANES), lambda i, j: (0, j)),
                  pl.BlockSpec((t, LANES), lambda i, j: (0, 0)),
                  pl.BlockSpec((t, LANES), lambda i, j: (0, 0))],
        out_specs=pl.BlockSpec((1, t, LANES), lambda i, j: (i, 0, j)),
        compiler_params=_cparams(("arbitrary", "arbitrary")),
        name="qkprep",
    )(qk, conv_w, cos_t, sin_t)


def _rope_tables(t):
    nf = ML_QK_DIM // 4
    inv = 1.0 / (ROPE_BASE ** (jnp.arange(nf, dtype=F32) / nf))
    pos = jnp.arange(t)
    ang_r = (pos // GRID_W).astype(F32)[:, None] * inv
    ang_c = (pos % GRID_W).astype(F32)[:, None] * inv
    cos = jnp.concatenate([jnp.cos(ang_r)] * 2 + [jnp.cos(ang_c)] * 2, axis=-1)
    sin = jnp.concatenate([-jnp.sin(ang_r), jnp.sin(ang_r), -jnp.sin(ang_c), jnp.sin(ang_c)], axis=-1)
    return jnp.tile(cos, (1, 2)), jnp.tile(sin, (1, 2))


def _ride_along_casts(rest, n_cast, n_out):
    outs = rest[n_cast:n_cast + n_out]
    for src, dst in zip(rest[:n_cast], rest[n_cast + n_out:2 * n_cast + n_out]):
        dst[...] = src[...].astype(dst.dtype)
    return outs, rest[2 * n_cast + n_out:]


def _cast_specs(casts, n_steps, index_map):
    specs = []
    for a in casts:
        rows = a.shape[0] // n_steps
        assert a.shape[0] % n_steps == 0 and rows % ROW_ALIGN == 0
        specs.append(pl.BlockSpec((rows, a.shape[1]), index_map))
    return specs


def _mlstm_kernel(qf_ref, kf_ref, vf_ref, gf_ref, qb_ref, kb_ref, vb_ref, gb_ref, bg_ref,
                  c0_ref, m0_ref, sel_ref, *rest, n_cast):
    (hf_ref, hb_ref, cout_ref, mout_ref), (c_scr, m_scr) = _ride_along_casts(rest, n_cast, 4)
    step = pl.program_id(1)
    n_steps = pl.num_programs(1)
    L = ML_CHUNK

    @pl.when(step == 0)
    def _():
        c_scr[...] = c0_ref[...]
        m_scr[...] = m0_ref[...]

    row_i = lax.broadcasted_iota(jnp.int32, (L, L), 0)
    col_i = lax.broadcasted_iota(jnp.int32, (L, L), 1)
    lane = lax.broadcasted_iota(jnp.int32, (1, LANES), 1)
    tri_lo = (col_i <= row_i)
    tri_up = (col_i >= row_i)
    ones_v = jnp.ones((L, ML_V_DIM), BF16)

    for s, d in [(s, d) for s in range(c_scr.shape[0]) for d in range(2)]:
        q_ref, k_ref, v_ref, g_ref, h_ref = ((qf_ref, kf_ref, vf_ref, gf_ref, hf_ref),
                                             (qb_ref, kb_ref, vb_ref, gb_ref, hb_ref))[d]
        tri = tri_lo if d == 0 else tri_up
        g = g_ref[s] + bg_ref[...]
        logf = jnp.minimum(g, 0.0) - jnp.log(1.0 + jnp.exp(-jnp.abs(g)))
        tri_b = jnp.where(tri, 1.0, 0.0).astype(BF16)
        l1 = logf.astype(BF16)
        r1 = logf - l1.astype(F32)
        l2 = r1.astype(BF16)
        l3 = (r1 - l2.astype(F32)).astype(BF16)
        cum = _dot(tri_b, l1) + (_dot(tri_b, l2) + _dot(tri_b, l3))
        f_lo = 4 + 8 * d
        z = jnp.where((lane >= f_lo) & (lane < f_lo + ML_HEADS), cum, g) * LOG2E
        zt = z.T
        z1 = z.astype(BF16)
        zr = z - z1.astype(F32)
        z2 = zr.astype(BF16)
        z3 = (zr - z2.astype(F32)).astype(BF16)
        cols = _dot(jnp.concatenate([z1, z2, z3], axis=1), sel_ref[d])
        end = L - 1 if d == 0 else 0
        wide = lambda x: jnp.concatenate([x, x], axis=1)
        for h in range(ML_HEADS):
            li, lb = 8 * d + h, f_lo + h
            pair, half = h // 2, h % 2
            head_mask = (lane >= 64 * half) & (lane < 64 * half + 64)
            bcol = cols[:, h * LANES:(h + 1) * LANES]
            icol = cols[:, (ML_HEADS + h) * LANES:(ML_HEADS + h + 1) * LANES]
            brow = zt[lb:lb + 1, :]
            irow = zt[li:li + 1, :]
            total = bcol[end:end + 1, :]
            m_prev = m_scr[s, d, h:h + 1, :]

            qm = jnp.where(head_mask, q_ref[s, :, pair * LANES:(pair + 1) * LANES], 0).astype(BF16)
            km = jnp.where(head_mask, k_ref[s, :, pair * LANES:(pair + 1) * LANES], 0).astype(BF16)
            vext = jnp.concatenate([v_ref[s, :, h * ML_V_DIM:(h + 1) * ML_V_DIM], ones_v], axis=1)

            dmat = jnp.where(tri, wide(bcol) + (irow - brow), NEG_BIG)
            m_prev_term = bcol + m_prev
            m_t = jnp.maximum(jnp.max(dmat, axis=-1, keepdims=True), m_prev_term)
            sp = _dot_nt(qm, km) * jnp.exp2(dmat - wide(m_t))
            inter = jnp.exp2(m_prev_term - m_t)
            c_prev = c_scr[s, d, h]
            r = _dot(jnp.concatenate([sp.astype(BF16), (qm.astype(F32) * inter).astype(BF16)], axis=1),
                     jnp.concatenate([vext, c_prev.astype(BF16)], axis=0))
            num = r[:, :ML_V_DIM]
            den = r[:, ML_V_DIM:]
            h_ref[s, :, h * ML_V_DIM:(h + 1) * ML_V_DIM] = (
                num / jnp.maximum(jnp.abs(den), jnp.exp2(-m_t))).astype(h_ref.dtype)

            a = total - bcol + icol
            m_loc = jnp.max(a, axis=0, keepdims=True)
            kw = (km.astype(F32) * jnp.exp2(a - m_loc)).astype(BF16)
            c_loc = _dot_tn(kw, vext)
            m_new = jnp.maximum(total + m_prev, m_loc)
            c_scr[s, d, h] = (wide(jnp.exp2(total + m_prev - m_new)) * c_prev
                              + wide(jnp.exp2(m_loc - m_new)) * c_loc)
            m_scr[s, d, h:h + 1, :] = m_new

    @pl.when(step == n_steps - 1)
    def _():
        cout_ref[...] = c_scr[...]
        mout_ref[...] = m_scr[...]


def _mlstm(qk, v, gates, bg, c0, m0, casts=()):
    b, t, _ = qk.shape
    L = ML_CHUNK
    nc = t // L
    ns = ML_SAMPLES if b % ML_SAMPLES == 0 else 1
    fwd = lambda n, blk: pl.BlockSpec((ns, L, n), lambda i, c: (i, c, blk))
    bwd = lambda n, blk: pl.BlockSpec((ns, L, n), lambda i, c: (i, nc - 1 - c, blk))
    st_c = pl.BlockSpec((ns,) + c0.shape[1:], lambda i, c: (i, 0, 0, 0, 0))
    st_m = pl.BlockSpec((ns,) + m0.shape[1:], lambda i, c: (i, 0, 0, 0))
    cast_specs = _cast_specs(casts, (b // ns) * nc, lambda i, c: (i * nc + c, 0))
    assert L == 2 * LANES
    gate_lane = np.asarray([[4 + 8 * dd + hh for hh in range(ML_HEADS)] + [8 * dd + hh for hh in range(ML_HEADS)]
                            for dd in range(2)])
    sel = (np.arange(3 * LANES)[None, :, None, None] % LANES == gate_lane[:, None, :, None])
    sel = jnp.asarray(np.broadcast_to(sel, (2, 3 * LANES, 2 * ML_HEADS, LANES))
                      .reshape(2, 3 * LANES, 2 * ML_HEADS * LANES), BF16)
    return pl.pallas_call(
        functools.partial(_mlstm_kernel, n_cast=len(casts)),
        out_shape=(jax.ShapeDtypeStruct((b, t, ML_V_W), BF16),
                   jax.ShapeDtypeStruct((b, t, ML_V_W), BF16),
                   jax.ShapeDtypeStruct(c0.shape, F32),
                   jax.ShapeDtypeStruct(m0.shape, F32),
                   *[jax.ShapeDtypeStruct(a.shape, BF16) for a in casts]),
        grid=(b // ns, nc),
        in_specs=[fwd(ML_QK_W, 0), fwd(ML_QK_W, 1), fwd(ML_V_W, 0), fwd(LANES, 0),
                  bwd(ML_QK_W, 0), bwd(ML_QK_W, 1), bwd(ML_V_W, 0), bwd(LANES, 0),
                  pl.BlockSpec((1, LANES), lambda i, c: (0, 0)), st_c, st_m,
                  pl.BlockSpec(sel.shape, lambda i, c: (0, 0, 0)), *cast_specs],
        out_specs=(fwd(ML_V_W, 0), bwd(ML_V_W, 0), st_c, st_m, *cast_specs),
        scratch_shapes=[pltpu.VMEM((ns,) + c0.shape[1:], F32), pltpu.VMEM((ns,) + m0.shape[1:], F32)],
        compiler_params=_cparams(("arbitrary", "arbitrary")),
        name="mlstm",
    )(qk, qk, v, gates, qk, qk, v, gates, bg, c0, m0, sel, *casts)


def _attn_kernel(case_ref, ws_ref, q_ref, k_ref, v_ref, kc_ref, vc_ref, bias_ref, *rest, n_cast):
    (o_ref,), _ = _ride_along_casts(rest, n_cast, 1)
    j = pl.program_id(2)
    nk = ATT_KROWS * GRID_W
    start = pl.multiple_of(ws_ref[j] * GRID_W, GRID_W)
    lane = lax.broadcasted_iota(jnp.int32, (1, LANES), 1)
    for pp in range(ATT_PAIRS):
        lanes = slice(pp * LANES, (pp + 1) * LANES)
        q = q_ref[0, :, lanes]
        k = k_ref[0, pl.ds(start, nk), lanes]
        v = v_ref[0, pl.ds(start, nk), lanes]
        kc = kc_ref[0, :, lanes]
        vc = vc_ref[0, :, lanes]
        tq = q.shape[0]
        masks = [(lane >= NA_HEAD_DIM * h) & (lane < NA_HEAD_DIM * (h + 1)) for h in range(2)]
        qs = jnp.concatenate(
            [(jnp.where(mk, q, 0).astype(F32) * (NA_HEAD_DIM ** -0.5 * LOG2E)).astype(BF16) for mk in masks], axis=0)
        s_both = _dot_nt(qs, k)
        sc_both = _dot_nt(qs, kc)
        v1 = jnp.concatenate([v, jnp.ones_like(v)], axis=1)
        vc1 = jnp.concatenate([vc, jnp.ones_like(vc)], axis=1)
        outs = []
        for h in range(2):
            s = s_both[h * tq:(h + 1) * tq] + bias_ref[0, 2 * pp + h]
            sc = sc_both[h * tq:(h + 1) * tq]
            m = jnp.maximum(jnp.max(s, axis=-1, keepdims=True), jnp.max(sc, axis=-1, keepdims=True))
            o = _dot(jnp.exp2(s - m).astype(BF16), v1) + _dot(jnp.exp2(sc - m).astype(BF16), vc1)
            outs.append(o[:, :LANES] / o[:, LANES:])
        o_ref[0, :, lanes] = jnp.where(masks[0], outs[0], outs[1]).astype(o_ref.dtype)


def _attn_plan(rows):
    kh = min(NA_WIN_H, rows)
    nj = rows // ATT_ROWS
    rs = lambda r: int(np.clip(r - kh // 2, 0, rows - kh))
    ws = [int(np.clip(ATT_ROWS * j - kh // 2, 0, rows - ATT_KROWS)) for j in range(nj)]
    sigs, case = [], []
    for j in range(nj):
        r0 = ATT_ROWS * j
        sig = (ws[j] - r0,) + tuple(rs(r0 + a) - r0 for a in range(ATT_ROWS))
        if sig not in sigs:
            sigs.append(sig)
        case.append(sigs.index(sig))
    return np.asarray(ws, np.int32), np.asarray(case, np.int32), sigs, kh


def _attn_bias(bias_table, sigs, kh):
    col_start = np.clip(np.arange(GRID_W) - NA_WIN_W // 2, 0, GRID_W - NA_WIN_W)
    c = np.arange(GRID_W)
    cidx = c[None, :] - c[:, None] + (NA_WIN_W - 1)
    col_ok = (c[None, :] >= col_start[:, None]) & (c[None, :] < col_start[:, None] + NA_WIN_W)
    expand = (np.arange(2 * NA_WIN_W - 1)[:, None, None] == cidx[None]).astype(np.float32)
    out = []
    for sig in sigs:
        wsr, rsr = sig[0], np.asarray(sig[1:])
        a = np.arange(ATT_ROWS)[:, None]
        rk = wsr + np.arange(ATT_KROWS)[None, :]
        row_ok = (rk >= rsr[:, None]) & (rk < rsr[:, None] + kh)
        ridx = np.clip(rk - a + (NA_WIN_H - 1), 0, 2 * NA_WIN_H - 2)
        rows = bias_table[:, ridx, :]
        full = jnp.einsum('haiv,vqk->haqik', rows, jnp.asarray(expand), precision=HIGHEST)
        ok = row_ok[:, None, :, None] & col_ok[None, :, None, :]
        full = jnp.where(ok[None], full * LOG2E, NEG_BIG)
        out.append(full.reshape(full.shape[0], ATT_ROWS * GRID_W, ATT_KROWS * GRID_W))
    return jnp.stack(out).astype(F32)


def _attn(a_lat, a_ctx, bias_table, casts=()):
    b, t, _ = a_lat.shape
    lc = a_ctx.shape[1]
    rows = t // GRID_W
    ws, case, sigs, kh = _attn_plan(rows)
    bias = _attn_bias(bias_table, sigs, kh)
    tq = ATT_ROWS * GRID_W
    nk = ATT_KROWS * GRID_W
    bw = ATT_PAIRS * LANES
    n_blk = NA_W // bw
    nj = rows // ATT_ROWS
    cast_specs = _cast_specs(casts, n_blk * b * nj, lambda p, i, j, cs, w: ((p * b + i) * nj + j, 0))
    grid_spec = pltpu.PrefetchScalarGridSpec(
        num_scalar_prefetch=2,
        grid=(n_blk, b, nj),
        in_specs=[pl.BlockSpec((1, tq, bw), lambda p, i, j, cs, w: (i, j, p)),
                  pl.BlockSpec((1, t, bw), lambda p, i, j, cs, w: (i, 0, n_blk + p)),
                  pl.BlockSpec((1, t, bw), lambda p, i, j, cs, w: (i, 0, 2 * n_blk + p)),
                  pl.BlockSpec((1, lc, bw), lambda p, i, j, cs, w: (i, 0, n_blk + p)),
                  pl.BlockSpec((1, lc, bw), lambda p, i, j, cs, w: (i, 0, 2 * n_blk + p)),
                  pl.BlockSpec((1, 2 * ATT_PAIRS, tq, nk), lambda p, i, j, cs, w: (cs[j], p, 0, 0)),
                  *cast_specs],
        out_specs=(pl.BlockSpec((1, tq, bw), lambda p, i, j, cs, w: (i, j, p)), *cast_specs),
    )
    return pl.pallas_call(
        functools.partial(_attn_kernel, n_cast=len(casts)),
        out_shape=(jax.ShapeDtypeStruct((b, t, NA_W), BF16),
                   *[jax.ShapeDtypeStruct(a.shape, BF16) for a in casts]),
        grid_spec=grid_spec,
        compiler_params=_cparams(("arbitrary", "arbitrary", "arbitrary")),
        name="nattn",
    )(jnp.asarray(case), jnp.asarray(ws), a_lat, a_lat, a_lat, a_ctx, a_ctx, bias, *casts)


def _outproj_kernel(att_ref, hf_ref, hb_ref, om_ref, x_ref, mod_ref, ng_ref, wa_ref, wm_ref,
                    lng_ref, lnb_ref, wrh_ref, wrl_ref, x1_ref, hm_ref, aff_ref, affr_ref):
    h = hf_ref[0].astype(F32) + hb_ref[0].astype(F32)
    parts = []
    for hd in range(ML_HEADS):
        hh = h[:, hd * ML_V_DIM:(hd + 1) * ML_V_DIM]
        mu = jnp.mean(hh, axis=-1, keepdims=True)
        var = jnp.mean(jnp.square(hh - mu), axis=-1, keepdims=True)
        parts.append((hh - mu) * lax.rsqrt(var + LN_EPS))
    hn = jnp.concatenate(parts, axis=1) * ng_ref[...]
    ml = (hn * _sigmoid(om_ref[0].astype(F32))).astype(BF16)
    mix = _dot(att_ref[0], wa_ref[...]) + _dot(ml, wm_ref[...])
    y = DEEPNORM_ALPHA * x_ref[0] + mod_ref[0, 2:3, :] * mix
    mu = jnp.mean(y, axis=-1, keepdims=True)
    var = jnp.mean(jnp.square(y - mu), axis=-1, keepdims=True)
    x1 = (y - mu) * lax.rsqrt(var + LN_EPS) * lng_ref[...] + lnb_ref[...]
    x1_ref[0] = x1
    hm = x1 * (1.0 + mod_ref[0, 4:5, :]) + mod_ref[0, 3:4, :]
    h_hi = hm.astype(BF16)
    h_lo = (hm - h_hi.astype(F32)).astype(BF16)
    bits = pltpu.bitcast(h_hi.astype(F32), jnp.uint32)
    half = bits.shape[1] // 2
    word = (bits[:, :half] >> 16) | (bits[:, half:] & jnp.uint32(0xFFFF0000))
    hm_ref[0] = pltpu.bitcast(word, jnp.int32)
    logits = _dot(h_hi, wrh_ref[...]) + (_dot(h_lo, wrh_ref[...]) + _dot(h_hi, wrl_ref[...]))
    lane = lax.broadcasted_iota(jnp.int32, (1, LANES), 1)
    logits = jnp.where(lane < N_EXPERTS, logits, NEG_BIG)
    e = jnp.exp(logits - jnp.max(logits, axis=-1, keepdims=True))
    aff = e / jnp.sum(e, axis=-1, keepdims=True)
    affr_ref[0] = aff
    aff_ref[0] = aff.T[:N_EXPERTS, :]


def _outproj(att, hf, hb, om, x, mod, ng, w_att, w_ml, lng, lnb, wr_hi, wr_lo, tm):
    b, t, d = x.shape
    row = lambda n: pl.BlockSpec((1, tm, n), lambda i, j: (i, j, 0))
    full = lambda w: pl.BlockSpec(w.shape, lambda i, j: (0,) * w.ndim)
    return pl.pallas_call(
        _outproj_kernel,
        out_shape=(jax.ShapeDtypeStruct((b, t, d), F32),
                   jax.ShapeDtypeStruct((b, t, d // 2), jnp.int32),
                   jax.ShapeDtypeStruct((b, N_EXPERTS, t), F32),
                   jax.ShapeDtypeStruct((b, t, LANES), F32)),
        grid=(b, t // tm),
        in_specs=[row(NA_W), row(ML_V_W), row(ML_V_W), row(ML_V_W), row(d),
                  pl.BlockSpec((1, 8, d), lambda i, j: (i, 0, 0)),
                  full(ng), full(w_att), full(w_ml), full(lng), full(lnb), full(wr_hi), full(wr_lo)],
        out_specs=(row(d), row(d // 2), pl.BlockSpec((1, N_EXPERTS, tm), lambda i, j: (i, 0, j)), row(LANES)),
        compiler_params=_cparams(("arbitrary", "arbitrary")),
        name="outproj",
    )(att, hf, hb, om, x, mod, ng, w_att, w_ml, lng, lnb, wr_hi, wr_lo)


UNSELECTED = -1e6


def _select_kernel(aff_ref, idx_ref, pos_ref, off_ref, cum_scr, sel_scr, offs_v, offs_s, dsem, *, cap):
    t = aff_ref.shape[2]
    nb = t // LANES
    lane = lax.broadcasted_iota(jnp.int32, (1, LANES), 1)
    keys = lambda: pltpu.bitcast(aff_ref[0], jnp.int32)

    def count(mask):
        return jnp.sum(jnp.where(mask, 1.0, 0.0), axis=-1, keepdims=True)

    def search(_, c):
        lo, hi = c
        mid = lo + jnp.right_shift(hi - lo, 1)
        ge = count(keys() >= mid) >= cap
        return jnp.where(ge, mid, lo), jnp.where(ge, hi, mid)

    lo0 = jnp.zeros((N_EXPERTS, 1), jnp.int32)
    hi0 = jnp.full((N_EXPERTS, 1), 0x7F800000, jnp.int32)
    thr, _ = lax.fori_loop(0, 31, search, (lo0, hi0))

    r_i = lax.broadcasted_iota(jnp.int32, (LANES, LANES), 0)
    c_i = lax.broadcasted_iota(jnp.int32, (LANES, LANES), 1)
    strict = jnp.where(r_i < c_i, 1.0, 0.0).astype(BF16)
    tr_i = lax.broadcasted_iota(jnp.int32, (t, LANES), 0)
    tc_i = lax.broadcasted_iota(jnp.int32, (t, LANES), 1)
    block_ind = jnp.where(jnp.right_shift(tr_i, 7) == tc_i, 1.0, 0.0).astype(BF16)

    def prefix(x01):
        xb = x01.astype(BF16)
        offs = _dot(_dot(xb, block_ind).astype(BF16), strict)
        for j in range(nb):
            off_j = jnp.sum(jnp.where(lane == j, offs, 0.0), axis=-1, keepdims=True)
            cum_scr[:, j * LANES:(j + 1) * LANES] = _dot(xb[:, j * LANES:(j + 1) * LANES], strict) + off_j
        return offs

    k = keys()
    gt = k > thr
    eq = k == thr
    need = cap - count(gt)
    prefix(jnp.where(eq, 1.0, 0.0))
    sel = jnp.where(gt | (eq & (cum_scr[...] < need)), 1.0, 0.0)
    sel_scr[...] = sel
    offs = prefix(sel)

    pad = jnp.zeros((LANES - N_EXPERTS, LANES), F32)
    for j in range(nb):
        blk = jnp.where(sel_scr[:, j * LANES:(j + 1) * LANES] > 0.0, cum_scr[:, j * LANES:(j + 1) * LANES], UNSELECTED)
        pos_ref[0, j * LANES:(j + 1) * LANES, :] = jnp.concatenate([blk, pad], axis=0).T
    off_ref[0] = jnp.concatenate([offs, pad], axis=0).T

    cum_scr[...] = cum_scr[...] + sel_scr[...]
    idx_ref[...] = jnp.zeros(idx_ref.shape, idx_ref.dtype)
    n_groups = cap // LANES
    real = lane < nb
    ends = pltpu.roll(offs, LANES - 1, axis=1)
    bounds = jnp.zeros((N_EXPERTS, LANES), F32)
    for g in range(n_groups):
        lo = jnp.sum(jnp.where(real & (ends <= g * LANES), 1.0, 0.0), axis=-1, keepdims=True)
        hi = jnp.sum(jnp.where(real & (offs <= g * LANES + LANES - 1), 1.0, 0.0), axis=-1, keepdims=True)
        bounds = jnp.where(lane == g, lo, jnp.where(lane == n_groups + g, hi, bounds))
    offs_v[...] = bounds.astype(jnp.int32)
    to_smem = pltpu.make_async_copy(offs_v, offs_s, dsem)
    to_smem.start()
    to_smem.wait()
    sub = lax.broadcasted_iota(jnp.int32, (LANES, LANES), 0).astype(F32)
    for e in range(N_EXPERTS):
        def group(pg, carry, e=e):
            first = jnp.asarray(pg * LANES, jnp.int32)
            jlo = offs_s[e, pg]
            jhi = offs_s[e, n_groups + pg]
            slots = first.astype(F32) + sub

            def block(jb, acc):
                c = cum_scr[pl.ds(e, 1), pl.ds(pl.multiple_of(jb * LANES, LANES), LANES)]
                return acc + jnp.where(jnp.broadcast_to(c, (LANES, LANES)) <= slots, 1.0, 0.0)

            acc = lax.fori_loop(jlo, jhi, block, jnp.zeros((LANES, LANES), F32))
            col = jnp.sum(acc, axis=-1, keepdims=True).astype(jnp.int32) + jlo * LANES
            idx_ref[0, pl.ds(pl.multiple_of(pg * LANES, LANES), LANES), e:e + 1] = col
            return carry

        lax.fori_loop(0, n_groups, group, 0)


def _select(aff_t, cap):
    b, e, t = aff_t.shape
    assert cap % LANES == 0 and t % LANES == 0 and t // LANES < LANES and 2 * (cap // LANES) <= LANES
    return pl.pallas_call(
        functools.partial(_select_kernel, cap=cap),
        out_shape=(jax.ShapeDtypeStruct((b, cap, LANES), jnp.int32),
                   jax.ShapeDtypeStruct((b, t, LANES), F32),
                   jax.ShapeDtypeStruct((b, LANES, LANES), F32)),
        grid=(b,),
        in_specs=[pl.BlockSpec((1, e, t), lambda i: (i, 0, 0))],
        out_specs=(pl.BlockSpec((1, cap, LANES), lambda i: (i, 0, 0)),
                   pl.BlockSpec((1, t, LANES), lambda i: (i, 0, 0)),
                   pl.BlockSpec((1, LANES, LANES), lambda i: (i, 0, 0))),
        scratch_shapes=[pltpu.VMEM((e, t), F32), pltpu.VMEM((e, t), F32),
                        pltpu.VMEM((e, LANES), jnp.int32), pltpu.SMEM((e, LANES), jnp.int32),
                        pltpu.SemaphoreType.DMA(())],
        compiler_params=_cparams(("arbitrary",)),
        name="select",
    )(aff_t)


SC_CORES = 2
SC_SUBCORES = 16
SC_CHUNK = 64
MOE_GROUPS = 4


def _gather_rows(table, idx):
    n = idx.shape[0]
    v, d = table.shape
    n_workers = SC_CORES * SC_SUBCORES
    per_w = n // n_workers
    assert n % (n_workers * SC_CHUNK) == 0 and d % LANES == 0 and table.dtype.itemsize == 4
    mesh = plsc.VectorSubcoreMesh(core_axis_name="c", subcore_axis_name="s",
                                  num_cores=SC_CORES, num_subcores=SC_SUBCORES)

    @functools.partial(
        pl.kernel, mesh=mesh,
        out_type=jax.ShapeDtypeStruct((n, d), table.dtype),
        scratch_types=[pltpu.VMEM((SC_CHUNK,), jnp.int32),
                       pltpu.VMEM((SC_CHUNK, d), table.dtype),
                       pltpu.SemaphoreType.DMA],
        name="row_gather")
    def gather(table_hbm, idx_hbm, out_hbm, idx_v, rows_v, sem):
        base = (lax.axis_index("s") * SC_CORES + lax.axis_index("c")) * per_w

        @pl.loop(0, per_w, step=SC_CHUNK)
        def _(o):
            pltpu.sync_copy(idx_hbm.at[pl.ds(base + o, SC_CHUNK)], idx_v)
            pltpu.async_copy(table_hbm.at[idx_v], rows_v, sem).wait()
            pltpu.sync_copy(rows_v, out_hbm.at[pl.ds(base + o, SC_CHUNK)])

    return gather(table, idx)


def _moe_kernel(x_ref, wg_ref, wu_ref, wd_ref, o_ref, x_scr, act_scr, *, tf):
    w = pltpu.bitcast(x_ref[0], jnp.uint32)
    half = w.shape[1]
    x_scr[:, :half] = pltpu.bitcast(w << 16, F32).astype(BF16)
    x_scr[:, half:] = pltpu.bitcast(w & jnp.uint32(0xFFFF0000), F32).astype(BF16)
    x = x_scr[...]
    for c in range(act_scr.shape[1] // tf):
        cols = slice(c * tf, (c + 1) * tf)
        hg = _dot(x, wg_ref[0, :, cols])
        hu = _dot(x, wu_ref[0, :, cols])
        act_scr[:, cols] = (hg * _sigmoid(hg) * hu).astype(BF16)
    o_ref[0] = _dot(act_scr[...], wd_ref[0]).astype(o_ref.dtype)


def _moe(xe, w_gate, w_up, w_down, e0, tm, tf):
    e, m, dw = xe.shape
    d = 2 * dw
    dff = w_gate.shape[2]
    return pl.pallas_call(
        functools.partial(_moe_kernel, tf=tf),
        out_shape=jax.ShapeDtypeStruct((e, m, d), BF16),
        grid=(e, m // tm),
        in_specs=[pl.BlockSpec((1, tm, dw), lambda i, j: (i, j, 0)),
                  pl.BlockSpec((1, d, dff), lambda i, j: (i + e0, 0, 0)),
                  pl.BlockSpec((1, d, dff), lambda i, j: (i + e0, 0, 0)),
                  pl.BlockSpec((1, dff, d), lambda i, j: (i + e0, 0, 0))],
        out_specs=pl.BlockSpec((1, tm, d), lambda i, j: (i, j, 0)),
        scratch_shapes=[pltpu.VMEM((tm, d), BF16), pltpu.VMEM((tm, dff), BF16)],
        compiler_params=_cparams(("arbitrary", "arbitrary")),
        name="moe",
    )(xe, w_gate, w_up, w_down)


CMB_TM = 256
CMB_WIN = 64


def _combine_kernel(p0_ref, x1_ref, pos_ref, aff_ref, mod_ref, lng_ref, lnb_ref, *rest, cap, n_parts):
    y_parts = rest[:n_parts]
    o_ref, ybuf, yextra, acc_ref, sem, sem_x = rest[n_parts:]
    per_part = N_EXPERTS // n_parts
    _combine_body(p0_ref, x1_ref, pos_ref, aff_ref, mod_ref, lng_ref, lnb_ref,
                  lambda e: y_parts[e // per_part].at[e % per_part], y_parts[0].shape[1],
                  o_ref, ybuf, yextra, acc_ref, sem, sem_x, cap)


def _combine_body(p0_ref, x1_ref, pos_ref, aff_ref, mod_ref, lng_ref, lnb_ref, y_of, m_rows, o_ref,
                  ybuf, yextra, acc_ref, sem, sem_x, cap):
    b = pl.program_id(0)
    j = pl.program_id(1)
    nt = pl.num_programs(1)
    g = b * nt + j
    n_steps = pl.num_programs(0) * nt
    slot = lax.rem(g, 2)
    win = CMB_WIN
    kdim = N_EXPERTS * win

    def window(bb, jj, r):
        base = (bb * (nt + 1) + jj) * N_EXPERTS
        los, rows = [], []
        for e in range(N_EXPERTS):
            lo = (p0_ref[base + e] // ROW_ALIGN) * ROW_ALIGN + r * win
            los.append(lo)
            rows.append(jnp.minimum(bb * cap + lo, m_rows - win))
        return los, rows

    def copies(rows, buf, sems):
        return [pltpu.make_async_copy(y_of(e).at[pl.ds(pl.multiple_of(rows[e], ROW_ALIGN), win), :],
                                      buf.at[pl.ds(e * win, win), :], sems.at[e])
                for e in range(N_EXPERTS)]

    los0, rows0 = window(b, j, 0)

    @pl.when(g == 0)
    def _():
        for c in copies(rows0, ybuf.at[0], sem.at[0]):
            c.start()

    @pl.when(g + 1 < n_steps)
    def _():
        wrap = j + 1 == nt
        _, rows_n = window(jnp.where(wrap, b + 1, b), jnp.where(wrap, 0, j + 1), 0)
        for c in copies(rows_n, ybuf.at[1 - slot], sem.at[1 - slot]):
            c.start()

    lane = lax.broadcasted_iota(jnp.int32, (1, LANES), 1)
    e_i = lax.broadcasted_iota(jnp.int32, (LANES, kdim), 0)
    k_i = lax.broadcasted_iota(jnp.int32, (LANES, kdim), 1)
    expand = jnp.where(k_i // win == e_i, 1.0, 0.0).astype(BF16)
    col_in_win = (lax.broadcasted_iota(jnp.int32, (1, kdim), 1) % win).astype(F32)
    aff_cols = _dot(aff_ref[0].astype(BF16), expand)

    def lane_vec(vals):
        v = jnp.zeros((1, LANES), F32)
        for e in range(N_EXPERTS):
            v = jnp.where(lane == e, jnp.asarray(vals[e]).astype(F32), v)
        return v

    def expand_rows(los, rows, buf):
        rel = pos_ref[0] - lane_vec(los)
        shift = lane_vec([b * cap + lo - row for lo, row in zip(los, rows)])
        tgt = jnp.where((rel >= 0.0) & (rel < float(win)), rel + shift, -1.0)
        tgt_cols = _dot(tgt.astype(BF16), expand)
        s = jnp.where(tgt_cols == col_in_win, aff_cols, 0.0).astype(BF16)
        return _dot(s, buf[...])

    for c in copies(rows0, ybuf.at[slot], sem.at[slot]):
        c.wait()
    acc_ref[...] = expand_rows(los0, rows0, ybuf.at[slot])

    base = (b * (nt + 1) + j) * N_EXPERTS
    rounds = jnp.int32(0)
    for e in range(N_EXPERTS):
        span = p0_ref[base + N_EXPERTS + e] - (p0_ref[base + e] // ROW_ALIGN) * ROW_ALIGN
        rounds = jnp.maximum(rounds, (span + win - 1) // win)

    def extra(r, carry):
        los, rows = window(b, j, r)
        cps = copies(rows, yextra, sem_x)
        for c in cps:
            c.start()
        for c in cps:
            c.wait()
        acc_ref[...] += expand_rows(los, rows, yextra)
        return carry

    lax.fori_loop(1, rounds, extra, 0)

    y = DEEPNORM_ALPHA * x1_ref[0] + mod_ref[0, 5:6, :] * acc_ref[...]
    mu = jnp.mean(y, axis=-1, keepdims=True)
    var = jnp.mean(jnp.square(y - mu), axis=-1, keepdims=True)
    o_ref[0] = (y - mu) * lax.rsqrt(var + LN_EPS) * lng_ref[...] + lnb_ref[...]


def _combine(p0tab, x1, pos_t, aff_r, mod, lng, lnb, ys, cap):
    b, t, d = x1.shape
    tm = CMB_TM
    kdim = N_EXPERTS * CMB_WIN
    row = lambda n: pl.BlockSpec((1, tm, n), lambda i, j, p: (i, j, 0))
    vec = pl.BlockSpec((1, d), lambda i, j, p: (0, 0))
    grid_spec = pltpu.PrefetchScalarGridSpec(
        num_scalar_prefetch=1,
        grid=(b, t // tm),
        in_specs=[row(d), row(LANES), row(LANES), pl.BlockSpec((1, 8, d), lambda i, j, p: (i, 0, 0)), vec, vec,
                  *[pl.BlockSpec(memory_space=pl.ANY) for _ in ys]],
        out_specs=row(d),
        scratch_shapes=[pltpu.VMEM((2, kdim, d), ys[0].dtype), pltpu.VMEM((kdim, d), ys[0].dtype),
                        pltpu.VMEM((tm, d), F32),
                        pltpu.SemaphoreType.DMA((2, N_EXPERTS)), pltpu.SemaphoreType.DMA((N_EXPERTS,))],
    )
    return pl.pallas_call(
        functools.partial(_combine_kernel, cap=cap, n_parts=len(ys)),
        out_shape=jax.ShapeDtypeStruct((b, t, d), F32),
        grid_spec=grid_spec,
        compiler_params=_cparams(("arbitrary", "arbitrary")),
        name="combine",
    )(p0tab, x1, pos_t, aff_r, mod, lng, lnb, *ys)


def _split_w_in(w_in):
    a_end = 3 * NA_W
    qk_end = a_end + 2 * ML_QK_W
    v_end = qk_end + ML_V_W
    o_end = v_end + ML_V_W
    wg = jnp.pad(w_in[:, o_end:], ((0, 0), (0, LANES - ML_N_GATES)))
    return tuple(w.astype(BF16) for w in
                 (w_in[:, :a_end], w_in[:, a_end:qk_end], w_in[:, qk_end:v_end], w_in[:, v_end:o_end], wg))


def _layer(x, ctx, c, c_ctx, w_ada, b_ada, w_in, b_gate, conv_qk, na_rel_bias, ml_norm_g, w_out,
           ln1_g, ln1_b, w_router, w_expert_gate, w_expert_up, w_expert_down, ln2_g, ln2_b):
    b, t, d = x.shape
    lc = ctx.shape[1]

    cs = jnp.zeros((16, d), F32).at[:b].set(c).at[b].set(c_ctx)
    m = _ada(cs, w_ada, b_ada)
    mod = jnp.pad(m[:b].reshape(b, 6, d), ((0, 0), (0, 2), (0, 0)))
    mod_ctx = jnp.broadcast_to(jnp.pad(m[b].reshape(6, d), ((0, 2), (0, 0)))[None], (b, 8, d))

    ws = _split_w_in(w_in)
    a_lat, qk_lat, v_lat, o_lat, g_lat = _inproj(x, mod, ws, tm=512)
    a_ctx, qk_ctx, v_ctx, _, g_ctx = _inproj(ctx, mod_ctx, ws, tm=lc)

    conv_w = jnp.pad(conv_qk, ((0, 8 - CONV_K), (0, 0)))
    cos_t, sin_t = _rope_tables(t)
    qk_lat = _qkprep(qk_lat, conv_w, cos_t, sin_t)
    qk_ctx = _qkprep(qk_ctx, conv_w, jnp.ones((lc, LANES), F32), jnp.zeros((lc, LANES), F32))

    bg = jnp.pad(b_gate, (0, LANES - ML_N_GATES)).reshape(1, LANES)
    c0 = jnp.zeros((b, 2, ML_HEADS, LANES, 2 * ML_V_DIM), F32)
    m0 = jnp.zeros((b, 2, 8, LANES), F32)
    n_e, _, dff = w_expert_gate.shape
    _, _, c1, m1 = _mlstm(qk_ctx, v_ctx, g_ctx, bg, c0, m0)
    hf, hb, _, _, wd_b = _mlstm(qk_lat, v_lat, g_lat, bg, c1, m1,
                                casts=(w_expert_down.reshape(n_e * dff, d),))
    att, wg_b, wu_b = _attn(a_lat, a_ctx, na_rel_bias,
                            casts=(w_expert_gate.reshape(n_e * d, dff), w_expert_up.reshape(n_e * d, dff)))

    wr = jnp.pad(w_router, ((0, 0), (0, LANES - N_EXPERTS)))
    wr_hi = wr.astype(BF16)
    wr_lo = (wr - wr_hi.astype(F32)).astype(BF16)
    x1, hm, aff_t, aff_r = _outproj(att, hf, hb, o_lat, x, mod, ml_norm_g.reshape(1, -1),
                                    w_out[:NA_W].astype(BF16), w_out[NA_W:].astype(BF16),
                                    ln1_g.reshape(1, d), ln1_b.reshape(1, d), wr_hi, wr_lo, tm=256)

    cap = CAPACITY_FACTOR * t // N_EXPERTS
    idx_t, pos_t, off_t = _select(aff_t, cap)
    rows = jnp.transpose(idx_t[:, :, :N_EXPERTS], (2, 0, 1)) + (jnp.arange(b, dtype=jnp.int32) * t)[None, :, None]
    per = N_EXPERTS // MOE_GROUPS
    wg3, wu3, wd3 = wg_b.reshape(n_e, d, dff), wu_b.reshape(n_e, d, dff), wd_b.reshape(n_e, dff, d)
    ys = []
    for k in range(MOE_GROUPS):
        xe = _gather_rows(hm.reshape(b * t, d // 2), rows[k * per:(k + 1) * per].reshape(-1))
        ys.append(_moe(xe.reshape(per, b * cap, d // 2), wg3, wu3, wd3, e0=k * per,
                       tm=min(1024, b * cap), tf=256))

    nt = t // CMB_TM
    p0 = off_t[:, 0:nt * (CMB_TM // LANES):CMB_TM // LANES, :N_EXPERTS]
    p0 = jnp.concatenate([p0, jnp.full((b, 1, N_EXPERTS), cap, F32)], axis=1)
    p0tab = p0.astype(jnp.int32).reshape(-1)
    return _combine(p0tab, x1, pos_t, aff_r, mod, ln2_g.reshape(1, d), ln2_b.reshape(1, d), ys, cap)


def kernel(x, c, ctx, c_ctx, w_ada, b_ada, w_in, b_gate, conv_qk, na_rel_bias, ml_norm_g, w_out,
           ln1_g, ln1_b, w_router, w_expert_gate, w_expert_up, w_expert_down, ln2_g, ln2_b):
    return _layer(x, ctx, c, c_ctx, w_ada[0], b_ada[0], w_in[0], b_gate[0], conv_qk[0], na_rel_bias[0],
                  ml_norm_g[0], w_out[0], ln1_g[0], ln1_b[0], w_router[0], w_expert_gate[0],
                  w_expert_up[0], w_expert_down[0], ln2_g[0], ln2_b[0])
```

```python
import functools

import numpy as np
import jax
import jax.numpy as jnp
from jax import lax
from jax.experimental import pallas as pl
from jax.experimental.pallas import tpu as pltpu
from jax.experimental.pallas import tpu_sc as plsc

F32 = jnp.float32
BF16 = jnp.bfloat16
HIGHEST = lax.Precision.HIGHEST

GRID_W = 64
NA_HEADS = 8
NA_HEAD_DIM = 64
NA_WIN_H = 8
NA_WIN_W = 16
NA_W = NA_HEADS * NA_HEAD_DIM
ML_HEADS = 4
ML_QK_DIM = 64
ML_V_DIM = 128
ML_QK_W = ML_HEADS * ML_QK_DIM
ML_V_W = ML_HEADS * ML_V_DIM
ML_CHUNK = 256
ML_SAMPLES = 1
ML_N_GATES = 4 * ML_HEADS
CONV_K = 5
ROPE_BASE = 10000.0
N_EXPERTS = 16
CAPACITY_FACTOR = 2
LN_EPS = 1e-5
DEPTH = 1
DEEPNORM_ALPHA = (2.0 * DEPTH) ** 0.25

LANES = 128
ROW_ALIGN = 16
NEG_BIG = -1e30
LOG2E = 1.4426950408889634
VMEM_LIMIT = 56 * 1024 * 1024

ATT_ROWS = 4
ATT_PAIRS = 4
ATT_KROWS = ATT_ROWS + NA_WIN_H - 1


def _cparams(sem):
    return pltpu.CompilerParams(dimension_semantics=sem, vmem_limit_bytes=VMEM_LIMIT)


def _sigmoid(x):
    return 1.0 / (1.0 + jnp.exp(-x))


def _dot(a, b):
    return jnp.dot(a, b, preferred_element_type=F32)


def _dot_nt(a, b):
    return lax.dot_general(a, b, (((1,), (1,)), ((), ())), preferred_element_type=F32)


def _dot_tn(a, b):
    return lax.dot_general(a, b, (((0,), (0,)), ((), ())), preferred_element_type=F32)


def _ada_kernel(c_ref, w_ref, b_ref, o_ref):
    c = c_ref[...]
    s = c * _sigmoid(c)
    o_ref[...] = jnp.dot(s, w_ref[...], precision=HIGHEST, preferred_element_type=F32) + b_ref[...]


def _ada(cs, w_ada, b_ada):
    rows, d = cs.shape
    n = w_ada.shape[1]
    tn = 512
    return pl.pallas_call(
        _ada_kernel,
        out_shape=jax.ShapeDtypeStruct((rows, n), F32),
        grid=(n // tn,),
        in_specs=[pl.BlockSpec((rows, d), lambda j: (0, 0)),
                  pl.BlockSpec((d, tn), lambda j: (0, j)),
                  pl.BlockSpec((1, tn), lambda j: (0, j))],
        out_specs=pl.BlockSpec((rows, tn), lambda j: (0, j)),
        compiler_params=_cparams(("arbitrary",)),
        name="ada",
    )(cs, w_ada, b_ada.reshape(1, n))


def _inproj_kernel(x_ref, mod_ref, wa_ref, wqk_ref, wv_ref, wo_ref, wg_ref,
                   a_ref, qk_ref, v_ref, o_ref, g_ref):
    xm = (x_ref[0] * (1.0 + mod_ref[0, 1:2, :]) + mod_ref[0, 0:1, :]).astype(BF16)
    a_ref[0] = _dot(xm, wa_ref[...]).astype(BF16)
    qk_ref[0] = _dot(xm, wqk_ref[...]).astype(BF16)
    v_ref[0] = _dot(xm, wv_ref[...]).astype(BF16)
    o_ref[0] = _dot(xm, wo_ref[...]).astype(BF16)
    g_ref[0] = _dot(xm, wg_ref[...])


def _inproj(x, mod, ws, tm):
    b, t, d = x.shape
    wa, wqk, wv, wo, wg = ws
    full = lambda w: pl.BlockSpec(w.shape, lambda i, j: (0, 0))
    row = lambda n: pl.BlockSpec((1, tm, n), lambda i, j: (i, j, 0))
    return pl.pallas_call(
        _inproj_kernel,
        out_shape=(jax.ShapeDtypeStruct((b, t, wa.shape[1]), BF16),
                   jax.ShapeDtypeStruct((b, t, wqk.shape[1]), BF16),
                   jax.ShapeDtypeStruct((b, t, wv.shape[1]), BF16),
                   jax.ShapeDtypeStruct((b, t, wo.shape[1]), BF16),
                   jax.ShapeDtypeStruct((b, t, wg.shape[1]), F32)),
        grid=(b, t // tm),
        in_specs=[row(d), pl.BlockSpec((1, 8, d), lambda i, j: (i, 0, 0)),
                  full(wa), full(wqk), full(wv), full(wo), full(wg)],
        out_specs=(row(wa.shape[1]), row(wqk.shape[1]), row(wv.shape[1]), row(wo.shape[1]),
                   row(wg.shape[1])),
        compiler_params=_cparams(("arbitrary", "arbitrary")),
        name="inproj",
    )(x, mod, wa, wqk, wv, wo, wg)


def _qkprep_kernel(x_ref, w_ref, cos_ref, sin_ref, o_ref, *, sub, halo):
    t = x_ref.shape[1]
    n_sub = t // sub
    lane = lax.broadcasted_iota(jnp.int32, (1, LANES), 1)
    first_half = (lane & 31) < 16
    scale = jnp.where(pl.program_id(1) < (ML_QK_W // LANES), ML_QK_DIM ** -0.5, 1.0).astype(F32)
    zeros = jnp.zeros((halo, LANES), x_ref.dtype)
    for s in range(n_sub):
        lo = s * sub
        top = x_ref[0, lo - halo:lo, :] if s > 0 else zeros
        bot = x_ref[0, lo + sub:lo + sub + halo, :] if s < n_sub - 1 else zeros
        ext = jnp.concatenate([top, x_ref[0, lo:lo + sub, :], bot], axis=0).astype(F32)
        n = sub + 2 * halo
        acc = jnp.zeros((sub, LANES), F32)
        for j in range(CONV_K):
            shift = (CONV_K // 2 - j) % n
            sh = ext if shift == 0 else pltpu.roll(ext, shift, axis=0)
            acc = acc + w_ref[j:j + 1, :] * sh[halo:halo + sub, :]
        y = acc * _sigmoid(acc)
        partner = jnp.where(first_half, pltpu.roll(y, LANES - 16, axis=1), pltpu.roll(y, 16, axis=1))
        y = y * cos_ref[lo:lo + sub, :] + partner * sin_ref[lo:lo + sub, :]
        o_ref[0, lo:lo + sub, :] = (y * scale).astype(o_ref.dtype)


def _qkprep(qk, conv_w, cos_t, sin_t):
    b, t, w = qk.shape
    sub = min(t, 1024)
    kern = functools.partial(_qkprep_kernel, sub=sub, halo=16)
    return pl.pallas_call(
        kern,
        out_shape=jax.ShapeDtypeStruct((b, t, w), BF16),
        grid=(b, w // LANES),
        in_specs=[pl.BlockSpec((1, t, LANES), lambda i, j: (i, 0, j)),
                  pl.BlockSpec((8, LANES), lambda i, j: (0, j)),
                  pl.BlockSpec((t, LANES), lambda i, j: (0, 0)),
                  pl.BlockSpec((t, LANES), lambda i, j: (0, 0))],
        out_specs=pl.BlockSpec((1, t, LANES), lambda i, j: (i, 0, j)),
        compiler_params=_cparams(("arbitrary", "arbitrary")),
        name="qkprep",
    )(qk, conv_w, cos_t, sin_t)


def _rope_tables(t):
    nf = ML_QK_DIM // 4
    inv = 1.0 / (ROPE_BASE ** (jnp.arange(nf, dtype=F32) / nf))
    pos = jnp.arange(t)
    ang_r = (pos // GRID_W).astype(F32)[:, None] * inv
    ang_c = (pos % GRID_W).astype(F32)[:, None] * inv
    cos = jnp.concatenate([jnp.cos(ang_r)] * 2 + [jnp.cos(ang_c)] * 2, axis=-1)
    sin = jnp.concatenate([-jnp.sin(ang_r), jnp.sin(ang_r), -jnp.sin(ang_c), jnp.sin(ang_c)], axis=-1)
    return jnp.tile(cos, (1, 2)), jnp.tile(sin, (1, 2))


def _ride_along_casts(rest, n_cast, n_out):
    outs = rest[n_cast:n_cast + n_out]
    for src, dst in zip(rest[:n_cast], rest[n_cast + n_out:2 * n_cast + n_out]):
        dst[...] = src[...].astype(dst.dtype)
    return outs, rest[2 * n_cast + n_out:]


def _cast_specs(casts, n_steps, index_map):
    specs = []
    for a in casts:
        rows = a.shape[0] // n_steps
        assert a.shape[0] % n_steps == 0 and rows % ROW_ALIGN == 0
        specs.append(pl.BlockSpec((rows, a.shape[1]), index_map))
    return specs


def _mlstm_kernel(qf_ref, kf_ref, vf_ref, gf_ref, qb_ref, kb_ref, vb_ref, gb_ref, bg_ref,
                  c0_ref, m0_ref, sel_ref, *rest, n_cast):
    (hf_ref, hb_ref, cout_ref, mout_ref), (c_scr, m_scr) = _ride_along_casts(rest, n_cast, 4)
    step = pl.program_id(1)
    n_steps = pl.num_programs(1)
    L = ML_CHUNK

    @pl.when(step == 0)
    def _():
        c_scr[...] = c0_ref[...]
        m_scr[...] = m0_ref[...]

    row_i = lax.broadcasted_iota(jnp.int32, (L, L), 0)
    col_i = lax.broadcasted_iota(jnp.int32, (L, L), 1)
    lane = lax.broadcasted_iota(jnp.int32, (1, LANES), 1)
    tri_lo = (col_i <= row_i)
    tri_up = (col_i >= row_i)
    ones_v = jnp.ones((L, ML_V_DIM), BF16)

    for s, d in [(s, d) for s in range(c_scr.shape[0]) for d in range(2)]:
        q_ref, k_ref, v_ref, g_ref, h_ref = ((qf_ref, kf_ref, vf_ref, gf_ref, hf_ref),
                                             (qb_ref, kb_ref, vb_ref, gb_ref, hb_ref))[d]
        tri = tri_lo if d == 0 else tri_up
        g = g_ref[s] + bg_ref[...]
        logf = jnp.minimum(g, 0.0) - jnp.log(1.0 + jnp.exp(-jnp.abs(g)))
        tri_b = jnp.where(tri, 1.0, 0.0).astype(BF16)
        l1 = logf.astype(BF16)
        r1 = logf - l1.astype(F32)
        l2 = r1.astype(BF16)
        l3 = (r1 - l2.astype(F32)).astype(BF16)
        cum = _dot(tri_b, l1) + (_dot(tri_b, l2) + _dot(tri_b, l3))
        f_lo = 4 + 8 * d
        z = jnp.where((lane >= f_lo) & (lane < f_lo + ML_HEADS), cum, g) * LOG2E
        zt = z.T
        z1 = z.astype(BF16)
        zr = z - z1.astype(F32)
        z2 = zr.astype(BF16)
        z3 = (zr - z2.astype(F32)).astype(BF16)
        cols = _dot(jnp.concatenate([z1, z2, z3], axis=1), sel_ref[d])
        end = L - 1 if d == 0 else 0
        wide = lambda x: jnp.concatenate([x, x], axis=1)
        qk_t = []
        for pair in range(ML_HEADS // 2):
            qp = q_ref[s, :, pair * LANES:(pair + 1) * LANES]
            qs = jnp.concatenate([jnp.where((lane >= 64 * hh) & (lane < 64 * hh + 64), qp, 0).astype(BF16)
                                  for hh in range(2)], axis=0)
            both = _dot_nt(qs, k_ref[s, :, pair * LANES:(pair + 1) * LANES])
            qk_t += [both[:L], both[L:]]
        for h in range(ML_HEADS):
            li, lb = 8 * d + h, f_lo + h
            pair, half = h // 2, h % 2
            head_mask = (lane >= 64 * half) & (lane < 64 * half + 64)
            bcol = cols[:, h * LANES:(h + 1) * LANES]
            icol = cols[:, (ML_HEADS + h) * LANES:(ML_HEADS + h + 1) * LANES]
            brow = zt[lb:lb + 1, :]
            irow = zt[li:li + 1, :]
            total = bcol[end:end + 1, :]
            m_prev = m_scr[s, d, h:h + 1, :]

            qm = jnp.where(head_mask, q_ref[s, :, pair * LANES:(pair + 1) * LANES], 0).astype(BF16)
            km = jnp.where(head_mask, k_ref[s, :, pair * LANES:(pair + 1) * LANES], 0).astype(BF16)
            vext = jnp.concatenate([v_ref[s, :, h * ML_V_DIM:(h + 1) * ML_V_DIM], ones_v], axis=1)

            dmat = jnp.where(tri, wide(bcol) + (irow - brow), NEG_BIG)
            m_prev_term = bcol + m_prev
            m_t = jnp.maximum(jnp.max(dmat, axis=-1, keepdims=True), m_prev_term)
            sp = qk_t[h] * jnp.exp2(dmat - wide(m_t))
            inter = jnp.exp2(m_prev_term - m_t)
            c_prev = c_scr[s, d, h]
            r = _dot(jnp.concatenate([sp.astype(BF16), (qm.astype(F32) * inter).astype(BF16)], axis=1),
                     jnp.concatenate([vext, c_prev.astype(BF16)], axis=0))
            num = r[:, :ML_V_DIM]
            den = r[:, ML_V_DIM:]
            h_ref[s, :, h * ML_V_DIM:(h + 1) * ML_V_DIM] = (
                num / jnp.maximum(jnp.abs(den), jnp.exp2(-m_t))).astype(h_ref.dtype)

            a = total - bcol + icol
            m_loc = jnp.max(a, axis=0, keepdims=True)
            kw = (km.astype(F32) * jnp.exp2(a - m_loc)).astype(BF16)
            c_loc = _dot_tn(kw, vext)
            m_new = jnp.maximum(total + m_prev, m_loc)
            c_scr[s, d, h] = (wide(jnp.exp2(total + m_prev - m_new)) * c_prev
                              + wide(jnp.exp2(m_loc - m_new)) * c_loc)
            m_scr[s, d, h:h + 1, :] = m_new

    @pl.when(step == n_steps - 1)
    def _():
        cout_ref[...] = c_scr[...]
        mout_ref[...] = m_scr[...]


def _mlstm(qk, v, gates, bg, c0, m0, casts=()):
    b, t, _ = qk.shape
    L = ML_CHUNK
    nc = t // L
    ns = ML_SAMPLES if b % ML_SAMPLES == 0 else 1
    fwd = lambda n, blk: pl.BlockSpec((ns, L, n), lambda i, c: (i, c, blk))
    bwd = lambda n, blk: pl.BlockSpec((ns, L, n), lambda i, c: (i, nc - 1 - c, blk))
    st_c = pl.BlockSpec((ns,) + c0.shape[1:], lambda i, c: (i, 0, 0, 0, 0))
    st_m = pl.BlockSpec((ns,) + m0.shape[1:], lambda i, c: (i, 0, 0, 0))
    cast_specs = _cast_specs(casts, (b // ns) * nc, lambda i, c: (i * nc + c, 0))
    assert L == 2 * LANES
    gate_lane = np.asarray([[4 + 8 * dd + hh for hh in range(ML_HEADS)] + [8 * dd + hh for hh in range(ML_HEADS)]
                            for dd in range(2)])
    sel = (np.arange(3 * LANES)[None, :, None, None] % LANES == gate_lane[:, None, :, None])
    sel = jnp.asarray(np.broadcast_to(sel, (2, 3 * LANES, 2 * ML_HEADS, LANES))
                      .reshape(2, 3 * LANES, 2 * ML_HEADS * LANES), BF16)
    return pl.pallas_call(
        functools.partial(_mlstm_kernel, n_cast=len(casts)),
        out_shape=(jax.ShapeDtypeStruct((b, t, ML_V_W), BF16),
                   jax.ShapeDtypeStruct((b, t, ML_V_W), BF16),
                   jax.ShapeDtypeStruct(c0.shape, F32),
                   jax.ShapeDtypeStruct(m0.shape, F32),
                   *[jax.ShapeDtypeStruct(a.shape, BF16) for a in casts]),
        grid=(b // ns, nc),
        in_specs=[fwd(ML_QK_W, 0), fwd(ML_QK_W, 1), fwd(ML_V_W, 0), fwd(LANES, 0),
                  bwd(ML_QK_W, 0), bwd(ML_QK_W, 1), bwd(ML_V_W, 0), bwd(LANES, 0),
                  pl.BlockSpec((1, LANES), lambda i, c: (0, 0)), st_c, st_m,
                  pl.BlockSpec(sel.shape, lambda i, c: (0, 0, 0)), *cast_specs],
        out_specs=(fwd(ML_V_W, 0), bwd(ML_V_W, 0), st_c, st_m, *cast_specs),
        scratch_shapes=[pltpu.VMEM((ns,) + c0.shape[1:], F32), pltpu.VMEM((ns,) + m0.shape[1:], F32)],
        compiler_params=_cparams(("arbitrary", "arbitrary")),
        name="mlstm",
    )(qk, qk, v, gates, qk, qk, v, gates, bg, c0, m0, sel, *casts)


def _attn_kernel(case_ref, ws_ref, q_ref, k_ref, v_ref, kc_ref, vc_ref, bias_ref, *rest, n_cast):
    (o_ref,), _ = _ride_along_casts(rest, n_cast, 1)
    j = pl.program_id(2)
    nk = ATT_KROWS * GRID_W
    start = pl.multiple_of(ws_ref[j] * GRID_W, GRID_W)
    lane = lax.broadcasted_iota(jnp.int32, (1, LANES), 1)
    for pp in range(ATT_PAIRS):
        lanes = slice(pp * LANES, (pp + 1) * LANES)
        q = q_ref[0, :, lanes]
        k = k_ref[0, pl.ds(start, nk), lanes]
        v = v_ref[0, pl.ds(start, nk), lanes]
        kc = kc_ref[0, :, lanes]
        vc = vc_ref[0, :, lanes]
        tq = q.shape[0]
        masks = [(lane >= NA_HEAD_DIM * h) & (lane < NA_HEAD_DIM * (h + 1)) for h in range(2)]
        qs = jnp.concatenate(
            [(jnp.where(mk, q, 0).astype(F32) * (NA_HEAD_DIM ** -0.5 * LOG2E)).astype(BF16) for mk in masks], axis=0)
        s_both = _dot_nt(qs, k)
        sc_both = _dot_nt(qs, kc)
        acc = jnp.zeros(q.shape, F32)
        for h in range(2):
            head_mask = masks[h]
            s = s_both[h * tq:(h + 1) * tq] + bias_ref[0, 2 * pp + h]
            sc = sc_both[h * tq:(h + 1) * tq]
            m = jnp.maximum(jnp.max(s, axis=-1, keepdims=True), jnp.max(sc, axis=-1, keepdims=True))
            p = jnp.exp2(s - m)
            pc = jnp.exp2(sc - m)
            vh = jnp.where(head_mask, v, 1).astype(BF16)
            vch = jnp.where(head_mask, vc, 1).astype(BF16)
            o = _dot(p.astype(BF16), vh) + _dot(pc.astype(BF16), vch)
            acc = acc + jnp.where(head_mask, o / pltpu.roll(o, NA_HEAD_DIM, axis=1), 0.0)
        o_ref[0, :, lanes] = acc.astype(o_ref.dtype)


def _attn_plan(rows):
    kh = min(NA_WIN_H, rows)
    nj = rows // ATT_ROWS
    rs = lambda r: int(np.clip(r - kh // 2, 0, rows - kh))
    ws = [int(np.clip(ATT_ROWS * j - kh // 2, 0, rows - ATT_KROWS)) for j in range(nj)]
    sigs, case = [], []
    for j in range(nj):
        r0 = ATT_ROWS * j
        sig = (ws[j] - r0,) + tuple(rs(r0 + a) - r0 for a in range(ATT_ROWS))
        if sig not in sigs:
            sigs.append(sig)
        case.append(sigs.index(sig))
    return np.asarray(ws, np.int32), np.asarray(case, np.int32), sigs, kh


def _attn_bias(bias_table, sigs, kh):
    col_start = np.clip(np.arange(GRID_W) - NA_WIN_W // 2, 0, GRID_W - NA_WIN_W)
    c = np.arange(GRID_W)
    cidx = c[None, :] - c[:, None] + (NA_WIN_W - 1)
    col_ok = (c[None, :] >= col_start[:, None]) & (c[None, :] < col_start[:, None] + NA_WIN_W)
    expand = (np.arange(2 * NA_WIN_W - 1)[:, None, None] == cidx[None]).astype(np.float32)
    out = []
    for sig in sigs:
        wsr, rsr = sig[0], np.asarray(sig[1:])
        a = np.arange(ATT_ROWS)[:, None]
        rk = wsr + np.arange(ATT_KROWS)[None, :]
        row_ok = (rk >= rsr[:, None]) & (rk < rsr[:, None] + kh)
        ridx = np.clip(rk - a + (NA_WIN_H - 1), 0, 2 * NA_WIN_H - 2)
        rows = bias_table[:, ridx, :]
        full = jnp.einsum('haiv,vqk->haqik', rows, jnp.asarray(expand), precision=HIGHEST)
        ok = row_ok[:, None, :, None] & col_ok[None, :, None, :]
        full = jnp.where(ok[None], full * LOG2E, NEG_BIG)
        out.append(full.reshape(full.shape[0], ATT_ROWS * GRID_W, ATT_KROWS * GRID_W))
    return jnp.stack(out).astype(F32)


def _attn(a_lat, a_ctx, bias_table, casts=()):
    b, t, _ = a_lat.shape
    lc = a_ctx.shape[1]
    rows = t // GRID_W
    ws, case, sigs, kh = _attn_plan(rows)
    bias = _attn_bias(bias_table, sigs, kh)
    tq = ATT_ROWS * GRID_W
    nk = ATT_KROWS * GRID_W
    bw = ATT_PAIRS * LANES
    n_blk = NA_W // bw
    nj = rows // ATT_ROWS
    cast_specs = _cast_specs(casts, n_blk * b * nj, lambda p, i, j, cs, w: ((p * b + i) * nj + j, 0))
    grid_spec = pltpu.PrefetchScalarGridSpec(
        num_scalar_prefetch=2,
        grid=(n_blk, b, nj),
        in_specs=[pl.BlockSpec((1, tq, bw), lambda p, i, j, cs, w: (i, j, p)),
                  pl.BlockSpec((1, t, bw), lambda p, i, j, cs, w: (i, 0, n_blk + p)),
                  pl.BlockSpec((1, t, bw), lambda p, i, j, cs, w: (i, 0, 2 * n_blk + p)),
                  pl.BlockSpec((1, lc, bw), lambda p, i, j, cs, w: (i, 0, n_blk + p)),
                  pl.BlockSpec((1, lc, bw), lambda p, i, j, cs, w: (i, 0, 2 * n_blk + p)),
                  pl.BlockSpec((1, 2 * ATT_PAIRS, tq, nk), lambda p, i, j, cs, w: (cs[j], p, 0, 0)),
                  *cast_specs],
        out_specs=(pl.BlockSpec((1, tq, bw), lambda p, i, j, cs, w: (i, j, p)), *cast_specs),
    )
    return pl.pallas_call(
        functools.partial(_attn_kernel, n_cast=len(casts)),
        out_shape=(jax.ShapeDtypeStruct((b, t, NA_W), BF16),
                   *[jax.ShapeDtypeStruct(a.shape, BF16) for a in casts]),
        grid_spec=grid_spec,
        compiler_params=_cparams(("arbitrary", "arbitrary", "arbitrary")),
        name="nattn",
    )(jnp.asarray(case), jnp.asarray(ws), a_lat, a_lat, a_lat, a_ctx, a_ctx, bias, *casts)


def _outproj_kernel(att_ref, hf_ref, hb_ref, om_ref, x_ref, mod_ref, ng_ref, wa_ref, wm_ref,
                    lng_ref, lnb_ref, wrh_ref, wrl_ref, x1_ref, hm_ref, aff_ref, affr_ref):
    h = hf_ref[0].astype(F32) + hb_ref[0].astype(F32)
    parts = []
    for hd in range(ML_HEADS):
        hh = h[:, hd * ML_V_DIM:(hd + 1) * ML_V_DIM]
        mu = jnp.mean(hh, axis=-1, keepdims=True)
        var = jnp.mean(jnp.square(hh - mu), axis=-1, keepdims=True)
        parts.append((hh - mu) * lax.rsqrt(var + LN_EPS))
    hn = jnp.concatenate(parts, axis=1) * ng_ref[...]
    ml = (hn * _sigmoid(om_ref[0].astype(F32))).astype(BF16)
    mix = _dot(att_ref[0], wa_ref[...]) + _dot(ml, wm_ref[...])
    y = DEEPNORM_ALPHA * x_ref[0] + mod_ref[0, 2:3, :] * mix
    mu = jnp.mean(y, axis=-1, keepdims=True)
    var = jnp.mean(jnp.square(y - mu), axis=-1, keepdims=True)
    x1 = (y - mu) * lax.rsqrt(var + LN_EPS) * lng_ref[...] + lnb_ref[...]
    x1_ref[0] = x1
    hm = x1 * (1.0 + mod_ref[0, 4:5, :]) + mod_ref[0, 3:4, :]
    h_hi = hm.astype(BF16)
    h_lo = (hm - h_hi.astype(F32)).astype(BF16)
    bits = pltpu.bitcast(h_hi.astype(F32), jnp.uint32)
    half = bits.shape[1] // 2
    word = (bits[:, :half] >> 16) | (bits[:, half:] & jnp.uint32(0xFFFF0000))
    hm_ref[0] = pltpu.bitcast(word, jnp.int32)
    logits = _dot(h_hi, wrh_ref[...]) + (_dot(h_lo, wrh_ref[...]) + _dot(h_hi, wrl_ref[...]))
    lane = lax.broadcasted_iota(jnp.int32, (1, LANES), 1)
    logits = jnp.where(lane < N_EXPERTS, logits, NEG_BIG)
    e = jnp.exp(logits - jnp.max(logits, axis=-1, keepdims=True))
    aff = e / jnp.sum(e, axis=-1, keepdims=True)
    affr_ref[0] = aff
    aff_ref[0] = aff.T[:N_EXPERTS, :]


def _outproj(att, hf, hb, om, x, mod, ng, w_att, w_ml, lng, lnb, wr_hi, wr_lo, tm):
    b, t, d = x.shape
    row = lambda n: pl.BlockSpec((1, tm, n), lambda i, j: (i, j, 0))
    full = lambda w: pl.BlockSpec(w.shape, lambda i, j: (0,) * w.ndim)
    return pl.pallas_call(
        _outproj_kernel,
        out_shape=(jax.ShapeDtypeStruct((b, t, d), F32),
                   jax.ShapeDtypeStruct((b, t, d // 2), jnp.int32),
                   jax.ShapeDtypeStruct((b, N_EXPERTS, t), F32),
                   jax.ShapeDtypeStruct((b, t, LANES), F32)),
        grid=(b, t // tm),
        in_specs=[row(NA_W), row(ML_V_W), row(ML_V_W), row(ML_V_W), row(d),
                  pl.BlockSpec((1, 8, d), lambda i, j: (i, 0, 0)),
                  full(ng), full(w_att), full(w_ml), full(lng), full(lnb), full(wr_hi), full(wr_lo)],
        out_specs=(row(d), row(d // 2), pl.BlockSpec((1, N_EXPERTS, tm), lambda i, j: (i, 0, j)), row(LANES)),
        compiler_params=_cparams(("arbitrary", "arbitrary")),
        name="outproj",
    )(att, hf, hb, om, x, mod, ng, w_att, w_ml, lng, lnb, wr_hi, wr_lo)


UNSELECTED = -1e6


def _select_kernel(aff_ref, idx_ref, pos_ref, off_ref, cum_scr, sel_scr, offs_v, offs_s, dsem, *, cap):
    t = aff_ref.shape[2]
    nb = t // LANES
    lane = lax.broadcasted_iota(jnp.int32, (1, LANES), 1)
    keys = lambda: pltpu.bitcast(aff_ref[0], jnp.int32)

    def count(mask):
        return jnp.sum(jnp.where(mask, 1.0, 0.0), axis=-1, keepdims=True)

    def search(_, c):
        lo, hi = c
        mid = lo + jnp.right_shift(hi - lo, 1)
        ge = count(keys() >= mid) >= cap
        return jnp.where(ge, mid, lo), jnp.where(ge, hi, mid)

    lo0 = jnp.zeros((N_EXPERTS, 1), jnp.int32)
    hi0 = jnp.full((N_EXPERTS, 1), 0x7F800000, jnp.int32)
    thr, _ = lax.fori_loop(0, 31, search, (lo0, hi0))

    r_i = lax.broadcasted_iota(jnp.int32, (LANES, LANES), 0)
    c_i = lax.broadcasted_iota(jnp.int32, (LANES, LANES), 1)
    strict = jnp.where(r_i < c_i, 1.0, 0.0).astype(BF16)
    tr_i = lax.broadcasted_iota(jnp.int32, (t, LANES), 0)
    tc_i = lax.broadcasted_iota(jnp.int32, (t, LANES), 1)
    block_ind = jnp.where(jnp.right_shift(tr_i, 7) == tc_i, 1.0, 0.0).astype(BF16)

    def prefix(x01):
        xb = x01.astype(BF16)
        offs = _dot(_dot(xb, block_ind).astype(BF16), strict)
        for j in range(nb):
            off_j = jnp.sum(jnp.where(lane == j, offs, 0.0), axis=-1, keepdims=True)
            cum_scr[:, j * LANES:(j + 1) * LANES] = _dot(xb[:, j * LANES:(j + 1) * LANES], strict) + off_j
        return offs

    k = keys()
    gt = k > thr
    eq = k == thr
    need = cap - count(gt)
    prefix(jnp.where(eq, 1.0, 0.0))
    sel = jnp.where(gt | (eq & (cum_scr[...] < need)), 1.0, 0.0)
    sel_scr[...] = sel
    offs = prefix(sel)

    pad = jnp.zeros((LANES - N_EXPERTS, LANES), F32)
    for j in range(nb):
        blk = jnp.where(sel_scr[:, j * LANES:(j + 1) * LANES] > 0.0, cum_scr[:, j * LANES:(j + 1) * LANES], UNSELECTED)
        pos_ref[0, j * LANES:(j + 1) * LANES, :] = jnp.concatenate([blk, pad], axis=0).T
    off_ref[0] = jnp.concatenate([offs, pad], axis=0).T

    cum_scr[...] = cum_scr[...] + sel_scr[...]
    idx_ref[...] = jnp.zeros(idx_ref.shape, idx_ref.dtype)
    n_groups = cap // LANES
    real = lane < nb
    ends = pltpu.roll(offs, LANES - 1, axis=1)
    bounds = jnp.zeros((N_EXPERTS, LANES), F32)
    for g in range(n_groups):
        lo = jnp.sum(jnp.where(real & (ends <= g * LANES), 1.0, 0.0), axis=-1, keepdims=True)
        hi = jnp.sum(jnp.where(real & (offs <= g * LANES + LANES - 1), 1.0, 0.0), axis=-1, keepdims=True)
        bounds = jnp.where(lane == g, lo, jnp.where(lane == n_groups + g, hi, bounds))
    offs_v[...] = bounds.astype(jnp.int32)
    to_smem = pltpu.make_async_copy(offs_v, offs_s, dsem)
    to_smem.start()
    to_smem.wait()
    sub = lax.broadcasted_iota(jnp.int32, (LANES, LANES), 0).astype(F32)
    for e in range(N_EXPERTS):
        def group(pg, carry, e=e):
            first = jnp.asarray(pg * LANES, jnp.int32)
            jlo = offs_s[e, pg]
            jhi = offs_s[e, n_groups + pg]
            slots = first.astype(F32) + sub

            def block(jb, acc):
                c = cum_scr[pl.ds(e, 1), pl.ds(pl.multiple_of(jb * LANES, LANES), LANES)]
                return acc + jnp.where(jnp.broadcast_to(c, (LANES, LANES)) <= slots, 1.0, 0.0)

            acc = lax.fori_loop(jlo, jhi, block, jnp.zeros((LANES, LANES), F32))
            col = jnp.sum(acc, axis=-1, keepdims=True).astype(jnp.int32) + jlo * LANES
            idx_ref[0, pl.ds(pl.multiple_of(pg * LANES, LANES), LANES), e:e + 1] = col
            return carry

        lax.fori_loop(0, n_groups, group, 0)


def _select(aff_t, cap):
    b, e, t = aff_t.shape
    assert cap % LANES == 0 and t % LANES == 0 and t // LANES < LANES and 2 * (cap // LANES) <= LANES
    return pl.pallas_call(
        functools.partial(_select_kernel, cap=cap),
        out_shape=(jax.ShapeDtypeStruct((b, cap, LANES), jnp.int32),
                   jax.ShapeDtypeStruct((b, t, LANES), F32),
                   jax.ShapeDtypeStruct((b, LANES, LANES), F32)),
        grid=(b,),
        in_specs=[pl.BlockSpec((1, e, t), lambda i: (i, 0, 0))],
        out_specs=(pl.BlockSpec((1, cap, LANES), lambda i: (i, 0, 0)),
                   pl.BlockSpec((1, t, LANES), lambda i: (i, 0, 0)),
                   pl.BlockSpec((1, LANES, LANES), lambda i: (i, 0, 0))),
        scratch_shapes=[pltpu.VMEM((e, t), F32), pltpu.VMEM((e, t), F32),
                        pltpu.VMEM((e, LANES), jnp.int32), pltpu.SMEM((e, LANES), jnp.int32),
                        pltpu.SemaphoreType.DMA(())],
        compiler_params=_cparams(("arbitrary",)),
        name="select",
    )(aff_t)


SC_CORES = 2
SC_SUBCORES = 16
SC_CHUNK = 64
MOE_GROUPS = 4


def _gather_rows(table, idx):
    n = idx.shape[0]
    v, d = table.shape
    n_workers = SC_CORES * SC_SUBCORES
    per_w = n // n_workers
    assert n % (n_workers * SC_CHUNK) == 0 and d % LANES == 0 and table.dtype.itemsize == 4
    mesh = plsc.VectorSubcoreMesh(core_axis_name="c", subcore_axis_name="s",
                                  num_cores=SC_CORES, num_subcores=SC_SUBCORES)

    @functools.partial(
        pl.kernel, mesh=mesh,
        out_type=jax.ShapeDtypeStruct((n, d), table.dtype),
        scratch_types=[pltpu.VMEM((SC_CHUNK,), jnp.int32),
                       pltpu.VMEM((SC_CHUNK, d), table.dtype),
                       pltpu.SemaphoreType.DMA],
        name="row_gather")
    def gather(table_hbm, idx_hbm, out_hbm, idx_v, rows_v, sem):
        base = (lax.axis_index("s") * SC_CORES + lax.axis_index("c")) * per_w

        @pl.loop(0, per_w, step=SC_CHUNK)
        def _(o):
            pltpu.sync_copy(idx_hbm.at[pl.ds(base + o, SC_CHUNK)], idx_v)
            pltpu.async_copy(table_hbm.at[idx_v], rows_v, sem).wait()
            pltpu.sync_copy(rows_v, out_hbm.at[pl.ds(base + o, SC_CHUNK)])

    return gather(table, idx)


def _moe_kernel(x_ref, wg_ref, wu_ref, wd_ref, o_ref, x_scr, act_scr, *, tf):
    w = pltpu.bitcast(x_ref[0], jnp.uint32)
    half = w.shape[1]
    x_scr[:, :half] = pltpu.bitcast(w << 16, F32).astype(BF16)
    x_scr[:, half:] = pltpu.bitcast(w & jnp.uint32(0xFFFF0000), F32).astype(BF16)
    x = x_scr[...]
    for c in range(act_scr.shape[1] // tf):
        cols = slice(c * tf, (c + 1) * tf)
        hg = _dot(x, wg_ref[0, :, cols])
        hu = _dot(x, wu_ref[0, :, cols])
        act_scr[:, cols] = (hg * _sigmoid(hg) * hu).astype(BF16)
    o_ref[0] = _dot(act_scr[...], wd_ref[0]).astype(o_ref.dtype)


def _moe(xe, w_gate, w_up, w_down, e0, tm, tf):
    e, m, dw = xe.shape
    d = 2 * dw
    dff = w_gate.shape[2]
    return pl.pallas_call(
        functools.partial(_moe_kernel, tf=tf),
        out_shape=jax.ShapeDtypeStruct((e, m, d), BF16),
        grid=(e, m // tm),
        in_specs=[pl.BlockSpec((1, tm, dw), lambda i, j: (i, j, 0)),
                  pl.BlockSpec((1, d, dff), lambda i, j: (i + e0, 0, 0)),
                  pl.BlockSpec((1, d, dff), lambda i, j: (i + e0, 0, 0)),
                  pl.BlockSpec((1, dff, d), lambda i, j: (i + e0, 0, 0))],
        out_specs=pl.BlockSpec((1, tm, d), lambda i, j: (i, j, 0)),
        scratch_shapes=[pltpu.VMEM((tm, d), BF16), pltpu.VMEM((tm, dff), BF16)],
        compiler_params=_cparams(("arbitrary", "arbitrary")),
        name="moe",
    )(xe, w_gate, w_up, w_down)


CMB_TM = 256
CMB_WIN = 64


def _combine_kernel(p0_ref, x1_ref, pos_ref, aff_ref, mod_ref, lng_ref, lnb_ref, *rest, cap, n_parts):
    y_parts = rest[:n_parts]
    o_ref, ybuf, yextra, acc_ref, sem, sem_x = rest[n_parts:]
    per_part = N_EXPERTS // n_parts
    _combine_body(p0_ref, x1_ref, pos_ref, aff_ref, mod_ref, lng_ref, lnb_ref,
                  lambda e: y_parts[e // per_part].at[e % per_part], y_parts[0].shape[1],
                  o_ref, ybuf, yextra, acc_ref, sem, sem_x, cap)


def _combine_body(p0_ref, x1_ref, pos_ref, aff_ref, mod_ref, lng_ref, lnb_ref, y_of, m_rows, o_ref,
                  ybuf, yextra, acc_ref, sem, sem_x, cap):
    b = pl.program_id(0)
    j = pl.program_id(1)
    nt = pl.num_programs(1)
    g = b * nt + j
    n_steps = pl.num_programs(0) * nt
    slot = lax.rem(g, 2)
    win = CMB_WIN
    kdim = N_EXPERTS * win

    def window(bb, jj, r):
        base = (bb * (nt + 1) + jj) * N_EXPERTS
        los, rows = [], []
        for e in range(N_EXPERTS):
            lo = (p0_ref[base + e] // ROW_ALIGN) * ROW_ALIGN + r * win
            los.append(lo)
            rows.append(jnp.minimum(bb * cap + lo, m_rows - win))
        return los, rows

    def copies(rows, buf, sems):
        return [pltpu.make_async_copy(y_of(e).at[pl.ds(pl.multiple_of(rows[e], ROW_ALIGN), win), :],
                                      buf.at[pl.ds(e * win, win), :], sems.at[e])
                for e in range(N_EXPERTS)]

    los0, rows0 = window(b, j, 0)

    @pl.when(g == 0)
    def _():
        for c in copies(rows0, ybuf.at[0], sem.at[0]):
            c.start()

    @pl.when(g + 1 < n_steps)
    def _():
        wrap = j + 1 == nt
        _, rows_n = window(jnp.where(wrap, b + 1, b), jnp.where(wrap, 0, j + 1), 0)
        for c in copies(rows_n, ybuf.at[1 - slot], sem.at[1 - slot]):
            c.start()

    lane = lax.broadcasted_iota(jnp.int32, (1, LANES), 1)
    e_i = lax.broadcasted_iota(jnp.int32, (LANES, kdim), 0)
    k_i = lax.broadcasted_iota(jnp.int32, (LANES, kdim), 1)
    expand = jnp.where(k_i // win == e_i, 1.0, 0.0).astype(BF16)
    col_in_win = (lax.broadcasted_iota(jnp.int32, (1, kdim), 1) % win).astype(F32)
    aff_cols = _dot(aff_ref[0].astype(BF16), expand)

    def lane_vec(vals):
        v = jnp.zeros((1, LANES), F32)
        for e in range(N_EXPERTS):
            v = jnp.where(lane == e, jnp.asarray(vals[e]).astype(F32), v)
        return v

    def expand_rows(los, rows, buf):
        rel = pos_ref[0] - lane_vec(los)
        shift = lane_vec([b * cap + lo - row for lo, row in zip(los, rows)])
        tgt = jnp.where((rel >= 0.0) & (rel < float(win)), rel + shift, -1.0)
        tgt_cols = _dot(tgt.astype(BF16), expand)
        s = jnp.where(tgt_cols == col_in_win, aff_cols, 0.0).astype(BF16)
        return _dot(s, buf[...])

    for c in copies(rows0, ybuf.at[slot], sem.at[slot]):
        c.wait()
    acc_ref[...] = expand_rows(los0, rows0, ybuf.at[slot])

    base = (b * (nt + 1) + j) * N_EXPERTS
    rounds = jnp.int32(0)
    for e in range(N_EXPERTS):
        span = p0_ref[base + N_EXPERTS + e] - (p0_ref[base + e] // ROW_ALIGN) * ROW_ALIGN
        rounds = jnp.maximum(rounds, (span + win - 1) // win)

    def extra(r, carry):
        los, rows = window(b, j, r)
        cps = copies(rows, yextra, sem_x)
        for c in cps:
            c.start()
        for c in cps:
            c.wait()
        acc_ref[...] += expand_rows(los, rows, yextra)
        return carry

    lax.fori_loop(1, rounds, extra, 0)

    y = DEEPNORM_ALPHA * x1_ref[0] + mod_ref[0, 5:6, :] * acc_ref[...]
    mu = jnp.mean(y, axis=-1, keepdims=True)
    var = jnp.mean(jnp.square(y - mu), axis=-1, keepdims=True)
    o_ref[0] = (y - mu) * lax.rsqrt(var + LN_EPS) * lng_ref[...] + lnb_ref[...]


def _combine(p0tab, x1, pos_t, aff_r, mod, lng, lnb, ys, cap):
    b, t, d = x1.shape
    tm = CMB_TM
    kdim = N_EXPERTS * CMB_WIN
    row = lambda n: pl.BlockSpec((1, tm, n), lambda i, j, p: (i, j, 0))
    vec = pl.BlockSpec((1, d), lambda i, j, p: (0, 0))
    grid_spec = pltpu.PrefetchScalarGridSpec(
        num_scalar_prefetch=1,
        grid=(b, t // tm),
        in_specs=[row(d), row(LANES), row(LANES), pl.BlockSpec((1, 8, d), lambda i, j, p: (i, 0, 0)), vec, vec,
                  *[pl.BlockSpec(memory_space=pl.ANY) for _ in ys]],
        out_specs=row(d),
        scratch_shapes=[pltpu.VMEM((2, kdim, d), ys[0].dtype), pltpu.VMEM((kdim, d), ys[0].dtype),
                        pltpu.VMEM((tm, d), F32),
                        pltpu.SemaphoreType.DMA((2, N_EXPERTS)), pltpu.SemaphoreType.DMA((N_EXPERTS,))],
    )
    return pl.pallas_call(
        functools.partial(_combine_kernel, cap=cap, n_parts=len(ys)),
        out_shape=jax.ShapeDtypeStruct((b, t, d), F32),
        grid_spec=grid_spec,
        compiler_params=_cparams(("arbitrary", "arbitrary")),
        name="combine",
    )(p0tab, x1, pos_t, aff_r, mod, lng, lnb, *ys)


def _split_w_in(w_in):
    a_end = 3 * NA_W
    qk_end = a_end + 2 * ML_QK_W
    v_end = qk_end + ML_V_W
    o_end = v_end + ML_V_W
    wg = jnp.pad(w_in[:, o_end:], ((0, 0), (0, LANES - ML_N_GATES)))
    return tuple(w.astype(BF16) for w in
                 (w_in[:, :a_end], w_in[:, a_end:qk_end], w_in[:, qk_end:v_end], w_in[:, v_end:o_end], wg))


def _layer(x, ctx, c, c_ctx, w_ada, b_ada, w_in, b_gate, conv_qk, na_rel_bias, ml_norm_g, w_out,
           ln1_g, ln1_b, w_router, w_expert_gate, w_expert_up, w_expert_down, ln2_g, ln2_b):
    b, t, d = x.shape
    lc = ctx.shape[1]

    cs = jnp.zeros((16, d), F32).at[:b].set(c).at[b].set(c_ctx)
    m = _ada(cs, w_ada, b_ada)
    mod = jnp.pad(m[:b].reshape(b, 6, d), ((0, 0), (0, 2), (0, 0)))
    mod_ctx = jnp.broadcast_to(jnp.pad(m[b].reshape(6, d), ((0, 2), (0, 0)))[None], (b, 8, d))

    ws = _split_w_in(w_in)
    a_lat, qk_lat, v_lat, o_lat, g_lat = _inproj(x, mod, ws, tm=512)
    a_ctx, qk_ctx, v_ctx, _, g_ctx = _inproj(ctx, mod_ctx, ws, tm=lc)

    conv_w = jnp.pad(conv_qk, ((0, 8 - CONV_K), (0, 0)))
    cos_t, sin_t = _rope_tables(t)
    qk_lat = _qkprep(qk_lat, conv_w, cos_t, sin_t)
    qk_ctx = _qkprep(qk_ctx, conv_w, jnp.ones((lc, LANES), F32), jnp.zeros((lc, LANES), F32))

    bg = jnp.pad(b_gate, (0, LANES - ML_N_GATES)).reshape(1, LANES)
    c0 = jnp.zeros((b, 2, ML_HEADS, LANES, 2 * ML_V_DIM), F32)
    m0 = jnp.zeros((b, 2, 8, LANES), F32)
    n_e, _, dff = w_expert_gate.shape
    _, _, c1, m1 = _mlstm(qk_ctx, v_ctx, g_ctx, bg, c0, m0)
    hf, hb, _, _, wd_b = _mlstm(qk_lat, v_lat, g_lat, bg, c1, m1,
                                casts=(w_expert_down.reshape(n_e * dff, d),))
    att, wg_b, wu_b = _attn(a_lat, a_ctx, na_rel_bias,
                            casts=(w_expert_gate.reshape(n_e * d, dff), w_expert_up.reshape(n_e * d, dff)))

    wr = jnp.pad(w_router, ((0, 0), (0, LANES - N_EXPERTS)))
    wr_hi = wr.astype(BF16)
    wr_lo = (wr - wr_hi.astype(F32)).astype(BF16)
    x1, hm, aff_t, aff_r = _outproj(att, hf, hb, o_lat, x, mod, ml_norm_g.reshape(1, -1),
                                    w_out[:NA_W].astype(BF16), w_out[NA_W:].astype(BF16),
                                    ln1_g.reshape(1, d), ln1_b.reshape(1, d), wr_hi, wr_lo, tm=256)

    cap = CAPACITY_FACTOR * t // N_EXPERTS
    idx_t, pos_t, off_t = _select(aff_t, cap)
    rows = jnp.transpose(idx_t[:, :, :N_EXPERTS], (2, 0, 1)) + (jnp.arange(b, dtype=jnp.int32) * t)[None, :, None]
    per = N_EXPERTS // MOE_GROUPS
    wg3, wu3, wd3 = wg_b.reshape(n_e, d, dff), wu_b.reshape(n_e, d, dff), wd_b.reshape(n_e, dff, d)
    ys = []
    for k in range(MOE_GROUPS):
        xe = _gather_rows(hm.reshape(b * t, d // 2), rows[k * per:(k + 1) * per].reshape(-1))
        ys.append(_moe(xe.reshape(per, b * cap, d // 2), wg3, wu3, wd3, e0=k * per,
                       tm=min(1024, b * cap), tf=256))

    nt = t // CMB_TM
    p0 = off_t[:, 0:nt * (CMB_TM // LANES):CMB_TM // LANES, :N_EXPERTS]
    p0 = jnp.concatenate([p0, jnp.full((b, 1, N_EXPERTS), cap, F32)], axis=1)
    p0tab = p0.astype(jnp.int32).reshape(-1)
    return _combine(p0tab, x1, pos_t, aff_r, mod, ln2_g.reshape(1, d), ln2_b.reshape(1, d), ys, cap)


def kernel(x, c, ctx, c_ctx, w_ada, b_ada, w_in, b_gate, conv_qk, na_rel_bias, ml_norm_g, w_out,
           ln1_g, ln1_b, w_router, w_expert_gate, w_expert_up, w_expert_down, ln2_g, ln2_b):
    return _layer(x, ctx, c, c_ctx, w_ada[0], b_ada[0], w_in[0], b_gate[0], conv_qk[0], na_rel_bias[0],
                  ml_norm_g[0], w_out[0], ln1_g[0], ln1_b[0], w_router[0], w_expert_gate[0],
                  w_expert_up[0], w_expert_down[0], ln2_g[0], ln2_b[0])
```

```python
import functools

import numpy as np
import jax
import jax.numpy as jnp
from jax import lax
from jax.experimental import pallas as pl
from jax.experimental.pallas import tpu as pltpu
from jax.experimental.pallas import tpu_sc as plsc

F32 = jnp.float32
BF16 = jnp.bfloat16
HIGHEST = lax.Precision.HIGHEST

GRID_W = 64
NA_HEADS = 8
NA_HEAD_DIM = 64
NA_WIN_H = 8
NA_WIN_W = 16
NA_W = NA_HEADS * NA_HEAD_DIM
ML_HEADS = 4
ML_QK_DIM = 64
ML_V_DIM = 128
ML_QK_W = ML_HEADS * ML_QK_DIM
ML_V_W = ML_HEADS * ML_V_DIM
ML_CHUNK = 256
ML_SAMPLES = 1
ML_N_GATES = 4 * ML_HEADS
CONV_K = 5
ROPE_BASE = 10000.0
N_EXPERTS = 16
CAPACITY_FACTOR = 2
LN_EPS = 1e-5
DEPTH = 1
DEEPNORM_ALPHA = (2.0 * DEPTH) ** 0.25

LANES = 128
ROW_ALIGN = 16
NEG_BIG = -1e30
LOG2E = 1.4426950408889634
VMEM_LIMIT = 56 * 1024 * 1024

ATT_ROWS = 4
ATT_PAIRS = 4
ATT_KROWS = ATT_ROWS + NA_WIN_H - 1


def _cparams(sem):
    return pltpu.CompilerParams(dimension_semantics=sem, vmem_limit_bytes=VMEM_LIMIT)


def _sigmoid(x):
    return 1.0 / (1.0 + jnp.exp(-x))


def _dot(a, b):
    return jnp.dot(a, b, preferred_element_type=F32)


def _dot_nt(a, b):
    return lax.dot_general(a, b, (((1,), (1,)), ((), ())), preferred_element_type=F32)


def _dot_tn(a, b):
    return lax.dot_general(a, b, (((0,), (0,)), ((), ())), preferred_element_type=F32)


def _ada_kernel(c_ref, w_ref, b_ref, o_ref):
    c = c_ref[...]
    s = c * _sigmoid(c)
    o_ref[...] = jnp.dot(s, w_ref[...], precision=HIGHEST, preferred_element_type=F32) + b_ref[...]


def _ada(cs, w_ada, b_ada):
    rows, d = cs.shape
    n = w_ada.shape[1]
    tn = 512
    return pl.pallas_call(
        _ada_kernel,
        out_shape=jax.ShapeDtypeStruct((rows, n), F32),
        grid=(n // tn,),
        in_specs=[pl.BlockSpec((rows, d), lambda j: (0, 0)),
                  pl.BlockSpec((d, tn), lambda j: (0, j)),
                  pl.BlockSpec((1, tn), lambda j: (0, j))],
        out_specs=pl.BlockSpec((rows, tn), lambda j: (0, j)),
        compiler_params=_cparams(("arbitrary",)),
        name="ada",
    )(cs, w_ada, b_ada.reshape(1, n))


def _inproj_kernel(x_ref, mod_ref, wa_ref, wqk_ref, wv_ref, wo_ref, wg_ref,
                   a_ref, qk_ref, v_ref, o_ref, g_ref):
    xm = (x_ref[0] * (1.0 + mod_ref[0, 1:2, :]) + mod_ref[0, 0:1, :]).astype(BF16)
    a_ref[0] = _dot(xm, wa_ref[...]).astype(BF16)
    qk_ref[0] = _dot(xm, wqk_ref[...]).astype(BF16)
    v_ref[0] = _dot(xm, wv_ref[...]).astype(BF16)
    o_ref[0] = _dot(xm, wo_ref[...]).astype(BF16)
    g_ref[0] = _dot(xm, wg_ref[...])


def _inproj(x, mod, ws, tm):
    b, t, d = x.shape
    wa, wqk, wv, wo, wg = ws
    full = lambda w: pl.BlockSpec(w.shape, lambda i, j: (0, 0))
    row = lambda n: pl.BlockSpec((1, tm, n), lambda i, j: (i, j, 0))
    return pl.pallas_call(
        _inproj_kernel,
        out_shape=(jax.ShapeDtypeStruct((b, t, wa.shape[1]), BF16),
                   jax.ShapeDtypeStruct((b, t, wqk.shape[1]), BF16),
                   jax.ShapeDtypeStruct((b, t, wv.shape[1]), BF16),
                   jax.ShapeDtypeStruct((b, t, wo.shape[1]), BF16),
                   jax.ShapeDtypeStruct((b, t, wg.shape[1]), F32)),
        grid=(b, t // tm),
        in_specs=[row(d), pl.BlockSpec((1, 8, d), lambda i, j: (i, 0, 0)),
                  full(wa), full(wqk), full(wv), full(wo), full(wg)],
        out_specs=(row(wa.shape[1]), row(wqk.shape[1]), row(wv.shape[1]), row(wo.shape[1]),
                   row(wg.shape[1])),
        compiler_params=_cparams(("arbitrary", "arbitrary")),
        name="inproj",
    )(x, mod, wa, wqk, wv, wo, wg)


def _qkprep_kernel(x_ref, w_ref, cos_ref, sin_ref, o_ref, *, sub, halo):
    t = x_ref.shape[1]
    n_sub = t // sub
    lane = lax.broadcasted_iota(jnp.int32, (1, LANES), 1)
    first_half = (lane & 31) < 16
    scale = jnp.where(pl.program_id(1) < (ML_QK_W // LANES), ML_QK_DIM ** -0.5, 1.0).astype(F32)
    zeros = jnp.zeros((halo, LANES), x_ref.dtype)
    for s in range(n_sub):
        lo = s * sub
        top = x_ref[0, lo - halo:lo, :] if s > 0 else zeros
        bot = x_ref[0, lo + sub:lo + sub + halo, :] if s < n_sub - 1 else zeros
        ext = jnp.concatenate([top, x_ref[0, lo:lo + sub, :], bot], axis=0).astype(F32)
        n = sub + 2 * halo
        acc = jnp.zeros((sub, LANES), F32)
        for j in range(CONV_K):
            shift = (CONV_K // 2 - j) % n
            sh = ext if shift == 0 else pltpu.roll(ext, shift, axis=0)
            acc = acc + w_ref[j:j + 1, :] * sh[halo:halo + sub, :]
        y = acc * _sigmoid(acc)
        partner = jnp.where(first_half, pltpu.roll(y, LANES - 16, axis=1), pltpu.roll(y, 16, axis=1))
        y = y * cos_ref[lo:lo + sub, :] + partner * sin_ref[lo:lo + sub, :]
        o_ref[0, lo:lo + sub, :] = (y * scale).astype(o_ref.dtype)


def _qkprep(qk, conv_w, cos_t, sin_t):
    b, t, w = qk.shape
    sub = min(t, 1024)
    kern = functools.partial(_qkprep_kernel, sub=sub, halo=16)
    return pl.pallas_call(
        kern,
        out_shape=jax.ShapeDtypeStruct((b, t, w), BF16),
        grid=(b, w // LANES),
        in_specs=[pl.BlockSpec((1, t, LANES), lambda i, j: (i, 0, j)),
                  pl.BlockSpec((8, LANES), lambda i, j: (0, j)),
                  pl.BlockSpec((t, LANES), lambda i, j: (0, 0)),
                  pl.BlockSpec((t, LANES), lambda i, j: (0, 0))],
        out_specs=pl.BlockSpec((1, t, LANES), lambda i, j: (i, 0, j)),
        compiler_params=_cparams(("arbitrary", "arbitrary")),
        name="qkprep",
    )(qk, conv_w, cos_t, sin_t)


def _rope_tables(t):
    nf = ML_QK_DIM // 4
    inv = 1.0 / (ROPE_BASE ** (jnp.arange(nf, dtype=F32) / nf))
    pos = jnp.arange(t)
    ang_r = (pos // GRID_W).astype(F32)[:, None] * inv
    ang_c = (pos % GRID_W).astype(F32)[:, None] * inv
    cos = jnp.concatenate([jnp.cos(ang_r)] * 2 + [jnp.cos(ang_c)] * 2, axis=-1)
    sin = jnp.concatenate([-jnp.sin(ang_r), jnp.sin(ang_r), -jnp.sin(ang_c), jnp.sin(ang_c)], axis=-1)
    return jnp.tile(cos, (1, 2)), jnp.tile(sin, (1, 2))


def _ride_along_casts(rest, n_cast, n_out):
    outs = rest[n_cast:n_cast + n_out]
    for src, dst in zip(rest[:n_cast], rest[n_cast + n_out:2 * n_cast + n_out]):
        dst[...] = src[...].astype(dst.dtype)
    return outs, rest[2 * n_cast + n_out:]


def _cast_specs(casts, n_steps, index_map):
    specs = []
    for a in casts:
        rows = a.shape[0] // n_steps
        assert a.shape[0] % n_steps == 0 and rows % ROW_ALIGN == 0
        specs.append(pl.BlockSpec((rows, a.shape[1]), index_map))
    return specs


def _mlstm_kernel(qf_ref, kf_ref, vf_ref, gf_ref, qb_ref, kb_ref, vb_ref, gb_ref, bg_ref,
                  c0_ref, m0_ref, sel_ref, *rest, n_cast):
    (hf_ref, hb_ref, cout_ref, mout_ref), (c_scr, m_scr) = _ride_along_casts(rest, n_cast, 4)
    step = pl.program_id(1)
    n_steps = pl.num_programs(1)
    L = ML_CHUNK

    @pl.when(step == 0)
    def _():
        c_scr[...] = c0_ref[...]
        m_scr[...] = m0_ref[...]

    row_i = lax.broadcasted_iota(jnp.int32, (L, L), 0)
    col_i = lax.broadcasted_iota(jnp.int32, (L, L), 1)
    lane = lax.broadcasted_iota(jnp.int32, (1, LANES), 1)
    tri_lo = (col_i <= row_i)
    tri_up = (col_i >= row_i)
    ones_v = jnp.ones((L, ML_V_DIM), BF16)

    for s, d in [(s, d) for s in range(c_scr.shape[0]) for d in range(2)]:
        q_ref, k_ref, v_ref, g_ref, h_ref = ((qf_ref, kf_ref, vf_ref, gf_ref, hf_ref),
                                             (qb_ref, kb_ref, vb_ref, gb_ref, hb_ref))[d]
        tri = tri_lo if d == 0 else tri_up
        g = g_ref[s] + bg_ref[...]
        logf = jnp.minimum(g, 0.0) - jnp.log(1.0 + jnp.exp(-jnp.abs(g)))
        tri_b = jnp.where(tri, 1.0, 0.0).astype(BF16)
        l1 = logf.astype(BF16)
        r1 = logf - l1.astype(F32)
        l2 = r1.astype(BF16)
        l3 = (r1 - l2.astype(F32)).astype(BF16)
        cum = _dot(tri_b, l1) + (_dot(tri_b, l2) + _dot(tri_b, l3))
        f_lo = 4 + 8 * d
        z = jnp.where((lane >= f_lo) & (lane < f_lo + ML_HEADS), cum, g) * LOG2E
        zt = z.T
        z1 = z.astype(BF16)
        zr = z - z1.astype(F32)
        z2 = zr.astype(BF16)
        z3 = (zr - z2.astype(F32)).astype(BF16)
        cols = _dot(jnp.concatenate([z1, z2, z3], axis=1), sel_ref[d])
        end = L - 1 if d == 0 else 0
        wide = lambda x: jnp.concatenate([x, x], axis=1)
        qk_t = []
        for pair in range(ML_HEADS // 2):
            qp = q_ref[s, :, pair * LANES:(pair + 1) * LANES]
            qs = jnp.concatenate([jnp.where((lane >= 64 * hh) & (lane < 64 * hh + 64), qp, 0).astype(BF16)
                                  for hh in range(2)], axis=0)
            both = _dot_nt(qs, k_ref[s, :, pair * LANES:(pair + 1) * LANES])
            qk_t += [both[:L], both[L:]]
        for h in range(ML_HEADS):
            li, lb = 8 * d + h, f_lo + h
            pair, half = h // 2, h % 2
            head_mask = (lane >= 64 * half) & (lane < 64 * half + 64)
            bcol = cols[:, h * LANES:(h + 1) * LANES]
            icol = cols[:, (ML_HEADS + h) * LANES:(ML_HEADS + h + 1) * LANES]
            brow = zt[lb:lb + 1, :]
            irow = zt[li:li + 1, :]
            total = bcol[end:end + 1, :]
            m_prev = m_scr[s, d, h:h + 1, :]

            qm = jnp.where(head_mask, q_ref[s, :, pair * LANES:(pair + 1) * LANES], 0).astype(BF16)
            km = jnp.where(head_mask, k_ref[s, :, pair * LANES:(pair + 1) * LANES], 0).astype(BF16)
            vext = jnp.concatenate([v_ref[s, :, h * ML_V_DIM:(h + 1) * ML_V_DIM], ones_v], axis=1)

            dmat = jnp.where(tri, wide(bcol) + (irow - brow), NEG_BIG)
            m_prev_term = bcol + m_prev
            m_t = jnp.maximum(jnp.max(dmat, axis=-1, keepdims=True), m_prev_term)
            sp = qk_t[h] * jnp.exp2(dmat - wide(m_t))
            inter = jnp.exp2(m_prev_term - m_t)
            c_prev = c_scr[s, d, h]
            r = _dot(jnp.concatenate([sp.astype(BF16), (qm.astype(F32) * inter).astype(BF16)], axis=1),
                     jnp.concatenate([vext, c_prev.astype(BF16)], axis=0))
            num = r[:, :ML_V_DIM]
            den = r[:, ML_V_DIM:]
            h_ref[s, :, h * ML_V_DIM:(h + 1) * ML_V_DIM] = (
                num / jnp.maximum(jnp.abs(den), jnp.exp2(-m_t))).astype(h_ref.dtype)

            a = total - bcol + icol
            m_loc = jnp.max(a, axis=0, keepdims=True)
            kw = (km.astype(F32) * jnp.exp2(a - m_loc)).astype(BF16)
            c_loc = _dot_tn(kw, vext)
            m_new = jnp.maximum(total + m_prev, m_loc)
            c_scr[s, d, h] = (wide(jnp.exp2(total + m_prev - m_new)) * c_prev
                              + wide(jnp.exp2(m_loc - m_new)) * c_loc)
            m_scr[s, d, h:h + 1, :] = m_new

    @pl.when(step == n_steps - 1)
    def _():
        cout_ref[...] = c_scr[...]
        mout_ref[...] = m_scr[...]


def _mlstm(qk, v, gates, bg, c0, m0, casts=()):
    b, t, _ = qk.shape
    L = ML_CHUNK
    nc = t // L
    ns = ML_SAMPLES if b % ML_SAMPLES == 0 else 1
    fwd = lambda n, blk: pl.BlockSpec((ns, L, n), lambda i, c: (i, c, blk))
    bwd = lambda n, blk: pl.BlockSpec((ns, L, n), lambda i, c: (i, nc - 1 - c, blk))
    st_c = pl.BlockSpec((ns,) + c0.shape[1:], lambda i, c: (i, 0, 0, 0, 0))
    st_m = pl.BlockSpec((ns,) + m0.shape[1:], lambda i, c: (i, 0, 0, 0))
    cast_specs = _cast_specs(casts, (b // ns) * nc, lambda i, c: (i * nc + c, 0))
    assert L == 2 * LANES
    gate_lane = np.asarray([[4 + 8 * dd + hh for hh in range(ML_HEADS)] + [8 * dd + hh for hh in range(ML_HEADS)]
                            for dd in range(2)])
    sel = (np.arange(3 * LANES)[None, :, None, None] % LANES == gate_lane[:, None, :, None])
    sel = jnp.asarray(np.broadcast_to(sel, (2, 3 * LANES, 2 * ML_HEADS, LANES))
                      .reshape(2, 3 * LANES, 2 * ML_HEADS * LANES), BF16)
    return pl.pallas_call(
        functools.partial(_mlstm_kernel, n_cast=len(casts)),
        out_shape=(jax.ShapeDtypeStruct((b, t, ML_V_W), BF16),
                   jax.ShapeDtypeStruct((b, t, ML_V_W), BF16),
                   jax.ShapeDtypeStruct(c0.shape, F32),
                   jax.ShapeDtypeStruct(m0.shape, F32),
                   *[jax.ShapeDtypeStruct(a.shape, BF16) for a in casts]),
        grid=(b // ns, nc),
        in_specs=[fwd(ML_QK_W, 0), fwd(ML_QK_W, 1), fwd(ML_V_W, 0), fwd(LANES, 0),
                  bwd(ML_QK_W, 0), bwd(ML_QK_W, 1), bwd(ML_V_W, 0), bwd(LANES, 0),
                  pl.BlockSpec((1, LANES), lambda i, c: (0, 0)), st_c, st_m,
                  pl.BlockSpec(sel.shape, lambda i, c: (0, 0, 0)), *cast_specs],
        out_specs=(fwd(ML_V_W, 0), bwd(ML_V_W, 0), st_c, st_m, *cast_specs),
        scratch_shapes=[pltpu.VMEM((ns,) + c0.shape[1:], F32), pltpu.VMEM((ns,) + m0.shape[1:], F32)],
        compiler_params=_cparams(("arbitrary", "arbitrary")),
        name="mlstm",
    )(qk, qk, v, gates, qk, qk, v, gates, bg, c0, m0, sel, *casts)


def _attn_kernel(case_ref, ws_ref, q_ref, k_ref, v_ref, kc_ref, vc_ref, bias_ref, *rest, n_cast):
    (o_ref,), _ = _ride_along_casts(rest, n_cast, 1)
    j = pl.program_id(2)
    nk = ATT_KROWS * GRID_W
    start = pl.multiple_of(ws_ref[j] * GRID_W, GRID_W)
    lane = lax.broadcasted_iota(jnp.int32, (1, LANES), 1)
    for pp in range(ATT_PAIRS):
        lanes = slice(pp * LANES, (pp + 1) * LANES)
        q = q_ref[0, :, lanes]
        k = k_ref[0, pl.ds(start, nk), lanes]
        v = v_ref[0, pl.ds(start, nk), lanes]
        kc = kc_ref[0, :, lanes]
        vc = vc_ref[0, :, lanes]
        tq = q.shape[0]
        masks = [(lane >= NA_HEAD_DIM * h) & (lane < NA_HEAD_DIM * (h + 1)) for h in range(2)]
        qs = jnp.concatenate(
            [(jnp.where(mk, q, 0).astype(F32) * (NA_HEAD_DIM ** -0.5 * LOG2E)).astype(BF16) for mk in masks], axis=0)
        s_both = _dot_nt(qs, k)
        sc_both = _dot_nt(qs, kc)
        acc = jnp.zeros(q.shape, F32)
        for h in range(2):
            head_mask = masks[h]
            s = s_both[h * tq:(h + 1) * tq] + bias_ref[0, 2 * pp + h]
            sc = sc_both[h * tq:(h + 1) * tq]
            m = jnp.maximum(jnp.max(s, axis=-1, keepdims=True), jnp.max(sc, axis=-1, keepdims=True))
            p = jnp.exp2(s - m)
            pc = jnp.exp2(sc - m)
            vh = jnp.where(head_mask, v, 1).astype(BF16)
            vch = jnp.where(head_mask, vc, 1).astype(BF16)
            o = _dot(p.astype(BF16), vh) + _dot(pc.astype(BF16), vch)
            acc = acc + jnp.where(head_mask, o / pltpu.roll(o, NA_HEAD_DIM, axis=1), 0.0)
        o_ref[0, :, lanes] = acc.astype(o_ref.dtype)


def _attn_plan(rows):
    kh = min(NA_WIN_H, rows)
    nj = rows // ATT_ROWS
    rs = lambda r: int(np.clip(r - kh // 2, 0, rows - kh))
    ws = [int(np.clip(ATT_ROWS * j - kh // 2, 0, rows - ATT_KROWS)) for j in range(nj)]
    sigs, case = [], []
    for j in range(nj):
        r0 = ATT_ROWS * j
        sig = (ws[j] - r0,) + tuple(rs(r0 + a) - r0 for a in range(ATT_ROWS))
        if sig not in sigs:
            sigs.append(sig)
        case.append(sigs.index(sig))
    return np.asarray(ws, np.int32), np.asarray(case, np.int32), sigs, kh


def _attn_bias(bias_table, sigs, kh):
    col_start = np.clip(np.arange(GRID_W) - NA_WIN_W // 2, 0, GRID_W - NA_WIN_W)
    c = np.arange(GRID_W)
    cidx = c[None, :] - c[:, None] + (NA_WIN_W - 1)
    col_ok = (c[None, :] >= col_start[:, None]) & (c[None, :] < col_start[:, None] + NA_WIN_W)
    expand = (np.arange(2 * NA_WIN_W - 1)[:, None, None] == cidx[None]).astype(np.float32)
    out = []
    for sig in sigs:
        wsr, rsr = sig[0], np.asarray(sig[1:])
        a = np.arange(ATT_ROWS)[:, None]
        rk = wsr + np.arange(ATT_KROWS)[None, :]
        row_ok = (rk >= rsr[:, None]) & (rk < rsr[:, None] + kh)
        ridx = np.clip(rk - a + (NA_WIN_H - 1), 0, 2 * NA_WIN_H - 2)
        rows = bias_table[:, ridx, :]
        full = jnp.einsum('haiv,vqk->haqik', rows, jnp.asarray(expand), precision=HIGHEST)
        ok = row_ok[:, None, :, None] & col_ok[None, :, None, :]
        full = jnp.where(ok[None], full * LOG2E, NEG_BIG)
        out.append(full.reshape(full.shape[0], ATT_ROWS * GRID_W, ATT_KROWS * GRID_W))
    return jnp.stack(out).astype(F32)


def _attn(a_lat, a_ctx, bias_table, casts=()):
    b, t, _ = a_lat.shape
    lc = a_ctx.shape[1]
    rows = t // GRID_W
    ws, case, sigs, kh = _attn_plan(rows)
    bias = _attn_bias(bias_table, sigs, kh)
    tq = ATT_ROWS * GRID_W
    nk = ATT_KROWS * GRID_W
    bw = ATT_PAIRS * LANES
    n_blk = NA_W // bw
    nj = rows // ATT_ROWS
    cast_specs = _cast_specs(casts, n_blk * b * nj, lambda p, i, j, cs, w: ((p * b + i) * nj + j, 0))
    grid_spec = pltpu.PrefetchScalarGridSpec(
        num_scalar_prefetch=2,
        grid=(n_blk, b, nj),
        in_specs=[pl.BlockSpec((1, tq, bw), lambda p, i, j, cs, w: (i, j, p)),
                  pl.BlockSpec((1, t, bw), lambda p, i, j, cs, w: (i, 0, n_blk + p)),
                  pl.BlockSpec((1, t, bw), lambda p, i, j, cs, w: (i, 0, 2 * n_blk + p)),
                  pl.BlockSpec((1, lc, bw), lambda p, i, j, cs, w: (i, 0, n_blk + p)),
                  pl.BlockSpec((1, lc, bw), lambda p, i, j, cs, w: (i, 0, 2 * n_blk + p)),
                  pl.BlockSpec((1, 2 * ATT_PAIRS, tq, nk), lambda p, i, j, cs, w: (cs[j], p, 0, 0)),
                  *cast_specs],
        out_specs=(pl.BlockSpec((1, tq, bw), lambda p, i, j, cs, w: (i, j, p)), *cast_specs),
    )
    return pl.pallas_call(
        functools.partial(_attn_kernel, n_cast=len(casts)),
        out_shape=(jax.ShapeDtypeStruct((b, t, NA_W), BF16),
                   *[jax.ShapeDtypeStruct(a.shape, BF16) for a in casts]),
        grid_spec=grid_spec,
        compiler_params=_cparams(("arbitrary", "arbitrary", "arbitrary")),
        name="nattn",
    )(jnp.asarray(case), jnp.asarray(ws), a_lat, a_lat, a_lat, a_ctx, a_ctx, bias, *casts)


def _outproj_kernel(att_ref, hf_ref, hb_ref, om_ref, x_ref, mod_ref, ng_ref, wa_ref, wm_ref,
                    lng_ref, lnb_ref, wrh_ref, wrl_ref, x1_ref, hm_ref, aff_ref, affr_ref):
    h = hf_ref[0].astype(F32) + hb_ref[0].astype(F32)
    parts = []
    for hd in range(ML_HEADS):
        hh = h[:, hd * ML_V_DIM:(hd + 1) * ML_V_DIM]
        mu = jnp.mean(hh, axis=-1, keepdims=True)
        var = jnp.mean(jnp.square(hh - mu), axis=-1, keepdims=True)
        parts.append((hh - mu) * lax.rsqrt(var + LN_EPS))
    hn = jnp.concatenate(parts, axis=1) * ng_ref[...]
    ml = (hn * _sigmoid(om_ref[0].astype(F32))).astype(BF16)
    mix = _dot(att_ref[0], wa_ref[...]) + _dot(ml, wm_ref[...])
    y = DEEPNORM_ALPHA * x_ref[0] + mod_ref[0, 2:3, :] * mix
    mu = jnp.mean(y, axis=-1, keepdims=True)
    var = jnp.mean(jnp.square(y - mu), axis=-1, keepdims=True)
    x1 = (y - mu) * lax.rsqrt(var + LN_EPS) * lng_ref[...] + lnb_ref[...]
    x1_ref[0] = x1
    hm = x1 * (1.0 + mod_ref[0, 4:5, :]) + mod_ref[0, 3:4, :]
    h_hi = hm.astype(BF16)
    h_lo = (hm - h_hi.astype(F32)).astype(BF16)
    bits = pltpu.bitcast(h_hi.astype(F32), jnp.uint32)
    half = bits.shape[1] // 2
    word = (bits[:, :half] >> 16) | (bits[:, half:] & jnp.uint32(0xFFFF0000))
    hm_ref[0] = pltpu.bitcast(word, jnp.int32)
    tm = hm.shape[0]
    hi_lo = _dot(jnp.concatenate([h_hi, h_lo], axis=0), wrh_ref[...])
    logits = hi_lo[:tm] + (hi_lo[tm:] + _dot(h_hi, wrl_ref[...]))
    lane = lax.broadcasted_iota(jnp.int32, (1, LANES), 1)
    logits = jnp.where(lane < N_EXPERTS, logits, NEG_BIG)
    e = jnp.exp(logits - jnp.max(logits, axis=-1, keepdims=True))
    aff = e / jnp.sum(e, axis=-1, keepdims=True)
    affr_ref[0] = aff
    aff_ref[0] = aff.T[:N_EXPERTS, :]


def _outproj(att, hf, hb, om, x, mod, ng, w_att, w_ml, lng, lnb, wr_hi, wr_lo, tm):
    b, t, d = x.shape
    row = lambda n: pl.BlockSpec((1, tm, n), lambda i, j: (i, j, 0))
    full = lambda w: pl.BlockSpec(w.shape, lambda i, j: (0,) * w.ndim)
    return pl.pallas_call(
        _outproj_kernel,
        out_shape=(jax.ShapeDtypeStruct((b, t, d), F32),
                   jax.ShapeDtypeStruct((b, t, d // 2), jnp.int32),
                   jax.ShapeDtypeStruct((b, N_EXPERTS, t), F32),
                   jax.ShapeDtypeStruct((b, t, LANES), F32)),
        grid=(b, t // tm),
        in_specs=[row(NA_W), row(ML_V_W), row(ML_V_W), row(ML_V_W), row(d),
                  pl.BlockSpec((1, 8, d), lambda i, j: (i, 0, 0)),
                  full(ng), full(w_att), full(w_ml), full(lng), full(lnb), full(wr_hi), full(wr_lo)],
        out_specs=(row(d), row(d // 2), pl.BlockSpec((1, N_EXPERTS, tm), lambda i, j: (i, 0, j)), row(LANES)),
        compiler_params=_cparams(("arbitrary", "arbitrary")),
        name="outproj",
    )(att, hf, hb, om, x, mod, ng, w_att, w_ml, lng, lnb, wr_hi, wr_lo)


UNSELECTED = -1e6


def _select_kernel(aff_ref, idx_ref, pos_ref, off_ref, cum_scr, sel_scr, offs_v, offs_s, dsem, *, cap):
    t = aff_ref.shape[2]
    nb = t // LANES
    lane = lax.broadcasted_iota(jnp.int32, (1, LANES), 1)
    keys = lambda: pltpu.bitcast(aff_ref[0], jnp.int32)

    def count(mask):
        return jnp.sum(jnp.where(mask, 1.0, 0.0), axis=-1, keepdims=True)

    def search(_, c):
        lo, hi = c
        mid = lo + jnp.right_shift(hi - lo, 1)
        ge = count(keys() >= mid) >= cap
        return jnp.where(ge, mid, lo), jnp.where(ge, hi, mid)

    lo0 = jnp.zeros((N_EXPERTS, 1), jnp.int32)
    hi0 = jnp.full((N_EXPERTS, 1), 0x7F800000, jnp.int32)
    thr, _ = lax.fori_loop(0, 31, search, (lo0, hi0))

    r_i = lax.broadcasted_iota(jnp.int32, (LANES, LANES), 0)
    c_i = lax.broadcasted_iota(jnp.int32, (LANES, LANES), 1)
    strict = jnp.where(r_i < c_i, 1.0, 0.0).astype(BF16)
    tr_i = lax.broadcasted_iota(jnp.int32, (t, LANES), 0)
    tc_i = lax.broadcasted_iota(jnp.int32, (t, LANES), 1)
    block_ind = jnp.where(jnp.right_shift(tr_i, 7) == tc_i, 1.0, 0.0).astype(BF16)

    def prefix(x01):
        xb = x01.astype(BF16)
        offs = _dot(_dot(xb, block_ind).astype(BF16), strict)
        for j in range(nb):
            off_j = jnp.sum(jnp.where(lane == j, offs, 0.0), axis=-1, keepdims=True)
            cum_scr[:, j * LANES:(j + 1) * LANES] = _dot(xb[:, j * LANES:(j + 1) * LANES], strict) + off_j
        return offs

    k = keys()
    gt = k > thr
    eq = k == thr
    need = cap - count(gt)
    prefix(jnp.where(eq, 1.0, 0.0))
    sel = jnp.where(gt | (eq & (cum_scr[...] < need)), 1.0, 0.0)
    sel_scr[...] = sel
    offs = prefix(sel)

    pad = jnp.zeros((LANES - N_EXPERTS, LANES), F32)
    for j in range(nb):
        blk = jnp.where(sel_scr[:, j * LANES:(j + 1) * LANES] > 0.0, cum_scr[:, j * LANES:(j + 1) * LANES], UNSELECTED)
        pos_ref[0, j * LANES:(j + 1) * LANES, :] = jnp.concatenate([blk, pad], axis=0).T
    off_ref[0] = jnp.concatenate([offs, pad], axis=0).T

    cum_scr[...] = cum_scr[...] + sel_scr[...]
    idx_ref[...] = jnp.zeros(idx_ref.shape, idx_ref.dtype)
    n_groups = cap // LANES
    real = lane < nb
    ends = pltpu.roll(offs, LANES - 1, axis=1)
    bounds = jnp.zeros((N_EXPERTS, LANES), F32)
    for g in range(n_groups):
        lo = jnp.sum(jnp.where(real & (ends <= g * LANES), 1.0, 0.0), axis=-1, keepdims=True)
        hi = jnp.sum(jnp.where(real & (offs <= g * LANES + LANES - 1), 1.0, 0.0), axis=-1, keepdims=True)
        bounds = jnp.where(lane == g, lo, jnp.where(lane == n_groups + g, hi, bounds))
    offs_v[...] = bounds.astype(jnp.int32)
    to_smem = pltpu.make_async_copy(offs_v, offs_s, dsem)
    to_smem.start()
    to_smem.wait()
    sub = lax.broadcasted_iota(jnp.int32, (LANES, LANES), 0).astype(F32)
    for e in range(N_EXPERTS):
        def group(pg, carry, e=e):
            first = jnp.asarray(pg * LANES, jnp.int32)
            jlo = offs_s[e, pg]
            jhi = offs_s[e, n_groups + pg]
            slots = first.astype(F32) + sub

            def block(jb, acc):
                c = cum_scr[pl.ds(e, 1), pl.ds(pl.multiple_of(jb * LANES, LANES), LANES)]
                return acc + jnp.where(jnp.broadcast_to(c, (LANES, LANES)) <= slots, 1.0, 0.0)

            acc = lax.fori_loop(jlo, jhi, block, jnp.zeros((LANES, LANES), F32))
            col = jnp.sum(acc, axis=-1, keepdims=True).astype(jnp.int32) + jlo * LANES
            idx_ref[0, pl.ds(pl.multiple_of(pg * LANES, LANES), LANES), e:e + 1] = col
            return carry

        lax.fori_loop(0, n_groups, group, 0)


def _select(aff_t, cap):
    b, e, t = aff_t.shape
    assert cap % LANES == 0 and t % LANES == 0 and t // LANES < LANES and 2 * (cap // LANES) <= LANES
    return pl.pallas_call(
        functools.partial(_select_kernel, cap=cap),
        out_shape=(jax.ShapeDtypeStruct((b, cap, LANES), jnp.int32),
                   jax.ShapeDtypeStruct((b, t, LANES), F32),
                   jax.ShapeDtypeStruct((b, LANES, LANES), F32)),
        grid=(b,),
        in_specs=[pl.BlockSpec((1, e, t), lambda i: (i, 0, 0))],
        out_specs=(pl.BlockSpec((1, cap, LANES), lambda i: (i, 0, 0)),
                   pl.BlockSpec((1, t, LANES), lambda i: (i, 0, 0)),
                   pl.BlockSpec((1, LANES, LANES), lambda i: (i, 0, 0))),
        scratch_shapes=[pltpu.VMEM((e, t), F32), pltpu.VMEM((e, t), F32),
                        pltpu.VMEM((e, LANES), jnp.int32), pltpu.SMEM((e, LANES), jnp.int32),
                        pltpu.SemaphoreType.DMA(())],
        compiler_params=_cparams(("arbitrary",)),
        name="select",
    )(aff_t)


SC_CORES = 2
SC_SUBCORES = 16
SC_CHUNK = 64
MOE_GROUPS = 4


def _gather_rows(table, idx):
    n = idx.shape[0]
    v, d = table.shape
    n_workers = SC_CORES * SC_SUBCORES
    per_w = n // n_workers
    assert n % (n_workers * SC_CHUNK) == 0 and d % LANES == 0 and table.dtype.itemsize == 4
    mesh = plsc.VectorSubcoreMesh(core_axis_name="c", subcore_axis_name="s",
                                  num_cores=SC_CORES, num_subcores=SC_SUBCORES)

    @functools.partial(
        pl.kernel, mesh=mesh,
        out_type=jax.ShapeDtypeStruct((n, d), table.dtype),
        scratch_types=[pltpu.VMEM((SC_CHUNK,), jnp.int32),
                       pltpu.VMEM((SC_CHUNK, d), table.dtype),
                       pltpu.SemaphoreType.DMA],
        name="row_gather")
    def gather(table_hbm, idx_hbm, out_hbm, idx_v, rows_v, sem):
        base = (lax.axis_index("s") * SC_CORES + lax.axis_index("c")) * per_w

        @pl.loop(0, per_w, step=SC_CHUNK)
        def _(o):
            pltpu.sync_copy(idx_hbm.at[pl.ds(base + o, SC_CHUNK)], idx_v)
            pltpu.async_copy(table_hbm.at[idx_v], rows_v, sem).wait()
            pltpu.sync_copy(rows_v, out_hbm.at[pl.ds(base + o, SC_CHUNK)])

    return gather(table, idx)


def _moe_kernel(x_ref, wg_ref, wu_ref, wd_ref, o_ref, x_scr, act_scr, *, tf):
    w = pltpu.bitcast(x_ref[0], jnp.uint32)
    half = w.shape[1]
    x_scr[:, :half] = pltpu.bitcast(w << 16, F32).astype(BF16)
    x_scr[:, half:] = pltpu.bitcast(w & jnp.uint32(0xFFFF0000), F32).astype(BF16)
    x = x_scr[...]
    for c in range(act_scr.shape[1] // tf):
        cols = slice(c * tf, (c + 1) * tf)
        hg = _dot(x, wg_ref[0, :, cols])
        hu = _dot(x, wu_ref[0, :, cols])
        act_scr[:, cols] = (hg * _sigmoid(hg) * hu).astype(BF16)
    o_ref[0] = _dot(act_scr[...], wd_ref[0]).astype(o_ref.dtype)


def _moe(xe, w_gate, w_up, w_down, e0, tm, tf):
    e, m, dw = xe.shape
    d = 2 * dw
    dff = w_gate.shape[2]
    return pl.pallas_call(
        functools.partial(_moe_kernel, tf=tf),
        out_shape=jax.ShapeDtypeStruct((e, m, d), BF16),
        grid=(e, m // tm),
        in_specs=[pl.BlockSpec((1, tm, dw), lambda i, j: (i, j, 0)),
                  pl.BlockSpec((1, d, dff), lambda i, j: (i + e0, 0, 0)),
                  pl.BlockSpec((1, d, dff), lambda i, j: (i + e0, 0, 0)),
                  pl.BlockSpec((1, dff, d), lambda i, j: (i + e0, 0, 0))],
        out_specs=pl.BlockSpec((1, tm, d), lambda i, j: (i, j, 0)),
        scratch_shapes=[pltpu.VMEM((tm, d), BF16), pltpu.VMEM((tm, dff), BF16)],
        compiler_params=_cparams(("arbitrary", "arbitrary")),
        name="moe",
    )(xe, w_gate, w_up, w_down)


CMB_TM = 256
CMB_WIN = 64


def _combine_kernel(p0_ref, x1_ref, pos_ref, aff_ref, mod_ref, lng_ref, lnb_ref, *rest, cap, n_parts):
    y_parts = rest[:n_parts]
    o_ref, ybuf, yextra, acc_ref, sem, sem_x = rest[n_parts:]
    per_part = N_EXPERTS // n_parts
    _combine_body(p0_ref, x1_ref, pos_ref, aff_ref, mod_ref, lng_ref, lnb_ref,
                  lambda e: y_parts[e // per_part].at[e % per_part], y_parts[0].shape[1],
                  o_ref, ybuf, yextra, acc_ref, sem, sem_x, cap)


def _combine_body(p0_ref, x1_ref, pos_ref, aff_ref, mod_ref, lng_ref, lnb_ref, y_of, m_rows, o_ref,
                  ybuf, yextra, acc_ref, sem, sem_x, cap):
    b = pl.program_id(0)
    j = pl.program_id(1)
    nt = pl.num_programs(1)
    g = b * nt + j
    n_steps = pl.num_programs(0) * nt
    slot = lax.rem(g, 2)
    win = CMB_WIN
    kdim = N_EXPERTS * win

    def window(bb, jj, r):
        base = (bb * (nt + 1) + jj) * N_EXPERTS
        los, rows = [], []
        for e in range(N_EXPERTS):
            lo = (p0_ref[base + e] // ROW_ALIGN) * ROW_ALIGN + r * win
            los.append(lo)
            rows.append(jnp.minimum(bb * cap + lo, m_rows - win))
        return los, rows

    def copies(rows, buf, sems):
        return [pltpu.make_async_copy(y_of(e).at[pl.ds(pl.multiple_of(rows[e], ROW_ALIGN), win), :],
                                      buf.at[pl.ds(e * win, win), :], sems.at[e])
                for e in range(N_EXPERTS)]

    los0, rows0 = window(b, j, 0)

    @pl.when(g == 0)
    def _():
        for c in copies(rows0, ybuf.at[0], sem.at[0]):
            c.start()

    @pl.when(g + 1 < n_steps)
    def _():
        wrap = j + 1 == nt
        _, rows_n = window(jnp.where(wrap, b + 1, b), jnp.where(wrap, 0, j + 1), 0)
        for c in copies(rows_n, ybuf.at[1 - slot], sem.at[1 - slot]):
            c.start()

    lane = lax.broadcasted_iota(jnp.int32, (1, LANES), 1)
    e_i = lax.broadcasted_iota(jnp.int32, (LANES, kdim), 0)
    k_i = lax.broadcasted_iota(jnp.int32, (LANES, kdim), 1)
    expand = jnp.where(k_i // win == e_i, 1.0, 0.0).astype(BF16)
    col_in_win = (lax.broadcasted_iota(jnp.int32, (1, kdim), 1) % win).astype(F32)
    aff_b = aff_ref[0].astype(BF16)
    tm = aff_b.shape[0]

    def lane_vec(vals):
        v = jnp.zeros((1, LANES), F32)
        for e in range(N_EXPERTS):
            v = jnp.where(lane == e, jnp.asarray(vals[e]).astype(F32), v)
        return v

    def expand_rows(los, rows, buf):
        rel = pos_ref[0] - lane_vec(los)
        shift = lane_vec([b * cap + lo - row for lo, row in zip(los, rows)])
        tgt = jnp.where((rel >= 0.0) & (rel < float(win)), rel + shift, -1.0)
        both = _dot(jnp.concatenate([tgt.astype(BF16), aff_b], axis=0), expand)
        s = jnp.where(both[:tm] == col_in_win, both[tm:], 0.0).astype(BF16)
        return _dot(s, buf[...])

    for c in copies(rows0, ybuf.at[slot], sem.at[slot]):
        c.wait()
    acc_ref[...] = expand_rows(los0, rows0, ybuf.at[slot])

    base = (b * (nt + 1) + j) * N_EXPERTS
    rounds = jnp.int32(0)
    for e in range(N_EXPERTS):
        span = p0_ref[base + N_EXPERTS + e] - (p0_ref[base + e] // ROW_ALIGN) * ROW_ALIGN
        rounds = jnp.maximum(rounds, (span + win - 1) // win)

    def extra(r, carry):
        los, rows = window(b, j, r)
        cps = copies(rows, yextra, sem_x)
        for c in cps:
            c.start()
        for c in cps:
            c.wait()
        acc_ref[...] += expand_rows(los, rows, yextra)
        return carry

    lax.fori_loop(1, rounds, extra, 0)

    y = DEEPNORM_ALPHA * x1_ref[0] + mod_ref[0, 5:6, :] * acc_ref[...]
    mu = jnp.mean(y, axis=-1, keepdims=True)
    var = jnp.mean(jnp.square(y - mu), axis=-1, keepdims=True)
    o_ref[0] = (y - mu) * lax.rsqrt(var + LN_EPS) * lng_ref[...] + lnb_ref[...]


def _combine(p0tab, x1, pos_t, aff_r, mod, lng, lnb, ys, cap):
    b, t, d = x1.shape
    tm = CMB_TM
    kdim = N_EXPERTS * CMB_WIN
    row = lambda n: pl.BlockSpec((1, tm, n), lambda i, j, p: (i, j, 0))
    vec = pl.BlockSpec((1, d), lambda i, j, p: (0, 0))
    grid_spec = pltpu.PrefetchScalarGridSpec(
        num_scalar_prefetch=1,
        grid=(b, t // tm),
        in_specs=[row(d), row(LANES), row(LANES), pl.BlockSpec((1, 8, d), lambda i, j, p: (i, 0, 0)), vec, vec,
                  *[pl.BlockSpec(memory_space=pl.ANY) for _ in ys]],
        out_specs=row(d),
        scratch_shapes=[pltpu.VMEM((2, kdim, d), ys[0].dtype), pltpu.VMEM((kdim, d), ys[0].dtype),
                        pltpu.VMEM((tm, d), F32),
                        pltpu.SemaphoreType.DMA((2, N_EXPERTS)), pltpu.SemaphoreType.DMA((N_EXPERTS,))],
    )
    return pl.pallas_call(
        functools.partial(_combine_kernel, cap=cap, n_parts=len(ys)),
        out_shape=jax.ShapeDtypeStruct((b, t, d), F32),
        grid_spec=grid_spec,
        compiler_params=_cparams(("arbitrary", "arbitrary")),
        name="combine",
    )(p0tab, x1, pos_t, aff_r, mod, lng, lnb, *ys)


def _split_w_in(w_in):
    a_end = 3 * NA_W
    qk_end = a_end + 2 * ML_QK_W
    v_end = qk_end + ML_V_W
    o_end = v_end + ML_V_W
    wg = jnp.pad(w_in[:, o_end:], ((0, 0), (0, LANES - ML_N_GATES)))
    return tuple(w.astype(BF16) for w in
                 (w_in[:, :a_end], w_in[:, a_end:qk_end], w_in[:, qk_end:v_end], w_in[:, v_end:o_end], wg))


def _layer(x, ctx, c, c_ctx, w_ada, b_ada, w_in, b_gate, conv_qk, na_rel_bias, ml_norm_g, w_out,
           ln1_g, ln1_b, w_router, w_expert_gate, w_expert_up, w_expert_down, ln2_g, ln2_b):
    b, t, d = x.shape
    lc = ctx.shape[1]

    cs = jnp.zeros((16, d), F32).at[:b].set(c).at[b].set(c_ctx)
    m = _ada(cs, w_ada, b_ada)
    mod = jnp.pad(m[:b].reshape(b, 6, d), ((0, 0), (0, 2), (0, 0)))
    mod_ctx = jnp.broadcast_to(jnp.pad(m[b].reshape(6, d), ((0, 2), (0, 0)))[None], (b, 8, d))

    ws = _split_w_in(w_in)
    a_lat, qk_lat, v_lat, o_lat, g_lat = _inproj(x, mod, ws, tm=512)
    a_ctx, qk_ctx, v_ctx, _, g_ctx = _inproj(ctx, mod_ctx, ws, tm=lc)

    conv_w = jnp.pad(conv_qk, ((0, 8 - CONV_K), (0, 0)))
    cos_t, sin_t = _rope_tables(t)
    qk_lat = _qkprep(qk_lat, conv_w, cos_t, sin_t)
    qk_ctx = _qkprep(qk_ctx, conv_w, jnp.ones((lc, LANES), F32), jnp.zeros((lc, LANES), F32))

    bg = jnp.pad(b_gate, (0, LANES - ML_N_GATES)).reshape(1, LANES)
    c0 = jnp.zeros((b, 2, ML_HEADS, LANES, 2 * ML_V_DIM), F32)
    m0 = jnp.zeros((b, 2, 8, LANES), F32)
    n_e, _, dff = w_expert_gate.shape
    _, _, c1, m1 = _mlstm(qk_ctx, v_ctx, g_ctx, bg, c0, m0)
    hf, hb, _, _, wd_b = _mlstm(qk_lat, v_lat, g_lat, bg, c1, m1,
                                casts=(w_expert_down.reshape(n_e * dff, d),))
    att, wg_b, wu_b = _attn(a_lat, a_ctx, na_rel_bias,
                            casts=(w_expert_gate.reshape(n_e * d, dff), w_expert_up.reshape(n_e * d, dff)))

    wr = jnp.pad(w_router, ((0, 0), (0, LANES - N_EXPERTS)))
    wr_hi = wr.astype(BF16)
    wr_lo = (wr - wr_hi.astype(F32)).astype(BF16)
    x1, hm, aff_t, aff_r = _outproj(att, hf, hb, o_lat, x, mod, ml_norm_g.reshape(1, -1),
                                    w_out[:NA_W].astype(BF16), w_out[NA_W:].astype(BF16),
                                    ln1_g.reshape(1, d), ln1_b.reshape(1, d), wr_hi, wr_lo, tm=256)

    cap = CAPACITY_FACTOR * t // N_EXPERTS
    idx_t, pos_t, off_t = _select(aff_t, cap)
    rows = jnp.transpose(idx_t[:, :, :N_EXPERTS], (2, 0, 1)) + (jnp.arange(b, dtype=jnp.int32) * t)[None, :, None]
    per = N_EXPERTS // MOE_GROUPS
    wg3, wu3, wd3 = wg_b.reshape(n_e, d, dff), wu_b.reshape(n_e, d, dff), wd_b.reshape(n_e, dff, d)
    ys = []
    for k in range(MOE_GROUPS):
        xe = _gather_rows(hm.reshape(b * t, d // 2), rows[k * per:(k + 1) * per].reshape(-1))
        ys.append(_moe(xe.reshape(per, b * cap, d // 2), wg3, wu3, wd3, e0=k * per,
                       tm=min(1024, b * cap), tf=256))

    nt = t // CMB_TM
    p0 = off_t[:, 0:nt * (CMB_TM // LANES):CMB_TM // LANES, :N_EXPERTS]
    p0 = jnp.concatenate([p0, jnp.full((b, 1, N_EXPERTS), cap, F32)], axis=1)
    p0tab = p0.astype(jnp.int32).reshape(-1)
    return _combine(p0tab, x1, pos_t, aff_r, mod, ln2_g.reshape(1, d), ln2_b.reshape(1, d), ys, cap)


def kernel(x, c, ctx, c_ctx, w_ada, b_ada, w_in, b_gate, conv_qk, na_rel_bias, ml_norm_g, w_out,
           ln1_g, ln1_b, w_router, w_expert_gate, w_expert_up, w_expert_down, ln2_g, ln2_b):
    return _layer(x, ctx, c, c_ctx, w_ada[0], b_ada[0], w_in[0], b_gate[0], conv_qk[0], na_rel_bias[0],
                  ml_norm_g[0], w_out[0], ln1_g[0], ln1_b[0], w_router[0], w_expert_gate[0],
                  w_expert_up[0], w_expert_down[0], ln2_g[0], ln2_b[0])
```
